```python
import jax, jax.numpy as jnp
from jax import lax
import numpy as np

D_MODEL = 1024
BATCH = 4
SEQ = 4096
DEPTH = 1

MOBA_HEADS = 8
MOBA_HEAD_DIM = 64
MOBA_WIDTH = MOBA_HEADS * MOBA_HEAD_DIM
MOBA_BLOCK = 256
MOBA_TOPK = 3
MOBA_ROW_BLOCK = 128
ROPE_THETA = 10000.0
RET_HEADS = 4
RET_QK_DIM = 128
RET_V_DIM = 256
RET_QK_WIDTH = RET_HEADS * RET_QK_DIM
RET_V_WIDTH = RET_HEADS * RET_V_DIM
RET_CHUNK = 256
RET_ANGLE_BASE = 10000.0
SEQ_PAD = 256
N_IN = 3 * MOBA_WIDTH + 2 * RET_QK_WIDTH + 2 * RET_V_WIDTH + 2 * D_MODEL
N_EXPERTS = 64
TOP_K = 6
N_GROUPS = 8
TOPK_GROUPS = 4
D_EXPERT = 256
D_SHARED = 256
ROUTED_SCALE = 2.5
MOE_ROW_BLOCK = 128
PLE_DIM = 256
LN_EPS = 1e-5
GN_EPS = 1e-6

kernel_name = 'moba_retention_moe_deepnorm_hybrid'

f32 = jnp.float32


def _layer_norm(x, g, b):
    xf = x.astype(f32)
    mu = jnp.mean(xf, -1, keepdims=True)
    var = jnp.mean(jnp.square(xf - mu), -1, keepdims=True)
    return ((xf - mu) * lax.rsqrt(var + LN_EPS) * g.astype(f32) + b.astype(f32)).astype(x.dtype)


def _rotary(x, inv_freq):
    S = x.shape[2]
    ang = jnp.arange(S, dtype=f32)[:, None] * inv_freq[None, :]
    cos = jnp.cos(ang).astype(x.dtype)
    sin = jnp.sin(ang).astype(x.dtype)
    x1, x2 = jnp.split(x, 2, axis=-1)
    return jnp.concatenate([x1 * cos - x2 * sin, x2 * cos + x1 * sin], axis=-1)


def _group_rows(ids, n_groups, row_block):
    n = ids.shape[0]
    order = jnp.argsort(ids)
    sorted_ids = ids[order]
    counts = jax.ops.segment_sum(jnp.ones((n,), jnp.int32), ids, num_segments=n_groups)
    starts = jnp.cumsum(counts) - counts
    padded = (counts + row_block - 1) // row_block * row_block
    pends = jnp.cumsum(padded)
    pstarts = pends - padded
    rank = jnp.arange(n, dtype=jnp.int32) - starts[sorted_ids]
    slot = jnp.zeros((n,), jnp.int32).at[order].set(pstarts[sorted_ids] + rank)
    n_rows = -(-(n + n_groups * (row_block - 1)) // row_block) * row_block
    block_start = jnp.arange(n_rows // row_block, dtype=jnp.int32) * row_block
    block_gid = jnp.minimum(jnp.searchsorted(pends, block_start, side='right'), n_groups - 1).astype(jnp.int32)
    row_item = jnp.full((n_rows,), n, jnp.int32).at[slot].set(jnp.arange(n, dtype=jnp.int32))
    return slot, row_item, block_gid


def _moba(q, k, v):
    B, H, S, Dh = q.shape
    nB = S // MOBA_BLOCK
    pos = jnp.arange(S, dtype=jnp.int32)
    q_blk = pos // MOBA_BLOCK
    kb = k.reshape(B * H * nB, MOBA_BLOCK, Dh)
    vb = v.reshape(B * H * nB, MOBA_BLOCK, Dh)
    k_mean = jnp.mean(kb.astype(f32), axis=1).reshape(B, H, nB, Dh)
    gate = jnp.einsum('bhsd,bhnd->bhsn', q.astype(f32), k_mean)
    past = jnp.arange(nB, dtype=jnp.int32)[None, :] < q_blk[:, None]
    gate = jnp.where(past, gate, -jnp.inf)
    _, top = lax.top_k(gate, min(MOBA_TOPK, nB))
    own = jnp.broadcast_to(q_blk[:, None], (B, H, S, 1))
    sel = jnp.concatenate([top, own], axis=-1)
    valid = jnp.concatenate([top < q_blk[:, None], jnp.ones((B, H, S, 1), bool)], axis=-1)
    J = sel.shape[-1]
    bh = jnp.arange(B * H, dtype=jnp.int32).reshape(B, H, 1, 1)
    gid = (bh * nB + sel).reshape(-1)
    slot, row_item, block_gid = _group_rows(gid, B * H * nB, MOBA_ROW_BLOCK)
    item_q = jnp.broadcast_to(bh * S + pos[None, None, :, None], (B, H, S, J)).reshape(-1)
    item_pos = jnp.broadcast_to(pos[None, None, :, None], (B, H, S, J)).reshape(-1)
    row_q = jnp.concatenate([item_q, jnp.zeros((1,), jnp.int32)])[row_item]
    row_pos = jnp.concatenate([item_pos, jnp.full((1,), -1, jnp.int32)])[row_item]
    row_valid = jnp.concatenate([valid.reshape(-1), jnp.zeros((1,), bool)])[row_item]
    R = MOBA_ROW_BLOCK
    nR = row_item.shape[0] // R
    q_rows = q.reshape(B * H * S, Dh)[row_q].reshape(nR, R, Dh).astype(f32)
    k_rows = kb[block_gid].astype(f32)
    v_rows = vb[block_gid].astype(f32)
    key_pos = (block_gid % nB)[:, None] * MOBA_BLOCK + jnp.arange(MOBA_BLOCK, dtype=jnp.int32)[None, :]
    mask = row_valid.reshape(nR, R, 1) & (key_pos[:, None, :] <= row_pos.reshape(nR, R, 1))
    s = jnp.einsum('nrd,nkd->nrk', q_rows, k_rows) * (Dh ** -0.5)
    s = jnp.where(mask, s, -jnp.inf)
    m = jnp.max(s, axis=-1, keepdims=True)
    m = jnp.where(jnp.isfinite(m), m, 0.0)
    e = jnp.exp(s - m)
    l = jnp.sum(e, axis=-1)
    l_safe = jnp.maximum(l, 1e-30)
    o = jnp.einsum('nrk,nkd->nrd', e, v_rows) / l_safe[..., None]
    lse = jnp.where(l > 0, m[..., 0] + jnp.log(l_safe), -jnp.inf)
    o_item = o.reshape(-1, Dh)[slot].reshape(B, H, S, J, Dh)
    lse_item = lse.reshape(-1)[slot].reshape(B, H, S, J)
    w = jax.nn.softmax(lse_item, axis=-1)
    out = jnp.einsum('bhsj,bhsjd->bhsd', w, o_item)
    return out.astype(q.dtype)


def _retention(q, k, v, gammas):
    B, H, S, dk = q.shape
    dv = v.shape[-1]
    C = RET_CHUNK
    nC = S // C
    qc = q.astype(f32).reshape(B, H, nC, C, dk)
    kc = k.astype(f32).reshape(B, H, nC, C, dk) * (dk ** -0.5)
    vc = v.astype(f32).reshape(B, H, nC, C, dv)
    log_g = jnp.log(gammas)
    idx = jnp.arange(C, dtype=f32)
    diff = idx[:, None] - idx[None, :]
    decay = jnp.where(diff >= 0, jnp.exp(jnp.maximum(diff, 0.0)[None] * log_g[:, None, None]), 0.0)
    scores = jnp.einsum('bhcnd,bhcmd->bhcnm', qc, kc) * decay[None, :, None]
    inner = jnp.einsum('bhcnm,bhcme->bhcne', scores, vc)
    zeta = jnp.exp((C - 1 - idx)[None, :] * log_g[:, None])
    kv = jnp.einsum('bhcmd,bhcme->cbhde', kc * zeta[None, :, None, :, None], vc)
    chunk_decay = jnp.exp(C * log_g)[None, :, None, None]

    def step(state, kv_c):
        return chunk_decay * state + kv_c, state

    _, prev = lax.scan(step, jnp.zeros((B, H, dk, dv), f32), kv)
    xi = jnp.exp((idx + 1.0)[None, :] * log_g[:, None])
    cross = jnp.einsum('bhcnd,cbhde->bhcne', qc, prev) * xi[None, :, None, :, None]
    return (inner + cross).reshape(B, H, S, dv)


def _mixer(x, w_in, w_moba_up, w_ret_up, w_out):
    B, S, D = x.shape
    Sp = -(-S // SEQ_PAD) * SEQ_PAD
    xp = jnp.pad(x, ((0, 0), (0, Sp - S), (0, 0)))
    proj = xp @ w_in
    splits = [MOBA_WIDTH, 2 * MOBA_WIDTH, 3 * MOBA_WIDTH,
              3 * MOBA_WIDTH + RET_QK_WIDTH, 3 * MOBA_WIDTH + 2 * RET_QK_WIDTH,
              3 * MOBA_WIDTH + 2 * RET_QK_WIDTH + RET_V_WIDTH,
              3 * MOBA_WIDTH + 2 * RET_QK_WIDTH + 2 * RET_V_WIDTH,
              3 * MOBA_WIDTH + 2 * RET_QK_WIDTH + 2 * RET_V_WIDTH + D_MODEL]
    qa, ka, va, qr, kr, vr, gr, ga, gb = jnp.split(proj, splits, axis=-1)

    def heads(t, n):
        return t.reshape(B, Sp, n, -1).transpose(0, 2, 1, 3)

    inv_a = 1.0 / (ROPE_THETA ** (jnp.arange(0, MOBA_HEAD_DIM, 2, dtype=f32) / MOBA_HEAD_DIM))
    oa = _moba(_rotary(heads(qa, MOBA_HEADS), inv_a), _rotary(heads(ka, MOBA_HEADS), inv_a), heads(va, MOBA_HEADS))
    oa = oa.transpose(0, 2, 1, 3).reshape(B, Sp, MOBA_WIDTH)
    inv_r = 1.0 / (RET_ANGLE_BASE ** jnp.linspace(0.0, 1.0, RET_QK_DIM // 2, dtype=f32))
    gammas = 1.0 - 2.0 ** (-5.0 - jnp.arange(RET_HEADS, dtype=f32))
    orr = _retention(_rotary(heads(qr, RET_HEADS), inv_r), _rotary(heads(kr, RET_HEADS), inv_r), heads(vr, RET_HEADS), gammas)
    mu = jnp.mean(orr, -1, keepdims=True)
    var = jnp.mean(jnp.square(orr - mu), -1, keepdims=True)
    orr = (orr - mu) * lax.rsqrt(var + GN_EPS)
    orr = orr.transpose(0, 2, 1, 3).reshape(B, Sp, RET_V_WIDTH).astype(x.dtype) * jax.nn.silu(gr)
    merged = jax.nn.sigmoid(ga) * (oa @ w_moba_up) + jax.nn.sigmoid(gb) * (orr @ w_ret_up)
    return (merged @ w_out)[:, :S]


def _moe(h, w_router, router_bias, w_eg, w_eu, w_ed, w_sg, w_su, w_sd):
    B, S, D = h.shape
    T = B * S
    x = h.reshape(T, D)
    scores = jax.nn.sigmoid(x.astype(f32) @ w_router.astype(f32))
    biased = scores + router_bias.astype(f32)[None, :]
    per_group = N_EXPERTS // N_GROUPS
    g_score = jnp.sum(lax.top_k(biased.reshape(T, N_GROUPS, per_group), 2)[0], axis=-1)
    _, top_g = lax.top_k(g_score, TOPK_GROUPS)
    g_mask = jnp.sum(jax.nn.one_hot(top_g, N_GROUPS, dtype=f32), axis=1) > 0
    e_mask = jnp.repeat(g_mask, per_group, axis=1)
    _, top_e = lax.top_k(jnp.where(e_mask, biased, -jnp.inf), TOP_K)
    w = jnp.take_along_axis(scores, top_e, axis=1)
    w = w / jnp.sum(w, -1, keepdims=True) * ROUTED_SCALE
    slot, row_item, block_eid = _group_rows(top_e.reshape(-1), N_EXPERTS, MOE_ROW_BLOCK)
    row_tok = row_item // TOP_K
    x_pad = jnp.concatenate([x, jnp.zeros((1, D), x.dtype)], axis=0)
    x_rows = x_pad[row_tok].reshape(-1, MOE_ROW_BLOCK, D)

    def expert_block(args):
        xb, e = args
        return (jax.nn.silu(xb @ w_eg[e]) * (xb @ w_eu[e])) @ w_ed[e]

    y_rows = lax.map(expert_block, (x_rows, block_eid)).reshape(-1, D)
    y_sel = y_rows[slot].reshape(T, TOP_K, D)
    routed = jnp.einsum('tk,tkd->td', w.astype(x.dtype), y_sel)
    shared = (jax.nn.silu(x @ w_sg) * (x @ w_su)) @ w_sd
    return (routed + shared).reshape(B, S, D)


def setup_inputs(seed: int = 0) -> dict:
    key = jax.random.key(seed)
    ks = jax.random.split(key, 22)
    L = DEPTH
    beta = (8.0 * DEPTH) ** -0.25

    def nrm(k, shape, fan_in, scale=1.0):
        return jax.random.normal(k, shape, f32) * (scale * fan_in ** -0.5)

    def gain(k, shape):
        return 1.0 + 0.02 * jax.random.normal(k, shape, f32)

    def bias(k, shape, s=0.02):
        return s * jax.random.normal(k, shape, f32)

    col_scale = jnp.concatenate([
        jnp.ones((2 * MOBA_WIDTH,), f32), jnp.full((MOBA_WIDTH,), beta, f32),
        jnp.ones((2 * RET_QK_WIDTH,), f32), jnp.full((RET_V_WIDTH,), beta, f32),
        jnp.ones((RET_V_WIDTH + 2 * D_MODEL,), f32)])
    return {
        'x': jax.random.normal(ks[0], (BATCH, SEQ, D_MODEL), f32),
        'p': jax.random.normal(ks[1], (DEPTH, BATCH, SEQ, PLE_DIM), f32),
        'w_in': nrm(ks[2], (L, D_MODEL, N_IN), D_MODEL) * col_scale,
        'w_moba_up': nrm(ks[3], (L, MOBA_WIDTH, D_MODEL), MOBA_WIDTH),
        'w_ret_up': nrm(ks[4], (L, RET_V_WIDTH, D_MODEL), RET_V_WIDTH),
        'w_out': nrm(ks[5], (L, D_MODEL, D_MODEL), D_MODEL, beta),
        'ln1_g': gain(ks[6], (L, D_MODEL)),
        'ln1_b': bias(ks[7], (L, D_MODEL)),
        'w_router': nrm(ks[8], (L, D_MODEL, N_EXPERTS), D_MODEL),
        'router_bias': bias(ks[9], (L, N_EXPERTS), 0.01),
        'w_exp_gate': nrm(ks[10], (L, N_EXPERTS, D_MODEL, D_EXPERT), D_MODEL),
        'w_exp_up': nrm(ks[11], (L, N_EXPERTS, D_MODEL, D_EXPERT), D_MODEL),
        'w_exp_down': nrm(ks[12], (L, N_EXPERTS, D_EXPERT, D_MODEL), D_EXPERT, beta),
        'w_sh_gate': nrm(ks[13], (L, D_MODEL, D_SHARED), D_MODEL),
        'w_sh_up': nrm(ks[14], (L, D_MODEL, D_SHARED), D_MODEL),
        'w_sh_down': nrm(ks[15], (L, D_SHARED, D_MODEL), D_SHARED, beta),
        'ln2_g': gain(ks[16], (L, D_MODEL)),
        'ln2_b': bias(ks[17], (L, D_MODEL)),
        'w_ple_proj': nrm(ks[18], (L, PLE_DIM, D_MODEL), PLE_DIM, beta),
        'w_ple_gate': nrm(ks[19], (L, D_MODEL, D_MODEL), D_MODEL),
        'ln3_g': gain(ks[20], (L, D_MODEL)),
        'ln3_b': bias(ks[21], (L, D_MODEL)),
    }


def reference(x, p, w_in, w_moba_up, w_ret_up, w_out, ln1_g, ln1_b, w_router, router_bias,
              w_exp_gate, w_exp_up, w_exp_down, w_sh_gate, w_sh_up, w_sh_down, ln2_g, ln2_b,
              w_ple_proj, w_ple_gate, ln3_g, ln3_b):
    alpha = (2.0 * DEPTH) ** 0.25
    h = x
    for i in range(DEPTH):
        h = _layer_norm(alpha * h + _mixer(h, w_in[i], w_moba_up[i], w_ret_up[i], w_out[i]), ln1_g[i], ln1_b[i])
        h = _layer_norm(alpha * h + _moe(h, w_router[i], router_bias[i], w_exp_gate[i], w_exp_up[i], w_exp_down[i],
                                         w_sh_gate[i], w_sh_up[i], w_sh_down[i]), ln2_g[i], ln2_b[i])
        ple = (p[i] @ w_ple_proj[i]) * jax.nn.sigmoid(h @ w_ple_gate[i])
        h = _layer_norm(alpha * h + ple, ln3_g[i], ln3_b[i])
    return h
```

```python
import functools

import jax
import jax.numpy as jnp
from jax import lax
from jax.experimental import pallas as pl
from jax.experimental.pallas import tpu as pltpu

f32 = jnp.float32
bf16 = jnp.bfloat16
i32 = jnp.int32

D_MODEL = 1024
DEPTH = 1
MOBA_HEADS = 8
MOBA_HEAD_DIM = 64
MOBA_WIDTH = MOBA_HEADS * MOBA_HEAD_DIM
MOBA_BLOCK = 256
MOBA_TOPK = 3
ROPE_THETA = 10000.0
RET_HEADS = 4
RET_QK_DIM = 128
RET_V_DIM = 256
RET_QK_WIDTH = RET_HEADS * RET_QK_DIM
RET_V_WIDTH = RET_HEADS * RET_V_DIM
RET_CHUNK = 256
RET_ANGLE_BASE = 10000.0
N_IN = 3 * MOBA_WIDTH + 2 * RET_QK_WIDTH + 2 * RET_V_WIDTH + 2 * D_MODEL
N_EXPERTS = 64
TOP_K = 6
N_GROUPS = 8
TOPK_GROUPS = 4
GROUP_SIZE = N_EXPERTS // N_GROUPS
D_EXPERT = 256
D_SHARED = 256
ROUTED_SCALE = 2.5
PLE_DIM = 256
LN_EPS = 1e-5
GN_EPS = 1e-6
ALPHA = (2.0 * DEPTH) ** 0.25

OFF_QA = 0
OFF_KA = MOBA_WIDTH
OFF_VA = 2 * MOBA_WIDTH
OFF_QR = 3 * MOBA_WIDTH
OFF_KR = OFF_QR + RET_QK_WIDTH
OFF_VR = OFF_KR + RET_QK_WIDTH
OFF_GR = OFF_VR + RET_V_WIDTH
OFF_GA = OFF_GR + RET_V_WIDTH
OFF_GB = OFF_GA + D_MODEL

LANES = 128
NEG = -1e30
VMEM_LIMIT = 56 * 1024 * 1024

PROJ_TM = 1024
PROJ_TN = 512
TOK_TM = 256
ROW_BLOCK = 128


def _dot(a, b):
    return jnp.dot(a, b, preferred_element_type=f32)


def _dot_nt(a, b):
    return lax.dot_general(a, b, (((1,), (1,)), ((), ())), preferred_element_type=f32)


def _dot_tn(a, b):
    return lax.dot_general(a, b, (((0,), (0,)), ((), ())), preferred_element_type=f32)


def _split_bf16(a):
    hi = a.astype(bf16)
    lo = (a - hi.astype(f32)).astype(bf16)
    return hi, lo


def _layer_norm(x, g, b):
    mu = jnp.mean(x, axis=-1, keepdims=True)
    xc = x - mu
    var = jnp.mean(xc * xc, axis=-1, keepdims=True)
    return xc * lax.rsqrt(var + LN_EPS) * g + b


def _sigmoid(x):
    return 1.0 / (1.0 + jnp.exp(-x))


def _params(*sem):
    return pltpu.CompilerParams(dimension_semantics=sem, vmem_limit_bytes=VMEM_LIMIT)


def _rotate_half_chunk(xc, half):
    if 2 * half == LANES:
        return pltpu.roll(xc, half, axis=1)
    lane = lax.broadcasted_iota(i32, xc.shape, 1)
    first = (lane & (2 * half - 1)) < half
    return jnp.where(first, pltpu.roll(xc, LANES - half, axis=1), pltpu.roll(xc, half, axis=1))


def _proj_kernel(x_ref, w_ref, cos_a_ref, sin_a_ref, cos_r_ref, sin_r_ref, o_ref, xb_ref):
    j = pl.program_id(1)

    @pl.when(j == 0)
    def _():
        xb_ref[...] = x_ref[...].astype(bf16)

    acc = _dot(xb_ref[...], w_ref[...])

    def rotary(cos_ref, sin_ref, half, scale):
        cos = cos_ref[...]
        sin = sin_ref[...]
        for c in range(PROJ_TN // LANES):
            xc = acc[:, c * LANES:(c + 1) * LANES]
            y = xc * cos + _rotate_half_chunk(xc, half) * sin
            if scale != 1.0:
                y = y * scale
            o_ref[:, c * LANES:(c + 1) * LANES] = y.astype(o_ref.dtype)

    j_qa = OFF_QA // PROJ_TN
    j_ka = OFF_KA // PROJ_TN
    j_qr = OFF_QR // PROJ_TN
    j_kr = OFF_KR // PROJ_TN

    @pl.when(j == j_qa)
    def _():
        rotary(cos_a_ref, sin_a_ref, MOBA_HEAD_DIM // 2, MOBA_HEAD_DIM ** -0.5)

    @pl.when(j == j_ka)
    def _():
        rotary(cos_a_ref, sin_a_ref, MOBA_HEAD_DIM // 2, 1.0)

    @pl.when(j == j_qr)
    def _():
        rotary(cos_r_ref, sin_r_ref, RET_QK_DIM // 2, 1.0)

    @pl.when(j == j_kr)
    def _():
        rotary(cos_r_ref, sin_r_ref, RET_QK_DIM // 2, RET_QK_DIM ** -0.5)

    plain = (j != j_qa) & (j != j_ka) & (j != j_qr) & (j != j_kr)

    @pl.when(plain)
    def _():
        o_ref[...] = acc.astype(o_ref.dtype)


def _rotary_tables(seq, inv_freq, head_dim):
    ang = jnp.arange(seq, dtype=f32)[:, None] * inv_freq[None, :]
    cos = jnp.cos(ang)
    sin = jnp.sin(ang)
    cos_h = jnp.concatenate([cos, cos], axis=-1)
    sin_h = jnp.concatenate([-sin, sin], axis=-1)
    reps = LANES // head_dim
    return jnp.tile(cos_h, (1, reps)), jnp.tile(sin_h, (1, reps))


def _project(x2d, w_in_b, seq):
    tokens = x2d.shape[0]
    tm = min(PROJ_TM, seq)
    inv_a = 1.0 / (ROPE_THETA ** (jnp.arange(0, MOBA_HEAD_DIM, 2, dtype=f32) / MOBA_HEAD_DIM))
    inv_r = 1.0 / (RET_ANGLE_BASE ** jnp.linspace(0.0, 1.0, RET_QK_DIM // 2, dtype=f32))
    cos_a, sin_a = _rotary_tables(seq, inv_a, MOBA_HEAD_DIM)
    cos_r, sin_r = _rotary_tables(seq, inv_r, RET_QK_DIM)
    seq_tiles = seq // tm
    tab = pl.BlockSpec((tm, LANES), lambda i, j: (i % seq_tiles, 0))
    return pl.pallas_call(
        _proj_kernel,
        grid=(tokens // tm, N_IN // PROJ_TN),
        in_specs=[
            pl.BlockSpec((tm, D_MODEL), lambda i, j: (i, 0)),
            pl.BlockSpec((D_MODEL, PROJ_TN), lambda i, j: (0, j)),
            tab, tab, tab, tab,
        ],
        out_specs=pl.BlockSpec((tm, PROJ_TN), lambda i, j: (i, j)),
        out_shape=jax.ShapeDtypeStruct((tokens, N_IN), bf16),
        scratch_shapes=[pltpu.VMEM((tm, D_MODEL), bf16)],
        compiler_params=_params("parallel", "arbitrary"),
        name="in_proj_rotary",
    )(x2d, w_in_b, cos_a, sin_a, cos_r, sin_r)


def _moba_kernel(q_ref, k_ref, v_ref, o_ref, kmean_ref, *, n_blocks):
    i = pl.program_id(2)
    blk = MOBA_BLOCK

    @pl.when(i == 0)
    def _():
        seq = n_blocks * blk
        member = (lax.broadcasted_iota(i32, (n_blocks, seq), 1) // blk
                  == lax.broadcasted_iota(i32, (n_blocks, seq), 0))
        kmean_ref[...] = _dot(member.astype(bf16), k_ref[...]) * (1.0 / blk)

    q = q_ref[...]
    km_hi, km_lo = _split_bf16(kmean_ref[...])
    lane = lax.broadcasted_iota(i32, (1, LANES), 1)
    row = lax.broadcasted_iota(i32, (blk, blk), 0)
    col = lax.broadcasted_iota(i32, (blk, blk), 1)
    causal = col <= row
    bid = lax.broadcasted_iota(i32, (blk, n_blocks), 1)
    heads_per_chunk = LANES // MOBA_HEAD_DIM

    outs = []
    for h in range(heads_per_chunk):
        hmask = (lane >= h * MOBA_HEAD_DIM) & (lane < (h + 1) * MOBA_HEAD_DIM)
        qh = jnp.where(hmask, q, jnp.zeros_like(q))

        gate = _dot_nt(qh, km_hi) + _dot_nt(qh, km_lo)
        g = jnp.where(bid < i, gate, -jnp.inf)
        sel = jnp.zeros((blk, n_blocks), jnp.bool_)
        for _ in range(MOBA_TOPK):
            m = jnp.max(g, axis=1, keepdims=True)
            idx = jnp.min(jnp.where(g == m, bid, n_blocks), axis=1, keepdims=True)
            hit = bid == idx
            sel = sel | (hit & (m > -jnp.inf))
            g = jnp.where(hit, -jnp.inf, g)
        sel_bias = jnp.where(sel, 0.0, NEG)

        start = pl.multiple_of(i * blk, blk)
        kj = k_ref[pl.ds(start, blk), :]
        vj = v_ref[pl.ds(start, blk), :]
        s = jnp.where(causal, _dot_nt(qh, kj), NEG)
        m0 = jnp.max(s, axis=1, keepdims=True)
        p = jnp.exp(s - m0)
        l0 = jnp.sum(p, axis=1, keepdims=True)
        acc0 = _dot(p.astype(bf16), vj)

        def body(j, carry):
            m_old, l_old, acc = carry
            st = pl.multiple_of(j * blk, blk)
            kj = k_ref[pl.ds(st, blk), :]
            vj = v_ref[pl.ds(st, blk), :]
            bias = jnp.sum(jnp.where(bid == j, sel_bias, 0.0), axis=1, keepdims=True)
            s = _dot_nt(qh, kj) + bias
            m_new = jnp.maximum(m_old, jnp.max(s, axis=1, keepdims=True))
            a = jnp.exp(m_old - m_new)
            p = jnp.exp(s - m_new)
            l_new = a * l_old + jnp.sum(p, axis=1, keepdims=True)
            acc = a * acc + _dot(p.astype(bf16), vj)
            return m_new, l_new, acc

        _, l_fin, acc_fin = lax.fori_loop(0, i, body, (m0, l0, acc0))
        outs.append(acc_fin / l_fin)

    out = outs[0]
    for h in range(1, heads_per_chunk):
        out = jnp.where(lane >= h * MOBA_HEAD_DIM, outs[h], out)
    o_ref[...] = out.astype(o_ref.dtype)


def _moba(proj, batch, seq):
    tokens = batch * seq
    n_blocks = seq // MOBA_BLOCK
    chunks = MOBA_WIDTH // LANES
    return pl.pallas_call(
        functools.partial(_moba_kernel, n_blocks=n_blocks),
        grid=(batch, chunks, n_blocks),
        in_specs=[
            pl.BlockSpec((MOBA_BLOCK, LANES), lambda b, c, i: (b * n_blocks + i, OFF_QA // LANES + c)),
            pl.BlockSpec((seq, LANES), lambda b, c, i: (b, OFF_KA // LANES + c)),
            pl.BlockSpec((seq, LANES), lambda b, c, i: (b, OFF_VA // LANES + c)),
        ],
        out_specs=pl.BlockSpec((MOBA_BLOCK, LANES), lambda b, c, i: (b * n_blocks + i, c)),
        out_shape=jax.ShapeDtypeStruct((tokens, MOBA_WIDTH), bf16),
        scratch_shapes=[pltpu.VMEM((n_blocks, LANES), f32)],
        compiler_params=_params("parallel", "parallel", "arbitrary"),
        name="moba_attention",
    )(proj, proj, proj)


def _ret_kernel(q_ref, k_ref, v_ref, g_ref, decay_ref, zeta_ref, xi_ref, cd_ref, o_ref, state_ref):
    c = pl.program_id(2)

    @pl.when(c == 0)
    def _():
        state_ref[...] = jnp.zeros_like(state_ref)

    q = q_ref[...]
    k = k_ref[...]
    v = v_ref[...]
    state = state_ref[...]
    scores = _dot_nt(q, k) * decay_ref[0]
    inner = _dot(scores.astype(bf16), v)
    cross = _dot(q, state.astype(bf16)) * xi_ref[0]
    o = inner + cross
    kz = (k.astype(f32) * zeta_ref[0]).astype(bf16)
    state_ref[...] = cd_ref[0] * state + _dot_tn(kz, v)

    mu = jnp.mean(o, axis=-1, keepdims=True)
    oc = o - mu
    var = jnp.mean(oc * oc, axis=-1, keepdims=True)
    on = oc * lax.rsqrt(var + GN_EPS)
    g = g_ref[...].astype(f32)
    o_ref[...] = (on * (g * _sigmoid(g))).astype(o_ref.dtype)


def _retention(proj, batch, seq):
    tokens = batch * seq
    C = RET_CHUNK
    n_chunks = seq // C
    gammas = 1.0 - 2.0 ** (-5.0 - jnp.arange(RET_HEADS, dtype=f32))
    log_g = jnp.log(gammas)
    idx = jnp.arange(C, dtype=f32)
    diff = idx[:, None] - idx[None, :]
    decay = jnp.where(diff >= 0, jnp.exp(jnp.maximum(diff, 0.0)[None] * log_g[:, None, None]), 0.0)
    zeta = jnp.exp((C - 1 - idx)[None, :] * log_g[:, None])
    xi = jnp.exp((idx + 1.0)[None, :] * log_g[:, None])
    zeta_t = jnp.broadcast_to(zeta[:, :, None], (RET_HEADS, C, RET_QK_DIM))
    xi_t = jnp.broadcast_to(xi[:, :, None], (RET_HEADS, C, RET_V_DIM))
    cd_t = jnp.broadcast_to(jnp.exp(C * log_g)[:, None, None], (RET_HEADS, 1, RET_V_DIM))
    return pl.pallas_call(
        _ret_kernel,
        grid=(batch, RET_HEADS, n_chunks),
        in_specs=[
            pl.BlockSpec((C, RET_QK_DIM), lambda b, h, c: (b * n_chunks + c, OFF_QR // RET_QK_DIM + h)),
            pl.BlockSpec((C, RET_QK_DIM), lambda b, h, c: (b * n_chunks + c, OFF_KR // RET_QK_DIM + h)),
            pl.BlockSpec((C, RET_V_DIM), lambda b, h, c: (b * n_chunks + c, OFF_VR // RET_V_DIM + h)),
            pl.BlockSpec((C, RET_V_DIM), lambda b, h, c: (b * n_chunks + c, OFF_GR // RET_V_DIM + h)),
            pl.BlockSpec((1, C, C), lambda b, h, c: (h, 0, 0)),
            pl.BlockSpec((1, C, RET_QK_DIM), lambda b, h, c: (h, 0, 0)),
            pl.BlockSpec((1, C, RET_V_DIM), lambda b, h, c: (h, 0, 0)),
            pl.BlockSpec((1, 1, RET_V_DIM), lambda b, h, c: (h, 0, 0)),
        ],
        out_specs=pl.BlockSpec((C, RET_V_DIM), lambda b, h, c: (b * n_chunks + c, h)),
        out_shape=jax.ShapeDtypeStruct((tokens, RET_V_WIDTH), bf16),
        scratch_shapes=[pltpu.VMEM((RET_QK_DIM, RET_V_DIM), f32)],
        compiler_params=_params("parallel", "parallel", "arbitrary"),
        name="retention",
    )(proj, proj, proj, proj, decay, zeta_t, xi_t, cd_t)


def _merge_router_kernel(oa_ref, orr_ref, ga0_ref, ga1_ref, gb0_ref, gb1_ref, x_ref,
                         wm_ref, wr_ref, wo_ref, g1_ref, b1_ref, wrt_hi_ref, wrt_lo_ref, rb_ref,
                         h_ref, e_ref, w_ref, rank_ref, cnt_ref, carry_ref):
    step = pl.program_id(0)
    tm = x_ref.shape[0]

    @pl.when(step == 0)
    def _():
        carry_ref[...] = jnp.zeros_like(carry_ref)

    a = _dot(oa_ref[...], wm_ref[...])
    r = _dot(orr_ref[...], wr_ref[...])
    ga = jnp.concatenate([ga0_ref[...], ga1_ref[...]], axis=1).astype(f32)
    gb = jnp.concatenate([gb0_ref[...], gb1_ref[...]], axis=1).astype(f32)
    merged = _sigmoid(ga) * a + _sigmoid(gb) * r
    mix = _dot(merged.astype(bf16), wo_ref[...])
    h = _layer_norm(ALPHA * x_ref[...] + mix, g1_ref[...], b1_ref[...])
    h_ref[...] = h

    h_hi, h_lo = _split_bf16(h)
    w_hi = wrt_hi_ref[...]
    logits = _dot_nt(w_hi, h_hi) + _dot_nt(w_hi, h_lo) + _dot_nt(wrt_lo_ref[...], h_hi)
    scores = _sigmoid(logits)
    biased = scores + rb_ref[...]

    v = biased.reshape(N_GROUPS, GROUP_SIZE, tm)
    sub = lax.broadcasted_iota(i32, v.shape, 1)
    m1 = jnp.max(v, axis=1, keepdims=True)
    i1 = jnp.min(jnp.where(v == m1, sub, GROUP_SIZE), axis=1, keepdims=True)
    m2 = jnp.max(jnp.where(sub == i1, -jnp.inf, v), axis=1, keepdims=True)
    gscore = jnp.broadcast_to(m1 + m2, v.shape).reshape(N_EXPERTS, tm)

    eid = lax.broadcasted_iota(i32, (N_EXPERTS, tm), 0)
    egrp = eid // GROUP_SIZE
    e_mask = jnp.zeros((N_EXPERTS, tm), jnp.bool_)
    for _ in range(TOPK_GROUPS):
        m = jnp.max(gscore, axis=0, keepdims=True)
        idx = jnp.min(jnp.where(gscore == m, egrp, N_GROUPS), axis=0, keepdims=True)
        hit = egrp == idx
        e_mask = e_mask | hit
        gscore = jnp.where(hit, -jnp.inf, gscore)
    cand = jnp.where(e_mask, biased, -jnp.inf)

    chosen = jnp.zeros((N_EXPERTS, tm), jnp.bool_)
    e_rows = []
    w_rows = []
    for _ in range(TOP_K):
        m = jnp.max(cand, axis=0, keepdims=True)
        idx = jnp.min(jnp.where(cand == m, eid, N_EXPERTS), axis=0, keepdims=True)
        hit = eid == idx
        chosen = chosen | hit
        e_rows.append(idx)
        w_rows.append(jnp.sum(jnp.where(hit, scores, 0.0), axis=0, keepdims=True))
        cand = jnp.where(hit, -jnp.inf, cand)
    w_sum = w_rows[0]
    for wk in w_rows[1:]:
        w_sum = w_sum + wk

    t_src = lax.broadcasted_iota(i32, (tm, tm), 0)
    t_dst = lax.broadcasted_iota(i32, (tm, tm), 1)
    before = (t_src < t_dst).astype(bf16)
    chosen_f = chosen.astype(f32)
    prior = _dot(chosen_f.astype(bf16), before) + carry_ref[...]
    e_ref[...] = jnp.zeros_like(e_ref)
    w_ref[...] = jnp.zeros_like(w_ref)
    rank_ref[...] = jnp.zeros_like(rank_ref)
    for k in range(TOP_K):
        e_ref[k:k + 1, :] = e_rows[k]
        w_ref[k:k + 1, :] = w_rows[k] / w_sum * ROUTED_SCALE
        rank_ref[k:k + 1, :] = jnp.sum(jnp.where(eid == e_rows[k], prior, 0.0), axis=0, keepdims=True).astype(i32)

    carry = carry_ref[...] + jnp.sum(chosen_f, axis=1, keepdims=True)
    carry_ref[...] = carry
    cnt_ref[...] = jnp.broadcast_to(carry, cnt_ref.shape).astype(i32)


def _merge_router(oa, orr, proj, x2d, wm, wr, wo, g1, b1, wrt_hi, wrt_lo, rb):
    tokens = x2d.shape[0]
    tm = TOK_TM
    half = D_MODEL // 2

    def gate_spec(off):
        return pl.BlockSpec((tm, half), lambda i: (i, off // half))

    def full(shape):
        return pl.BlockSpec(shape, lambda i: (0,) * len(shape))

    row8 = pl.BlockSpec((8, tm), lambda i: (0, i))
    return pl.pallas_call(
        _merge_router_kernel,
        grid=(tokens // tm,),
        in_specs=[
            pl.BlockSpec((tm, MOBA_WIDTH), lambda i: (i, 0)),
            pl.BlockSpec((tm, RET_V_WIDTH), lambda i: (i, 0)),
            gate_spec(OFF_GA), gate_spec(OFF_GA + half), gate_spec(OFF_GB), gate_spec(OFF_GB + half),
            pl.BlockSpec((tm, D_MODEL), lambda i: (i, 0)),
            full((MOBA_WIDTH, D_MODEL)), full((RET_V_WIDTH, D_MODEL)), full((D_MODEL, D_MODEL)),
            full((1, D_MODEL)), full((1, D_MODEL)),
            full((N_EXPERTS, D_MODEL)), full((N_EXPERTS, D_MODEL)), full((N_EXPERTS, 1)),
        ],
        out_specs=[
            pl.BlockSpec((tm, D_MODEL), lambda i: (i, 0)),
            row8, row8, row8,
            full((N_EXPERTS, LANES)),
        ],
        out_shape=[
            jax.ShapeDtypeStruct((tokens, D_MODEL), f32),
            jax.ShapeDtypeStruct((8, tokens), i32),
            jax.ShapeDtypeStruct((8, tokens), f32),
            jax.ShapeDtypeStruct((8, tokens), i32),
            jax.ShapeDtypeStruct((N_EXPERTS, LANES), i32),
        ],
        scratch_shapes=[pltpu.VMEM((N_EXPERTS, 1), f32)],
        compiler_params=_params("arbitrary"),
        name="merge_ln1_router",
    )(oa, orr, proj, proj, proj, proj, x2d, wm, wr, wo, g1, b1, wrt_hi, wrt_lo, rb)


def _dispatch_kernel(slot_ref, tail_ref, used_ref, h_ref, rows_ref, zero_ref, sem_ref, zsem_ref):
    step = pl.program_id(0)
    tm = h_ref.shape[0]

    @pl.when(step == 0)
    def _():
        zero_ref[...] = jnp.zeros_like(zero_ref)

        def clear(e, _):
            tail_block = tail_ref[e]
            start = pl.multiple_of(jnp.maximum(tail_block, 0) * ROW_BLOCK, ROW_BLOCK)

            @pl.when(tail_block >= 0)
            def _():
                cp = pltpu.make_async_copy(zero_ref, rows_ref.at[pl.ds(start, ROW_BLOCK), :], zsem_ref)
                cp.start()
                cp.wait()
            return 0

        lax.fori_loop(0, N_EXPERTS, clear, 0)

        def clear_unused(b, _):
            start = pl.multiple_of(b * ROW_BLOCK, ROW_BLOCK)
            cp = pltpu.make_async_copy(zero_ref, rows_ref.at[pl.ds(start, ROW_BLOCK), :], zsem_ref)
            cp.start()
            cp.wait()
            return 0

        lax.fori_loop(used_ref[0], rows_ref.shape[0] // ROW_BLOCK, clear_unused, 0)

    base = step * (tm * TOP_K)

    def row_copy(t, k):
        slot = slot_ref[base + t * TOP_K + k]
        return pltpu.make_async_copy(h_ref.at[pl.ds(t, 1), :], rows_ref.at[pl.ds(slot, 1), :], sem_ref)

    def issue(t, _):
        for k in range(TOP_K):
            row_copy(t, k).start()
        return 0

    lax.fori_loop(0, tm, issue, 0)

    def drain(t, _):
        for k in range(TOP_K):
            row_copy(t, k).wait()
        return 0

    lax.fori_loop(0, tm, drain, 0)


def _dispatch(slot_flat, tail_block, n_used, h, n_rows):
    tokens = h.shape[0]
    tm = TOK_TM
    grid_spec = pltpu.PrefetchScalarGridSpec(
        num_scalar_prefetch=3,
        grid=(tokens // tm,),
        in_specs=[pl.BlockSpec((tm, D_MODEL), lambda i, s, t, u: (i, 0))],
        out_specs=pl.BlockSpec(memory_space=pl.ANY),
        scratch_shapes=[
            pltpu.VMEM((ROW_BLOCK, D_MODEL), f32),
            pltpu.SemaphoreType.DMA,
            pltpu.SemaphoreType.DMA,
        ],
    )
    return pl.pallas_call(
        _dispatch_kernel,
        grid_spec=grid_spec,
        out_shape=jax.ShapeDtypeStruct((n_rows, D_MODEL), f32),
        compiler_params=_params("arbitrary"),
        name="moe_dispatch",
    )(slot_flat, tail_block, n_used, h)


def _expert_kernel(eid_ref, used_ref, x_ref, wg_ref, wu_ref, wd_ref, y_ref):
    b = pl.program_id(0)

    @pl.when(b < used_ref[0])
    def _():
        x = x_ref[...].astype(bf16)
        g = _dot(x, wg_ref[0])
        u = _dot(x, wu_ref[0])
        mid = (g * _sigmoid(g)) * u
        y_ref[...] = _dot(mid.astype(bf16), wd_ref[0])

    @pl.when(b >= used_ref[0])
    def _():
        y_ref[...] = jnp.zeros_like(y_ref)


def _experts(block_eid, n_used, rows, wg, wu, wd):
    n_rows = rows.shape[0]
    n_blocks = n_rows // ROW_BLOCK

    def row_map(b, eid, used):
        return (b, 0)

    def w_map(b, eid, used):
        return (eid[b], 0, 0)

    grid_spec = pltpu.PrefetchScalarGridSpec(
        num_scalar_prefetch=2,
        grid=(n_blocks,),
        in_specs=[
            pl.BlockSpec((ROW_BLOCK, D_MODEL), row_map),
            pl.BlockSpec((1, D_MODEL, D_EXPERT), w_map),
            pl.BlockSpec((1, D_MODEL, D_EXPERT), w_map),
            pl.BlockSpec((1, D_EXPERT, D_MODEL), w_map),
        ],
        out_specs=pl.BlockSpec((ROW_BLOCK, D_MODEL), lambda b, eid, used: (b, 0)),
    )
    return pl.pallas_call(
        _expert_kernel,
        grid_spec=grid_spec,
        out_shape=jax.ShapeDtypeStruct((n_rows, D_MODEL), f32),
        compiler_params=_params("arbitrary"),
        name="moe_experts",
    )(block_eid, n_used, rows, wg, wu, wd)


def _combine_kernel(slot_ref, y_ref, h_ref, wt_ref, p_ref, wsg_ref, wsu_ref, wsd_ref, wpp_ref, wpg_ref,
                    g2_ref, b2_ref, g3_ref, b3_ref, o_ref, ybuf_ref, sem_ref):
    step = pl.program_id(0)
    tm = h_ref.shape[0]
    base = step * (tm * TOP_K)

    def row_copy(t, k):
        slot = slot_ref[base + t * TOP_K + k]
        return pltpu.make_async_copy(y_ref.at[pl.ds(slot, 1), :], ybuf_ref.at[k, pl.ds(t, 1), :], sem_ref)

    def issue(t, _):
        for k in range(TOP_K):
            row_copy(t, k).start()
        return 0

    lax.fori_loop(0, tm, issue, 0)

    h = h_ref[...]
    hb = h.astype(bf16)
    sg = _dot(hb, wsg_ref[...])
    su = _dot(hb, wsu_ref[...])
    shared = _dot(((sg * _sigmoid(sg)) * su).astype(bf16), wsd_ref[...])
    ple_in = _dot(p_ref[...].astype(bf16), wpp_ref[...])

    def drain(t, _):
        for k in range(TOP_K):
            row_copy(t, k).wait()
        return 0

    lax.fori_loop(0, tm, drain, 0)

    wt = wt_ref[...]
    routed = wt[:, 0:1] * ybuf_ref[0]
    for k in range(1, TOP_K):
        routed = routed + wt[:, k:k + 1] * ybuf_ref[k]

    h2 = _layer_norm(ALPHA * h + (routed + shared), g2_ref[...], b2_ref[...])
    ple = ple_in * _sigmoid(_dot(h2.astype(bf16), wpg_ref[...]))
    o_ref[...] = _layer_norm(ALPHA * h2 + ple, g3_ref[...], b3_ref[...])


def _combine(slot_flat, y_rows, h, w_tok, p2d, wsg, wsu, wsd, wpp, wpg, g2, b2, g3, b3):
    tokens = h.shape[0]
    tm = TOK_TM

    def full(shape):
        return pl.BlockSpec(shape, lambda i, s: (0,) * len(shape))

    grid_spec = pltpu.PrefetchScalarGridSpec(
        num_scalar_prefetch=1,
        grid=(tokens // tm,),
        in_specs=[
            pl.BlockSpec(memory_space=pl.ANY),
            pl.BlockSpec((tm, D_MODEL), lambda i, s: (i, 0)),
            pl.BlockSpec((tm, 8), lambda i, s: (i, 0)),
            pl.BlockSpec((tm, PLE_DIM), lambda i, s: (i, 0)),
            full((D_MODEL, D_SHARED)), full((D_MODEL, D_SHARED)), full((D_SHARED, D_MODEL)),
            full((PLE_DIM, D_MODEL)), full((D_MODEL, D_MODEL)),
            full((1, D_MODEL)), full((1, D_MODEL)), full((1, D_MODEL)), full((1, D_MODEL)),
        ],
        out_specs=pl.BlockSpec((tm, D_MODEL), lambda i, s: (i, 0)),
        scratch_shapes=[
            pltpu.VMEM((TOP_K, tm, D_MODEL), f32),
            pltpu.SemaphoreType.DMA,
        ],
    )
    return pl.pallas_call(
        _combine_kernel,
        grid_spec=grid_spec,
        out_shape=jax.ShapeDtypeStruct((tokens, D_MODEL), f32),
        compiler_params=_params("arbitrary"),
        name="moe_combine_ln2_ple_ln3",
    )(slot_flat, y_rows, h, w_tok, p2d, wsg, wsu, wsd, wpp, wpg, g2, b2, g3, b3)


def _row_layout(counts):
    padded = (counts + ROW_BLOCK - 1) // ROW_BLOCK * ROW_BLOCK
    pends = jnp.cumsum(padded)
    pstarts = pends - padded
    tail_block = jnp.where(counts > 0, pends // ROW_BLOCK - 1, -1).astype(i32)
    return pstarts.astype(i32), pends.astype(i32), tail_block


def _layer(x2d, p2d, batch, seq, w_in, w_moba_up, w_ret_up, w_out, ln1_g, ln1_b, w_router, router_bias,
           w_eg, w_eu, w_ed, w_sg, w_su, w_sd, ln2_g, ln2_b, w_ple_proj, w_ple_gate, ln3_g, ln3_b):
    tokens = batch * seq
    row = lambda a: a.reshape(1, -1).astype(f32)

    proj = _project(x2d, w_in.astype(bf16), seq)
    oa = _moba(proj, batch, seq)
    orr = _retention(proj, batch, seq)

    wrt = w_router.astype(f32).T
    wrt_hi = wrt.astype(bf16)
    wrt_lo = (wrt - wrt_hi.astype(f32)).astype(bf16)
    h1, e8, w8, rank8, cnt = _merge_router(
        oa, orr, proj, x2d, w_moba_up.astype(bf16), w_ret_up.astype(bf16), w_out.astype(bf16),
        row(ln1_g), row(ln1_b), wrt_hi, wrt_lo, router_bias.astype(f32).reshape(N_EXPERTS, 1))

    counts = cnt[:, 0]
    pstarts, pends, tail_block = _row_layout(counts)
    n_rows = -(-(tokens * TOP_K + N_EXPERTS * (ROW_BLOCK - 1)) // ROW_BLOCK) * ROW_BLOCK
    n_blocks = n_rows // ROW_BLOCK
    block_start = jnp.arange(n_blocks, dtype=i32) * ROW_BLOCK
    block_eid = jnp.minimum(jnp.searchsorted(pends, block_start, side='right'), N_EXPERTS - 1).astype(i32)
    n_used = (pends[-1:] // ROW_BLOCK).astype(i32)
    slot = pstarts[e8[:TOP_K]] + rank8[:TOP_K]
    slot_flat = slot.T.reshape(-1)

    rows = _dispatch(slot_flat, tail_block, n_used, h1, n_rows)
    y_rows = _experts(block_eid, n_used, rows, w_eg.astype(bf16), w_eu.astype(bf16), w_ed.astype(bf16))
    return _combine(slot_flat, y_rows, h1, w8.T, p2d,
                    w_sg.astype(bf16), w_su.astype(bf16), w_sd.astype(bf16),
                    w_ple_proj.astype(bf16), w_ple_gate.astype(bf16),
                    row(ln2_g), row(ln2_b), row(ln3_g), row(ln3_b))


def kernel(x, p, w_in, w_moba_up, w_ret_up, w_out, ln1_g, ln1_b, w_router, router_bias, w_exp_gate, w_exp_up,
           w_exp_down, w_sh_gate, w_sh_up, w_sh_down, ln2_g, ln2_b, w_ple_proj, w_ple_gate, ln3_g, ln3_b):
    batch, seq, d = x.shape
    assert d == D_MODEL and seq % max(MOBA_BLOCK, RET_CHUNK) == 0 and (batch * seq) % TOK_TM == 0
    assert w_in.shape[0] == DEPTH
    h = x.reshape(batch * seq, d)
    for i in range(DEPTH):
        h = _layer(h, p[i].reshape(batch * seq, PLE_DIM), batch, seq,
                   w_in[i], w_moba_up[i], w_ret_up[i], w_out[i], ln1_g[i], ln1_b[i], w_router[i], router_bias[i],
                   w_exp_gate[i], w_exp_up[i], w_exp_down[i], w_sh_gate[i], w_sh_up[i], w_sh_down[i],
                   ln2_g[i], ln2_b[i], w_ple_proj[i], w_ple_gate[i], ln3_g[i], ln3_b[i])
    return h.reshape(batch, seq, d)
```

```python
import functools

import jax
import jax.numpy as jnp
from jax import lax
from jax.experimental import pallas as pl
from jax.experimental.pallas import tpu as pltpu

f32 = jnp.float32
bf16 = jnp.bfloat16
i32 = jnp.int32

D_MODEL = 1024
DEPTH = 1
MOBA_HEADS = 8
MOBA_HEAD_DIM = 64
MOBA_WIDTH = MOBA_HEADS * MOBA_HEAD_DIM
MOBA_BLOCK = 256
MOBA_TOPK = 3
ROPE_THETA = 10000.0
RET_HEADS = 4
RET_QK_DIM = 128
RET_V_DIM = 256
RET_QK_WIDTH = RET_HEADS * RET_QK_DIM
RET_V_WIDTH = RET_HEADS * RET_V_DIM
RET_CHUNK = 256
RET_ANGLE_BASE = 10000.0
N_IN = 3 * MOBA_WIDTH + 2 * RET_QK_WIDTH + 2 * RET_V_WIDTH + 2 * D_MODEL
N_EXPERTS = 64
TOP_K = 6
N_GROUPS = 8
TOPK_GROUPS = 4
GROUP_SIZE = N_EXPERTS // N_GROUPS
D_EXPERT = 256
D_SHARED = 256
ROUTED_SCALE = 2.5
PLE_DIM = 256
LN_EPS = 1e-5
GN_EPS = 1e-6
ALPHA = (2.0 * DEPTH) ** 0.25

OFF_QA = 0
OFF_KA = MOBA_WIDTH
OFF_VA = 2 * MOBA_WIDTH
OFF_QR = 3 * MOBA_WIDTH
OFF_KR = OFF_QR + RET_QK_WIDTH
OFF_VR = OFF_KR + RET_QK_WIDTH
OFF_GR = OFF_VR + RET_V_WIDTH
OFF_GA = OFF_GR + RET_V_WIDTH
OFF_GB = OFF_GA + D_MODEL

LANES = 128
NEG = -1e30
VMEM_LIMIT = 56 * 1024 * 1024

PROJ_TM = 1024
PROJ_TN = 512
TOK_TM = 256
ROW_BLOCK = 256
UNIT = 8
LOC_ROWS = -(-(TOP_K * TOK_TM + N_EXPERTS * (UNIT - 1)) // LANES) * LANES
LOC_UNITS = LOC_ROWS // UNIT
HALF = D_MODEL // 2


def _dot(a, b):
    return jnp.dot(a, b, preferred_element_type=f32)


def _dot_nt(a, b):
    return lax.dot_general(a, b, (((1,), (1,)), ((), ())), preferred_element_type=f32)


def _dot_tn(a, b):
    return lax.dot_general(a, b, (((0,), (0,)), ((), ())), preferred_element_type=f32)


def _split_bf16(a):
    hi = a.astype(bf16)
    lo = (a - hi.astype(f32)).astype(bf16)
    return hi, lo


def _layer_norm(x, g, b):
    mu = jnp.mean(x, axis=-1, keepdims=True)
    xc = x - mu
    var = jnp.mean(xc * xc, axis=-1, keepdims=True)
    return xc * lax.rsqrt(var + LN_EPS) * g + b


def _sigmoid(x):
    return 1.0 / (1.0 + jnp.exp(-x))


def _params(*sem):
    return pltpu.CompilerParams(dimension_semantics=sem, vmem_limit_bytes=VMEM_LIMIT)


def _rotate_half_chunk(xc, half):
    if 2 * half == LANES:
        return pltpu.roll(xc, half, axis=1)
    lane = lax.broadcasted_iota(i32, xc.shape, 1)
    first = (lane & (2 * half - 1)) < half
    return jnp.where(first, pltpu.roll(xc, LANES - half, axis=1), pltpu.roll(xc, half, axis=1))


def _proj_kernel(x_ref, w_ref, cos_a_ref, sin_a_ref, cos_r_ref, sin_r_ref, o_ref, xb_ref):
    j = pl.program_id(1)

    @pl.when(j == 0)
    def _():
        xb_ref[...] = x_ref[...].astype(bf16)

    acc = _dot(xb_ref[...], w_ref[...])

    def rotary(cos_ref, sin_ref, half, scale):
        cos = cos_ref[...]
        sin = sin_ref[...]
        for c in range(PROJ_TN // LANES):
            xc = acc[:, c * LANES:(c + 1) * LANES]
            y = xc * cos + _rotate_half_chunk(xc, half) * sin
            if scale != 1.0:
                y = y * scale
            o_ref[:, c * LANES:(c + 1) * LANES] = y.astype(o_ref.dtype)

    j_qa = OFF_QA // PROJ_TN
    j_ka = OFF_KA // PROJ_TN
    j_qr = OFF_QR // PROJ_TN
    j_kr = OFF_KR // PROJ_TN

    @pl.when(j == j_qa)
    def _():
        rotary(cos_a_ref, sin_a_ref, MOBA_HEAD_DIM // 2, MOBA_HEAD_DIM ** -0.5)

    @pl.when(j == j_ka)
    def _():
        rotary(cos_a_ref, sin_a_ref, MOBA_HEAD_DIM // 2, 1.0)

    @pl.when(j == j_qr)
    def _():
        rotary(cos_r_ref, sin_r_ref, RET_QK_DIM // 2, 1.0)

    @pl.when(j == j_kr)
    def _():
        rotary(cos_r_ref, sin_r_ref, RET_QK_DIM // 2, RET_QK_DIM ** -0.5)

    plain = (j != j_qa) & (j != j_ka) & (j != j_qr) & (j != j_kr)

    @pl.when(plain)
    def _():
        o_ref[...] = acc.astype(o_ref.dtype)


def _rotary_tables(seq, inv_freq, head_dim):
    ang = jnp.arange(seq, dtype=f32)[:, None] * inv_freq[None, :]
    cos = jnp.cos(ang)
    sin = jnp.sin(ang)
    cos_h = jnp.concatenate([cos, cos], axis=-1)
    sin_h = jnp.concatenate([-sin, sin], axis=-1)
    reps = LANES // head_dim
    return jnp.tile(cos_h, (1, reps)), jnp.tile(sin_h, (1, reps))


def _project(x2d, w_in_b, seq):
    tokens = x2d.shape[0]
    tm = min(PROJ_TM, seq)
    inv_a = 1.0 / (ROPE_THETA ** (jnp.arange(0, MOBA_HEAD_DIM, 2, dtype=f32) / MOBA_HEAD_DIM))
    inv_r = 1.0 / (RET_ANGLE_BASE ** jnp.linspace(0.0, 1.0, RET_QK_DIM // 2, dtype=f32))
    cos_a, sin_a = _rotary_tables(seq, inv_a, MOBA_HEAD_DIM)
    cos_r, sin_r = _rotary_tables(seq, inv_r, RET_QK_DIM)
    seq_tiles = seq // tm
    tab = pl.BlockSpec((tm, LANES), lambda i, j: (i % seq_tiles, 0))
    return pl.pallas_call(
        _proj_kernel,
        grid=(tokens // tm, N_IN // PROJ_TN),
        in_specs=[
            pl.BlockSpec((tm, D_MODEL), lambda i, j: (i, 0)),
            pl.BlockSpec((D_MODEL, PROJ_TN), lambda i, j: (0, j)),
            tab, tab, tab, tab,
        ],
        out_specs=pl.BlockSpec((tm, PROJ_TN), lambda i, j: (i, j)),
        out_shape=jax.ShapeDtypeStruct((tokens, N_IN), bf16),
        scratch_shapes=[pltpu.VMEM((tm, D_MODEL), bf16)],
        compiler_params=_params("parallel", "arbitrary"),
        name="in_proj_rotary",
    )(x2d, w_in_b, cos_a, sin_a, cos_r, sin_r)


def _moba_kernel(q_ref, k_ref, v_ref, o_ref, kmean_ref, vt_ref, selb_ref, *, n_blocks):
    i = pl.program_id(2)
    blk = MOBA_BLOCK
    hd = MOBA_HEAD_DIM
    heads = LANES // hd

    @pl.when(i == 0)
    def _():
        seq = n_blocks * blk
        member = (lax.broadcasted_iota(i32, (n_blocks, seq), 1) // blk
                  == lax.broadcasted_iota(i32, (n_blocks, seq), 0))
        kmean_ref[...] = _dot(member.astype(bf16), k_ref[...]) * (1.0 / blk)
        for j in range(n_blocks):
            vt_ref[j] = v_ref[j * blk:(j + 1) * blk, :].astype(f32).T.astype(bf16)

    q = q_ref[...]
    km_hi, km_lo = _split_bf16(kmean_ref[...])
    lane = lax.broadcasted_iota(i32, (1, LANES), 1)
    key_i = lax.broadcasted_iota(i32, (blk, blk), 0)
    qry_i = lax.broadcasted_iota(i32, (blk, blk), 1)
    causal = key_i <= qry_i
    bid = lax.broadcasted_iota(i32, (n_blocks, blk), 0)

    qts = []
    for h in range(heads):
        hmask = (lane >= h * hd) & (lane < (h + 1) * hd)
        qh = jnp.where(hmask, q, jnp.zeros_like(q))
        qts.append(qh.astype(f32).T.astype(bf16))
        gate = _dot_nt(km_hi, qh) + _dot_nt(km_lo, qh)
        g = jnp.where(bid < i, gate, -jnp.inf)
        sel = jnp.zeros((n_blocks, blk), jnp.bool_)
        for _ in range(MOBA_TOPK):
            m = jnp.max(g, axis=0, keepdims=True)
            idx = jnp.min(jnp.where(g == m, bid, n_blocks), axis=0, keepdims=True)
            hit = bid == idx
            sel = sel | (hit & (m > -jnp.inf))
            g = jnp.where(hit, -jnp.inf, g)
        selb_ref[h] = jnp.where(sel, 0.0, NEG)

    kj = k_ref[pl.ds(pl.multiple_of(i * blk, blk), blk), :]
    vtj = vt_ref[i]
    state = []
    for h in range(heads):
        s = jnp.where(causal, _dot(kj, qts[h]), NEG)
        m0 = jnp.max(s, axis=0, keepdims=True)
        p = jnp.exp(s - m0)
        l0 = jnp.sum(p, axis=0, keepdims=True)
        acc0 = _dot(vtj[h * hd:(h + 1) * hd, :], p.astype(bf16))
        state += [m0, l0, acc0]

    def update(carry, js):
        kjs = [k_ref[pl.ds(pl.multiple_of(j * blk, blk), blk), :] for j in js]
        vtjs = [vt_ref[j] for j in js]
        new = []
        for h in range(heads):
            m_old, l_old, acc = carry[3 * h:3 * h + 3]
            ss = [_dot(kj, qts[h]) + selb_ref[h, pl.ds(j, 1), :] for kj, j in zip(kjs, js)]
            m_new = m_old
            for s in ss:
                m_new = jnp.maximum(m_new, jnp.max(s, axis=0, keepdims=True))
            a = jnp.exp(m_old - m_new)
            l_new = a * l_old
            acc = a * acc
            for s, vtj in zip(ss, vtjs):
                p = jnp.exp(s - m_new)
                l_new = l_new + jnp.sum(p, axis=0, keepdims=True)
                acc = acc + _dot(vtj[h * hd:(h + 1) * hd, :], p.astype(bf16))
            new += [m_new, l_new, acc]
        return tuple(new)

    pairs = i // 2
    fin = lax.fori_loop(0, pairs, lambda jj, c: update(c, (2 * jj, 2 * jj + 1)), tuple(state))
    fin = lax.fori_loop(2 * pairs, i, lambda j, c: update(c, (j,)), fin)
    out_t = jnp.concatenate([fin[3 * h + 2] / fin[3 * h + 1] for h in range(heads)], axis=0)
    o_ref[...] = out_t.T.astype(o_ref.dtype)


def _moba(proj, batch, seq):
    tokens = batch * seq
    n_blocks = seq // MOBA_BLOCK
    chunks = MOBA_WIDTH // LANES
    return pl.pallas_call(
        functools.partial(_moba_kernel, n_blocks=n_blocks),
        grid=(batch, chunks, n_blocks),
        in_specs=[
            pl.BlockSpec((MOBA_BLOCK, LANES), lambda b, c, i: (b * n_blocks + i, OFF_QA // LANES + c)),
            pl.BlockSpec((seq, LANES), lambda b, c, i: (b, OFF_KA // LANES + c)),
            pl.BlockSpec((seq, LANES), lambda b, c, i: (b, OFF_VA // LANES + c)),
        ],
        out_specs=pl.BlockSpec((MOBA_BLOCK, LANES), lambda b, c, i: (b * n_blocks + i, c)),
        out_shape=jax.ShapeDtypeStruct((tokens, MOBA_WIDTH), bf16),
        scratch_shapes=[
            pltpu.VMEM((n_blocks, LANES), f32),
            pltpu.VMEM((n_blocks, LANES, MOBA_BLOCK), bf16),
            pltpu.VMEM((LANES // MOBA_HEAD_DIM, n_blocks, MOBA_BLOCK), f32),
        ],
        compiler_params=_params("parallel", "parallel", "arbitrary"),
        name="moba_attention",
    )(proj, proj, proj)


def _ret_kernel(q_ref, k_ref, v_ref, g_ref, decay_ref, zeta_ref, xi_ref, cd_ref, o_ref, state_ref):
    c = pl.program_id(2)

    @pl.when(c == 0)
    def _():
        state_ref[...] = jnp.zeros_like(state_ref)

    q = q_ref[...]
    k = k_ref[...]
    v = v_ref[...]
    state = state_ref[...]
    scores = _dot_nt(q, k) * decay_ref[0]
    inner = _dot(scores.astype(bf16), v)
    cross = _dot(q, state.astype(bf16)) * xi_ref[0]
    o = inner + cross
    kz = (k.astype(f32) * zeta_ref[0]).astype(bf16)
    state_ref[...] = cd_ref[0] * state + _dot_tn(kz, v)

    mu = jnp.mean(o, axis=-1, keepdims=True)
    oc = o - mu
    var = jnp.mean(oc * oc, axis=-1, keepdims=True)
    on = oc * lax.rsqrt(var + GN_EPS)
    g = g_ref[...].astype(f32)
    o_ref[...] = (on * (g * _sigmoid(g))).astype(o_ref.dtype)


def _retention(proj, batch, seq):
    tokens = batch * seq
    C = RET_CHUNK
    n_chunks = seq // C
    gammas = 1.0 - 2.0 ** (-5.0 - jnp.arange(RET_HEADS, dtype=f32))
    log_g = jnp.log(gammas)
    idx = jnp.arange(C, dtype=f32)
    diff = idx[:, None] - idx[None, :]
    decay = jnp.where(diff >= 0, jnp.exp(jnp.maximum(diff, 0.0)[None] * log_g[:, None, None]), 0.0)
    zeta = jnp.exp((C - 1 - idx)[None, :] * log_g[:, None])
    xi = jnp.exp((idx + 1.0)[None, :] * log_g[:, None])
    zeta_t = jnp.broadcast_to(zeta[:, :, None], (RET_HEADS, C, RET_QK_DIM))
    xi_t = jnp.broadcast_to(xi[:, :, None], (RET_HEADS, C, RET_V_DIM))
    cd_t = jnp.broadcast_to(jnp.exp(C * log_g)[:, None, None], (RET_HEADS, 1, RET_V_DIM))
    return pl.pallas_call(
        _ret_kernel,
        grid=(batch, RET_HEADS, n_chunks),
        in_specs=[
            pl.BlockSpec((C, RET_QK_DIM), lambda b, h, c: (b * n_chunks + c, OFF_QR // RET_QK_DIM + h)),
            pl.BlockSpec((C, RET_QK_DIM), lambda b, h, c: (b * n_chunks + c, OFF_KR // RET_QK_DIM + h)),
            pl.BlockSpec((C, RET_V_DIM), lambda b, h, c: (b * n_chunks + c, OFF_VR // RET_V_DIM + h)),
            pl.BlockSpec((C, RET_V_DIM), lambda b, h, c: (b * n_chunks + c, OFF_GR // RET_V_DIM + h)),
            pl.BlockSpec((1, C, C), lambda b, h, c: (h, 0, 0)),
            pl.BlockSpec((1, C, RET_QK_DIM), lambda b, h, c: (h, 0, 0)),
            pl.BlockSpec((1, C, RET_V_DIM), lambda b, h, c: (h, 0, 0)),
            pl.BlockSpec((1, 1, RET_V_DIM), lambda b, h, c: (h, 0, 0)),
        ],
        out_specs=pl.BlockSpec((C, RET_V_DIM), lambda b, h, c: (b * n_chunks + c, h)),
        out_shape=jax.ShapeDtypeStruct((tokens, RET_V_WIDTH), bf16),
        scratch_shapes=[pltpu.VMEM((RET_QK_DIM, RET_V_DIM), f32)],
        compiler_params=_params("parallel", "parallel", "arbitrary"),
        name="retention",
    )(proj, proj, proj, proj, decay, zeta_t, xi_t, cd_t)


def _merge_router_kernel(oa_ref, orr_ref, ga0_ref, ga1_ref, gb0_ref, gb1_ref, x_ref,
                         wm_ref, wr_ref, wo_ref, g1_ref, b1_ref, wrt_hi_ref, wrt_lo_ref, rb_ref,
                         h_ref, pos_ref, w_ref, tab_ref, cnt_ref, carry_ref):
    step = pl.program_id(0)
    tm = x_ref.shape[0]

    @pl.when(step == 0)
    def _():
        carry_ref[...] = jnp.zeros_like(carry_ref)

    a = _dot(oa_ref[...], wm_ref[...])
    r = _dot(orr_ref[...], wr_ref[...])
    ga = jnp.concatenate([ga0_ref[...], ga1_ref[...]], axis=1).astype(f32)
    gb = jnp.concatenate([gb0_ref[...], gb1_ref[...]], axis=1).astype(f32)
    merged = _sigmoid(ga) * a + _sigmoid(gb) * r
    mix = _dot(merged.astype(bf16), wo_ref[...])
    h = _layer_norm(ALPHA * x_ref[...] + mix, g1_ref[...], b1_ref[...])
    h_ref[...] = h

    h_hi, h_lo = _split_bf16(h)
    w_hi = wrt_hi_ref[...]
    logits = _dot_nt(w_hi, h_hi) + _dot_nt(w_hi, h_lo) + _dot_nt(wrt_lo_ref[...], h_hi)
    scores = _sigmoid(logits)
    biased = scores + rb_ref[...]

    v = biased.reshape(N_GROUPS, GROUP_SIZE, tm)
    sub = lax.broadcasted_iota(i32, v.shape, 1)
    m1 = jnp.max(v, axis=1, keepdims=True)
    i1 = jnp.min(jnp.where(v == m1, sub, GROUP_SIZE), axis=1, keepdims=True)
    m2 = jnp.max(jnp.where(sub == i1, -jnp.inf, v), axis=1, keepdims=True)
    gscore = jnp.broadcast_to(m1 + m2, v.shape).reshape(N_EXPERTS, tm)

    eid = lax.broadcasted_iota(i32, (N_EXPERTS, tm), 0)
    egrp = eid // GROUP_SIZE
    e_mask = jnp.zeros((N_EXPERTS, tm), jnp.bool_)
    for _ in range(TOPK_GROUPS):
        m = jnp.max(gscore, axis=0, keepdims=True)
        idx = jnp.min(jnp.where(gscore == m, egrp, N_GROUPS), axis=0, keepdims=True)
        hit = egrp == idx
        e_mask = e_mask | hit
        gscore = jnp.where(hit, -jnp.inf, gscore)
    cand = jnp.where(e_mask, biased, -jnp.inf)

    chosen = jnp.zeros((N_EXPERTS, tm), jnp.bool_)
    e_rows = []
    w_rows = []
    for _ in range(TOP_K):
        m = jnp.max(cand, axis=0, keepdims=True)
        idx = jnp.min(jnp.where(cand == m, eid, N_EXPERTS), axis=0, keepdims=True)
        hit = eid == idx
        chosen = chosen | hit
        e_rows.append(idx)
        w_rows.append(jnp.sum(jnp.where(hit, scores, 0.0), axis=0, keepdims=True))
        cand = jnp.where(hit, -jnp.inf, cand)
    w_sum = w_rows[0]
    for wk in w_rows[1:]:
        w_sum = w_sum + wk

    t_src = lax.broadcasted_iota(i32, (tm, tm), 0)
    t_dst = lax.broadcasted_iota(i32, (tm, tm), 1)
    before = (t_src < t_dst).astype(bf16)
    chosen_f = chosen.astype(f32)
    prior = _dot(chosen_f.astype(bf16), before)
    cnt = jnp.sum(chosen_f, axis=1, keepdims=True)
    cnt_pad = jnp.ceil(cnt * (1.0 / UNIT)) * UNIT
    e_src = lax.broadcasted_iota(i32, (N_EXPERTS, N_EXPERTS), 1)
    e_dst = lax.broadcasted_iota(i32, (N_EXPERTS, N_EXPERTS), 0)
    earlier = (e_src < e_dst).astype(bf16)
    cnt_pad_l = jnp.broadcast_to(cnt_pad, (N_EXPERTS, LANES))
    loc_start_l = _dot(earlier, cnt_pad_l.astype(bf16))
    loc_start = loc_start_l[:, 0:1]
    where_e = prior + loc_start
    pos_ref[...] = jnp.zeros_like(pos_ref)
    w_ref[...] = jnp.zeros_like(w_ref)
    for k in range(TOP_K):
        pos_ref[k:k + 1, :] = jnp.sum(jnp.where(eid == e_rows[k], where_e, 0.0), axis=0, keepdims=True).astype(i32)
        w_ref[k:k + 1, :] = w_rows[k] / w_sum * ROUTED_SCALE

    carry = carry_ref[...]
    tl = lax.broadcasted_iota(i32, (N_EXPERTS, LANES), 1)
    table = jnp.where(tl == 0, cnt_pad_l, jnp.where(tl == 1, loc_start_l, jnp.broadcast_to(carry, (N_EXPERTS, LANES))))
    tab_ref[0] = table.astype(i32)
    carry = carry + cnt_pad
    carry_ref[...] = carry
    cnt_ref[...] = jnp.broadcast_to(carry, cnt_ref.shape).astype(i32)


def _merge_router(oa, orr, proj, x2d, wm, wr, wo, g1, b1, wrt_hi, wrt_lo, rb):
    tokens = x2d.shape[0]
    tm = TOK_TM
    half = D_MODEL // 2

    def gate_spec(off):
        return pl.BlockSpec((tm, half), lambda i: (i, off // half))

    def full(shape):
        return pl.BlockSpec(shape, lambda i: (0,) * len(shape))

    row8 = pl.BlockSpec((8, tm), lambda i: (0, i))
    return pl.pallas_call(
        _merge_router_kernel,
        grid=(tokens // tm,),
        in_specs=[
            pl.BlockSpec((tm, MOBA_WIDTH), lambda i: (i, 0)),
            pl.BlockSpec((tm, RET_V_WIDTH), lambda i: (i, 0)),
            gate_spec(OFF_GA), gate_spec(OFF_GA + half), gate_spec(OFF_GB), gate_spec(OFF_GB + half),
            pl.BlockSpec((tm, D_MODEL), lambda i: (i, 0)),
            full((MOBA_WIDTH, D_MODEL)), full((RET_V_WIDTH, D_MODEL)), full((D_MODEL, D_MODEL)),
            full((1, D_MODEL)), full((1, D_MODEL)),
            full((N_EXPERTS, D_MODEL)), full((N_EXPERTS, D_MODEL)), full((N_EXPERTS, 1)),
        ],
        out_specs=[
            pl.BlockSpec((tm, D_MODEL), lambda i: (i, 0)),
            row8, row8,
            pl.BlockSpec((1, N_EXPERTS, LANES), lambda i: (i, 0, 0)),
            full((N_EXPERTS, LANES)),
        ],
        out_shape=[
            jax.ShapeDtypeStruct((tokens, D_MODEL), f32),
            jax.ShapeDtypeStruct((8, tokens), i32),
            jax.ShapeDtypeStruct((8, tokens), f32),
            jax.ShapeDtypeStruct((tokens // tm, N_EXPERTS, LANES), i32),
            jax.ShapeDtypeStruct((N_EXPERTS, LANES), i32),
        ],
        scratch_shapes=[pltpu.VMEM((N_EXPERTS, 1), f32)],
        compiler_params=_params("arbitrary"),
        name="merge_ln1_router",
    )(oa, orr, proj, proj, proj, proj, x2d, wm, wr, wo, g1, b1, wrt_hi, wrt_lo, rb)


def _pack_rows(x):
    bits = lax.bitcast_convert_type(x, jnp.uint32)
    return (bits[:, :HALF] & jnp.uint32(0xFFFF0000)) | (bits[:, HALF:] >> 16)


def _unpack_rows(words):
    left = lax.bitcast_convert_type(words & jnp.uint32(0xFFFF0000), f32)
    right = lax.bitcast_convert_type(words << 16, f32)
    return left.astype(bf16), right.astype(bf16)


def _unit_slices(local_unit, global_unit):
    loc = pl.ds(pl.multiple_of(local_unit * UNIT, UNIT), UNIT)
    glob = pl.ds(pl.multiple_of(global_unit * UNIT, UNIT), UNIT)
    return loc, glob


def _dispatch_kernel(gu_ref, nu_ref, tail_ref, used_ref, h_ref, pos_ref, rows_ref,
                     xs_ref, zero_ref, sem_ref, zsem_ref):
    step = pl.program_id(0)
    tm = h_ref.shape[0]

    @pl.when(step == 0)
    def _():
        zero_ref[...] = jnp.zeros_like(zero_ref)

        def clear_block(block):
            start = pl.multiple_of(block * ROW_BLOCK, ROW_BLOCK)
            cp = pltpu.make_async_copy(zero_ref, rows_ref.at[pl.ds(start, ROW_BLOCK), :], zsem_ref)
            cp.start()
            cp.wait()

        def clear_tail(e, _):
            tail_block = tail_ref[e]

            @pl.when(tail_block >= 0)
            def _():
                clear_block(tail_block)
            return 0

        lax.fori_loop(0, N_EXPERTS, clear_tail, 0)

        def clear_unused(b, _):
            clear_block(b)
            return 0

        lax.fori_loop(used_ref[0], rows_ref.shape[0] // ROW_BLOCK, clear_unused, 0)

    pos = pos_ref[...]
    r_iota = lax.broadcasted_iota(i32, (LOC_ROWS, tm), 0)
    onehot = r_iota == pos[0:1, :]
    for k in range(1, TOP_K):
        onehot = onehot | (r_iota == pos[k:k + 1, :])
    xs = _dot(onehot.astype(bf16), h_ref[...].astype(bf16))
    xs_ref[...] = _pack_rows(xs)

    base = step * LOC_UNITS
    n_units = nu_ref[step]

    def unit_copy(s):
        loc, glob = _unit_slices(s, gu_ref[base + s])
        return pltpu.make_async_copy(xs_ref.at[loc, :], rows_ref.at[glob, :], sem_ref)

    def issue(s, _):
        unit_copy(s).start()
        return 0

    lax.fori_loop(0, n_units, issue, 0)

    def drain(s, _):
        unit_copy(s).wait()
        return 0

    lax.fori_loop(0, n_units, drain, 0)


def _dispatch(unit_map, n_units, tail_block, n_used, h, pos8, n_rows):
    tokens = h.shape[0]
    tm = TOK_TM
    grid_spec = pltpu.PrefetchScalarGridSpec(
        num_scalar_prefetch=4,
        grid=(tokens // tm,),
        in_specs=[
            pl.BlockSpec((tm, D_MODEL), lambda i, *_: (i, 0)),
            pl.BlockSpec((8, tm), lambda i, *_: (0, i)),
        ],
        out_specs=pl.BlockSpec(memory_space=pl.ANY),
        scratch_shapes=[
            pltpu.VMEM((LOC_ROWS, HALF), jnp.uint32),
            pltpu.VMEM((ROW_BLOCK, HALF), jnp.uint32),
            pltpu.SemaphoreType.DMA,
            pltpu.SemaphoreType.DMA,
        ],
    )
    return pl.pallas_call(
        _dispatch_kernel,
        grid_spec=grid_spec,
        out_shape=jax.ShapeDtypeStruct((n_rows, HALF), jnp.uint32),
        compiler_params=_params("arbitrary"),
        name="moe_dispatch",
    )(unit_map, n_units, tail_block, n_used, h, pos8)


def _expert_kernel(eid_ref, used_ref, x_ref, wg_ref, wu_ref, wd_ref, y_ref, wgb_ref, wub_ref, wdb_ref):
    b = pl.program_id(0)
    changed = (b == 0) | (eid_ref[b] != eid_ref[jnp.maximum(b - 1, 0)])

    @pl.when(changed)
    def _():
        wgb_ref[...] = wg_ref[0].astype(bf16)
        wub_ref[...] = wu_ref[0].astype(bf16)
        wdb_ref[...] = wd_ref[0].astype(bf16)

    @pl.when(b < used_ref[0])
    def _():
        xl, xr = _unpack_rows(x_ref[...])
        g = _dot(xl, wgb_ref[:HALF, :]) + _dot(xr, wgb_ref[HALF:, :])
        u = _dot(xl, wub_ref[:HALF, :]) + _dot(xr, wub_ref[HALF:, :])
        mid = (g * _sigmoid(g)) * u
        y = _dot(mid.astype(bf16), wdb_ref[...])
        y_ref[...] = _pack_rows(y.astype(bf16).astype(f32))

    @pl.when(b >= used_ref[0])
    def _():
        y_ref[...] = jnp.zeros_like(y_ref)


def _experts(block_eid, n_used, rows, wg, wu, wd):
    n_rows = rows.shape[0]
    n_blocks = n_rows // ROW_BLOCK

    def row_map(b, eid, used):
        return (b, 0)

    def w_map(b, eid, used):
        return (eid[b], 0, 0)

    grid_spec = pltpu.PrefetchScalarGridSpec(
        num_scalar_prefetch=2,
        grid=(n_blocks,),
        in_specs=[
            pl.BlockSpec((ROW_BLOCK, HALF), row_map),
            pl.BlockSpec((1, D_MODEL, D_EXPERT), w_map),
            pl.BlockSpec((1, D_MODEL, D_EXPERT), w_map),
            pl.BlockSpec((1, D_EXPERT, D_MODEL), w_map),
        ],
        out_specs=pl.BlockSpec((ROW_BLOCK, HALF), row_map),
        scratch_shapes=[
            pltpu.VMEM((D_MODEL, D_EXPERT), bf16),
            pltpu.VMEM((D_MODEL, D_EXPERT), bf16),
            pltpu.VMEM((D_EXPERT, D_MODEL), bf16),
        ],
    )
    return pl.pallas_call(
        _expert_kernel,
        grid_spec=grid_spec,
        out_shape=jax.ShapeDtypeStruct((n_rows, HALF), jnp.uint32),
        compiler_params=_params("arbitrary"),
        name="moe_experts",
    )(block_eid, n_used, rows, wg, wu, wd)


def _combine_kernel(gu_ref, nu_ref, y_ref, h_ref, pos_ref, wt_ref, p_ref, wsg_ref, wsu_ref, wsd_ref, wpp_ref,
                    wpg_ref, g2_ref, b2_ref, g3_ref, b3_ref, o_ref, ybuf_ref, sem_ref):
    step = pl.program_id(0)
    tm = h_ref.shape[0]

    @pl.when(step == 0)
    def _():
        ybuf_ref[...] = jnp.zeros_like(ybuf_ref)

    base = step * LOC_UNITS
    n_units = nu_ref[step]

    def unit_copy(s):
        loc, glob = _unit_slices(s, gu_ref[base + s])
        return pltpu.make_async_copy(y_ref.at[glob, :], ybuf_ref.at[loc, :], sem_ref)

    def issue(s, _):
        unit_copy(s).start()
        return 0

    lax.fori_loop(0, n_units, issue, 0)

    h = h_ref[...]
    hb = h.astype(bf16)
    sg = _dot(hb, wsg_ref[...])
    su = _dot(hb, wsu_ref[...])
    shared = _dot(((sg * _sigmoid(sg)) * su).astype(bf16), wsd_ref[...])
    ple_in = _dot(p_ref[...].astype(bf16), wpp_ref[...])

    pos = pos_ref[...]
    wt = wt_ref[...]
    c_iota = lax.broadcasted_iota(i32, (tm, LOC_ROWS), 1)
    spread = jnp.where(c_iota == pos[:, 0:1], wt[:, 0:1], 0.0)
    for k in range(1, TOP_K):
        spread = spread + jnp.where(c_iota == pos[:, k:k + 1], wt[:, k:k + 1], 0.0)
    spread = spread.astype(bf16)

    def drain(s, _):
        unit_copy(s).wait()
        return 0

    lax.fori_loop(0, n_units, drain, 0)

    yl, yr = _unpack_rows(ybuf_ref[...])
    routed = jnp.concatenate([_dot(spread, yl), _dot(spread, yr)], axis=1)

    h2 = _layer_norm(ALPHA * h + (routed + shared), g2_ref[...], b2_ref[...])
    ple = ple_in * _sigmoid(_dot(h2.astype(bf16), wpg_ref[...]))
    o_ref[...] = _layer_norm(ALPHA * h2 + ple, g3_ref[...], b3_ref[...])


def _combine(unit_map, n_units, y_rows, h, pos_tok, w_tok, p2d, wsg, wsu, wsd, wpp, wpg, g2, b2, g3, b3):
    tokens = h.shape[0]
    tm = TOK_TM

    def full(shape):
        return pl.BlockSpec(shape, lambda i, *_: (0,) * len(shape))

    def tile(width):
        return pl.BlockSpec((tm, width), lambda i, *_: (i, 0))

    grid_spec = pltpu.PrefetchScalarGridSpec(
        num_scalar_prefetch=2,
        grid=(tokens // tm,),
        in_specs=[
            pl.BlockSpec(memory_space=pl.ANY),
            tile(D_MODEL), tile(8), tile(8), tile(PLE_DIM),
            full((D_MODEL, D_SHARED)), full((D_MODEL, D_SHARED)), full((D_SHARED, D_MODEL)),
            full((PLE_DIM, D_MODEL)), full((D_MODEL, D_MODEL)),
            full((1, D_MODEL)), full((1, D_MODEL)), full((1, D_MODEL)), full((1, D_MODEL)),
        ],
        out_specs=tile(D_MODEL),
        scratch_shapes=[
            pltpu.VMEM((LOC_ROWS, HALF), jnp.uint32),
            pltpu.SemaphoreType.DMA,
        ],
    )
    return pl.pallas_call(
        _combine_kernel,
        grid_spec=grid_spec,
        out_shape=jax.ShapeDtypeStruct((tokens, D_MODEL), f32),
        compiler_params=_params("arbitrary"),
        name="moe_combine_ln2_ple_ln3",
    )(unit_map, n_units, y_rows, h, pos_tok, w_tok, p2d, wsg, wsu, wsd, wpp, wpg, g2, b2, g3, b3)


def _row_layout(table, totals, n_blocks):
    cnt_u = table[:, :, 0] // UNIT
    loc_u = table[:, :, 1] // UNIT
    padded = (totals + ROW_BLOCK - 1) // ROW_BLOCK * ROW_BLOCK
    pends = jnp.cumsum(padded)
    pstarts = pends - padded
    glob_u = (pstarts[None, :] + table[:, :, 2]) // UNIT
    n_units = jnp.sum(cnt_u, axis=1).astype(i32)
    s = jnp.arange(LOC_UNITS, dtype=i32)
    loc_end = loc_u + cnt_u
    e_of = jnp.sum((loc_end[:, None, :] <= s[None, :, None]).astype(i32), axis=-1)
    is_e = e_of[:, :, None] == jnp.arange(N_EXPERTS, dtype=i32)
    shift = jnp.sum(jnp.where(is_e, (glob_u - loc_u)[:, None, :], 0), axis=-1)
    unit_map = (shift + s[None, :]).astype(i32).reshape(-1)
    tail_block = jnp.where(totals > 0, pends // ROW_BLOCK - 1, -1).astype(i32)
    block_start = jnp.arange(n_blocks, dtype=i32) * ROW_BLOCK
    ends_before = jnp.sum((pends[None, :] <= block_start[:, None]).astype(i32), axis=1)
    block_eid = jnp.minimum(ends_before, N_EXPERTS - 1).astype(i32)
    n_used = (pends[-1:] // ROW_BLOCK).astype(i32)
    return unit_map, n_units, tail_block, block_eid, n_used


def _layer(x2d, p2d, batch, seq, w_in, w_moba_up, w_ret_up, w_out, ln1_g, ln1_b, w_router, router_bias,
           w_eg, w_eu, w_ed, w_sg, w_su, w_sd, ln2_g, ln2_b, w_ple_proj, w_ple_gate, ln3_g, ln3_b):
    tokens = batch * seq
    row = lambda a: a.reshape(1, -1).astype(f32)

    proj = _project(x2d, w_in.astype(bf16), seq)
    oa = _moba(proj, batch, seq)
    orr = _retention(proj, batch, seq)

    wrt = w_router.astype(f32).T
    wrt_hi = wrt.astype(bf16)
    wrt_lo = (wrt - wrt_hi.astype(f32)).astype(bf16)
    h1, pos8, w8, table, totals = _merge_router(
        oa, orr, proj, x2d, w_moba_up.astype(bf16), w_ret_up.astype(bf16), w_out.astype(bf16),
        row(ln1_g), row(ln1_b), wrt_hi, wrt_lo, router_bias.astype(f32).reshape(N_EXPERTS, 1))

    n_tiles = tokens // TOK_TM
    max_rows = tokens * TOP_K + n_tiles * N_EXPERTS * (UNIT - 1) + N_EXPERTS * (ROW_BLOCK - 1)
    n_blocks = -(-max_rows // ROW_BLOCK)
    unit_map, n_units, tail_block, block_eid, n_used = _row_layout(table[:, :, :3], totals[:, 0], n_blocks)

    rows = _dispatch(unit_map, n_units, tail_block, n_used, h1, pos8, n_blocks * ROW_BLOCK)
    y_rows = _experts(block_eid, n_used, rows, w_eg, w_eu, w_ed)
    return _combine(unit_map, n_units, y_rows, h1, pos8.T, w8.T, p2d,
                    w_sg.astype(bf16), w_su.astype(bf16), w_sd.astype(bf16),
                    w_ple_proj.astype(bf16), w_ple_gate.astype(bf16),
                    row(ln2_g), row(ln2_b), row(ln3_g), row(ln3_b))


def kernel(x, p, w_in, w_moba_up, w_ret_up, w_out, ln1_g, ln1_b, w_router, router_bias, w_exp_gate, w_exp_up,
           w_exp_down, w_sh_gate, w_sh_up, w_sh_down, ln2_g, ln2_b, w_ple_proj, w_ple_gate, ln3_g, ln3_b):
    batch, seq, d = x.shape
    assert d == D_MODEL and seq % max(MOBA_BLOCK, RET_CHUNK) == 0 and (batch * seq) % TOK_TM == 0
    assert w_in.shape[0] == DEPTH
    h = x.reshape(batch * seq, d)
    for i in range(DEPTH):
        h = _layer(h, p[i].reshape(batch * seq, PLE_DIM), batch, seq,
                   w_in[i], w_moba_up[i], w_ret_up[i], w_out[i], ln1_g[i], ln1_b[i], w_router[i], router_bias[i],
                   w_exp_gate[i], w_exp_up[i], w_exp_down[i], w_sh_gate[i], w_sh_up[i], w_sh_down[i],
                   ln2_g[i], ln2_b[i], w_ple_proj[i], w_ple_gate[i], ln3_g[i], ln3_b[i])
    return h.reshape(batch, seq, d)
```

```python
import functools

import jax
import jax.numpy as jnp
from jax import lax
from jax.experimental import pallas as pl
from jax.experimental.pallas import tpu as pltpu

f32 = jnp.float32
bf16 = jnp.bfloat16
i32 = jnp.int32

D_MODEL = 1024
DEPTH = 1
MOBA_HEADS = 8
MOBA_HEAD_DIM = 64
MOBA_WIDTH = MOBA_HEADS * MOBA_HEAD_DIM
MOBA_BLOCK = 256
MOBA_TOPK = 3
ROPE_THETA = 10000.0
RET_HEADS = 4
RET_QK_DIM = 128
RET_V_DIM = 256
RET_QK_WIDTH = RET_HEADS * RET_QK_DIM
RET_V_WIDTH = RET_HEADS * RET_V_DIM
RET_CHUNK = 256
RET_ANGLE_BASE = 10000.0
N_IN = 3 * MOBA_WIDTH + 2 * RET_QK_WIDTH + 2 * RET_V_WIDTH + 2 * D_MODEL
N_EXPERTS = 64
TOP_K = 6
N_GROUPS = 8
TOPK_GROUPS = 4
GROUP_SIZE = N_EXPERTS // N_GROUPS
D_EXPERT = 256
D_SHARED = 256
ROUTED_SCALE = 2.5
PLE_DIM = 256
LN_EPS = 1e-5
GN_EPS = 1e-6
ALPHA = (2.0 * DEPTH) ** 0.25

OFF_QA = 0
OFF_KA = MOBA_WIDTH
OFF_VA = 2 * MOBA_WIDTH
OFF_QR = 3 * MOBA_WIDTH
OFF_KR = OFF_QR + RET_QK_WIDTH
OFF_VR = OFF_KR + RET_QK_WIDTH
OFF_GR = OFF_VR + RET_V_WIDTH
OFF_GA = OFF_GR + RET_V_WIDTH
OFF_GB = OFF_GA + D_MODEL

LANES = 128
NEG = -1e30
VMEM_LIMIT = 56 * 1024 * 1024

PROJ_TM = 1024
PROJ_TN = 512
TOK_TM = 256
ROW_BLOCK = 512
EXPERT_CHUNK = 256
UNIT = 8
LOC_ROWS = -(-(TOP_K * TOK_TM + N_EXPERTS * (UNIT - 1)) // LANES) * LANES
LOC_UNITS = LOC_ROWS // UNIT
HALF = D_MODEL // 2


def _dot(a, b):
    return jnp.dot(a, b, preferred_element_type=f32)


def _dot_nt(a, b):
    return lax.dot_general(a, b, (((1,), (1,)), ((), ())), preferred_element_type=f32)


def _dot_tn(a, b):
    return lax.dot_general(a, b, (((0,), (0,)), ((), ())), preferred_element_type=f32)


def _split_bf16(a):
    hi = a.astype(bf16)
    lo = (a - hi.astype(f32)).astype(bf16)
    return hi, lo


def _layer_norm(x, g, b):
    mu = jnp.mean(x, axis=-1, keepdims=True)
    xc = x - mu
    var = jnp.mean(xc * xc, axis=-1, keepdims=True)
    return xc * lax.rsqrt(var + LN_EPS) * g + b


def _sigmoid(x):
    return 1.0 / (1.0 + jnp.exp(-x))


def _params(*sem, flags=None):
    return pltpu.CompilerParams(dimension_semantics=sem, vmem_limit_bytes=VMEM_LIMIT, flags=flags)


def _rotate_half_chunk(xc, half):
    if 2 * half == LANES:
        return pltpu.roll(xc, half, axis=1)
    lane = lax.broadcasted_iota(i32, xc.shape, 1)
    first = (lane & (2 * half - 1)) < half
    return jnp.where(first, pltpu.roll(xc, LANES - half, axis=1), pltpu.roll(xc, half, axis=1))


def _proj_kernel(x_ref, w_ref, cos_a_ref, sin_a_ref, cos_r_ref, sin_r_ref, o_ref, xb_ref):
    j = pl.program_id(1)

    @pl.when(j == 0)
    def _():
        xb_ref[...] = x_ref[...].astype(bf16)

    acc = _dot(xb_ref[...], w_ref[...])

    def rotary(cos_ref, sin_ref, half, scale):
        cos = cos_ref[...]
        sin = sin_ref[...]
        for c in range(PROJ_TN // LANES):
            xc = acc[:, c * LANES:(c + 1) * LANES]
            y = xc * cos + _rotate_half_chunk(xc, half) * sin
            if scale != 1.0:
                y = y * scale
            o_ref[:, c * LANES:(c + 1) * LANES] = y.astype(o_ref.dtype)

    j_qa = OFF_QA // PROJ_TN
    j_ka = OFF_KA // PROJ_TN
    j_qr = OFF_QR // PROJ_TN
    j_kr = OFF_KR // PROJ_TN

    @pl.when(j == j_qa)
    def _():
        rotary(cos_a_ref, sin_a_ref, MOBA_HEAD_DIM // 2, MOBA_HEAD_DIM ** -0.5)

    @pl.when(j == j_ka)
    def _():
        rotary(cos_a_ref, sin_a_ref, MOBA_HEAD_DIM // 2, 1.0)

    @pl.when(j == j_qr)
    def _():
        rotary(cos_r_ref, sin_r_ref, RET_QK_DIM // 2, 1.0)

    @pl.when(j == j_kr)
    def _():
        rotary(cos_r_ref, sin_r_ref, RET_QK_DIM // 2, RET_QK_DIM ** -0.5)

    plain = (j != j_qa) & (j != j_ka) & (j != j_qr) & (j != j_kr)

    @pl.when(plain)
    def _():
        o_ref[...] = acc.astype(o_ref.dtype)


def _rotary_tables(seq, inv_freq, head_dim):
    ang = jnp.arange(seq, dtype=f32)[:, None] * inv_freq[None, :]
    cos = jnp.cos(ang)
    sin = jnp.sin(ang)
    cos_h = jnp.concatenate([cos, cos], axis=-1)
    sin_h = jnp.concatenate([-sin, sin], axis=-1)
    reps = LANES // head_dim
    return jnp.tile(cos_h, (1, reps)), jnp.tile(sin_h, (1, reps))


def _project(x2d, w_in_b, seq):
    tokens = x2d.shape[0]
    tm = min(PROJ_TM, seq)
    inv_a = 1.0 / (ROPE_THETA ** (jnp.arange(0, MOBA_HEAD_DIM, 2, dtype=f32) / MOBA_HEAD_DIM))
    inv_r = 1.0 / (RET_ANGLE_BASE ** jnp.linspace(0.0, 1.0, RET_QK_DIM // 2, dtype=f32))
    cos_a, sin_a = _rotary_tables(seq, inv_a, MOBA_HEAD_DIM)
    cos_r, sin_r = _rotary_tables(seq, inv_r, RET_QK_DIM)
    seq_tiles = seq // tm
    tab = pl.BlockSpec((tm, LANES), lambda i, j: (i % seq_tiles, 0))
    return pl.pallas_call(
        _proj_kernel,
        grid=(tokens // tm, N_IN // PROJ_TN),
        in_specs=[
            pl.BlockSpec((tm, D_MODEL), lambda i, j: (i, 0)),
            pl.BlockSpec((D_MODEL, PROJ_TN), lambda i, j: (0, j)),
            tab, tab, tab, tab,
        ],
        out_specs=pl.BlockSpec((tm, PROJ_TN), lambda i, j: (i, j)),
        out_shape=jax.ShapeDtypeStruct((tokens, N_IN), bf16),
        scratch_shapes=[pltpu.VMEM((tm, D_MODEL), bf16)],
        compiler_params=_params("parallel", "arbitrary"),
        name="in_proj_rotary",
    )(x2d, w_in_b, cos_a, sin_a, cos_r, sin_r)


def _moba_kernel(q_ref, k_ref, v_ref, o_ref, kmean_ref, vt_ref, selb_ref, gate_ref, *, n_blocks):
    i = pl.program_id(2)
    blk = MOBA_BLOCK
    hd = MOBA_HEAD_DIM
    heads = LANES // hd

    @pl.when(i == 0)
    def _():
        seq = n_blocks * blk
        member = (lax.broadcasted_iota(i32, (n_blocks, seq), 1) // blk
                  == lax.broadcasted_iota(i32, (n_blocks, seq), 0))
        kmean_ref[...] = _dot(member.astype(bf16), k_ref[...]) * (1.0 / blk)
        for j in range(n_blocks):
            vt_ref[j] = v_ref[j * blk:(j + 1) * blk, :].astype(f32).T.astype(bf16)

    km_hi, km_lo = _split_bf16(kmean_ref[...])
    key_i = lax.broadcasted_iota(i32, (blk, blk), 0)
    qry_i = lax.broadcasted_iota(i32, (blk, blk), 1)
    causal = key_i <= qry_i
    bid = lax.broadcasted_iota(i32, (n_blocks, blk), 0)
    d_row = lax.broadcasted_iota(i32, (LANES, blk), 0)
    qt = q_ref[...].astype(f32).T

    qts = []
    for h in range(heads):
        in_head = (d_row >= h * hd) & (d_row < (h + 1) * hd)
        qth = jnp.where(in_head, qt, 0.0).astype(bf16)
        qts.append(qth)
        gate = _dot(km_hi, qth) + _dot(km_lo, qth)
        g = jnp.where(bid < i, gate, -jnp.inf)
        gate_ref[...] = g
        beaten = jnp.zeros((n_blocks, blk), f32)
        for j in range(n_blocks):
            gj = jnp.broadcast_to(gate_ref[j:j + 1, :], (n_blocks, blk))
            beats = (gj > g) | ((gj == g) & (bid > j))
            beaten = beaten + jnp.where(beats, 1.0, 0.0)
        sel = (beaten < MOBA_TOPK) & (bid < i)
        selb_ref[h] = jnp.where(sel, 0.0, NEG)

    kj = k_ref[pl.ds(pl.multiple_of(i * blk, blk), blk), :]
    vtj = vt_ref[i]
    state = []
    for h in range(heads):
        s = jnp.where(causal, _dot(kj, qts[h]), NEG)
        m0 = jnp.max(s, axis=0, keepdims=True)
        p = jnp.exp(s - m0)
        l0 = jnp.sum(p, axis=0, keepdims=True)
        acc0 = _dot(vtj[h * hd:(h + 1) * hd, :], p.astype(bf16))
        state += [m0, l0, acc0]

    qt_all = jnp.concatenate(qts, axis=1)

    def update(carry, js):
        nj = len(js)
        k_all = k_ref[pl.ds(pl.multiple_of(js[0] * blk, blk), nj * blk), :]
        s_all = _dot(k_all, qt_all)
        vtjs = [vt_ref[j] for j in js]
        new = []
        for h in range(heads):
            m_old, l_old, acc = carry[3 * h:3 * h + 3]
            ss = [s_all[n * blk:(n + 1) * blk, h * blk:(h + 1) * blk] + selb_ref[h, pl.ds(j, 1), :]
                  for n, j in enumerate(js)]
            m_new = m_old
            for s in ss:
                m_new = jnp.maximum(m_new, jnp.max(s, axis=0, keepdims=True))
            a = jnp.exp(m_old - m_new)
            l_new = a * l_old
            acc = a * acc
            for s, vtj in zip(ss, vtjs):
                p = jnp.exp(s - m_new)
                l_new = l_new + jnp.sum(p, axis=0, keepdims=True)
                acc = acc + _dot(vtj[h * hd:(h + 1) * hd, :], p.astype(bf16))
            new += [m_new, l_new, acc]
        return tuple(new)

    quads = i // 4
    fin = lax.fori_loop(0, quads, lambda n, c: update(c, tuple(4 * n + t for t in range(4))), tuple(state))
    done = 4 * quads
    has_pair = (i - done) // 2
    fin = lax.fori_loop(0, has_pair, lambda n, c: update(c, (done, done + 1)), fin)
    done = done + 2 * has_pair
    fin = lax.fori_loop(done, i, lambda j, c: update(c, (j,)), fin)
    out_t = jnp.concatenate([fin[3 * h + 2] / fin[3 * h + 1] for h in range(heads)], axis=0)
    o_ref[...] = out_t.T.astype(o_ref.dtype)


def _moba(proj, batch, seq):
    tokens = batch * seq
    n_blocks = seq // MOBA_BLOCK
    chunks = MOBA_WIDTH // LANES
    return pl.pallas_call(
        functools.partial(_moba_kernel, n_blocks=n_blocks),
        grid=(batch, chunks, n_blocks),
        in_specs=[
            pl.BlockSpec((MOBA_BLOCK, LANES), lambda b, c, i: (b * n_blocks + i, OFF_QA // LANES + c)),
            pl.BlockSpec((seq, LANES), lambda b, c, i: (b, OFF_KA // LANES + c)),
            pl.BlockSpec((seq, LANES), lambda b, c, i: (b, OFF_VA // LANES + c)),
        ],
        out_specs=pl.BlockSpec((MOBA_BLOCK, LANES), lambda b, c, i: (b * n_blocks + i, c)),
        out_shape=jax.ShapeDtypeStruct((tokens, MOBA_WIDTH), bf16),
        scratch_shapes=[
            pltpu.VMEM((n_blocks, LANES), f32),
            pltpu.VMEM((n_blocks, LANES, MOBA_BLOCK), bf16),
            pltpu.VMEM((LANES // MOBA_HEAD_DIM, n_blocks, MOBA_BLOCK), f32),
            pltpu.VMEM((n_blocks, MOBA_BLOCK), f32),
        ],
        compiler_params=_params("parallel", "parallel", "arbitrary"),
        name="moba_attention",
    )(proj, proj, proj)


def _ret_kernel(q_ref, k_ref, v_ref, g_ref, decay_ref, zeta_ref, xi_ref, cd_ref, o_ref, state_ref):
    c = pl.program_id(2)

    @pl.when(c == 0)
    def _():
        state_ref[...] = jnp.zeros_like(state_ref)

    q = q_ref[...]
    k = k_ref[...]
    v = v_ref[...]
    state = state_ref[...]
    scores = _dot_nt(q, k) * decay_ref[0]
    inner = _dot(scores.astype(bf16), v)
    cross = _dot(q, state.astype(bf16)) * xi_ref[0]
    o = inner + cross
    kz = (k.astype(f32) * zeta_ref[0]).astype(bf16)
    state_ref[...] = cd_ref[0] * state + _dot_tn(kz, v)

    mu = jnp.mean(o, axis=-1, keepdims=True)
    oc = o - mu
    var = jnp.mean(oc * oc, axis=-1, keepdims=True)
    on = oc * lax.rsqrt(var + GN_EPS)
    g = g_ref[...].astype(f32)
    o_ref[...] = (on * (g * _sigmoid(g))).astype(o_ref.dtype)


def _retention(proj, batch, seq):
    tokens = batch * seq
    C = RET_CHUNK
    n_chunks = seq // C
    gammas = 1.0 - 2.0 ** (-5.0 - jnp.arange(RET_HEADS, dtype=f32))
    log_g = jnp.log(gammas)
    idx = jnp.arange(C, dtype=f32)
    diff = idx[:, None] - idx[None, :]
    decay = jnp.where(diff >= 0, jnp.exp(jnp.maximum(diff, 0.0)[None] * log_g[:, None, None]), 0.0)
    zeta = jnp.exp((C - 1 - idx)[None, :] * log_g[:, None])
    xi = jnp.exp((idx + 1.0)[None, :] * log_g[:, None])
    zeta_t = jnp.broadcast_to(zeta[:, :, None], (RET_HEADS, C, RET_QK_DIM))
    xi_t = jnp.broadcast_to(xi[:, :, None], (RET_HEADS, C, RET_V_DIM))
    cd_t = jnp.broadcast_to(jnp.exp(C * log_g)[:, None, None], (RET_HEADS, 1, RET_V_DIM))
    return pl.pallas_call(
        _ret_kernel,
        grid=(batch, RET_HEADS, n_chunks),
        in_specs=[
            pl.BlockSpec((C, RET_QK_DIM), lambda b, h, c: (b * n_chunks + c, OFF_QR // RET_QK_DIM + h)),
            pl.BlockSpec((C, RET_QK_DIM), lambda b, h, c: (b * n_chunks + c, OFF_KR // RET_QK_DIM + h)),
            pl.BlockSpec((C, RET_V_DIM), lambda b, h, c: (b * n_chunks + c, OFF_VR // RET_V_DIM + h)),
            pl.BlockSpec((C, RET_V_DIM), lambda b, h, c: (b * n_chunks + c, OFF_GR // RET_V_DIM + h)),
            pl.BlockSpec((1, C, C), lambda b, h, c: (h, 0, 0)),
            pl.BlockSpec((1, C, RET_QK_DIM), lambda b, h, c: (h, 0, 0)),
            pl.BlockSpec((1, C, RET_V_DIM), lambda b, h, c: (h, 0, 0)),
            pl.BlockSpec((1, 1, RET_V_DIM), lambda b, h, c: (h, 0, 0)),
        ],
        out_specs=pl.BlockSpec((C, RET_V_DIM), lambda b, h, c: (b * n_chunks + c, h)),
        out_shape=jax.ShapeDtypeStruct((tokens, RET_V_WIDTH), bf16),
        scratch_shapes=[pltpu.VMEM((RET_QK_DIM, RET_V_DIM), f32)],
        compiler_params=_params("parallel", "parallel", "arbitrary"),
        name="retention",
    )(proj, proj, proj, proj, decay, zeta_t, xi_t, cd_t)


def _merge_router_kernel(oa_ref, orr_ref, ga0_ref, ga1_ref, gb0_ref, gb1_ref, x_ref,
                         wm_ref, wr_ref, wo_ref, g1_ref, b1_ref, wrt_hi_ref, wrt_lo_ref, rb_ref,
                         h_ref, pos_ref, w_ref, tab_ref, cnt_ref, carry_ref):
    step = pl.program_id(0)
    tm = x_ref.shape[0]

    @pl.when(step == 0)
    def _():
        carry_ref[...] = jnp.zeros_like(carry_ref)

    a = _dot(oa_ref[...], wm_ref[...])
    r = _dot(orr_ref[...], wr_ref[...])
    ga = jnp.concatenate([ga0_ref[...], ga1_ref[...]], axis=1).astype(f32)
    gb = jnp.concatenate([gb0_ref[...], gb1_ref[...]], axis=1).astype(f32)
    merged = _sigmoid(ga) * a + _sigmoid(gb) * r
    mix = _dot(merged.astype(bf16), wo_ref[...])
    h = _layer_norm(ALPHA * x_ref[...] + mix, g1_ref[...], b1_ref[...])
    h_ref[...] = h

    h_hi, h_lo = _split_bf16(h)
    w_hi = wrt_hi_ref[...]
    logits = _dot_nt(w_hi, h_hi) + _dot_nt(w_hi, h_lo) + _dot_nt(wrt_lo_ref[...], h_hi)
    scores = _sigmoid(logits)
    biased = scores + rb_ref[...]

    v = biased.reshape(N_GROUPS, GROUP_SIZE, tm)
    sub = lax.broadcasted_iota(i32, v.shape, 1)
    m1 = jnp.max(v, axis=1, keepdims=True)
    i1 = jnp.min(jnp.where(v == m1, sub, GROUP_SIZE), axis=1, keepdims=True)
    m2 = jnp.max(jnp.where(sub == i1, -jnp.inf, v), axis=1, keepdims=True)
    gscore = jnp.broadcast_to(m1 + m2, v.shape).reshape(N_EXPERTS, tm)

    eid = lax.broadcasted_iota(i32, (N_EXPERTS, tm), 0)
    egrp = eid // GROUP_SIZE
    e_mask = jnp.zeros((N_EXPERTS, tm), jnp.bool_)
    for _ in range(TOPK_GROUPS):
        m = jnp.max(gscore, axis=0, keepdims=True)
        idx = jnp.min(jnp.where(gscore == m, egrp, N_GROUPS), axis=0, keepdims=True)
        hit = egrp == idx
        e_mask = e_mask | hit
        gscore = jnp.where(hit, -jnp.inf, gscore)
    cand = jnp.where(e_mask, biased, -jnp.inf)

    chosen = jnp.zeros((N_EXPERTS, tm), jnp.bool_)
    e_rows = []
    w_rows = []
    for _ in range(TOP_K):
        m = jnp.max(cand, axis=0, keepdims=True)
        idx = jnp.min(jnp.where(cand == m, eid, N_EXPERTS), axis=0, keepdims=True)
        hit = eid == idx
        chosen = chosen | hit
        e_rows.append(idx)
        w_rows.append(jnp.sum(jnp.where(hit, scores, 0.0), axis=0, keepdims=True))
        cand = jnp.where(hit, -jnp.inf, cand)
    w_sum = w_rows[0]
    for wk in w_rows[1:]:
        w_sum = w_sum + wk

    t_src = lax.broadcasted_iota(i32, (tm, tm), 0)
    t_dst = lax.broadcasted_iota(i32, (tm, tm), 1)
    before = (t_src < t_dst).astype(bf16)
    chosen_f = chosen.astype(f32)
    prior = _dot(chosen_f.astype(bf16), before)
    cnt = jnp.sum(chosen_f, axis=1, keepdims=True)
    cnt_pad = jnp.ceil(cnt * (1.0 / UNIT)) * UNIT
    e_src = lax.broadcasted_iota(i32, (N_EXPERTS, N_EXPERTS), 1)
    e_dst = lax.broadcasted_iota(i32, (N_EXPERTS, N_EXPERTS), 0)
    earlier = (e_src < e_dst).astype(bf16)
    cnt_pad_l = jnp.broadcast_to(cnt_pad, (N_EXPERTS, LANES))
    loc_start_l = _dot(earlier, cnt_pad_l.astype(bf16))
    loc_start = loc_start_l[:, 0:1]
    where_e = prior + loc_start
    pos_ref[...] = jnp.zeros_like(pos_ref)
    w_ref[...] = jnp.zeros_like(w_ref)
    for k in range(TOP_K):
        pos_ref[k:k + 1, :] = jnp.sum(jnp.where(eid == e_rows[k], where_e, 0.0), axis=0, keepdims=True).astype(i32)
        w_ref[k:k + 1, :] = w_rows[k] / w_sum * ROUTED_SCALE

    carry = carry_ref[...]
    tl = lax.broadcasted_iota(i32, (N_EXPERTS, LANES), 1)
    table = jnp.where(tl == 0, cnt_pad_l, jnp.where(tl == 1, loc_start_l, jnp.broadcast_to(carry, (N_EXPERTS, LANES))))
    tab_ref[0] = table.astype(i32)
    carry = carry + cnt_pad
    carry_ref[...] = carry
    cnt_ref[...] = jnp.broadcast_to(carry, cnt_ref.shape).astype(i32)


def _merge_router(oa, orr, proj, x2d, wm, wr, wo, g1, b1, wrt_hi, wrt_lo, rb):
    tokens = x2d.shape[0]
    tm = TOK_TM
    half = D_MODEL // 2

    def gate_spec(off):
        return pl.BlockSpec((tm, half), lambda i: (i, off // half))

    def full(shape):
        return pl.BlockSpec(shape, lambda i: (0,) * len(shape))

    row8 = pl.BlockSpec((8, tm), lambda i: (0, i))
    return pl.pallas_call(
        _merge_router_kernel,
        grid=(tokens // tm,),
        in_specs=[
            pl.BlockSpec((tm, MOBA_WIDTH), lambda i: (i, 0)),
            pl.BlockSpec((tm, RET_V_WIDTH), lambda i: (i, 0)),
            gate_spec(OFF_GA), gate_spec(OFF_GA + half), gate_spec(OFF_GB), gate_spec(OFF_GB + half),
            pl.BlockSpec((tm, D_MODEL), lambda i: (i, 0)),
            full((MOBA_WIDTH, D_MODEL)), full((RET_V_WIDTH, D_MODEL)), full((D_MODEL, D_MODEL)),
            full((1, D_MODEL)), full((1, D_MODEL)),
            full((N_EXPERTS, D_MODEL)), full((N_EXPERTS, D_MODEL)), full((N_EXPERTS, 1)),
        ],
        out_specs=[
            pl.BlockSpec((tm, D_MODEL), lambda i: (i, 0)),
            row8, row8,
            pl.BlockSpec((1, N_EXPERTS, LANES), lambda i: (i, 0, 0)),
            full((N_EXPERTS, LANES)),
        ],
        out_shape=[
            jax.ShapeDtypeStruct((tokens, D_MODEL), f32),
            jax.ShapeDtypeStruct((8, tokens), i32),
            jax.ShapeDtypeStruct((8, tokens), f32),
            jax.ShapeDtypeStruct((tokens // tm, N_EXPERTS, LANES), i32),
            jax.ShapeDtypeStruct((N_EXPERTS, LANES), i32),
        ],
        scratch_shapes=[pltpu.VMEM((N_EXPERTS, 1), f32)],
        compiler_params=_params("arbitrary"),
        name="merge_ln1_router",
    )(oa, orr, proj, proj, proj, proj, x2d, wm, wr, wo, g1, b1, wrt_hi, wrt_lo, rb)


def _pack_rows(x):
    bits = lax.bitcast_convert_type(x, jnp.uint32)
    return (bits[:, :HALF] & jnp.uint32(0xFFFF0000)) | (bits[:, HALF:] >> 16)


def _unpack_rows(words):
    left = lax.bitcast_convert_type(words & jnp.uint32(0xFFFF0000), f32)
    right = lax.bitcast_convert_type(words << 16, f32)
    return left.astype(bf16), right.astype(bf16)


def _unit_slices(local_unit, global_unit):
    loc = pl.ds(pl.multiple_of(local_unit * UNIT, UNIT), UNIT)
    glob = pl.ds(pl.multiple_of(global_unit * UNIT, UNIT), UNIT)
    return loc, glob


def _dispatch_kernel(gu_ref, nu_ref, tail_ref, used_ref, h_ref, pos_ref, rows_ref,
                     xs_ref, zero_ref, sem_ref, zsem_ref):
    step = pl.program_id(0)
    tm = h_ref.shape[0]

    @pl.when(step == 0)
    def _():
        zero_ref[...] = jnp.zeros_like(zero_ref)

        def clear_block(block):
            start = pl.multiple_of(block * ROW_BLOCK, ROW_BLOCK)
            cp = pltpu.make_async_copy(zero_ref, rows_ref.at[pl.ds(start, ROW_BLOCK), :], zsem_ref)
            cp.start()
            cp.wait()

        def clear_tail(e, _):
            tail_block = tail_ref[e]

            @pl.when(tail_block >= 0)
            def _():
                clear_block(tail_block)
            return 0

        lax.fori_loop(0, N_EXPERTS, clear_tail, 0)

        def clear_unused(b, _):
            clear_block(b)
            return 0

        lax.fori_loop(used_ref[0], rows_ref.shape[0] // ROW_BLOCK, clear_unused, 0)

    def unit_copy(t, s):
        loc, glob = _unit_slices(s, gu_ref[t * LOC_UNITS + s])
        return pltpu.make_async_copy(xs_ref.at[t % 2, loc, :], rows_ref.at[glob, :], sem_ref.at[t % 2])

    def drain(t):
        def wait_one(s, _):
            unit_copy(t, s).wait()
            return 0

        lax.fori_loop(0, nu_ref[t], wait_one, 0)

    @pl.when(step >= 2)
    def _():
        drain(step - 2)

    pos = pos_ref[...]
    r_iota = lax.broadcasted_iota(i32, (LOC_ROWS, tm), 0)
    onehot = r_iota == pos[0:1, :]
    for k in range(1, TOP_K):
        onehot = onehot | (r_iota == pos[k:k + 1, :])
    xs = _dot(onehot.astype(bf16), h_ref[...].astype(bf16))
    xs_ref[step % 2] = _pack_rows(xs)

    def issue(s, _):
        unit_copy(step, s).start()
        return 0

    lax.fori_loop(0, nu_ref[step], issue, 0)

    @pl.when(step == pl.num_programs(0) - 1)
    def _():
        @pl.when(step >= 1)
        def _():
            drain(step - 1)

        drain(step)


def _dispatch(unit_map, n_units, tail_block, n_used, h, pos8, n_rows):
    tokens = h.shape[0]
    tm = TOK_TM
    grid_spec = pltpu.PrefetchScalarGridSpec(
        num_scalar_prefetch=4,
        grid=(tokens // tm,),
        in_specs=[
            pl.BlockSpec((tm, D_MODEL), lambda i, *_: (i, 0)),
            pl.BlockSpec((8, tm), lambda i, *_: (0, i)),
        ],
        out_specs=pl.BlockSpec(memory_space=pl.ANY),
        scratch_shapes=[
            pltpu.VMEM((2, LOC_ROWS, HALF), jnp.uint32),
            pltpu.VMEM((ROW_BLOCK, HALF), jnp.uint32),
            pltpu.SemaphoreType.DMA((2,)),
            pltpu.SemaphoreType.DMA,
        ],
    )
    return pl.pallas_call(
        _dispatch_kernel,
        grid_spec=grid_spec,
        out_shape=jax.ShapeDtypeStruct((n_rows, HALF), jnp.uint32),
        compiler_params=_params("arbitrary"),
        name="moe_dispatch",
    )(unit_map, n_units, tail_block, n_used, h, pos8)


def _expert_kernel(eid_ref, used_ref, x_ref, wg_ref, wu_ref, wd_ref, y_ref, wgb_ref, wub_ref, wdb_ref):
    b = pl.program_id(0)
    changed = (b == 0) | (eid_ref[b] != eid_ref[jnp.maximum(b - 1, 0)])

    @pl.when(changed)
    def _():
        wgb_ref[...] = wg_ref[0].astype(bf16)
        wub_ref[...] = wu_ref[0].astype(bf16)
        wdb_ref[...] = wd_ref[0].astype(bf16)

    @pl.when(b < used_ref[0])
    def _():
        for c in range(ROW_BLOCK // EXPERT_CHUNK):
            rows = slice(c * EXPERT_CHUNK, (c + 1) * EXPERT_CHUNK)
            xl, xr = _unpack_rows(x_ref[rows, :])
            g = _dot(xl, wgb_ref[:HALF, :]) + _dot(xr, wgb_ref[HALF:, :])
            u = _dot(xl, wub_ref[:HALF, :]) + _dot(xr, wub_ref[HALF:, :])
            mid = (g * _sigmoid(g)) * u
            y = _dot(mid.astype(bf16), wdb_ref[...])
            y_ref[rows, :] = _pack_rows(y.astype(bf16).astype(f32))

    @pl.when(b >= used_ref[0])
    def _():
        y_ref[...] = jnp.zeros_like(y_ref)


def _experts(block_eid, n_used, rows, wg, wu, wd):
    n_rows = rows.shape[0]
    n_blocks = n_rows // ROW_BLOCK

    def row_map(b, eid, used):
        return (b, 0)

    def w_map(b, eid, used):
        return (eid[b], 0, 0)

    grid_spec = pltpu.PrefetchScalarGridSpec(
        num_scalar_prefetch=2,
        grid=(n_blocks,),
        in_specs=[
            pl.BlockSpec((ROW_BLOCK, HALF), row_map),
            pl.BlockSpec((1, D_MODEL, D_EXPERT), w_map),
            pl.BlockSpec((1, D_MODEL, D_EXPERT), w_map),
            pl.BlockSpec((1, D_EXPERT, D_MODEL), w_map),
        ],
        out_specs=pl.BlockSpec((ROW_BLOCK, HALF), row_map),
        scratch_shapes=[
            pltpu.VMEM((D_MODEL, D_EXPERT), bf16),
            pltpu.VMEM((D_MODEL, D_EXPERT), bf16),
            pltpu.VMEM((D_EXPERT, D_MODEL), bf16),
        ],
    )
    return pl.pallas_call(
        _expert_kernel,
        grid_spec=grid_spec,
        out_shape=jax.ShapeDtypeStruct((n_rows, HALF), jnp.uint32),
        compiler_params=_params("arbitrary"),
        name="moe_experts",
    )(block_eid, n_used, rows, wg, wu, wd)


def _combine_kernel(gu_ref, nu_ref, y_ref, h_ref, pos_ref, wt_ref, p_ref, wsg_ref, wsu_ref, wsd_ref, wpp_ref,
                    wpg_ref, g2_ref, b2_ref, g3_ref, b3_ref, o_ref, ybuf_ref, sem_ref):
    step = pl.program_id(0)
    tm = h_ref.shape[0]

    def unit_copy(t, s):
        loc, glob = _unit_slices(s, gu_ref[t * LOC_UNITS + s])
        return pltpu.make_async_copy(y_ref.at[glob, :], ybuf_ref.at[t % 2, loc, :], sem_ref.at[t % 2])

    def fetch(t):
        def start_one(s, _):
            unit_copy(t, s).start()
            return 0

        lax.fori_loop(0, nu_ref[t], start_one, 0)

    @pl.when(step == 0)
    def _():
        ybuf_ref[...] = jnp.zeros_like(ybuf_ref)
        fetch(step)

    @pl.when(step + 1 < pl.num_programs(0))
    def _():
        fetch(step + 1)

    h = h_ref[...]
    hb = h.astype(bf16)
    sg = _dot(hb, wsg_ref[...])
    su = _dot(hb, wsu_ref[...])
    shared = _dot(((sg * _sigmoid(sg)) * su).astype(bf16), wsd_ref[...])
    ple_in = _dot(p_ref[...].astype(bf16), wpp_ref[...])

    pos = pos_ref[...]
    wt = wt_ref[...]
    c_iota = lax.broadcasted_iota(i32, (tm, LOC_ROWS), 1)
    spread = jnp.where(c_iota == pos[:, 0:1], wt[:, 0:1], 0.0)
    for k in range(1, TOP_K):
        spread = spread + jnp.where(c_iota == pos[:, k:k + 1], wt[:, k:k + 1], 0.0)
    spread = spread.astype(bf16)

    def wait_one(s, _):
        unit_copy(step, s).wait()
        return 0

    lax.fori_loop(0, nu_ref[step], wait_one, 0)

    yl, yr = _unpack_rows(ybuf_ref[step % 2])
    routed = jnp.concatenate([_dot(spread, yl), _dot(spread, yr)], axis=1)

    h2 = _layer_norm(ALPHA * h + (routed + shared), g2_ref[...], b2_ref[...])
    ple = ple_in * _sigmoid(_dot(h2.astype(bf16), wpg_ref[...]))
    o_ref[...] = _layer_norm(ALPHA * h2 + ple, g3_ref[...], b3_ref[...])


def _combine(unit_map, n_units, y_rows, h, pos_tok, w_tok, p2d, wsg, wsu, wsd, wpp, wpg, g2, b2, g3, b3):
    tokens = h.shape[0]
    tm = TOK_TM

    def full(shape):
        return pl.BlockSpec(shape, lambda i, *_: (0,) * len(shape))

    def tile(width):
        return pl.BlockSpec((tm, width), lambda i, *_: (i, 0))

    grid_spec = pltpu.PrefetchScalarGridSpec(
        num_scalar_prefetch=2,
        grid=(tokens // tm,),
        in_specs=[
            pl.BlockSpec(memory_space=pl.ANY),
            tile(D_MODEL), tile(8), tile(8), tile(PLE_DIM),
            full((D_MODEL, D_SHARED)), full((D_MODEL, D_SHARED)), full((D_SHARED, D_MODEL)),
            full((PLE_DIM, D_MODEL)), full((D_MODEL, D_MODEL)),
            full((1, D_MODEL)), full((1, D_MODEL)), full((1, D_MODEL)), full((1, D_MODEL)),
        ],
        out_specs=tile(D_MODEL),
        scratch_shapes=[
            pltpu.VMEM((2, LOC_ROWS, HALF), jnp.uint32),
            pltpu.SemaphoreType.DMA((2,)),
        ],
    )
    return pl.pallas_call(
        _combine_kernel,
        grid_spec=grid_spec,
        out_shape=jax.ShapeDtypeStruct((tokens, D_MODEL), f32),
        compiler_params=_params("arbitrary"),
        name="moe_combine_ln2_ple_ln3",
    )(unit_map, n_units, y_rows, h, pos_tok, w_tok, p2d, wsg, wsu, wsd, wpp, wpg, g2, b2, g3, b3)


def _row_layout(table, totals, n_blocks):
    cnt_u = table[:, :, 0] // UNIT
    loc_u = table[:, :, 1] // UNIT
    padded = (totals + ROW_BLOCK - 1) // ROW_BLOCK * ROW_BLOCK
    pends = jnp.cumsum(padded)
    pstarts = pends - padded
    glob_u = (pstarts[None, :] + table[:, :, 2]) // UNIT
    n_units = jnp.sum(cnt_u, axis=1).astype(i32)
    s = jnp.arange(LOC_UNITS, dtype=i32)
    loc_end = loc_u + cnt_u
    e_of = jnp.sum((loc_end[:, None, :] <= s[None, :, None]).astype(i32), axis=-1)
    is_e = e_of[:, :, None] == jnp.arange(N_EXPERTS, dtype=i32)
    shift = jnp.sum(jnp.where(is_e, (glob_u - loc_u)[:, None, :], 0), axis=-1)
    unit_map = (shift + s[None, :]).astype(i32).reshape(-1)
    tail_block = jnp.where(totals > 0, pends // ROW_BLOCK - 1, -1).astype(i32)
    block_start = jnp.arange(n_blocks, dtype=i32) * ROW_BLOCK
    ends_before = jnp.sum((pends[None, :] <= block_start[:, None]).astype(i32), axis=1)
    block_eid = jnp.minimum(ends_before, N_EXPERTS - 1).astype(i32)
    n_used = (pends[-1:] // ROW_BLOCK).astype(i32)
    return unit_map, n_units, tail_block, block_eid, n_used


def _layer(x2d, p2d, batch, seq, w_in, w_moba_up, w_ret_up, w_out, ln1_g, ln1_b, w_router, router_bias,
           w_eg, w_eu, w_ed, w_sg, w_su, w_sd, ln2_g, ln2_b, w_ple_proj, w_ple_gate, ln3_g, ln3_b):
    tokens = batch * seq
    row = lambda a: a.reshape(1, -1).astype(f32)

    proj = _project(x2d, w_in.astype(bf16), seq)
    oa = _moba(proj, batch, seq)
    orr = _retention(proj, batch, seq)

    wrt = w_router.astype(f32).T
    wrt_hi = wrt.astype(bf16)
    wrt_lo = (wrt - wrt_hi.astype(f32)).astype(bf16)
    h1, pos8, w8, table, totals = _merge_router(
        oa, orr, proj, x2d, w_moba_up.astype(bf16), w_ret_up.astype(bf16), w_out.astype(bf16),
        row(ln1_g), row(ln1_b), wrt_hi, wrt_lo, router_bias.astype(f32).reshape(N_EXPERTS, 1))

    n_tiles = tokens // TOK_TM
    max_rows = tokens * TOP_K + n_tiles * N_EXPERTS * (UNIT - 1) + N_EXPERTS * (ROW_BLOCK - 1)
    n_blocks = -(-max_rows // ROW_BLOCK)
    unit_map, n_units, tail_block, block_eid, n_used = _row_layout(table[:, :, :3], totals[:, 0], n_blocks)

    rows = _dispatch(unit_map, n_units, tail_block, n_used, h1, pos8, n_blocks * ROW_BLOCK)
    y_rows = _experts(block_eid, n_used, rows, w_eg, w_eu, w_ed)
    return _combine(unit_map, n_units, y_rows, h1, pos8.T, w8.T, p2d,
                    w_sg.astype(bf16), w_su.astype(bf16), w_sd.astype(bf16),
                    w_ple_proj.astype(bf16), w_ple_gate.astype(bf16),
                    row(ln2_g), row(ln2_b), row(ln3_g), row(ln3_b))


def kernel(x, p, w_in, w_moba_up, w_ret_up, w_out, ln1_g, ln1_b, w_router, router_bias, w_exp_gate, w_exp_up,
           w_exp_down, w_sh_gate, w_sh_up, w_sh_down, ln2_g, ln2_b, w_ple_proj, w_ple_gate, ln3_g, ln3_b):
    batch, seq, d = x.shape
    assert d == D_MODEL and seq % max(MOBA_BLOCK, RET_CHUNK) == 0 and (batch * seq) % TOK_TM == 0
    assert w_in.shape[0] == DEPTH
    h = x.reshape(batch * seq, d)
    for i in range(DEPTH):
        h = _layer(h, p[i].reshape(batch * seq, PLE_DIM), batch, seq,
                   w_in[i], w_moba_up[i], w_ret_up[i], w_out[i], ln1_g[i], ln1_b[i], w_router[i], router_bias[i],
                   w_exp_gate[i], w_exp_up[i], w_exp_down[i], w_sh_gate[i], w_sh_up[i], w_sh_down[i],
                   ln2_g[i], ln2_b[i], w_ple_proj[i], w_ple_gate[i], ln3_g[i], ln3_b[i])
    return h.reshape(batch, seq, d)
```

```python
import functools

import jax
import jax.numpy as jnp
from jax import lax
from jax.experimental import pallas as pl
from jax.experimental.pallas import tpu as pltpu

f32 = jnp.float32
bf16 = jnp.bfloat16
i32 = jnp.int32

D_MODEL = 1024
DEPTH = 1
MOBA_HEADS = 8
MOBA_HEAD_DIM = 64
MOBA_WIDTH = MOBA_HEADS * MOBA_HEAD_DIM
MOBA_BLOCK = 256
MOBA_TOPK = 3
ROPE_THETA = 10000.0
RET_HEADS = 4
RET_QK_DIM = 128
RET_V_DIM = 256
RET_QK_WIDTH = RET_HEADS * RET_QK_DIM
RET_V_WIDTH = RET_HEADS * RET_V_DIM
RET_CHUNK = 256
RET_ANGLE_BASE = 10000.0
N_IN = 3 * MOBA_WIDTH + 2 * RET_QK_WIDTH + 2 * RET_V_WIDTH + 2 * D_MODEL
N_EXPERTS = 64
TOP_K = 6
N_GROUPS = 8
TOPK_GROUPS = 4
GROUP_SIZE = N_EXPERTS // N_GROUPS
D_EXPERT = 256
D_SHARED = 256
ROUTED_SCALE = 2.5
PLE_DIM = 256
LN_EPS = 1e-5
GN_EPS = 1e-6
ALPHA = (2.0 * DEPTH) ** 0.25

OFF_QA = 0
OFF_KA = MOBA_WIDTH
OFF_VA = 2 * MOBA_WIDTH
OFF_QR = 3 * MOBA_WIDTH
OFF_KR = OFF_QR + RET_QK_WIDTH
OFF_VR = OFF_KR + RET_QK_WIDTH
OFF_GR = OFF_VR + RET_V_WIDTH
OFF_GA = OFF_GR + RET_V_WIDTH
OFF_GB = OFF_GA + D_MODEL

LANES = 128
NEG = -1e30
VMEM_LIMIT = 56 * 1024 * 1024

PROJ_TM = 1024
PROJ_TN = 512
TOK_TM = 256
ROW_BLOCK = 512
EXPERT_CHUNK = 256
UNIT = 8
LOC_ROWS = -(-(TOP_K * TOK_TM + N_EXPERTS * (UNIT - 1)) // LANES) * LANES
LOC_UNITS = LOC_ROWS // UNIT
HALF = D_MODEL // 2


def _dot(a, b):
    return jnp.dot(a, b, preferred_element_type=f32)


def _dot_nt(a, b):
    return lax.dot_general(a, b, (((1,), (1,)), ((), ())), preferred_element_type=f32)


def _dot_tn(a, b):
    return lax.dot_general(a, b, (((0,), (0,)), ((), ())), preferred_element_type=f32)


def _split_bf16(a):
    hi = a.astype(bf16)
    lo = (a - hi.astype(f32)).astype(bf16)
    return hi, lo


def _layer_norm(x, g, b):
    mu = jnp.mean(x, axis=-1, keepdims=True)
    xc = x - mu
    var = jnp.mean(xc * xc, axis=-1, keepdims=True)
    return xc * lax.rsqrt(var + LN_EPS) * g + b


def _sigmoid(x):
    return 1.0 / (1.0 + jnp.exp(-x))


def _params(*sem, flags=None):
    return pltpu.CompilerParams(dimension_semantics=sem, vmem_limit_bytes=VMEM_LIMIT, flags=flags)


def _rotate_half_chunk(xc, half):
    if 2 * half == LANES:
        return pltpu.roll(xc, half, axis=1)
    lane = lax.broadcasted_iota(i32, xc.shape, 1)
    first = (lane & (2 * half - 1)) < half
    return jnp.where(first, pltpu.roll(xc, LANES - half, axis=1), pltpu.roll(xc, half, axis=1))


def _proj_kernel(x_ref, w_ref, cos_a_ref, sin_a_ref, cos_r_ref, sin_r_ref, o_ref, xb_ref):
    j = pl.program_id(1)

    @pl.when(j == 0)
    def _():
        xb_ref[...] = x_ref[...].astype(bf16)

    acc = _dot(xb_ref[...], w_ref[...])

    def rotary(cos_ref, sin_ref, half, scale):
        cos = cos_ref[...]
        sin = sin_ref[...]
        for c in range(PROJ_TN // LANES):
            xc = acc[:, c * LANES:(c + 1) * LANES]
            y = xc * cos + _rotate_half_chunk(xc, half) * sin
            if scale != 1.0:
                y = y * scale
            o_ref[:, c * LANES:(c + 1) * LANES] = y.astype(o_ref.dtype)

    j_qa = OFF_QA // PROJ_TN
    j_ka = OFF_KA // PROJ_TN
    j_qr = OFF_QR // PROJ_TN
    j_kr = OFF_KR // PROJ_TN

    @pl.when(j == j_qa)
    def _():
        rotary(cos_a_ref, sin_a_ref, MOBA_HEAD_DIM // 2, MOBA_HEAD_DIM ** -0.5)

    @pl.when(j == j_ka)
    def _():
        rotary(cos_a_ref, sin_a_ref, MOBA_HEAD_DIM // 2, 1.0)

    @pl.when(j == j_qr)
    def _():
        rotary(cos_r_ref, sin_r_ref, RET_QK_DIM // 2, 1.0)

    @pl.when(j == j_kr)
    def _():
        rotary(cos_r_ref, sin_r_ref, RET_QK_DIM // 2, RET_QK_DIM ** -0.5)

    plain = (j != j_qa) & (j != j_ka) & (j != j_qr) & (j != j_kr)

    @pl.when(plain)
    def _():
        o_ref[...] = acc.astype(o_ref.dtype)


def _rotary_tables(seq, inv_freq, head_dim):
    ang = jnp.arange(seq, dtype=f32)[:, None] * inv_freq[None, :]
    cos = jnp.cos(ang)
    sin = jnp.sin(ang)
    cos_h = jnp.concatenate([cos, cos], axis=-1)
    sin_h = jnp.concatenate([-sin, sin], axis=-1)
    reps = LANES // head_dim
    return jnp.tile(cos_h, (1, reps)), jnp.tile(sin_h, (1, reps))


def _project(x2d, w_in_b, seq):
    tokens = x2d.shape[0]
    tm = min(PROJ_TM, seq)
    inv_a = 1.0 / (ROPE_THETA ** (jnp.arange(0, MOBA_HEAD_DIM, 2, dtype=f32) / MOBA_HEAD_DIM))
    inv_r = 1.0 / (RET_ANGLE_BASE ** jnp.linspace(0.0, 1.0, RET_QK_DIM // 2, dtype=f32))
    cos_a, sin_a = _rotary_tables(seq, inv_a, MOBA_HEAD_DIM)
    cos_r, sin_r = _rotary_tables(seq, inv_r, RET_QK_DIM)
    seq_tiles = seq // tm
    tab = pl.BlockSpec((tm, LANES), lambda i, j: (i % seq_tiles, 0))
    return pl.pallas_call(
        _proj_kernel,
        grid=(tokens // tm, N_IN // PROJ_TN),
        in_specs=[
            pl.BlockSpec((tm, D_MODEL), lambda i, j: (i, 0)),
            pl.BlockSpec((D_MODEL, PROJ_TN), lambda i, j: (0, j)),
            tab, tab, tab, tab,
        ],
        out_specs=pl.BlockSpec((tm, PROJ_TN), lambda i, j: (i, j)),
        out_shape=jax.ShapeDtypeStruct((tokens, N_IN), bf16),
        scratch_shapes=[pltpu.VMEM((tm, D_MODEL), bf16)],
        compiler_params=_params("parallel", "arbitrary"),
        name="in_proj_rotary",
    )(x2d, w_in_b, cos_a, sin_a, cos_r, sin_r)


def _moba_kernel(q_ref, k_ref, v_ref, o_ref, kmean_ref, vt_ref, selb_ref, gate_ref, *, n_blocks):
    i = pl.program_id(2)
    blk = MOBA_BLOCK
    hd = MOBA_HEAD_DIM
    heads = LANES // hd

    @pl.when(i == 0)
    def _():
        seq = n_blocks * blk
        member = (lax.broadcasted_iota(i32, (n_blocks, seq), 1) // blk
                  == lax.broadcasted_iota(i32, (n_blocks, seq), 0))
        kmean_ref[...] = _dot(member.astype(bf16), k_ref[...]) * (1.0 / blk)
        for j in range(n_blocks):
            vt_ref[j] = v_ref[j * blk:(j + 1) * blk, :].astype(f32).T.astype(bf16)

    km_hi, km_lo = _split_bf16(kmean_ref[...])
    key_i = lax.broadcasted_iota(i32, (blk, blk), 0)
    qry_i = lax.broadcasted_iota(i32, (blk, blk), 1)
    causal = key_i <= qry_i
    bid = lax.broadcasted_iota(i32, (n_blocks, blk), 0)
    d_row = lax.broadcasted_iota(i32, (LANES, blk), 0)
    qt = q_ref[...].astype(f32).T

    qts = []
    for h in range(heads):
        in_head = (d_row >= h * hd) & (d_row < (h + 1) * hd)
        qth = jnp.where(in_head, qt, 0.0).astype(bf16)
        qts.append(qth)
        gate = _dot(km_hi, qth) + _dot(km_lo, qth)
        g = jnp.where(bid < i, gate, -jnp.inf)
        gate_ref[...] = g
        beaten = jnp.zeros((n_blocks, blk), f32)
        for j in range(n_blocks):
            gj = jnp.broadcast_to(gate_ref[j:j + 1, :], (n_blocks, blk))
            beats = (gj > g) | ((gj == g) & (bid > j))
            beaten = beaten + jnp.where(beats, 1.0, 0.0)
        sel = (beaten < MOBA_TOPK) & (bid < i)
        selb_ref[h] = jnp.where(sel, 0.0, NEG)

    kj = k_ref[pl.ds(pl.multiple_of(i * blk, blk), blk), :]
    vtj = vt_ref[i]
    state = []
    for h in range(heads):
        s = jnp.where(causal, _dot(kj, qts[h]), NEG)
        m0 = jnp.max(s, axis=0, keepdims=True)
        p = jnp.exp(s - m0)
        l0 = jnp.sum(p, axis=0, keepdims=True)
        acc0 = _dot(vtj[h * hd:(h + 1) * hd, :], p.astype(bf16))
        state += [m0, l0, acc0]

    qt_all = jnp.concatenate(qts, axis=1)

    def update(carry, js):
        nj = len(js)
        k_all = k_ref[pl.ds(pl.multiple_of(js[0] * blk, blk), nj * blk), :]
        s_all = _dot(k_all, qt_all)
        vtjs = [vt_ref[j] for j in js]
        new = []
        for h in range(heads):
            m_old, l_old, acc = carry[3 * h:3 * h + 3]
            ss = [s_all[n * blk:(n + 1) * blk, h * blk:(h + 1) * blk] + selb_ref[h, pl.ds(j, 1), :]
                  for n, j in enumerate(js)]
            m_new = m_old
            for s in ss:
                m_new = jnp.maximum(m_new, jnp.max(s, axis=0, keepdims=True))
            a = jnp.exp(m_old - m_new)
            l_new = a * l_old
            acc = a * acc
            for s, vtj in zip(ss, vtjs):
                p = jnp.exp(s - m_new)
                l_new = l_new + jnp.sum(p, axis=0, keepdims=True)
                acc = acc + _dot(vtj[h * hd:(h + 1) * hd, :], p.astype(bf16))
            new += [m_new, l_new, acc]
        return tuple(new)

    quads = i // 4
    fin = lax.fori_loop(0, quads, lambda n, c: update(c, tuple(4 * n + t for t in range(4))), tuple(state))
    done = 4 * quads
    has_pair = (i - done) // 2
    fin = lax.fori_loop(0, has_pair, lambda n, c: update(c, (done, done + 1)), fin)
    done = done + 2 * has_pair
    fin = lax.fori_loop(done, i, lambda j, c: update(c, (j,)), fin)
    out_t = jnp.concatenate([fin[3 * h + 2] / fin[3 * h + 1] for h in range(heads)], axis=0)
    o_ref[...] = out_t.T.astype(o_ref.dtype)


def _moba(proj, batch, seq):
    tokens = batch * seq
    n_blocks = seq // MOBA_BLOCK
    chunks = MOBA_WIDTH // LANES
    return pl.pallas_call(
        functools.partial(_moba_kernel, n_blocks=n_blocks),
        grid=(batch, chunks, n_blocks),
        in_specs=[
            pl.BlockSpec((MOBA_BLOCK, LANES), lambda b, c, i: (b * n_blocks + i, OFF_QA // LANES + c)),
            pl.BlockSpec((seq, LANES), lambda b, c, i: (b, OFF_KA // LANES + c)),
            pl.BlockSpec((seq, LANES), lambda b, c, i: (b, OFF_VA // LANES + c)),
        ],
        out_specs=pl.BlockSpec((MOBA_BLOCK, LANES), lambda b, c, i: (b * n_blocks + i, c)),
        out_shape=jax.ShapeDtypeStruct((tokens, MOBA_WIDTH), bf16),
        scratch_shapes=[
            pltpu.VMEM((n_blocks, LANES), f32),
            pltpu.VMEM((n_blocks, LANES, MOBA_BLOCK), bf16),
            pltpu.VMEM((LANES // MOBA_HEAD_DIM, n_blocks, MOBA_BLOCK), f32),
            pltpu.VMEM((n_blocks, MOBA_BLOCK), f32),
        ],
        compiler_params=_params("parallel", "parallel", "arbitrary"),
        name="moba_attention",
    )(proj, proj, proj)


def _ret_kernel(q_ref, k_ref, v0_ref, v1_ref, g0_ref, g1_ref, decay_ref, zeta_ref, xi_ref, cd_ref,
                o_ref, state_ref):
    c = pl.program_id(1)
    heads_per_half = RET_HEADS // 2
    v_refs = (v0_ref, v1_ref)
    g_refs = (g0_ref, g1_ref)

    @pl.when(c == 0)
    def _():
        state_ref[...] = jnp.zeros_like(state_ref)

    for h in range(RET_HEADS):
        qk_cols = slice(h * RET_QK_DIM, (h + 1) * RET_QK_DIM)
        v_cols = slice(h * RET_V_DIM, (h + 1) * RET_V_DIM)
        half_cols = slice((h % heads_per_half) * RET_V_DIM, (h % heads_per_half + 1) * RET_V_DIM)
        q = q_ref[:, qk_cols]
        k = k_ref[:, qk_cols]
        v = v_refs[h // heads_per_half][:, half_cols]
        state = state_ref[h]
        scores = _dot_nt(q, k) * decay_ref[h]
        inner = _dot(scores.astype(bf16), v)
        cross = _dot(q, state.astype(bf16)) * xi_ref[h]
        o = inner + cross
        kz = (k.astype(f32) * zeta_ref[h]).astype(bf16)
        state_ref[h] = cd_ref[h] * state + _dot_tn(kz, v)

        mu = jnp.mean(o, axis=-1, keepdims=True)
        oc = o - mu
        var = jnp.mean(oc * oc, axis=-1, keepdims=True)
        on = oc * lax.rsqrt(var + GN_EPS)
        g = g_refs[h // heads_per_half][:, half_cols].astype(f32)
        o_ref[:, v_cols] = (on * (g * _sigmoid(g))).astype(o_ref.dtype)


def _retention(proj, batch, seq):
    tokens = batch * seq
    C = RET_CHUNK
    n_chunks = seq // C
    half_v = RET_V_WIDTH // 2
    gammas =1.0 - 2.0 ** (-5.0 - jnp.arange(RET_HEADS, dtype=f32))
    log_g = jnp.log(gammas)
    idx = jnp.arange(C, dtype=f32)
    diff = idx[:, None] - idx[None, :]
    decay = jnp.where(diff >= 0, jnp.exp(jnp.maximum(diff, 0.0)[None] * log_g[:, None, None]), 0.0)
    zeta = jnp.exp((C - 1 - idx)[None, :] * log_g[:, None])
    xi = jnp.exp((idx + 1.0)[None, :] * log_g[:, None])
    zeta_t = jnp.broadcast_to(zeta[:, :, None], (RET_HEADS, C, RET_QK_DIM))
    xi_t = jnp.broadcast_to(xi[:, :, None], (RET_HEADS, C, RET_V_DIM))
    cd_t = jnp.broadcast_to(jnp.exp(C * log_g)[:, None, None], (RET_HEADS, 1, RET_V_DIM))
    return pl.pallas_call(
        _ret_kernel,
        grid=(batch, n_chunks),
        in_specs=[
            pl.BlockSpec((C, RET_QK_WIDTH), lambda b, c: (b * n_chunks + c, OFF_QR // RET_QK_WIDTH)),
            pl.BlockSpec((C, RET_QK_WIDTH), lambda b, c: (b * n_chunks + c, OFF_KR // RET_QK_WIDTH)),
            pl.BlockSpec((C, half_v), lambda b, c: (b * n_chunks + c, OFF_VR // half_v)),
            pl.BlockSpec((C, half_v), lambda b, c: (b * n_chunks + c, OFF_VR // half_v + 1)),
            pl.BlockSpec((C, half_v), lambda b, c: (b * n_chunks + c, OFF_GR // half_v)),
            pl.BlockSpec((C, half_v), lambda b, c: (b * n_chunks + c, OFF_GR // half_v + 1)),
            pl.BlockSpec((RET_HEADS, C, C), lambda b, c: (0, 0, 0)),
            pl.BlockSpec((RET_HEADS, C, RET_QK_DIM), lambda b, c: (0, 0, 0)),
            pl.BlockSpec((RET_HEADS, C, RET_V_DIM), lambda b, c: (0, 0, 0)),
            pl.BlockSpec((RET_HEADS, 1, RET_V_DIM), lambda b, c: (0, 0, 0)),
        ],
        out_specs=pl.BlockSpec((C, RET_V_WIDTH), lambda b, c: (b * n_chunks + c, 0)),
        out_shape=jax.ShapeDtypeStruct((tokens, RET_V_WIDTH), bf16),
        scratch_shapes=[pltpu.VMEM((RET_HEADS, RET_QK_DIM, RET_V_DIM), f32)],
        compiler_params=_params("parallel", "arbitrary"),
        name="retention",
    )(proj, proj, proj, proj, proj, proj, decay, zeta_t, xi_t, cd_t)


def _merge_router_kernel(oa_ref, orr_ref, ga0_ref, ga1_ref, gb0_ref, gb1_ref, x_ref,
                         wm_ref, wr_ref, wo_ref, g1_ref, b1_ref, wrt_hi_ref, wrt_lo_ref, rb_ref,
                         h_ref, pos_ref, w_ref, tab_ref, cnt_ref, carry_ref):
    step = pl.program_id(0)
    tm = x_ref.shape[0]

    @pl.when(step == 0)
    def _():
        carry_ref[...] = jnp.zeros_like(carry_ref)

    a = _dot(oa_ref[...], wm_ref[...])
    r = _dot(orr_ref[...], wr_ref[...])
    ga = jnp.concatenate([ga0_ref[...], ga1_ref[...]], axis=1).astype(f32)
    gb = jnp.concatenate([gb0_ref[...], gb1_ref[...]], axis=1).astype(f32)
    merged = _sigmoid(ga) * a + _sigmoid(gb) * r
    mix = _dot(merged.astype(bf16), wo_ref[...])
    h = _layer_norm(ALPHA * x_ref[...] + mix, g1_ref[...], b1_ref[...])
    h_ref[...] = h

    h_hi, h_lo = _split_bf16(h)
    w_hi = wrt_hi_ref[...]
    logits = _dot_nt(w_hi, h_hi) + _dot_nt(w_hi, h_lo) + _dot_nt(wrt_lo_ref[...], h_hi)
    scores = _sigmoid(logits)
    biased = scores + rb_ref[...]

    v = biased.reshape(N_GROUPS, GROUP_SIZE, tm)
    sub = lax.broadcasted_iota(i32, v.shape, 1)
    m1 = jnp.max(v, axis=1, keepdims=True)
    i1 = jnp.min(jnp.where(v == m1, sub, GROUP_SIZE), axis=1, keepdims=True)
    m2 = jnp.max(jnp.where(sub == i1, -jnp.inf, v), axis=1, keepdims=True)
    gscore = jnp.broadcast_to(m1 + m2, v.shape).reshape(N_EXPERTS, tm)

    eid = lax.broadcasted_iota(i32, (N_EXPERTS, tm), 0)
    egrp = eid // GROUP_SIZE
    e_mask = jnp.zeros((N_EXPERTS, tm), jnp.bool_)
    for _ in range(TOPK_GROUPS):
        m = jnp.max(gscore, axis=0, keepdims=True)
        idx = jnp.min(jnp.where(gscore == m, egrp, N_GROUPS), axis=0, keepdims=True)
        hit = egrp == idx
        e_mask = e_mask | hit
        gscore = jnp.where(hit, -jnp.inf, gscore)
    cand = jnp.where(e_mask, biased, -jnp.inf)

    chosen = jnp.zeros((N_EXPERTS, tm), jnp.bool_)
    e_rows = []
    w_rows = []
    for _ in range(TOP_K):
        m = jnp.max(cand, axis=0, keepdims=True)
        idx = jnp.min(jnp.where(cand == m, eid, N_EXPERTS), axis=0, keepdims=True)
        hit = eid == idx
        chosen = chosen | hit
        e_rows.append(idx)
        w_rows.append(jnp.sum(jnp.where(hit, scores, 0.0), axis=0, keepdims=True))
        cand = jnp.where(hit, -jnp.inf, cand)
    w_sum = w_rows[0]
    for wk in w_rows[1:]:
        w_sum = w_sum + wk

    t_src = lax.broadcasted_iota(i32, (tm, tm), 0)
    t_dst = lax.broadcasted_iota(i32, (tm, tm), 1)
    before = (t_src < t_dst).astype(bf16)
    chosen_f = chosen.astype(f32)
    prior = _dot(chosen_f.astype(bf16), before)
    cnt = jnp.sum(chosen_f, axis=1, keepdims=True)
    cnt_pad = jnp.ceil(cnt * (1.0 / UNIT)) * UNIT
    e_src = lax.broadcasted_iota(i32, (N_EXPERTS, N_EXPERTS), 1)
    e_dst = lax.broadcasted_iota(i32, (N_EXPERTS, N_EXPERTS), 0)
    earlier = (e_src < e_dst).astype(bf16)
    cnt_pad_l = jnp.broadcast_to(cnt_pad, (N_EXPERTS, LANES))
    loc_start_l = _dot(earlier, cnt_pad_l.astype(bf16))
    loc_start = loc_start_l[:, 0:1]
    where_e = prior + loc_start
    pos_ref[...] = jnp.zeros_like(pos_ref)
    w_ref[...] = jnp.zeros_like(w_ref)
    for k in range(TOP_K):
        pos_ref[k:k + 1, :] = jnp.sum(jnp.where(eid == e_rows[k], where_e, 0.0), axis=0, keepdims=True).astype(i32)
        w_ref[k:k + 1, :] = w_rows[k] / w_sum * ROUTED_SCALE

    carry = carry_ref[...]
    tl = lax.broadcasted_iota(i32, (N_EXPERTS, LANES), 1)
    table = jnp.where(tl == 0, cnt_pad_l, jnp.where(tl == 1, loc_start_l, jnp.broadcast_to(carry, (N_EXPERTS, LANES))))
    tab_ref[0] = table.astype(i32)
    carry = carry + cnt_pad
    carry_ref[...] = carry
    cnt_ref[...] = jnp.broadcast_to(carry, cnt_ref.shape).astype(i32)


def _merge_router(oa, orr, proj, x2d, wm, wr, wo, g1, b1, wrt_hi, wrt_lo, rb):
    tokens = x2d.shape[0]
    tm = TOK_TM
    half = D_MODEL // 2

    def gate_spec(off):
        return pl.BlockSpec((tm, half), lambda i: (i, off // half))

    def full(shape):
        return pl.BlockSpec(shape, lambda i: (0,) * len(shape))

    row8 = pl.BlockSpec((8, tm), lambda i: (0, i))
    return pl.pallas_call(
        _merge_router_kernel,
        grid=(tokens // tm,),
        in_specs=[
            pl.BlockSpec((tm, MOBA_WIDTH), lambda i: (i, 0)),
            pl.BlockSpec((tm, RET_V_WIDTH), lambda i: (i, 0)),
            gate_spec(OFF_GA), gate_spec(OFF_GA + half), gate_spec(OFF_GB), gate_spec(OFF_GB + half),
            pl.BlockSpec((tm, D_MODEL), lambda i: (i, 0)),
            full((MOBA_WIDTH, D_MODEL)), full((RET_V_WIDTH, D_MODEL)), full((D_MODEL, D_MODEL)),
            full((1, D_MODEL)), full((1, D_MODEL)),
            full((N_EXPERTS, D_MODEL)), full((N_EXPERTS, D_MODEL)), full((N_EXPERTS, 1)),
        ],
        out_specs=[
            pl.BlockSpec((tm, D_MODEL), lambda i: (i, 0)),
            row8, row8,
            pl.BlockSpec((1, N_EXPERTS, LANES), lambda i: (i, 0, 0)),
            full((N_EXPERTS, LANES)),
        ],
        out_shape=[
            jax.ShapeDtypeStruct((tokens, D_MODEL), f32),
            jax.ShapeDtypeStruct((8, tokens), i32),
            jax.ShapeDtypeStruct((8, tokens), f32),
            jax.ShapeDtypeStruct((tokens // tm, N_EXPERTS, LANES), i32),
            jax.ShapeDtypeStruct((N_EXPERTS, LANES), i32),
        ],
        scratch_shapes=[pltpu.VMEM((N_EXPERTS, 1), f32)],
        compiler_params=_params("arbitrary"),
        name="merge_ln1_router",
    )(oa, orr, proj, proj, proj, proj, x2d, wm, wr, wo, g1, b1, wrt_hi, wrt_lo, rb)


def _pack_rows(x):
    bits = lax.bitcast_convert_type(x, jnp.uint32)
    return (bits[:, :HALF] & jnp.uint32(0xFFFF0000)) | (bits[:, HALF:] >> 16)


def _unpack_rows(words):
    left = lax.bitcast_convert_type(words & jnp.uint32(0xFFFF0000), f32)
    right = lax.bitcast_convert_type(words << 16, f32)
    return left.astype(bf16), right.astype(bf16)


def _unit_slices(local_unit, global_unit):
    loc = pl.ds(pl.multiple_of(local_unit * UNIT, UNIT), UNIT)
    glob = pl.ds(pl.multiple_of(global_unit * UNIT, UNIT), UNIT)
    return loc, glob


def _dispatch_kernel(gu_ref, nu_ref, tail_ref, used_ref, h_ref, pos_ref, rows_ref,
                     xs_ref, zero_ref, sem_ref, zsem_ref):
    step = pl.program_id(0)
    tm = h_ref.shape[0]

    @pl.when(step == 0)
    def _():
        zero_ref[...] = jnp.zeros_like(zero_ref)

        def clear_copy(block):
            start = pl.multiple_of(block * ROW_BLOCK, ROW_BLOCK)
            return pltpu.make_async_copy(zero_ref, rows_ref.at[pl.ds(start, ROW_BLOCK), :], zsem_ref)

        def clear_all(act):
            def tail(e, _):
                tail_block = tail_ref[e]

                @pl.when(tail_block >= 0)
                def _():
                    act(clear_copy(tail_block))
                return 0

            lax.fori_loop(0, N_EXPERTS, tail, 0)

            def unused(b, _):
                act(clear_copy(b))
                return 0

            lax.fori_loop(used_ref[0], rows_ref.shape[0] // ROW_BLOCK, unused, 0)

        clear_all(lambda cp: cp.start())
        clear_all(lambda cp: cp.wait())

    def unit_copy(t, s):
        loc, glob = _unit_slices(s, gu_ref[t * LOC_UNITS + s])
        return pltpu.make_async_copy(xs_ref.at[t % 2, loc, :], rows_ref.at[glob, :], sem_ref.at[t % 2])

    def drain(t):
        def wait_one(s, _):
            unit_copy(t, s).wait()
            return 0

        lax.fori_loop(0, nu_ref[t], wait_one, 0)

    @pl.when(step >= 2)
    def _():
        drain(step - 2)

    pos = pos_ref[...]
    r_iota = lax.broadcasted_iota(i32, (LOC_ROWS, tm), 0)
    onehot = r_iota == pos[0:1, :]
    for k in range(1, TOP_K):
        onehot = onehot | (r_iota == pos[k:k + 1, :])
    xs = _dot(onehot.astype(bf16), h_ref[...].astype(bf16))
    xs_ref[step % 2] = _pack_rows(xs)

    def issue(s, _):
        unit_copy(step, s).start()
        return 0

    lax.fori_loop(0, nu_ref[step], issue, 0)

    @pl.when(step == pl.num_programs(0) - 1)
    def _():
        @pl.when(step >= 1)
        def _():
            drain(step - 1)

        drain(step)


def _dispatch(unit_map, n_units, tail_block, n_used, h, pos8, n_rows):
    tokens = h.shape[0]
    tm = TOK_TM
    grid_spec = pltpu.PrefetchScalarGridSpec(
        num_scalar_prefetch=4,
        grid=(tokens // tm,),
        in_specs=[
            pl.BlockSpec((tm, D_MODEL), lambda i, *_: (i, 0)),
            pl.BlockSpec((8, tm), lambda i, *_: (0, i)),
        ],
        out_specs=pl.BlockSpec(memory_space=pl.ANY),
        scratch_shapes=[
            pltpu.VMEM((2, LOC_ROWS, HALF), jnp.uint32),
            pltpu.VMEM((ROW_BLOCK, HALF), jnp.uint32),
            pltpu.SemaphoreType.DMA((2,)),
            pltpu.SemaphoreType.DMA,
        ],
    )
    return pl.pallas_call(
        _dispatch_kernel,
        grid_spec=grid_spec,
        out_shape=jax.ShapeDtypeStruct((n_rows, HALF), jnp.uint32),
        compiler_params=_params("arbitrary"),
        name="moe_dispatch",
    )(unit_map, n_units, tail_block, n_used, h, pos8)


def _expert_kernel(eid_ref, used_ref, x_ref, wg_ref, wu_ref, wd_ref, y_ref, wgb_ref, wub_ref, wdb_ref):
    b = pl.program_id(0)
    changed = (b == 0) | (eid_ref[b] != eid_ref[jnp.maximum(b - 1, 0)])

    @pl.when(changed)
    def _():
        wgb_ref[...] = wg_ref[0].astype(bf16)
        wub_ref[...] = wu_ref[0].astype(bf16)
        wdb_ref[...] = wd_ref[0].astype(bf16)

    @pl.when(b < used_ref[0])
    def _():
        for c in range(ROW_BLOCK // EXPERT_CHUNK):
            rows = slice(c * EXPERT_CHUNK, (c + 1) * EXPERT_CHUNK)
            xl, xr = _unpack_rows(x_ref[rows, :])
            g = _dot(xl, wgb_ref[:HALF, :]) + _dot(xr, wgb_ref[HALF:, :])
            u = _dot(xl, wub_ref[:HALF, :]) + _dot(xr, wub_ref[HALF:, :])
            mid = (g * _sigmoid(g)) * u
            y = _dot(mid.astype(bf16), wdb_ref[...])
            y_ref[rows, :] = _pack_rows(y.astype(bf16).astype(f32))

    @pl.when(b >= used_ref[0])
    def _():
        y_ref[...] = jnp.zeros_like(y_ref)


def _experts(block_eid, n_used, rows, wg, wu, wd):
    n_rows = rows.shape[0]
    n_blocks = n_rows // ROW_BLOCK

    def row_map(b, eid, used):
        return (b, 0)

    def w_map(b, eid, used):
        return (eid[b], 0, 0)

    grid_spec = pltpu.PrefetchScalarGridSpec(
        num_scalar_prefetch=2,
        grid=(n_blocks,),
        in_specs=[
            pl.BlockSpec((ROW_BLOCK, HALF), row_map),
            pl.BlockSpec((1, D_MODEL, D_EXPERT), w_map),
            pl.BlockSpec((1, D_MODEL, D_EXPERT), w_map),
            pl.BlockSpec((1, D_EXPERT, D_MODEL), w_map),
        ],
        out_specs=pl.BlockSpec((ROW_BLOCK, HALF), row_map),
        scratch_shapes=[
            pltpu.VMEM((D_MODEL, D_EXPERT), bf16),
            pltpu.VMEM((D_MODEL, D_EXPERT), bf16),
            pltpu.VMEM((D_EXPERT, D_MODEL), bf16),
        ],
    )
    return pl.pallas_call(
        _expert_kernel,
        grid_spec=grid_spec,
        out_shape=jax.ShapeDtypeStruct((n_rows, HALF), jnp.uint32),
        compiler_params=_params("arbitrary"),
        name="moe_experts",
    )(block_eid, n_used, rows, wg, wu, wd)


def _combine_kernel(gu_ref, nu_ref, y_ref, h_ref, pos_ref, wt_ref, p_ref, wsg_ref, wsu_ref, wsd_ref, wpp_ref,
                    wpg_ref, g2_ref, b2_ref, g3_ref, b3_ref, o_ref, ybuf_ref, sem_ref):
    step = pl.program_id(0)
    tm = h_ref.shape[0]

    @pl.when(step == 0)
    def _():
        ybuf_ref[...] = jnp.zeros_like(ybuf_ref)

    def unit_copy(s):
        loc, glob = _unit_slices(s, gu_ref[step * LOC_UNITS + s])
        return pltpu.make_async_copy(y_ref.at[glob, :], ybuf_ref.at[loc, :], sem_ref)

    def start_one(s, _):
        unit_copy(s).start()
        return 0

    lax.fori_loop(0, nu_ref[step], start_one, 0)

    h = h_ref[...]
    hb = h.astype(bf16)
    sg = _dot(hb, wsg_ref[...])
    su = _dot(hb, wsu_ref[...])
    shared = _dot(((sg * _sigmoid(sg)) * su).astype(bf16), wsd_ref[...])
    ple_in = _dot(p_ref[...].astype(bf16), wpp_ref[...])

    pos = pos_ref[...]
    wt = wt_ref[...]
    c_iota = lax.broadcasted_iota(i32, (tm, LOC_ROWS), 1)
    spread = jnp.where(c_iota == pos[:, 0:1], wt[:, 0:1], 0.0)
    for k in range(1, TOP_K):
        spread = spread + jnp.where(c_iota == pos[:, k:k + 1], wt[:, k:k + 1], 0.0)
    spread = spread.astype(bf16)

    def wait_one(s, _):
        unit_copy(s).wait()
        return 0

    lax.fori_loop(0, nu_ref[step], wait_one, 0)

    yl, yr = _unpack_rows(ybuf_ref[...])
    routed = jnp.concatenate([_dot(spread, yl), _dot(spread, yr)], axis=1)

    h2 = _layer_norm(ALPHA * h + (routed + shared), g2_ref[...], b2_ref[...])
    ple = ple_in * _sigmoid(_dot(h2.astype(bf16), wpg_ref[...]))
    o_ref[...] = _layer_norm(ALPHA * h2 + ple, g3_ref[...], b3_ref[...])


def _combine(unit_map, n_units, y_rows, h, pos_tok, w_tok, p2d, wsg, wsu, wsd, wpp, wpg, g2, b2, g3, b3):
    tokens = h.shape[0]
    tm = TOK_TM

    def full(shape):
        return pl.BlockSpec(shape, lambda i, *_: (0,) * len(shape))

    def tile(width):
        return pl.BlockSpec((tm, width), lambda i, *_: (i, 0))

    grid_spec = pltpu.PrefetchScalarGridSpec(
        num_scalar_prefetch=2,
        grid=(tokens // tm,),
        in_specs=[
            pl.BlockSpec(memory_space=pl.ANY),
            tile(D_MODEL), tile(8), tile(8), tile(PLE_DIM),
            full((D_MODEL, D_SHARED)), full((D_MODEL, D_SHARED)), full((D_SHARED, D_MODEL)),
            full((PLE_DIM, D_MODEL)), full((D_MODEL, D_MODEL)),
            full((1, D_MODEL)), full((1, D_MODEL)), full((1, D_MODEL)), full((1, D_MODEL)),
        ],
        out_specs=tile(D_MODEL),
        scratch_shapes=[
            pltpu.VMEM((LOC_ROWS, HALF), jnp.uint32),
            pltpu.SemaphoreType.DMA,
        ],
    )
    return pl.pallas_call(
        _combine_kernel,
        grid_spec=grid_spec,
        out_shape=jax.ShapeDtypeStruct((tokens, D_MODEL), f32),
        compiler_params=_params("arbitrary"),
        name="moe_combine_ln2_ple_ln3",
    )(unit_map, n_units, y_rows, h, pos_tok, w_tok, p2d, wsg, wsu, wsd, wpp, wpg, g2, b2, g3, b3)


def _row_layout(table, totals, n_blocks):
    cnt_u = table[:, :, 0] // UNIT
    loc_u = table[:, :, 1] // UNIT
    padded = (totals + ROW_BLOCK - 1) // ROW_BLOCK * ROW_BLOCK
    pends = jnp.cumsum(padded)
    pstarts = pends - padded
    glob_u = (pstarts[None, :] + table[:, :, 2]) // UNIT
    n_units = jnp.sum(cnt_u, axis=1).astype(i32)
    s = jnp.arange(LOC_UNITS, dtype=i32)
    loc_end = loc_u + cnt_u
    e_of = jnp.sum((loc_end[:, None, :] <= s[None, :, None]).astype(i32), axis=-1)
    is_e = e_of[:, :, None] == jnp.arange(N_EXPERTS, dtype=i32)
    shift = jnp.sum(jnp.where(is_e, (glob_u - loc_u)[:, None, :], 0), axis=-1)
    unit_map = (shift + s[None, :]).astype(i32).reshape(-1)
    tail_block = jnp.where(totals > 0, pends // ROW_BLOCK - 1, -1).astype(i32)
    block_start = jnp.arange(n_blocks, dtype=i32) * ROW_BLOCK
    ends_before = jnp.sum((pends[None, :] <= block_start[:, None]).astype(i32), axis=1)
    block_eid = jnp.minimum(ends_before, N_EXPERTS - 1).astype(i32)
    n_used = (pends[-1:] // ROW_BLOCK).astype(i32)
    return unit_map, n_units, tail_block, block_eid, n_used


def _layer(x2d, p2d, batch, seq, w_in, w_moba_up, w_ret_up, w_out, ln1_g, ln1_b, w_router, router_bias,
           w_eg, w_eu, w_ed, w_sg, w_su, w_sd, ln2_g, ln2_b, w_ple_proj, w_ple_gate, ln3_g, ln3_b):
    tokens = batch * seq
    row = lambda a: a.reshape(1, -1).astype(f32)

    proj = _project(x2d, w_in.astype(bf16), seq)
    oa = _moba(proj, batch, seq)
    orr = _retention(proj, batch, seq)

    wrt = w_router.astype(f32).T
    wrt_hi = wrt.astype(bf16)
    wrt_lo = (wrt - wrt_hi.astype(f32)).astype(bf16)
    h1, pos8, w8, table, totals = _merge_router(
        oa, orr, proj, x2d, w_moba_up.astype(bf16), w_ret_up.astype(bf16), w_out.astype(bf16),
        row(ln1_g), row(ln1_b), wrt_hi, wrt_lo, router_bias.astype(f32).reshape(N_EXPERTS, 1))

    n_tiles = tokens // TOK_TM
    max_rows = tokens * TOP_K + n_tiles * N_EXPERTS * (UNIT - 1) + N_EXPERTS * (ROW_BLOCK - 1)
    n_blocks = -(-max_rows // ROW_BLOCK)
    unit_map, n_units, tail_block, block_eid, n_used = _row_layout(table[:, :, :3], totals[:, 0], n_blocks)

    rows = _dispatch(unit_map, n_units, tail_block, n_used, h1, pos8, n_blocks * ROW_BLOCK)
    y_rows = _experts(block_eid, n_used, rows, w_eg, w_eu, w_ed)
    return _combine(unit_map, n_units, y_rows, h1, pos8.T, w8.T, p2d,
                    w_sg.astype(bf16), w_su.astype(bf16), w_sd.astype(bf16),
                    w_ple_proj.astype(bf16), w_ple_gate.astype(bf16),
                    row(ln2_g), row(ln2_b), row(ln3_g), row(ln3_b))


def kernel(x, p, w_in, w_moba_up, w_ret_up, w_out, ln1_g, ln1_b, w_router, router_bias, w_exp_gate, w_exp_up,
           w_exp_down, w_sh_gate, w_sh_up, w_sh_down, ln2_g, ln2_b, w_ple_proj, w_ple_gate, ln3_g, ln3_b):
    batch, seq, d = x.shape
    assert d == D_MODEL and seq % max(MOBA_BLOCK, RET_CHUNK) == 0 and (batch * seq) % TOK_TM == 0
    assert w_in.shape[0] == DEPTH
    h = x.reshape(batch * seq, d)
    for i in range(DEPTH):
        h = _layer(h, p[i].reshape(batch * seq, PLE_DIM), batch, seq,
                   w_in[i], w_moba_up[i], w_ret_up[i], w_out[i], ln1_g[i], ln1_b[i], w_router[i], router_bias[i],
                   w_exp_gate[i], w_exp_up[i], w_exp_down[i], w_sh_gate[i], w_sh_up[i], w_sh_down[i],
                   ln2_g[i], ln2_b[i], w_ple_proj[i], w_ple_gate[i], ln3_g[i], ln3_b[i])
    return h.reshape(batch, seq, d)
```

```python
import functools

import jax
import jax.numpy as jnp
from jax import lax
from jax.experimental import pallas as pl
from jax.experimental.pallas import tpu as pltpu

f32 = jnp.float32
bf16 = jnp.bfloat16
i32 = jnp.int32

D_MODEL = 1024
DEPTH = 1
MOBA_HEADS = 8
MOBA_HEAD_DIM = 64
MOBA_WIDTH = MOBA_HEADS * MOBA_HEAD_DIM
MOBA_BLOCK = 256
MOBA_TOPK = 3
ROPE_THETA = 10000.0
RET_HEADS = 4
RET_QK_DIM = 128
RET_V_DIM = 256
RET_QK_WIDTH = RET_HEADS * RET_QK_DIM
RET_V_WIDTH = RET_HEADS * RET_V_DIM
RET_CHUNK = 256
RET_ANGLE_BASE = 10000.0
N_IN = 3 * MOBA_WIDTH + 2 * RET_QK_WIDTH + 2 * RET_V_WIDTH + 2 * D_MODEL
N_EXPERTS = 64
TOP_K = 6
N_GROUPS = 8
TOPK_GROUPS = 4
GROUP_SIZE = N_EXPERTS // N_GROUPS
D_EXPERT = 256
D_SHARED = 256
ROUTED_SCALE = 2.5
PLE_DIM = 256
LN_EPS = 1e-5
GN_EPS = 1e-6
ALPHA = (2.0 * DEPTH) ** 0.25

OFF_QA = 0
OFF_KA = MOBA_WIDTH
OFF_VA = 2 * MOBA_WIDTH
OFF_QR = 3 * MOBA_WIDTH
OFF_KR = OFF_QR + RET_QK_WIDTH
OFF_VR = OFF_KR + RET_QK_WIDTH
OFF_GR = OFF_VR + RET_V_WIDTH
OFF_GA = OFF_GR + RET_V_WIDTH
OFF_GB = OFF_GA + D_MODEL

LANES = 128
NEG = -1e30
VMEM_LIMIT = 56 * 1024 * 1024

PROJ_TM = 1024
PROJ_TN = 512
MOBA_STEP_WIDTH = 256
MERGE_TILES = 2
TOK_TM = 256
ROW_BLOCK = 512
EXPERT_CHUNK = 256
UNIT = 8
LOC_ROWS = -(-(TOP_K * TOK_TM + N_EXPERTS * (UNIT - 1)) // LANES) * LANES
LOC_UNITS = LOC_ROWS // UNIT
HALF = D_MODEL // 2


def _dot(a, b):
    return jnp.dot(a, b, preferred_element_type=f32)


def _dot_nt(a, b):
    return lax.dot_general(a, b, (((1,), (1,)), ((), ())), preferred_element_type=f32)


def _dot_tn(a, b):
    return lax.dot_general(a, b, (((0,), (0,)), ((), ())), preferred_element_type=f32)


def _split_bf16(a):
    hi = a.astype(bf16)
    lo = (a - hi.astype(f32)).astype(bf16)
    return hi, lo


def _layer_norm(x, g, b):
    mu = jnp.mean(x, axis=-1, keepdims=True)
    xc = x - mu
    var = jnp.mean(xc * xc, axis=-1, keepdims=True)
    return xc * lax.rsqrt(var + LN_EPS) * g + b


def _sigmoid(x):
    return 1.0 / (1.0 + jnp.exp(-x))


def _params(*sem, flags=None):
    return pltpu.CompilerParams(dimension_semantics=sem, vmem_limit_bytes=VMEM_LIMIT, flags=flags)


def _rotate_half_chunk(xc, half):
    if 2 * half == LANES:
        return pltpu.roll(xc, half, axis=1)
    lane = lax.broadcasted_iota(i32, xc.shape, 1)
    first = (lane & (2 * half - 1)) < half
    return jnp.where(first, pltpu.roll(xc, LANES - half, axis=1), pltpu.roll(xc, half, axis=1))


def _proj_kernel(x_ref, w_ref, cos_a_ref, sin_a_ref, cos_r_ref, sin_r_ref, o_ref, xb_ref):
    j = pl.program_id(1)

    @pl.when(j == 0)
    def _():
        xb_ref[...] = x_ref[...].astype(bf16)

    acc = _dot(xb_ref[...], w_ref[...])

    def rotary(cos_ref, sin_ref, half, scale):
        cos = cos_ref[...]
        sin = sin_ref[...]
        for c in range(PROJ_TN // LANES):
            xc = acc[:, c * LANES:(c + 1) * LANES]
            y = xc * cos + _rotate_half_chunk(xc, half) * sin
            if scale != 1.0:
                y = y * scale
            o_ref[:, c * LANES:(c + 1) * LANES] = y.astype(o_ref.dtype)

    j_qa = OFF_QA // PROJ_TN
    j_ka = OFF_KA // PROJ_TN
    j_qr = OFF_QR // PROJ_TN
    j_kr = OFF_KR // PROJ_TN

    @pl.when(j == j_qa)
    def _():
        rotary(cos_a_ref, sin_a_ref, MOBA_HEAD_DIM // 2, MOBA_HEAD_DIM ** -0.5)

    @pl.when(j == j_ka)
    def _():
        rotary(cos_a_ref, sin_a_ref, MOBA_HEAD_DIM // 2, 1.0)

    @pl.when(j == j_qr)
    def _():
        rotary(cos_r_ref, sin_r_ref, RET_QK_DIM // 2, 1.0)

    @pl.when(j == j_kr)
    def _():
        rotary(cos_r_ref, sin_r_ref, RET_QK_DIM // 2, RET_QK_DIM ** -0.5)

    plain = (j != j_qa) & (j != j_ka) & (j != j_qr) & (j != j_kr)

    @pl.when(plain)
    def _():
        o_ref[...] = acc.astype(o_ref.dtype)


def _rotary_tables(seq, inv_freq, head_dim):
    ang = jnp.arange(seq, dtype=f32)[:, None] * inv_freq[None, :]
    cos = jnp.cos(ang)
    sin = jnp.sin(ang)
    cos_h = jnp.concatenate([cos, cos], axis=-1)
    sin_h = jnp.concatenate([-sin, sin], axis=-1)
    reps = LANES // head_dim
    return jnp.tile(cos_h, (1, reps)), jnp.tile(sin_h, (1, reps))


def _project(x2d, w_in_b, seq):
    tokens = x2d.shape[0]
    tm = min(PROJ_TM, seq)
    inv_a = 1.0 / (ROPE_THETA ** (jnp.arange(0, MOBA_HEAD_DIM, 2, dtype=f32) / MOBA_HEAD_DIM))
    inv_r = 1.0 / (RET_ANGLE_BASE ** jnp.linspace(0.0, 1.0, RET_QK_DIM // 2, dtype=f32))
    cos_a, sin_a = _rotary_tables(seq, inv_a, MOBA_HEAD_DIM)
    cos_r, sin_r = _rotary_tables(seq, inv_r, RET_QK_DIM)
    seq_tiles = seq // tm
    tab = pl.BlockSpec((tm, LANES), lambda i, j: (i % seq_tiles, 0))
    return pl.pallas_call(
        _proj_kernel,
        grid=(tokens // tm, N_IN // PROJ_TN),
        in_specs=[
            pl.BlockSpec((tm, D_MODEL), lambda i, j: (i, 0)),
            pl.BlockSpec((D_MODEL, PROJ_TN), lambda i, j: (0, j)),
            tab, tab, tab, tab,
        ],
        out_specs=pl.BlockSpec((tm, PROJ_TN), lambda i, j: (i, j)),
        out_shape=jax.ShapeDtypeStruct((tokens, N_IN), bf16),
        scratch_shapes=[pltpu.VMEM((tm, D_MODEL), bf16)],
        compiler_params=_params("parallel", "arbitrary"),
        name="in_proj_rotary",
    )(x2d, w_in_b, cos_a, sin_a, cos_r, sin_r)


def _moba_kernel(q_ref, k_ref, v_ref, o_ref, kmean_ref, vt_ref, selb_ref, gate_ref, *, n_blocks):
    i = pl.program_id(2)
    blk = MOBA_BLOCK
    hd = MOBA_HEAD_DIM
    width = q_ref.shape[1]
    chunks = width // LANES
    heads = width // hd
    per_chunk = LANES // hd

    def chunk_of(h):
        return slice(h // per_chunk * LANES, (h // per_chunk + 1) * LANES)

    @pl.when(i == 0)
    def _():
        seq = n_blocks * blk
        member = (lax.broadcasted_iota(i32, (n_blocks, seq), 1) // blk
                  == lax.broadcasted_iota(i32, (n_blocks, seq), 0))
        kmean_ref[...] = _dot(member.astype(bf16), k_ref[...]) * (1.0 / blk)
        for j in range(n_blocks):
            vt_ref[j] = v_ref[j * blk:(j + 1) * blk, :].astype(f32).T.astype(bf16)

    km_hi, km_lo = _split_bf16(kmean_ref[...])
    key_i = lax.broadcasted_iota(i32, (blk, blk), 0)
    qry_i = lax.broadcasted_iota(i32, (blk, blk), 1)
    causal = key_i <= qry_i
    bid = lax.broadcasted_iota(i32, (n_blocks, blk), 0)
    d_row = lax.broadcasted_iota(i32, (LANES, blk), 0)
    qt = q_ref[...].astype(f32).T

    qts = []
    for h in range(heads):
        r0 = h % per_chunk * hd
        in_head = (d_row >= r0) & (d_row < r0 + hd)
        qth = jnp.where(in_head, qt[chunk_of(h), :], 0.0).astype(bf16)
        qts.append(qth)
        gate = _dot(km_hi[:, chunk_of(h)], qth) + _dot(km_lo[:, chunk_of(h)], qth)
        g = jnp.where(bid < i, gate, -jnp.inf)
        gate_ref[...] = g
        beaten = jnp.zeros((n_blocks, blk), f32)
        for j in range(n_blocks):
            gj = jnp.broadcast_to(gate_ref[j:j + 1, :], (n_blocks, blk))
            beats = (gj > g) | ((gj == g) & (bid > j))
            beaten = beaten + jnp.where(beats, 1.0, 0.0)
        sel = (beaten < MOBA_TOPK) & (bid < i)
        selb_ref[h] = jnp.where(sel, 0.0, NEG)

    kj = k_ref[pl.ds(pl.multiple_of(i * blk, blk), blk), :]
    vtj = vt_ref[i]
    state = []
    for h in range(heads):
        s = jnp.where(causal, _dot(kj[:, chunk_of(h)], qts[h]), NEG)
        m0 = jnp.max(s, axis=0, keepdims=True)
        p = jnp.exp(s - m0)
        l0 = jnp.sum(p, axis=0, keepdims=True)
        acc0 = _dot(vtj[h * hd:(h + 1) * hd, :], p.astype(bf16))
        state += [m0, l0, acc0]

    qt_alls = [jnp.concatenate(qts[c * per_chunk:(c + 1) * per_chunk], axis=1) for c in range(chunks)]

    def update(carry, js):
        nj = len(js)
        k_all = k_ref[pl.ds(pl.multiple_of(js[0] * blk, blk), nj * blk), :]
        s_alls = [_dot(k_all[:, c * LANES:(c + 1) * LANES], qt_alls[c]) for c in range(chunks)]
        vtjs = [vt_ref[j] for j in js]
        new = []
        for h in range(heads):
            m_old, l_old, acc = carry[3 * h:3 * h + 3]
            col = h % per_chunk * blk
            ss = [s_alls[h // per_chunk][n * blk:(n + 1) * blk, col:col + blk] + selb_ref[h, pl.ds(j, 1), :]
                  for n, j in enumerate(js)]
            m_new = m_old
            for s in ss:
                m_new = jnp.maximum(m_new, jnp.max(s, axis=0, keepdims=True))
            a = jnp.exp(m_old - m_new)
            l_new = a * l_old
            acc = a * acc
            for s, vtj in zip(ss, vtjs):
                p = jnp.exp(s - m_new)
                l_new = l_new + jnp.sum(p, axis=0, keepdims=True)
                acc = acc + _dot(vtj[h * hd:(h + 1) * hd, :], p.astype(bf16))
            new += [m_new, l_new, acc]
        return tuple(new)

    quads = i // 4
    fin = lax.fori_loop(0, quads, lambda n, c: update(c, tuple(4 * n + t for t in range(4))), tuple(state))
    done = 4 * quads
    has_pair = (i - done) // 2
    fin = lax.fori_loop(0, has_pair, lambda n, c: update(c, (done, done + 1)), fin)
    done = done + 2 * has_pair
    fin = lax.fori_loop(done, i, lambda j, c: update(c, (j,)), fin)
    out_t = jnp.concatenate([fin[3 * h + 2] / fin[3 * h + 1] for h in range(heads)], axis=0)
    o_ref[...] = out_t.T.astype(o_ref.dtype)


def _moba(proj, batch, seq):
    tokens = batch * seq
    n_blocks = seq // MOBA_BLOCK
    width = MOBA_STEP_WIDTH
    return pl.pallas_call(
        functools.partial(_moba_kernel, n_blocks=n_blocks),
        grid=(batch, MOBA_WIDTH // width, n_blocks),
        in_specs=[
            pl.BlockSpec((MOBA_BLOCK, width), lambda b, c, i: (b * n_blocks + i, OFF_QA // width + c)),
            pl.BlockSpec((seq, width), lambda b, c, i: (b, OFF_KA // width + c)),
            pl.BlockSpec((seq, width), lambda b, c, i: (b, OFF_VA // width + c)),
        ],
        out_specs=pl.BlockSpec((MOBA_BLOCK, width), lambda b, c, i: (b * n_blocks + i, c)),
        out_shape=jax.ShapeDtypeStruct((tokens, MOBA_WIDTH), bf16),
        scratch_shapes=[
            pltpu.VMEM((n_blocks, width), f32),
            pltpu.VMEM((n_blocks, width, MOBA_BLOCK), bf16),
            pltpu.VMEM((width // MOBA_HEAD_DIM, n_blocks, MOBA_BLOCK), f32),
            pltpu.VMEM((n_blocks, MOBA_BLOCK), f32),
        ],
        compiler_params=_params("parallel", "parallel", "arbitrary"),
        name="moba_attention",
    )(proj, proj, proj)


def _ret_kernel(q_ref, k_ref, v0_ref, v1_ref, g0_ref, g1_ref, decay_ref, zeta_ref, xi_ref, cd_ref,
                o_ref, state_ref):
    c = pl.program_id(1)
    heads_per_half = RET_HEADS // 2
    v_refs = (v0_ref, v1_ref)
    g_refs = (g0_ref, g1_ref)

    @pl.when(c == 0)
    def _():
        state_ref[...] = jnp.zeros_like(state_ref)

    for h in range(RET_HEADS):
        qk_cols = slice(h * RET_QK_DIM, (h + 1) * RET_QK_DIM)
        v_cols = slice(h * RET_V_DIM, (h + 1) * RET_V_DIM)
        half_cols = slice((h % heads_per_half) * RET_V_DIM, (h % heads_per_half + 1) * RET_V_DIM)
        q = q_ref[:, qk_cols]
        k = k_ref[:, qk_cols]
        v = v_refs[h // heads_per_half][:, half_cols]
        state = state_ref[h]
        scores = _dot_nt(q, k) * decay_ref[h]
        inner = _dot(scores.astype(bf16), v)
        cross = _dot(q, state.astype(bf16)) * xi_ref[h]
        o = inner + cross
        kz = (k.astype(f32) * zeta_ref[h]).astype(bf16)
        state_ref[h] = cd_ref[h] * state + _dot_tn(kz, v)

        mu = jnp.mean(o, axis=-1, keepdims=True)
        oc = o - mu
        var = jnp.mean(oc * oc, axis=-1, keepdims=True)
        on = oc * lax.rsqrt(var + GN_EPS)
        g = g_refs[h // heads_per_half][:, half_cols].astype(f32)
        o_ref[:, v_cols] = (on * (g * _sigmoid(g))).astype(o_ref.dtype)


def _retention(proj, batch, seq):
    tokens = batch * seq
    C = RET_CHUNK
    n_chunks = seq // C
    half_v = RET_V_WIDTH // 2
    gammas =1.0 - 2.0 ** (-5.0 - jnp.arange(RET_HEADS, dtype=f32))
    log_g = jnp.log(gammas)
    idx = jnp.arange(C, dtype=f32)
    diff = idx[:, None] - idx[None, :]
    decay = jnp.where(diff >= 0, jnp.exp(jnp.maximum(diff, 0.0)[None] * log_g[:, None, None]), 0.0)
    zeta = jnp.exp((C - 1 - idx)[None, :] * log_g[:, None])
    xi = jnp.exp((idx + 1.0)[None, :] * log_g[:, None])
    zeta_t = jnp.broadcast_to(zeta[:, :, None], (RET_HEADS, C, RET_QK_DIM))
    xi_t = jnp.broadcast_to(xi[:, :, None], (RET_HEADS, C, RET_V_DIM))
    cd_t = jnp.broadcast_to(jnp.exp(C * log_g)[:, None, None], (RET_HEADS, 1, RET_V_DIM))
    return pl.pallas_call(
        _ret_kernel,
        grid=(batch, n_chunks),
        in_specs=[
            pl.BlockSpec((C, RET_QK_WIDTH), lambda b, c: (b * n_chunks + c, OFF_QR // RET_QK_WIDTH)),
            pl.BlockSpec((C, RET_QK_WIDTH), lambda b, c: (b * n_chunks + c, OFF_KR // RET_QK_WIDTH)),
            pl.BlockSpec((C, half_v), lambda b, c: (b * n_chunks + c, OFF_VR // half_v)),
            pl.BlockSpec((C, half_v), lambda b, c: (b * n_chunks + c, OFF_VR // half_v + 1)),
            pl.BlockSpec((C, half_v), lambda b, c: (b * n_chunks + c, OFF_GR // half_v)),
            pl.BlockSpec((C, half_v), lambda b, c: (b * n_chunks + c, OFF_GR // half_v + 1)),
            pl.BlockSpec((RET_HEADS, C, C), lambda b, c: (0, 0, 0)),
            pl.BlockSpec((RET_HEADS, C, RET_QK_DIM), lambda b, c: (0, 0, 0)),
            pl.BlockSpec((RET_HEADS, C, RET_V_DIM), lambda b, c: (0, 0, 0)),
            pl.BlockSpec((RET_HEADS, 1, RET_V_DIM), lambda b, c: (0, 0, 0)),
        ],
        out_specs=pl.BlockSpec((C, RET_V_WIDTH), lambda b, c: (b * n_chunks + c, 0)),
        out_shape=jax.ShapeDtypeStruct((tokens, RET_V_WIDTH), bf16),
        scratch_shapes=[pltpu.VMEM((RET_HEADS, RET_QK_DIM, RET_V_DIM), f32)],
        compiler_params=_params("parallel", "arbitrary"),
        name="retention",
    )(proj, proj, proj, proj, proj, proj, decay, zeta_t, xi_t, cd_t)


def _merge_router_kernel(oa_ref, orr_ref, ga0_ref, ga1_ref, gb0_ref, gb1_ref, x_ref,
                         wm_ref, wr_ref, wo_ref, g1_ref, b1_ref, wrt_hi_ref, wrt_lo_ref, rb_ref,
                         h_ref, pos_ref, w_ref, tab_ref, cnt_ref, carry_ref):
    step = pl.program_id(0)
    tm = TOK_TM

    @pl.when(step == 0)
    def _():
        carry_ref[...] = jnp.zeros_like(carry_ref)

    for t in range(x_ref.shape[0] // tm):
        rows = slice(t * tm, (t + 1) * tm)
        _merge_router_tile(t, rows, tm, oa_ref, orr_ref, ga0_ref, ga1_ref, gb0_ref, gb1_ref, x_ref,
                           wm_ref, wr_ref, wo_ref, g1_ref, b1_ref, wrt_hi_ref, wrt_lo_ref, rb_ref,
                           h_ref, pos_ref, w_ref, tab_ref, cnt_ref, carry_ref)


def _merge_router_tile(t, rows, tm, oa_ref, orr_ref, ga0_ref, ga1_ref, gb0_ref, gb1_ref, x_ref,
                       wm_ref, wr_ref, wo_ref, g1_ref, b1_ref, wrt_hi_ref, wrt_lo_ref, rb_ref,
                       h_ref, pos_ref, w_ref, tab_ref, cnt_ref, carry_ref):
    a = _dot(oa_ref[rows, :], wm_ref[...])
    r = _dot(orr_ref[rows, :], wr_ref[...])
    ga = jnp.concatenate([ga0_ref[rows, :], ga1_ref[rows, :]], axis=1).astype(f32)
    gb = jnp.concatenate([gb0_ref[rows, :], gb1_ref[rows, :]], axis=1).astype(f32)
    merged = _sigmoid(ga) * a + _sigmoid(gb) * r
    mix = _dot(merged.astype(bf16), wo_ref[...])
    h = _layer_norm(ALPHA * x_ref[rows, :] + mix, g1_ref[...], b1_ref[...])
    h_ref[rows, :] = h

    h_hi, h_lo = _split_bf16(h)
    w_hi = wrt_hi_ref[...]
    logits = _dot_nt(w_hi, h_hi) + _dot_nt(w_hi, h_lo) + _dot_nt(wrt_lo_ref[...], h_hi)
    scores = _sigmoid(logits)
    biased = scores + rb_ref[...]

    v = biased.reshape(N_GROUPS, GROUP_SIZE, tm)
    sub = lax.broadcasted_iota(i32, v.shape, 1)
    m1 = jnp.max(v, axis=1, keepdims=True)
    i1 = jnp.min(jnp.where(v == m1, sub, GROUP_SIZE), axis=1, keepdims=True)
    m2 = jnp.max(jnp.where(sub == i1, -jnp.inf, v), axis=1, keepdims=True)
    gscore = jnp.broadcast_to(m1 + m2, v.shape).reshape(N_EXPERTS, tm)

    eid = lax.broadcasted_iota(i32, (N_EXPERTS, tm), 0)
    egrp = eid // GROUP_SIZE
    e_mask = jnp.zeros((N_EXPERTS, tm), jnp.bool_)
    for _ in range(TOPK_GROUPS):
        m = jnp.max(gscore, axis=0, keepdims=True)
        idx = jnp.min(jnp.where(gscore == m, egrp, N_GROUPS), axis=0, keepdims=True)
        hit = egrp == idx
        e_mask = e_mask | hit
        gscore = jnp.where(hit, -jnp.inf, gscore)
    cand = jnp.where(e_mask, biased, -jnp.inf)

    chosen = jnp.zeros((N_EXPERTS, tm), jnp.bool_)
    e_rows = []
    w_rows = []
    for _ in range(TOP_K):
        m = jnp.max(cand, axis=0, keepdims=True)
        idx = jnp.min(jnp.where(cand == m, eid, N_EXPERTS), axis=0, keepdims=True)
        hit = eid == idx
        chosen = chosen | hit
        e_rows.append(idx)
        w_rows.append(jnp.sum(jnp.where(hit, scores, 0.0), axis=0, keepdims=True))
        cand = jnp.where(hit, -jnp.inf, cand)
    w_sum = w_rows[0]
    for wk in w_rows[1:]:
        w_sum = w_sum + wk

    t_src = lax.broadcasted_iota(i32, (tm, tm), 0)
    t_dst = lax.broadcasted_iota(i32, (tm, tm), 1)
    before = (t_src < t_dst).astype(bf16)
    chosen_f = chosen.astype(f32)
    prior = _dot(chosen_f.astype(bf16), before)
    cnt = jnp.sum(chosen_f, axis=1, keepdims=True)
    cnt_pad = jnp.ceil(cnt * (1.0 / UNIT)) * UNIT
    e_src = lax.broadcasted_iota(i32, (N_EXPERTS, N_EXPERTS), 1)
    e_dst = lax.broadcasted_iota(i32, (N_EXPERTS, N_EXPERTS), 0)
    earlier = (e_src < e_dst).astype(bf16)
    cnt_pad_l = jnp.broadcast_to(cnt_pad, (N_EXPERTS, LANES))
    loc_start_l = _dot(earlier, cnt_pad_l.astype(bf16))
    loc_start = loc_start_l[:, 0:1]
    where_e = prior + loc_start
    pos_ref[:, rows] = jnp.zeros((pos_ref.shape[0], tm), pos_ref.dtype)
    w_ref[:, rows] = jnp.zeros((w_ref.shape[0], tm), w_ref.dtype)
    for k in range(TOP_K):
        pos_ref[k:k + 1, rows] = jnp.sum(jnp.where(eid == e_rows[k], where_e, 0.0), axis=0, keepdims=True).astype(i32)
        w_ref[k:k + 1, rows] = w_rows[k] / w_sum * ROUTED_SCALE

    carry = carry_ref[...]
    tl = lax.broadcasted_iota(i32, (N_EXPERTS, LANES), 1)
    table = jnp.where(tl == 0, cnt_pad_l, jnp.where(tl == 1, loc_start_l, jnp.broadcast_to(carry, (N_EXPERTS, LANES))))
    tab_ref[t] = table.astype(i32)
    carry = carry + cnt_pad
    carry_ref[...] = carry
    cnt_ref[...] = jnp.broadcast_to(carry, cnt_ref.shape).astype(i32)


def _merge_router(oa, orr, proj, x2d, wm, wr, wo, g1, b1, wrt_hi, wrt_lo, rb):
    tokens = x2d.shape[0]
    tm = MERGE_TILES * TOK_TM
    half = D_MODEL // 2

    def gate_spec(off):
        return pl.BlockSpec((tm, half), lambda i: (i, off // half))

    def full(shape):
        return pl.BlockSpec(shape, lambda i: (0,) * len(shape))

    row8 = pl.BlockSpec((8, tm), lambda i: (0, i))
    return pl.pallas_call(
        _merge_router_kernel,
        grid=(tokens // tm,),
        in_specs=[
            pl.BlockSpec((tm, MOBA_WIDTH), lambda i: (i, 0)),
            pl.BlockSpec((tm, RET_V_WIDTH), lambda i: (i, 0)),
            gate_spec(OFF_GA), gate_spec(OFF_GA + half), gate_spec(OFF_GB), gate_spec(OFF_GB + half),
            pl.BlockSpec((tm, D_MODEL), lambda i: (i, 0)),
            full((MOBA_WIDTH, D_MODEL)), full((RET_V_WIDTH, D_MODEL)), full((D_MODEL, D_MODEL)),
            full((1, D_MODEL)), full((1, D_MODEL)),
            full((N_EXPERTS, D_MODEL)), full((N_EXPERTS, D_MODEL)), full((N_EXPERTS, 1)),
        ],
        out_specs=[
            pl.BlockSpec((tm, D_MODEL), lambda i: (i, 0)),
            row8, row8,
            pl.BlockSpec((MERGE_TILES, N_EXPERTS, LANES), lambda i: (i, 0, 0)),
            full((N_EXPERTS, LANES)),
        ],
        out_shape=[
            jax.ShapeDtypeStruct((tokens, D_MODEL), f32),
            jax.ShapeDtypeStruct((8, tokens), i32),
            jax.ShapeDtypeStruct((8, tokens), f32),
            jax.ShapeDtypeStruct((tokens // TOK_TM, N_EXPERTS, LANES), i32),
            jax.ShapeDtypeStruct((N_EXPERTS, LANES), i32),
        ],
        scratch_shapes=[pltpu.VMEM((N_EXPERTS, 1), f32)],
        compiler_params=_params("arbitrary"),
        name="merge_ln1_router",
    )(oa, orr, proj, proj, proj, proj, x2d, wm, wr, wo, g1, b1, wrt_hi, wrt_lo, rb)


def _pack_rows(x):
    bits = lax.bitcast_convert_type(x, jnp.uint32)
    return (bits[:, :HALF] & jnp.uint32(0xFFFF0000)) | (bits[:, HALF:] >> 16)


def _unpack_rows(words):
    left = lax.bitcast_convert_type(words & jnp.uint32(0xFFFF0000), f32)
    right = lax.bitcast_convert_type(words << 16, f32)
    return left.astype(bf16), right.astype(bf16)


def _unit_slices(local_unit, global_unit):
    loc = pl.ds(pl.multiple_of(local_unit * UNIT, UNIT), UNIT)
    glob = pl.ds(pl.multiple_of(global_unit * UNIT, UNIT), UNIT)
    return loc, glob


def _start_alternating(n, make_copy):
    def pair(m, _):
        make_copy(2 * m).start(priority=0)
        make_copy(2 * m + 1).start(priority=1)
        return 0

    lax.fori_loop(0, n // 2, pair, 0)

    @pl.when(n % 2 == 1)
    def _():
        make_copy(n - 1).start(priority=0)


def _dispatch_kernel(gu_ref, nu_ref, tail_ref, used_ref, h_ref, pos_ref, rows_ref,
                     xs_ref, zero_ref, sem_ref, zsem_ref):
    step = pl.program_id(0)
    tm = h_ref.shape[0]

    @pl.when(step == 0)
    def _():
        zero_ref[...] = jnp.zeros_like(zero_ref)

        def clear_copy(block):
            start = pl.multiple_of(block * ROW_BLOCK, ROW_BLOCK)
            return pltpu.make_async_copy(zero_ref, rows_ref.at[pl.ds(start, ROW_BLOCK), :], zsem_ref)

        def clear_all(act):
            def tail(e, _):
                tail_block = tail_ref[e]

                @pl.when(tail_block >= 0)
                def _():
                    act(clear_copy(tail_block))
                return 0

            lax.fori_loop(0, N_EXPERTS, tail, 0)

            def unused(b, _):
                act(clear_copy(b))
                return 0

            lax.fori_loop(used_ref[0], rows_ref.shape[0] // ROW_BLOCK, unused, 0)

        clear_all(lambda cp: cp.start())
        clear_all(lambda cp: cp.wait())

    def unit_copy(t, s):
        loc, glob = _unit_slices(s, gu_ref[t * LOC_UNITS + s])
        return pltpu.make_async_copy(xs_ref.at[t % 2, loc, :], rows_ref.at[glob, :], sem_ref.at[t % 2])

    def drain(t):
        def wait_one(s, _):
            unit_copy(t, s).wait()
            return 0

        lax.fori_loop(0, nu_ref[t], wait_one, 0)

    @pl.when(step >= 2)
    def _():
        drain(step - 2)

    pos = pos_ref[...]
    r_iota = lax.broadcasted_iota(i32, (LOC_ROWS, tm), 0)
    onehot = r_iota == pos[0:1, :]
    for k in range(1, TOP_K):
        onehot = onehot | (r_iota == pos[k:k + 1, :])
    xs = _dot(onehot.astype(bf16), h_ref[...].astype(bf16))
    xs_ref[step % 2] = _pack_rows(xs)

    _start_alternating(nu_ref[step], lambda s: unit_copy(step, s))

    @pl.when(step == pl.num_programs(0) - 1)
    def _():
        @pl.when(step >= 1)
        def _():
            drain(step - 1)

        drain(step)


def _dispatch(unit_map, n_units, tail_block, n_used, h, pos8, n_rows):
    tokens = h.shape[0]
    tm = TOK_TM
    grid_spec = pltpu.PrefetchScalarGridSpec(
        num_scalar_prefetch=4,
        grid=(tokens // tm,),
        in_specs=[
            pl.BlockSpec((tm, D_MODEL), lambda i, *_: (i, 0)),
            pl.BlockSpec((8, tm), lambda i, *_: (0, i)),
        ],
        out_specs=pl.BlockSpec(memory_space=pl.ANY),
        scratch_shapes=[
            pltpu.VMEM((2, LOC_ROWS, HALF), jnp.uint32),
            pltpu.VMEM((ROW_BLOCK, HALF), jnp.uint32),
            pltpu.SemaphoreType.DMA((2,)),
            pltpu.SemaphoreType.DMA,
        ],
    )
    return pl.pallas_call(
        _dispatch_kernel,
        grid_spec=grid_spec,
        out_shape=jax.ShapeDtypeStruct((n_rows, HALF), jnp.uint32),
        compiler_params=_params("arbitrary"),
        name="moe_dispatch",
    )(unit_map, n_units, tail_block, n_used, h, pos8)


def _expert_kernel(eid_ref, used_ref, x_ref, wg_ref, wu_ref, wd_ref, y_ref, wgb_ref, wub_ref, wdb_ref):
    b = pl.program_id(0)
    changed = (b == 0) | (eid_ref[b] != eid_ref[jnp.maximum(b - 1, 0)])

    @pl.when(changed)
    def _():
        wgb_ref[...] = wg_ref[0].astype(bf16)
        wub_ref[...] = wu_ref[0].astype(bf16)
        wdb_ref[...] = wd_ref[0].astype(bf16)

    @pl.when(b < used_ref[0])
    def _():
        for c in range(ROW_BLOCK // EXPERT_CHUNK):
            rows = slice(c * EXPERT_CHUNK, (c + 1) * EXPERT_CHUNK)
            xl, xr = _unpack_rows(x_ref[rows, :])
            g = _dot(xl, wgb_ref[:HALF, :]) + _dot(xr, wgb_ref[HALF:, :])
            u = _dot(xl, wub_ref[:HALF, :]) + _dot(xr, wub_ref[HALF:, :])
            mid = (g * _sigmoid(g)) * u
            y = _dot(mid.astype(bf16), wdb_ref[...])
            y_ref[rows, :] = _pack_rows(y.astype(bf16).astype(f32))

    @pl.when(b >= used_ref[0])
    def _():
        y_ref[...] = jnp.zeros_like(y_ref)


def _experts(block_eid, n_used, rows, wg, wu, wd):
    n_rows = rows.shape[0]
    n_blocks = n_rows // ROW_BLOCK

    def row_map(b, eid, used):
        return (b, 0)

    def w_map(b, eid, used):
        return (eid[b], 0, 0)

    grid_spec = pltpu.PrefetchScalarGridSpec(
        num_scalar_prefetch=2,
        grid=(n_blocks,),
        in_specs=[
            pl.BlockSpec((ROW_BLOCK, HALF), row_map),
            pl.BlockSpec((1, D_MODEL, D_EXPERT), w_map),
            pl.BlockSpec((1, D_MODEL, D_EXPERT), w_map),
            pl.BlockSpec((1, D_EXPERT, D_MODEL), w_map),
        ],
        out_specs=pl.BlockSpec((ROW_BLOCK, HALF), row_map),
        scratch_shapes=[
            pltpu.VMEM((D_MODEL, D_EXPERT), bf16),
            pltpu.VMEM((D_MODEL, D_EXPERT), bf16),
            pltpu.VMEM((D_EXPERT, D_MODEL), bf16),
        ],
    )
    return pl.pallas_call(
        _expert_kernel,
        grid_spec=grid_spec,
        out_shape=jax.ShapeDtypeStruct((n_rows, HALF), jnp.uint32),
        compiler_params=_params("arbitrary"),
        name="moe_experts",
    )(block_eid, n_used, rows, wg, wu, wd)


def _combine_kernel(gu_ref, nu_ref, y_ref, h_ref, pos_ref, wt_ref, p_ref, wsg_ref, wsu_ref, wsd_ref, wpp_ref,
                    wpg_ref, g2_ref, b2_ref, g3_ref, b3_ref, o_ref, ybuf_ref, sem_ref):
    step = pl.program_id(0)
    tm = h_ref.shape[0]

    @pl.when(step == 0)
    def _():
        ybuf_ref[...] = jnp.zeros_like(ybuf_ref)

    def unit_copy(s):
        loc, glob = _unit_slices(s, gu_ref[step * LOC_UNITS + s])
        return pltpu.make_async_copy(y_ref.at[glob, :], ybuf_ref.at[loc, :], sem_ref)

    _start_alternating(nu_ref[step], unit_copy)

    h = h_ref[...]
    hb = h.astype(bf16)
    sg = _dot(hb, wsg_ref[...])
    su = _dot(hb, wsu_ref[...])
    shared = _dot(((sg * _sigmoid(sg)) * su).astype(bf16), wsd_ref[...])
    ple_in = _dot(p_ref[...].astype(bf16), wpp_ref[...])

    pos = pos_ref[...]
    wt = wt_ref[...]
    c_iota = lax.broadcasted_iota(i32, (tm, LOC_ROWS), 1)
    spread = jnp.where(c_iota == pos[:, 0:1], wt[:, 0:1], 0.0)
    for k in range(1, TOP_K):
        spread = jnp.where(c_iota == pos[:, k:k + 1], wt[:, k:k + 1], spread)
    spread = spread.astype(bf16)

    def wait_one(s, _):
        unit_copy(s).wait()
        return 0

    lax.fori_loop(0, nu_ref[step], wait_one, 0)

    yl, yr = _unpack_rows(ybuf_ref[...])
    routed = jnp.concatenate([_dot(spread, yl), _dot(spread, yr)], axis=1)

    h2 = _layer_norm(ALPHA * h + (routed + shared), g2_ref[...], b2_ref[...])
    ple = ple_in * _sigmoid(_dot(h2.astype(bf16), wpg_ref[...]))
    o_ref[...] = _layer_norm(ALPHA * h2 + ple, g3_ref[...], b3_ref[...])


def _combine(unit_map, n_units, y_rows, h, pos_tok, w_tok, p2d, wsg, wsu, wsd, wpp, wpg, g2, b2, g3, b3):
    tokens = h.shape[0]
    tm = TOK_TM

    def full(shape):
        return pl.BlockSpec(shape, lambda i, *_: (0,) * len(shape))

    def tile(width):
        return pl.BlockSpec((tm, width), lambda i, *_: (i, 0))

    grid_spec = pltpu.PrefetchScalarGridSpec(
        num_scalar_prefetch=2,
        grid=(tokens // tm,),
        in_specs=[
            pl.BlockSpec(memory_space=pl.ANY),
            tile(D_MODEL), tile(8), tile(8), tile(PLE_DIM),
            full((D_MODEL, D_SHARED)), full((D_MODEL, D_SHARED)), full((D_SHARED, D_MODEL)),
            full((PLE_DIM, D_MODEL)), full((D_MODEL, D_MODEL)),
            full((1, D_MODEL)), full((1, D_MODEL)), full((1, D_MODEL)), full((1, D_MODEL)),
        ],
        out_specs=tile(D_MODEL),
        scratch_shapes=[
            pltpu.VMEM((LOC_ROWS, HALF), jnp.uint32),
            pltpu.SemaphoreType.DMA,
        ],
    )
    return pl.pallas_call(
        _combine_kernel,
        grid_spec=grid_spec,
        out_shape=jax.ShapeDtypeStruct((tokens, D_MODEL), f32),
        compiler_params=_params("arbitrary"),
        name="moe_combine_ln2_ple_ln3",
    )(unit_map, n_units, y_rows, h, pos_tok, w_tok, p2d, wsg, wsu, wsd, wpp, wpg, g2, b2, g3, b3)


def _row_layout(table, totals, n_blocks):
    cnt_u = table[:, :, 0] // UNIT
    loc_u = table[:, :, 1] // UNIT
    padded = (totals + ROW_BLOCK - 1) // ROW_BLOCK * ROW_BLOCK
    pends = jnp.cumsum(padded)
    pstarts = pends - padded
    glob_u = (pstarts[None, :] + table[:, :, 2]) // UNIT
    n_units = jnp.sum(cnt_u, axis=1).astype(i32)
    s = jnp.arange(LOC_UNITS, dtype=i32)
    loc_end = loc_u + cnt_u
    e_of = jnp.sum((loc_end[:, None, :] <= s[None, :, None]).astype(i32), axis=-1)
    is_e = e_of[:, :, None] == jnp.arange(N_EXPERTS, dtype=i32)
    shift = jnp.sum(jnp.where(is_e, (glob_u - loc_u)[:, None, :], 0), axis=-1)
    unit_map = (shift + s[None, :]).astype(i32).reshape(-1)
    tail_block = jnp.where(totals > 0, pends // ROW_BLOCK - 1, -1).astype(i32)
    block_start = jnp.arange(n_blocks, dtype=i32) * ROW_BLOCK
    ends_before = jnp.sum((pends[None, :] <= block_start[:, None]).astype(i32), axis=1)
    block_eid = jnp.minimum(ends_before, N_EXPERTS - 1).astype(i32)
    n_used = (pends[-1:] // ROW_BLOCK).astype(i32)
    return unit_map, n_units, tail_block, block_eid, n_used


def _layer(x2d, p2d, batch, seq, w_in, w_moba_up, w_ret_up, w_out, ln1_g, ln1_b, w_router, router_bias,
           w_eg, w_eu, w_ed, w_sg, w_su, w_sd, ln2_g, ln2_b, w_ple_proj, w_ple_gate, ln3_g, ln3_b):
    tokens = batch * seq
    row = lambda a: a.reshape(1, -1).astype(f32)

    proj = _project(x2d, w_in.astype(bf16), seq)
    oa = _moba(proj, batch, seq)
    orr = _retention(proj, batch, seq)

    wrt = w_router.astype(f32).T
    wrt_hi = wrt.astype(bf16)
    wrt_lo = (wrt - wrt_hi.astype(f32)).astype(bf16)
    h1, pos8, w8, table, totals = _merge_router(
        oa, orr, proj, x2d, w_moba_up.astype(bf16), w_ret_up.astype(bf16), w_out.astype(bf16),
        row(ln1_g), row(ln1_b), wrt_hi, wrt_lo, router_bias.astype(f32).reshape(N_EXPERTS, 1))

    n_tiles = tokens // TOK_TM
    max_rows = tokens * TOP_K + n_tiles * N_EXPERTS * (UNIT - 1) + N_EXPERTS * (ROW_BLOCK - 1)
    n_blocks = -(-max_rows // ROW_BLOCK)
    unit_map, n_units, tail_block, block_eid, n_used = _row_layout(table[:, :, :3], totals[:, 0], n_blocks)

    rows = _dispatch(unit_map, n_units, tail_block, n_used, h1, pos8, n_blocks * ROW_BLOCK)
    y_rows = _experts(block_eid, n_used, rows, w_eg, w_eu, w_ed)
    return _combine(unit_map, n_units, y_rows, h1, pos8.T, w8.T, p2d,
                    w_sg.astype(bf16), w_su.astype(bf16), w_sd.astype(bf16),
                    w_ple_proj.astype(bf16), w_ple_gate.astype(bf16),
                    row(ln2_g), row(ln2_b), row(ln3_g), row(ln3_b))


def kernel(x, p, w_in, w_moba_up, w_ret_up, w_out, ln1_g, ln1_b, w_router, router_bias, w_exp_gate, w_exp_up,
           w_exp_down, w_sh_gate, w_sh_up, w_sh_down, ln2_g, ln2_b, w_ple_proj, w_ple_gate, ln3_g, ln3_b):
    batch, seq, d = x.shape
    assert d == D_MODEL and seq % max(MOBA_BLOCK, RET_CHUNK) == 0 and (batch * seq) % (MERGE_TILES * TOK_TM) == 0
    assert w_in.shape[0] == DEPTH
    h = x.reshape(batch * seq, d)
    for i in range(DEPTH):
        h = _layer(h, p[i].reshape(batch * seq, PLE_DIM), batch, seq,
                   w_in[i], w_moba_up[i], w_ret_up[i], w_out[i], ln1_g[i], ln1_b[i], w_router[i], router_bias[i],
                   w_exp_gate[i], w_exp_up[i], w_exp_down[i], w_sh_gate[i], w_sh_up[i], w_sh_down[i],
                   ln2_g[i], ln2_b[i], w_ple_proj[i], w_ple_gate[i], ln3_g[i], ln3_b[i])
    return h.reshape(batch, seq, d)
```

```python
import functools

import jax
import jax.numpy as jnp
from jax import lax
from jax.experimental import pallas as pl
from jax.experimental.pallas import tpu as pltpu

f32 = jnp.float32
bf16 = jnp.bfloat16
i32 = jnp.int32

D_MODEL = 1024
DEPTH = 1
MOBA_HEADS = 8
MOBA_HEAD_DIM = 64
MOBA_WIDTH = MOBA_HEADS * MOBA_HEAD_DIM
MOBA_BLOCK = 256
MOBA_TOPK = 3
ROPE_THETA = 10000.0
RET_HEADS = 4
RET_QK_DIM = 128
RET_V_DIM = 256
RET_QK_WIDTH = RET_HEADS * RET_QK_DIM
RET_V_WIDTH = RET_HEADS * RET_V_DIM
RET_CHUNK = 256
RET_ANGLE_BASE = 10000.0
N_IN = 3 * MOBA_WIDTH + 2 * RET_QK_WIDTH + 2 * RET_V_WIDTH + 2 * D_MODEL
N_EXPERTS = 64
TOP_K = 6
N_GROUPS = 8
TOPK_GROUPS = 4
GROUP_SIZE = N_EXPERTS // N_GROUPS
D_EXPERT = 256
D_SHARED = 256
ROUTED_SCALE = 2.5
PLE_DIM = 256
LN_EPS = 1e-5
GN_EPS = 1e-6
ALPHA = (2.0 * DEPTH) ** 0.25

OFF_QA = 0
OFF_KA = MOBA_WIDTH
OFF_VA = 2 * MOBA_WIDTH
OFF_QR = 3 * MOBA_WIDTH
OFF_KR = OFF_QR + RET_QK_WIDTH
OFF_VR = OFF_KR + RET_QK_WIDTH
OFF_GR = OFF_VR + RET_V_WIDTH
OFF_GA = OFF_GR + RET_V_WIDTH
OFF_GB = OFF_GA + D_MODEL

LANES = 128
NEG = -1e30
VMEM_LIMIT = 56 * 1024 * 1024

PROJ_TM = 1024
PROJ_TN = 512
MOBA_STEP_WIDTH = 256
MERGE_TILES = 2
TOK_TM = 256
ROW_BLOCK = 512
EXPERT_CHUNK = 256
UNIT = 8
LOC_ROWS = -(-(TOP_K * TOK_TM + N_EXPERTS * (UNIT - 1)) // LANES) * LANES
LOC_UNITS = LOC_ROWS // UNIT
ISSUE_BURSTS = 8
HALF = D_MODEL // 2


def _dot(a, b):
    return jnp.dot(a, b, preferred_element_type=f32)


def _dot_nt(a, b):
    return lax.dot_general(a, b, (((1,), (1,)), ((), ())), preferred_element_type=f32)


def _dot_tn(a, b):
    return lax.dot_general(a, b, (((0,), (0,)), ((), ())), preferred_element_type=f32)


def _split_bf16(a):
    hi = a.astype(bf16)
    lo = (a - hi.astype(f32)).astype(bf16)
    return hi, lo


def _layer_norm(x, g, b):
    mu = jnp.mean(x, axis=-1, keepdims=True)
    xc = x - mu
    var = jnp.mean(xc * xc, axis=-1, keepdims=True)
    return xc * lax.rsqrt(var + LN_EPS) * g + b


def _sigmoid(x):
    return 1.0 / (1.0 + jnp.exp(-x))


def _params(*sem, flags=None):
    return pltpu.CompilerParams(dimension_semantics=sem, vmem_limit_bytes=VMEM_LIMIT, flags=flags)


def _rotate_half_chunk(xc, half):
    if 2 * half == LANES:
        return pltpu.roll(xc, half, axis=1)
    lane = lax.broadcasted_iota(i32, xc.shape, 1)
    first = (lane & (2 * half - 1)) < half
    return jnp.where(first, pltpu.roll(xc, LANES - half, axis=1), pltpu.roll(xc, half, axis=1))


def _proj_kernel(x_ref, w_ref, cos_a_ref, sin_a_ref, cos_r_ref, sin_r_ref, o_ref, xb_ref):
    j = pl.program_id(1)

    @pl.when(j == 0)
    def _():
        xb_ref[...] = x_ref[...].astype(bf16)

    acc = _dot(xb_ref[...], w_ref[...])

    def rotary(cos_ref, sin_ref, half, scale):
        cos = cos_ref[...]
        sin = sin_ref[...]
        for c in range(PROJ_TN // LANES):
            xc = acc[:, c * LANES:(c + 1) * LANES]
            y = xc * cos + _rotate_half_chunk(xc, half) * sin
            if scale != 1.0:
                y = y * scale
            o_ref[:, c * LANES:(c + 1) * LANES] = y.astype(o_ref.dtype)

    j_qa = OFF_QA // PROJ_TN
    j_ka = OFF_KA // PROJ_TN
    j_qr = OFF_QR // PROJ_TN
    j_kr = OFF_KR // PROJ_TN

    @pl.when(j == j_qa)
    def _():
        rotary(cos_a_ref, sin_a_ref, MOBA_HEAD_DIM // 2, MOBA_HEAD_DIM ** -0.5)

    @pl.when(j == j_ka)
    def _():
        rotary(cos_a_ref, sin_a_ref, MOBA_HEAD_DIM // 2, 1.0)

    @pl.when(j == j_qr)
    def _():
        rotary(cos_r_ref, sin_r_ref, RET_QK_DIM // 2, 1.0)

    @pl.when(j == j_kr)
    def _():
        rotary(cos_r_ref, sin_r_ref, RET_QK_DIM // 2, RET_QK_DIM ** -0.5)

    plain = (j != j_qa) & (j != j_ka) & (j != j_qr) & (j != j_kr)

    @pl.when(plain)
    def _():
        o_ref[...] = acc.astype(o_ref.dtype)


def _rotary_tables(seq, inv_freq, head_dim):
    ang = jnp.arange(seq, dtype=f32)[:, None] * inv_freq[None, :]
    cos = jnp.cos(ang)
    sin = jnp.sin(ang)
    cos_h = jnp.concatenate([cos, cos], axis=-1)
    sin_h = jnp.concatenate([-sin, sin], axis=-1)
    reps = LANES // head_dim
    return jnp.tile(cos_h, (1, reps)), jnp.tile(sin_h, (1, reps))


def _project(x2d, w_in_b, seq):
    tokens = x2d.shape[0]
    tm = min(PROJ_TM, seq)
    inv_a = 1.0 / (ROPE_THETA ** (jnp.arange(0, MOBA_HEAD_DIM, 2, dtype=f32) / MOBA_HEAD_DIM))
    inv_r = 1.0 / (RET_ANGLE_BASE ** jnp.linspace(0.0, 1.0, RET_QK_DIM // 2, dtype=f32))
    cos_a, sin_a = _rotary_tables(seq, inv_a, MOBA_HEAD_DIM)
    cos_r, sin_r = _rotary_tables(seq, inv_r, RET_QK_DIM)
    seq_tiles = seq // tm
    tab = pl.BlockSpec((tm, LANES), lambda i, j: (i % seq_tiles, 0))
    return pl.pallas_call(
        _proj_kernel,
        grid=(tokens // tm, N_IN // PROJ_TN),
        in_specs=[
            pl.BlockSpec((tm, D_MODEL), lambda i, j: (i, 0)),
            pl.BlockSpec((D_MODEL, PROJ_TN), lambda i, j: (0, j)),
            tab, tab, tab, tab,
        ],
        out_specs=pl.BlockSpec((tm, PROJ_TN), lambda i, j: (i, j)),
        out_shape=jax.ShapeDtypeStruct((tokens, N_IN), bf16),
        scratch_shapes=[pltpu.VMEM((tm, D_MODEL), bf16)],
        compiler_params=_params("parallel", "arbitrary"),
        name="in_proj_rotary",
    )(x2d, w_in_b, cos_a, sin_a, cos_r, sin_r)


def _moba_kernel(q_ref, k_ref, v_ref, o_ref, kmean_ref, vt_ref, selb_ref, gate_ref, *, n_blocks):
    i = pl.program_id(2)
    blk = MOBA_BLOCK
    hd = MOBA_HEAD_DIM
    width = q_ref.shape[1]
    chunks = width // LANES
    heads = width // hd
    per_chunk = LANES // hd

    def chunk_of(h):
        return slice(h // per_chunk * LANES, (h // per_chunk + 1) * LANES)

    @pl.when(i == 0)
    def _():
        seq = n_blocks * blk
        member = (lax.broadcasted_iota(i32, (n_blocks, seq), 1) // blk
                  == lax.broadcasted_iota(i32, (n_blocks, seq), 0))
        kmean_ref[...] = _dot(member.astype(bf16), k_ref[...]) * (1.0 / blk)
        for j in range(n_blocks):
            vt_ref[j] = v_ref[j * blk:(j + 1) * blk, :].astype(f32).T.astype(bf16)

    km_hi, km_lo = _split_bf16(kmean_ref[...])
    key_i = lax.broadcasted_iota(i32, (blk, blk), 0)
    qry_i = lax.broadcasted_iota(i32, (blk, blk), 1)
    causal = key_i <= qry_i
    bid = lax.broadcasted_iota(i32, (n_blocks, blk), 0)
    d_row = lax.broadcasted_iota(i32, (LANES, blk), 0)
    qt = q_ref[...].astype(f32).T

    qts = []
    for h in range(heads):
        r0 = h % per_chunk * hd
        in_head = (d_row >= r0) & (d_row < r0 + hd)
        qth = jnp.where(in_head, qt[chunk_of(h), :], 0.0).astype(bf16)
        qts.append(qth)
        gate = _dot(km_hi[:, chunk_of(h)], qth) + _dot(km_lo[:, chunk_of(h)], qth)
        g = jnp.where(bid < i, gate, -jnp.inf)
        gate_ref[...] = g
        beaten = jnp.zeros((n_blocks, blk), f32)
        for j in range(n_blocks):
            gj = jnp.broadcast_to(gate_ref[j:j + 1, :], (n_blocks, blk))
            beats = (gj > g) | ((gj == g) & (bid > j))
            beaten = beaten + jnp.where(beats, 1.0, 0.0)
        sel = (beaten < MOBA_TOPK) & (bid < i)
        selb_ref[h] = jnp.where(sel, 0.0, NEG)

    kj = k_ref[pl.ds(pl.multiple_of(i * blk, blk), blk), :]
    vtj = vt_ref[i]
    state = []
    for h in range(heads):
        s = jnp.where(causal, _dot(kj[:, chunk_of(h)], qts[h]), NEG)
        m0 = jnp.max(s, axis=0, keepdims=True)
        p = jnp.exp(s - m0)
        l0 = jnp.sum(p, axis=0, keepdims=True)
        acc0 = _dot(vtj[h * hd:(h + 1) * hd, :], p.astype(bf16))
        state += [m0, l0, acc0]

    qt_alls = [jnp.concatenate(qts[c * per_chunk:(c + 1) * per_chunk], axis=1) for c in range(chunks)]

    def update(carry, js):
        nj = len(js)
        k_all = k_ref[pl.ds(pl.multiple_of(js[0] * blk, blk), nj * blk), :]
        s_alls = [_dot(k_all[:, c * LANES:(c + 1) * LANES], qt_alls[c]) for c in range(chunks)]
        vtjs = [vt_ref[j] for j in js]
        new = []
        for h in range(heads):
            m_old, l_old, acc = carry[3 * h:3 * h + 3]
            col = h % per_chunk * blk
            ss = [s_alls[h // per_chunk][n * blk:(n + 1) * blk, col:col + blk] + selb_ref[h, pl.ds(j, 1), :]
                  for n, j in enumerate(js)]
            m_new = m_old
            for s in ss:
                m_new = jnp.maximum(m_new, jnp.max(s, axis=0, keepdims=True))
            a = jnp.exp(m_old - m_new)
            l_new = a * l_old
            acc = a * acc
            for s, vtj in zip(ss, vtjs):
                p = jnp.exp(s - m_new)
                l_new = l_new + jnp.sum(p, axis=0, keepdims=True)
                acc = acc + _dot(vtj[h * hd:(h + 1) * hd, :], p.astype(bf16))
            new += [m_new, l_new, acc]
        return tuple(new)

    quads = i // 4
    fin = lax.fori_loop(0, quads, lambda n, c: update(c, tuple(4 * n + t for t in range(4))), tuple(state))
    done = 4 * quads
    has_pair = (i - done) // 2
    fin = lax.fori_loop(0, has_pair, lambda n, c: update(c, (done, done + 1)), fin)
    done = done + 2 * has_pair
    fin = lax.fori_loop(done, i, lambda j, c: update(c, (j,)), fin)
    out_t = jnp.concatenate([fin[3 * h + 2] / fin[3 * h + 1] for h in range(heads)], axis=0)
    o_ref[...] = out_t.T.astype(o_ref.dtype)


def _moba(proj, batch, seq):
    tokens = batch * seq
    n_blocks = seq // MOBA_BLOCK
    width = MOBA_STEP_WIDTH
    return pl.pallas_call(
        functools.partial(_moba_kernel, n_blocks=n_blocks),
        grid=(batch, MOBA_WIDTH // width, n_blocks),
        in_specs=[
            pl.BlockSpec((MOBA_BLOCK, width), lambda b, c, i: (b * n_blocks + i, OFF_QA // width + c)),
            pl.BlockSpec((seq, width), lambda b, c, i: (b, OFF_KA // width + c)),
            pl.BlockSpec((seq, width), lambda b, c, i: (b, OFF_VA // width + c)),
        ],
        out_specs=pl.BlockSpec((MOBA_BLOCK, width), lambda b, c, i: (b * n_blocks + i, c)),
        out_shape=jax.ShapeDtypeStruct((tokens, MOBA_WIDTH), bf16),
        scratch_shapes=[
            pltpu.VMEM((n_blocks, width), f32),
            pltpu.VMEM((n_blocks, width, MOBA_BLOCK), bf16),
            pltpu.VMEM((width // MOBA_HEAD_DIM, n_blocks, MOBA_BLOCK), f32),
            pltpu.VMEM((n_blocks, MOBA_BLOCK), f32),
        ],
        compiler_params=_params("parallel", "parallel", "arbitrary"),
        name="moba_attention",
    )(proj, proj, proj)


def _ret_kernel(q_ref, k_ref, v0_ref, v1_ref, g0_ref, g1_ref, decay_ref, zeta_ref, xi_ref, cd_ref,
                o_ref, state_ref):
    c = pl.program_id(1)
    heads_per_half = RET_HEADS // 2
    v_refs = (v0_ref, v1_ref)
    g_refs = (g0_ref, g1_ref)

    @pl.when(c == 0)
    def _():
        state_ref[...] = jnp.zeros_like(state_ref)

    for h in range(RET_HEADS):
        qk_cols = slice(h * RET_QK_DIM, (h + 1) * RET_QK_DIM)
        v_cols = slice(h * RET_V_DIM, (h + 1) * RET_V_DIM)
        half_cols = slice((h % heads_per_half) * RET_V_DIM, (h % heads_per_half + 1) * RET_V_DIM)
        q = q_ref[:, qk_cols]
        k = k_ref[:, qk_cols]
        v = v_refs[h // heads_per_half][:, half_cols]
        state = state_ref[h]
        scores = _dot_nt(q, k) * decay_ref[h]
        inner = _dot(scores.astype(bf16), v)
        cross = _dot(q, state.astype(bf16)) * xi_ref[h]
        o = inner + cross
        kz = (k.astype(f32) * zeta_ref[h]).astype(bf16)
        state_ref[h] = cd_ref[h] * state + _dot_tn(kz, v)

        mu = jnp.mean(o, axis=-1, keepdims=True)
        oc = o - mu
        var = jnp.mean(oc * oc, axis=-1, keepdims=True)
        on = oc * lax.rsqrt(var + GN_EPS)
        g = g_refs[h // heads_per_half][:, half_cols].astype(f32)
        o_ref[:, v_cols] = (on * (g * _sigmoid(g))).astype(o_ref.dtype)


def _retention(proj, batch, seq):
    tokens = batch * seq
    C = RET_CHUNK
    n_chunks = seq // C
    half_v = RET_V_WIDTH // 2
    gammas =1.0 - 2.0 ** (-5.0 - jnp.arange(RET_HEADS, dtype=f32))
    log_g = jnp.log(gammas)
    idx = jnp.arange(C, dtype=f32)
    diff = idx[:, None] - idx[None, :]
    decay = jnp.where(diff >= 0, jnp.exp(jnp.maximum(diff, 0.0)[None] * log_g[:, None, None]), 0.0)
    zeta = jnp.exp((C - 1 - idx)[None, :] * log_g[:, None])
    xi = jnp.exp((idx + 1.0)[None, :] * log_g[:, None])
    zeta_t = jnp.broadcast_to(zeta[:, :, None], (RET_HEADS, C, RET_QK_DIM))
    xi_t = jnp.broadcast_to(xi[:, :, None], (RET_HEADS, C, RET_V_DIM))
    cd_t = jnp.broadcast_to(jnp.exp(C * log_g)[:, None, None], (RET_HEADS, 1, RET_V_DIM))
    return pl.pallas_call(
        _ret_kernel,
        grid=(batch, n_chunks),
        in_specs=[
            pl.BlockSpec((C, RET_QK_WIDTH), lambda b, c: (b * n_chunks + c, OFF_QR // RET_QK_WIDTH)),
            pl.BlockSpec((C, RET_QK_WIDTH), lambda b, c: (b * n_chunks + c, OFF_KR // RET_QK_WIDTH)),
            pl.BlockSpec((C, half_v), lambda b, c: (b * n_chunks + c, OFF_VR // half_v)),
            pl.BlockSpec((C, half_v), lambda b, c: (b * n_chunks + c, OFF_VR // half_v + 1)),
            pl.BlockSpec((C, half_v), lambda b, c: (b * n_chunks + c, OFF_GR // half_v)),
            pl.BlockSpec((C, half_v), lambda b, c: (b * n_chunks + c, OFF_GR // half_v + 1)),
            pl.BlockSpec((RET_HEADS, C, C), lambda b, c: (0, 0, 0)),
            pl.BlockSpec((RET_HEADS, C, RET_QK_DIM), lambda b, c: (0, 0, 0)),
            pl.BlockSpec((RET_HEADS, C, RET_V_DIM), lambda b, c: (0, 0, 0)),
            pl.BlockSpec((RET_HEADS, 1, RET_V_DIM), lambda b, c: (0, 0, 0)),
        ],
        out_specs=pl.BlockSpec((C, RET_V_WIDTH), lambda b, c: (b * n_chunks + c, 0)),
        out_shape=jax.ShapeDtypeStruct((tokens, RET_V_WIDTH), bf16),
        scratch_shapes=[pltpu.VMEM((RET_HEADS, RET_QK_DIM, RET_V_DIM), f32)],
        compiler_params=_params("parallel", "arbitrary"),
        name="retention",
    )(proj, proj, proj, proj, proj, proj, decay, zeta_t, xi_t, cd_t)


def _merge_router_kernel(oa_ref, orr_ref, ga0_ref, ga1_ref, gb0_ref, gb1_ref, x_ref,
                         wm_ref, wr_ref, wo_ref, g1_ref, b1_ref, wrt_hi_ref, wrt_lo_ref, rb_ref,
                         h_ref, pos_ref, w_ref, tab_ref, cnt_ref, carry_ref):
    step = pl.program_id(0)
    tm = TOK_TM

    @pl.when(step == 0)
    def _():
        carry_ref[...] = jnp.zeros_like(carry_ref)

    for t in range(x_ref.shape[0] // tm):
        rows = slice(t * tm, (t + 1) * tm)
        _merge_router_tile(t, rows, tm, oa_ref, orr_ref, ga0_ref, ga1_ref, gb0_ref, gb1_ref, x_ref,
                           wm_ref, wr_ref, wo_ref, g1_ref, b1_ref, wrt_hi_ref, wrt_lo_ref, rb_ref,
                           h_ref, pos_ref, w_ref, tab_ref, cnt_ref, carry_ref)


def _merge_router_tile(t, rows, tm, oa_ref, orr_ref, ga0_ref, ga1_ref, gb0_ref, gb1_ref, x_ref,
                       wm_ref, wr_ref, wo_ref, g1_ref, b1_ref, wrt_hi_ref, wrt_lo_ref, rb_ref,
                       h_ref, pos_ref, w_ref, tab_ref, cnt_ref, carry_ref):
    a = _dot(oa_ref[rows, :], wm_ref[...])
    r = _dot(orr_ref[rows, :], wr_ref[...])
    ga = jnp.concatenate([ga0_ref[rows, :], ga1_ref[rows, :]], axis=1).astype(f32)
    gb = jnp.concatenate([gb0_ref[rows, :], gb1_ref[rows, :]], axis=1).astype(f32)
    merged = _sigmoid(ga) * a + _sigmoid(gb) * r
    mix = _dot(merged.astype(bf16), wo_ref[...])
    h = _layer_norm(ALPHA * x_ref[rows, :] + mix, g1_ref[...], b1_ref[...])
    h_ref[rows, :] = h

    h_hi, h_lo = _split_bf16(h)
    w_hi = wrt_hi_ref[...]
    logits = _dot_nt(w_hi, h_hi) + _dot_nt(w_hi, h_lo) + _dot_nt(wrt_lo_ref[...], h_hi)
    scores = _sigmoid(logits)
    biased = scores + rb_ref[...]

    v = biased.reshape(N_GROUPS, GROUP_SIZE, tm)
    sub = lax.broadcasted_iota(i32, v.shape, 1)
    m1 = jnp.max(v, axis=1, keepdims=True)
    i1 = jnp.min(jnp.where(v == m1, sub, GROUP_SIZE), axis=1, keepdims=True)
    m2 = jnp.max(jnp.where(sub == i1, -jnp.inf, v), axis=1, keepdims=True)
    gscore = jnp.broadcast_to(m1 + m2, v.shape).reshape(N_EXPERTS, tm)

    eid = lax.broadcasted_iota(i32, (N_EXPERTS, tm), 0)
    egrp = eid // GROUP_SIZE
    e_mask = jnp.zeros((N_EXPERTS, tm), jnp.bool_)
    for _ in range(TOPK_GROUPS):
        m = jnp.max(gscore, axis=0, keepdims=True)
        idx = jnp.min(jnp.where(gscore == m, egrp, N_GROUPS), axis=0, keepdims=True)
        hit = egrp == idx
        e_mask = e_mask | hit
        gscore = jnp.where(hit, -jnp.inf, gscore)
    cand = jnp.where(e_mask, biased, -jnp.inf)

    chosen = jnp.zeros((N_EXPERTS, tm), jnp.bool_)
    e_rows = []
    w_rows = []
    for _ in range(TOP_K):
        m = jnp.max(cand, axis=0, keepdims=True)
        idx = jnp.min(jnp.where(cand == m, eid, N_EXPERTS), axis=0, keepdims=True)
        hit = eid == idx
        chosen = chosen | hit
        e_rows.append(idx)
        w_rows.append(jnp.sum(jnp.where(hit, scores, 0.0), axis=0, keepdims=True))
        cand = jnp.where(hit, -jnp.inf, cand)
    w_sum = w_rows[0]
    for wk in w_rows[1:]:
        w_sum = w_sum + wk

    t_src = lax.broadcasted_iota(i32, (tm, tm), 0)
    t_dst = lax.broadcasted_iota(i32, (tm, tm), 1)
    before = (t_src < t_dst).astype(bf16)
    chosen_f = chosen.astype(f32)
    prior = _dot(chosen_f.astype(bf16), before)
    cnt = jnp.sum(chosen_f, axis=1, keepdims=True)
    cnt_pad = jnp.ceil(cnt * (1.0 / UNIT)) * UNIT
    e_src = lax.broadcasted_iota(i32, (N_EXPERTS, N_EXPERTS), 1)
    e_dst = lax.broadcasted_iota(i32, (N_EXPERTS, N_EXPERTS), 0)
    earlier = (e_src < e_dst).astype(bf16)
    cnt_pad_l = jnp.broadcast_to(cnt_pad, (N_EXPERTS, LANES))
    loc_start_l = _dot(earlier, cnt_pad_l.astype(bf16))
    loc_start = loc_start_l[:, 0:1]
    where_e = prior + loc_start
    pos_ref[:, rows] = jnp.zeros((pos_ref.shape[0], tm), pos_ref.dtype)
    w_ref[:, rows] = jnp.zeros((w_ref.shape[0], tm), w_ref.dtype)
    for k in range(TOP_K):
        pos_ref[k:k + 1, rows] = jnp.sum(jnp.where(eid == e_rows[k], where_e, 0.0), axis=0, keepdims=True).astype(i32)
        w_ref[k:k + 1, rows] = w_rows[k] / w_sum * ROUTED_SCALE

    carry = carry_ref[...]
    tl = lax.broadcasted_iota(i32, (N_EXPERTS, LANES), 1)
    table = jnp.where(tl == 0, cnt_pad_l, jnp.where(tl == 1, loc_start_l, jnp.broadcast_to(carry, (N_EXPERTS, LANES))))
    tab_ref[t] = table.astype(i32)
    carry = carry + cnt_pad
    carry_ref[...] = carry
    cnt_ref[...] = jnp.broadcast_to(carry, cnt_ref.shape).astype(i32)


def _merge_router(oa, orr, proj, x2d, wm, wr, wo, g1, b1, wrt_hi, wrt_lo, rb):
    tokens = x2d.shape[0]
    tm = MERGE_TILES * TOK_TM
    half = D_MODEL // 2

    def gate_spec(off):
        return pl.BlockSpec((tm, half), lambda i: (i, off // half))

    def full(shape):
        return pl.BlockSpec(shape, lambda i: (0,) * len(shape))

    row8 = pl.BlockSpec((8, tm), lambda i: (0, i))
    return pl.pallas_call(
        _merge_router_kernel,
        grid=(tokens // tm,),
        in_specs=[
            pl.BlockSpec((tm, MOBA_WIDTH), lambda i: (i, 0)),
            pl.BlockSpec((tm, RET_V_WIDTH), lambda i: (i, 0)),
            gate_spec(OFF_GA), gate_spec(OFF_GA + half), gate_spec(OFF_GB), gate_spec(OFF_GB + half),
            pl.BlockSpec((tm, D_MODEL), lambda i: (i, 0)),
            full((MOBA_WIDTH, D_MODEL)), full((RET_V_WIDTH, D_MODEL)), full((D_MODEL, D_MODEL)),
            full((1, D_MODEL)), full((1, D_MODEL)),
            full((N_EXPERTS, D_MODEL)), full((N_EXPERTS, D_MODEL)), full((N_EXPERTS, 1)),
        ],
        out_specs=[
            pl.BlockSpec((tm, D_MODEL), lambda i: (i, 0)),
            row8, row8,
            pl.BlockSpec((MERGE_TILES, N_EXPERTS, LANES), lambda i: (i, 0, 0)),
            full((N_EXPERTS, LANES)),
        ],
        out_shape=[
            jax.ShapeDtypeStruct((tokens, D_MODEL), f32),
            jax.ShapeDtypeStruct((8, tokens), i32),
            jax.ShapeDtypeStruct((8, tokens), f32),
            jax.ShapeDtypeStruct((tokens // TOK_TM, N_EXPERTS, LANES), i32),
            jax.ShapeDtypeStruct((N_EXPERTS, LANES), i32),
        ],
        scratch_shapes=[pltpu.VMEM((N_EXPERTS, 1), f32)],
        compiler_params=_params("arbitrary"),
        name="merge_ln1_router",
    )(oa, orr, proj, proj, proj, proj, x2d, wm, wr, wo, g1, b1, wrt_hi, wrt_lo, rb)


def _pack_rows(x):
    bits = lax.bitcast_convert_type(x, jnp.uint32)
    return (bits[:, :HALF] & jnp.uint32(0xFFFF0000)) | (bits[:, HALF:] >> 16)


def _unpack_rows(words):
    left = lax.bitcast_convert_type(words & jnp.uint32(0xFFFF0000), f32)
    right = lax.bitcast_convert_type(words << 16, f32)
    return left.astype(bf16), right.astype(bf16)


def _unit_slices(local_unit, global_unit):
    loc = pl.ds(pl.multiple_of(local_unit * UNIT, UNIT), UNIT)
    glob = pl.ds(pl.multiple_of(global_unit * UNIT, UNIT), UNIT)
    return loc, glob


def _start_alternating(lo, hi, make_copy):
    n = hi - lo

    def pair(m, _):
        make_copy(lo + 2 * m).start(priority=0)
        make_copy(lo + 2 * m + 1).start(priority=1)
        return 0

    lax.fori_loop(0, n // 2, pair, 0)

    @pl.when(n % 2 == 1)
    def _():
        make_copy(hi - 1).start(priority=0)


def _burst_bounds(n, b):
    return b * n // ISSUE_BURSTS, (b + 1) * n // ISSUE_BURSTS


def _dispatch_kernel(gu_ref, nu_ref, tail_ref, used_ref, h_ref, pos_ref, rows_ref,
                     xs_ref, zero_ref, sem_ref, zsem_ref):
    step = pl.program_id(0)
    tm = h_ref.shape[0]

    @pl.when(step == 0)
    def _():
        zero_ref[...] = jnp.zeros_like(zero_ref)

        def clear_copy(block):
            start = pl.multiple_of(block * ROW_BLOCK, ROW_BLOCK)
            return pltpu.make_async_copy(zero_ref, rows_ref.at[pl.ds(start, ROW_BLOCK), :], zsem_ref)

        def clear_all(act):
            def tail(e, _):
                tail_block = tail_ref[e]

                @pl.when(tail_block >= 0)
                def _():
                    act(clear_copy(tail_block))
                return 0

            lax.fori_loop(0, N_EXPERTS, tail, 0)

            def unused(b, _):
                act(clear_copy(b))
                return 0

            lax.fori_loop(used_ref[0], rows_ref.shape[0] // ROW_BLOCK, unused, 0)

        clear_all(lambda cp: cp.start())
        clear_all(lambda cp: cp.wait())

    def unit_copy(t, s):
        loc, glob = _unit_slices(s, gu_ref[t * LOC_UNITS + s])
        return pltpu.make_async_copy(xs_ref.at[t % 2, loc, :], rows_ref.at[glob, :], sem_ref.at[t % 2])

    def drain(t):
        def wait_one(s, _):
            unit_copy(t, s).wait()
            return 0

        lax.fori_loop(0, nu_ref[t], wait_one, 0)

    @pl.when(step >= 2)
    def _():
        drain(step - 2)

    prev = jnp.maximum(step - 1, 0)
    n_prev = jnp.where(step >= 1, nu_ref[prev], 0)
    pos = pos_ref[...]
    hb = h_ref[...].astype(bf16)
    chunk = LOC_ROWS // ISSUE_BURSTS
    for c in range(ISSUE_BURSTS):
        r_iota = lax.broadcasted_iota(i32, (chunk, tm), 0) + c * chunk
        onehot = r_iota == pos[0:1, :]
        for k in range(1, TOP_K):
            onehot = onehot | (r_iota == pos[k:k + 1, :])
        xs = _dot(onehot.astype(bf16), hb)
        xs_ref[step % 2, c * chunk:(c + 1) * chunk, :] = _pack_rows(xs)
        lo, hi = _burst_bounds(n_prev, c)
        _start_alternating(lo, hi, lambda s: unit_copy(prev, s))

    @pl.when(step == pl.num_programs(0) - 1)
    def _():
        _start_alternating(0, nu_ref[step], lambda s: unit_copy(step, s))

        @pl.when(step >= 1)
        def _():
            drain(step - 1)

        drain(step)


def _dispatch(unit_map, n_units, tail_block, n_used, h, pos8, n_rows):
    tokens = h.shape[0]
    tm = TOK_TM
    grid_spec = pltpu.PrefetchScalarGridSpec(
        num_scalar_prefetch=4,
        grid=(tokens // tm,),
        in_specs=[
            pl.BlockSpec((tm, D_MODEL), lambda i, *_: (i, 0)),
            pl.BlockSpec((8, tm), lambda i, *_: (0, i)),
        ],
        out_specs=pl.BlockSpec(memory_space=pl.ANY),
        scratch_shapes=[
            pltpu.VMEM((2, LOC_ROWS, HALF), jnp.uint32),
            pltpu.VMEM((ROW_BLOCK, HALF), jnp.uint32),
            pltpu.SemaphoreType.DMA((2,)),
            pltpu.SemaphoreType.DMA,
        ],
    )
    return pl.pallas_call(
        _dispatch_kernel,
        grid_spec=grid_spec,
        out_shape=jax.ShapeDtypeStruct((n_rows, HALF), jnp.uint32),
        compiler_params=_params("arbitrary"),
        name="moe_dispatch",
    )(unit_map, n_units, tail_block, n_used, h, pos8)


def _expert_kernel(eid_ref, used_ref, x_ref, wg_ref, wu_ref, wd_ref, y_ref, wgb_ref, wub_ref, wdb_ref):
    b = pl.program_id(0)
    changed = (b == 0) | (eid_ref[b] != eid_ref[jnp.maximum(b - 1, 0)])

    @pl.when(changed)
    def _():
        wgb_ref[...] = wg_ref[0].astype(bf16)
        wub_ref[...] = wu_ref[0].astype(bf16)
        wdb_ref[...] = wd_ref[0].astype(bf16)

    @pl.when(b < used_ref[0])
    def _():
        for c in range(ROW_BLOCK // EXPERT_CHUNK):
            rows = slice(c * EXPERT_CHUNK, (c + 1) * EXPERT_CHUNK)
            xl, xr = _unpack_rows(x_ref[rows, :])
            g = _dot(xl, wgb_ref[:HALF, :]) + _dot(xr, wgb_ref[HALF:, :])
            u = _dot(xl, wub_ref[:HALF, :]) + _dot(xr, wub_ref[HALF:, :])
            mid = (g * _sigmoid(g)) * u
            y = _dot(mid.astype(bf16), wdb_ref[...])
            y_ref[rows, :] = _pack_rows(y.astype(bf16).astype(f32))

    @pl.when(b >= used_ref[0])
    def _():
        y_ref[...] = jnp.zeros_like(y_ref)


def _experts(block_eid, n_used, rows, wg, wu, wd):
    n_rows = rows.shape[0]
    n_blocks = n_rows // ROW_BLOCK

    def row_map(b, eid, used):
        return (b, 0)

    def w_map(b, eid, used):
        return (eid[b], 0, 0)

    grid_spec = pltpu.PrefetchScalarGridSpec(
        num_scalar_prefetch=2,
        grid=(n_blocks,),
        in_specs=[
            pl.BlockSpec((ROW_BLOCK, HALF), row_map),
            pl.BlockSpec((1, D_MODEL, D_EXPERT), w_map),
            pl.BlockSpec((1, D_MODEL, D_EXPERT), w_map),
            pl.BlockSpec((1, D_EXPERT, D_MODEL), w_map),
        ],
        out_specs=pl.BlockSpec((ROW_BLOCK, HALF), row_map),
        scratch_shapes=[
            pltpu.VMEM((D_MODEL, D_EXPERT), bf16),
            pltpu.VMEM((D_MODEL, D_EXPERT), bf16),
            pltpu.VMEM((D_EXPERT, D_MODEL), bf16),
        ],
    )
    return pl.pallas_call(
        _expert_kernel,
        grid_spec=grid_spec,
        out_shape=jax.ShapeDtypeStruct((n_rows, HALF), jnp.uint32),
        compiler_params=_params("arbitrary"),
        name="moe_experts",
    )(block_eid, n_used, rows, wg, wu, wd)


def _combine_kernel(gu_ref, nu_ref, y_ref, h_ref, pos_ref, wt_ref, p_ref, wsg_ref, wsu_ref, wsd_ref, wpp_ref,
                    wpg_ref, g2_ref, b2_ref, g3_ref, b3_ref, o_ref, ybuf_ref, sem_ref):
    step = pl.program_id(0)
    tm = h_ref.shape[0]

    last = pl.num_programs(0) - 1

    def unit_copy(t, s):
        loc, glob = _unit_slices(s, gu_ref[t * LOC_UNITS + s])
        return pltpu.make_async_copy(y_ref.at[glob, :], ybuf_ref.at[t % 2, loc, :], sem_ref.at[t % 2])

    @pl.when(step == 0)
    def _():
        ybuf_ref[...] = jnp.zeros_like(ybuf_ref)
        _start_alternating(0, nu_ref[step], lambda s: unit_copy(step, s))

    nxt = jnp.minimum(step + 1, last)
    n_next = jnp.where(step < last, nu_ref[nxt], 0)
    bursts = iter(range(ISSUE_BURSTS))

    def fetch_burst():
        lo, hi = _burst_bounds(n_next, next(bursts))
        _start_alternating(lo, hi, lambda s: unit_copy(nxt, s))

    def wait_one(s, _):
        unit_copy(step, s).wait()
        return 0

    lax.fori_loop(0, nu_ref[step], wait_one, 0)
    fetch_burst()

    h = h_ref[...]
    hb = h.astype(bf16)
    sg = _dot(hb, wsg_ref[...])
    su = _dot(hb, wsu_ref[...])
    mid = ((sg * _sigmoid(sg)) * su).astype(bf16)
    fetch_burst()
    shared = _dot(mid, wsd_ref[...])
    fetch_burst()
    ple_in = _dot(p_ref[...].astype(bf16), wpp_ref[...])
    fetch_burst()

    pos = pos_ref[...]
    wt = wt_ref[...]
    n_chunks = ISSUE_BURSTS - 4
    chunk = LOC_ROWS // n_chunks
    routed_l = jnp.zeros((tm, HALF), f32)
    routed_r = jnp.zeros((tm, HALF), f32)
    for c in range(n_chunks):
        c_iota = lax.broadcasted_iota(i32, (tm, chunk), 1) + c * chunk
        spread = jnp.where(c_iota == pos[:, 0:1], wt[:, 0:1], 0.0)
        for k in range(1, TOP_K):
            spread = jnp.where(c_iota == pos[:, k:k + 1], wt[:, k:k + 1], spread)
        spread = spread.astype(bf16)
        yl, yr = _unpack_rows(ybuf_ref[step % 2, c * chunk:(c + 1) * chunk, :])
        routed_l = routed_l + _dot(spread, yl)
        routed_r = routed_r + _dot(spread, yr)
        fetch_burst()
    routed = jnp.concatenate([routed_l, routed_r], axis=1)

    h2 = _layer_norm(ALPHA * h + (routed + shared), g2_ref[...], b2_ref[...])
    ple = ple_in * _sigmoid(_dot(h2.astype(bf16), wpg_ref[...]))
    o_ref[...] = _layer_norm(ALPHA * h2 + ple, g3_ref[...], b3_ref[...])


def _combine(unit_map, n_units, y_rows, h, pos_tok, w_tok, p2d, wsg, wsu, wsd, wpp, wpg, g2, b2, g3, b3):
    tokens = h.shape[0]
    tm = TOK_TM

    def full(shape):
        return pl.BlockSpec(shape, lambda i, *_: (0,) * len(shape))

    def tile(width):
        return pl.BlockSpec((tm, width), lambda i, *_: (i, 0))

    grid_spec = pltpu.PrefetchScalarGridSpec(
        num_scalar_prefetch=2,
        grid=(tokens // tm,),
        in_specs=[
            pl.BlockSpec(memory_space=pl.ANY),
            tile(D_MODEL), tile(8), tile(8), tile(PLE_DIM),
            full((D_MODEL, D_SHARED)), full((D_MODEL, D_SHARED)), full((D_SHARED, D_MODEL)),
            full((PLE_DIM, D_MODEL)), full((D_MODEL, D_MODEL)),
            full((1, D_MODEL)), full((1, D_MODEL)), full((1, D_MODEL)), full((1, D_MODEL)),
        ],
        out_specs=tile(D_MODEL),
        scratch_shapes=[
            pltpu.VMEM((2, LOC_ROWS, HALF), jnp.uint32),
            pltpu.SemaphoreType.DMA((2,)),
        ],
    )
    return pl.pallas_call(
        _combine_kernel,
        grid_spec=grid_spec,
        out_shape=jax.ShapeDtypeStruct((tokens, D_MODEL), f32),
        compiler_params=_params("arbitrary"),
        name="moe_combine_ln2_ple_ln3",
    )(unit_map, n_units, y_rows, h, pos_tok, w_tok, p2d, wsg, wsu, wsd, wpp, wpg, g2, b2, g3, b3)


def _row_layout(table, totals, n_blocks):
    cnt_u = table[:, :, 0] // UNIT
    loc_u = table[:, :, 1] // UNIT
    padded = (totals + ROW_BLOCK - 1) // ROW_BLOCK * ROW_BLOCK
    pends = jnp.cumsum(padded)
    pstarts = pends - padded
    glob_u = (pstarts[None, :] + table[:, :, 2]) // UNIT
    n_units = jnp.sum(cnt_u, axis=1).astype(i32)
    s = jnp.arange(LOC_UNITS, dtype=i32)
    loc_end = loc_u + cnt_u
    e_of = jnp.sum((loc_end[:, None, :] <= s[None, :, None]).astype(i32), axis=-1)
    is_e = e_of[:, :, None] == jnp.arange(N_EXPERTS, dtype=i32)
    shift = jnp.sum(jnp.where(is_e, (glob_u - loc_u)[:, None, :], 0), axis=-1)
    unit_map = (shift + s[None, :]).astype(i32).reshape(-1)
    tail_block = jnp.where(totals > 0, pends // ROW_BLOCK - 1, -1).astype(i32)
    block_start = jnp.arange(n_blocks, dtype=i32) * ROW_BLOCK
    ends_before = jnp.sum((pends[None, :] <= block_start[:, None]).astype(i32), axis=1)
    block_eid = jnp.minimum(ends_before, N_EXPERTS - 1).astype(i32)
    n_used = (pends[-1:] // ROW_BLOCK).astype(i32)
    return unit_map, n_units, tail_block, block_eid, n_used


def _layer(x2d, p2d, batch, seq, w_in, w_moba_up, w_ret_up, w_out, ln1_g, ln1_b, w_router, router_bias,
           w_eg, w_eu, w_ed, w_sg, w_su, w_sd, ln2_g, ln2_b, w_ple_proj, w_ple_gate, ln3_g, ln3_b):
    tokens = batch * seq
    row = lambda a: a.reshape(1, -1).astype(f32)

    proj = _project(x2d, w_in.astype(bf16), seq)
    oa = _moba(proj, batch, seq)
    orr = _retention(proj, batch, seq)

    wrt = w_router.astype(f32).T
    wrt_hi = wrt.astype(bf16)
    wrt_lo = (wrt - wrt_hi.astype(f32)).astype(bf16)
    h1, pos8, w8, table, totals = _merge_router(
        oa, orr, proj, x2d, w_moba_up.astype(bf16), w_ret_up.astype(bf16), w_out.astype(bf16),
        row(ln1_g), row(ln1_b), wrt_hi, wrt_lo, router_bias.astype(f32).reshape(N_EXPERTS, 1))

    n_tiles = tokens // TOK_TM
    max_rows = tokens * TOP_K + n_tiles * N_EXPERTS * (UNIT - 1) + N_EXPERTS * (ROW_BLOCK - 1)
    n_blocks = -(-max_rows // ROW_BLOCK)
    unit_map, n_units, tail_block, block_eid, n_used = _row_layout(table[:, :, :3], totals[:, 0], n_blocks)

    rows = _dispatch(unit_map, n_units, tail_block, n_used, h1, pos8, n_blocks * ROW_BLOCK)
    y_rows = _experts(block_eid, n_used, rows, w_eg, w_eu, w_ed)
    return _combine(unit_map, n_units, y_rows, h1, pos8.T, w8.T, p2d,
                    w_sg.astype(bf16), w_su.astype(bf16), w_sd.astype(bf16),
                    w_ple_proj.astype(bf16), w_ple_gate.astype(bf16),
                    row(ln2_g), row(ln2_b), row(ln3_g), row(ln3_b))


def kernel(x, p, w_in, w_moba_up, w_ret_up, w_out, ln1_g, ln1_b, w_router, router_bias, w_exp_gate, w_exp_up,
           w_exp_down, w_sh_gate, w_sh_up, w_sh_down, ln2_g, ln2_b, w_ple_proj, w_ple_gate, ln3_g, ln3_b):
    batch, seq, d = x.shape
    assert d == D_MODEL and seq % max(MOBA_BLOCK, RET_CHUNK) == 0 and (batch * seq) % (MERGE_TILES * TOK_TM) == 0
    assert w_in.shape[0] == DEPTH
    h = x.reshape(batch * seq, d)
    for i in range(DEPTH):
        h = _layer(h, p[i].reshape(batch * seq, PLE_DIM), batch, seq,
                   w_in[i], w_moba_up[i], w_ret_up[i], w_out[i], ln1_g[i], ln1_b[i], w_router[i], router_bias[i],
                   w_exp_gate[i], w_exp_up[i], w_exp_down[i], w_sh_gate[i], w_sh_up[i], w_sh_down[i],
                   ln2_g[i], ln2_b[i], w_ple_proj[i], w_ple_gate[i], ln3_g[i], ln3_b[i])
    return h.reshape(batch, seq, d)
```

```python
import functools

import jax
import jax.numpy as jnp
from jax import lax
from jax.experimental import pallas as pl
from jax.experimental.pallas import tpu as pltpu

f32 = jnp.float32
bf16 = jnp.bfloat16
i32 = jnp.int32

D_MODEL = 1024
DEPTH = 1
MOBA_HEADS = 8
MOBA_HEAD_DIM = 64
MOBA_WIDTH = MOBA_HEADS * MOBA_HEAD_DIM
MOBA_BLOCK = 256
MOBA_TOPK = 3
ROPE_THETA = 10000.0
RET_HEADS = 4
RET_QK_DIM = 128
RET_V_DIM = 256
RET_QK_WIDTH = RET_HEADS * RET_QK_DIM
RET_V_WIDTH = RET_HEADS * RET_V_DIM
RET_CHUNK = 256
RET_ANGLE_BASE = 10000.0
N_IN = 3 * MOBA_WIDTH + 2 * RET_QK_WIDTH + 2 * RET_V_WIDTH + 2 * D_MODEL
N_EXPERTS = 64
TOP_K = 6
N_GROUPS = 8
TOPK_GROUPS = 4
GROUP_SIZE = N_EXPERTS // N_GROUPS
D_EXPERT = 256
D_SHARED = 256
ROUTED_SCALE = 2.5
PLE_DIM = 256
LN_EPS = 1e-5
GN_EPS = 1e-6
ALPHA = (2.0 * DEPTH) ** 0.25

OFF_QA = 0
OFF_KA = MOBA_WIDTH
OFF_VA = 2 * MOBA_WIDTH
OFF_QR = 3 * MOBA_WIDTH
OFF_KR = OFF_QR + RET_QK_WIDTH
OFF_VR = OFF_KR + RET_QK_WIDTH
OFF_GR = OFF_VR + RET_V_WIDTH
OFF_GA = OFF_GR + RET_V_WIDTH
OFF_GB = OFF_GA + D_MODEL

LANES = 128
NEG = -1e30
VMEM_LIMIT = 56 * 1024 * 1024

PROJ_TM = 2048
PROJ_TN = 512
MOBA_STEP_WIDTH = 256
MERGE_TILES = 2
TOK_TM = 256
ROW_BLOCK = 512
EXPERT_CHUNK = 256
UNIT = 8
LOC_ROWS = -(-(TOP_K * TOK_TM + N_EXPERTS * (UNIT - 1)) // LANES) * LANES
LOC_UNITS = LOC_ROWS // UNIT
HALF = D_MODEL // 2


def _dot(a, b):
    return jnp.dot(a, b, preferred_element_type=f32)


def _dot_nt(a, b):
    return lax.dot_general(a, b, (((1,), (1,)), ((), ())), preferred_element_type=f32)


def _dot_tn(a, b):
    return lax.dot_general(a, b, (((0,), (0,)), ((), ())), preferred_element_type=f32)


def _split_bf16(a):
    hi = a.astype(bf16)
    lo = (a - hi.astype(f32)).astype(bf16)
    return hi, lo


def _layer_norm(x, g, b):
    mu = jnp.mean(x, axis=-1, keepdims=True)
    xc = x - mu
    var = jnp.mean(xc * xc, axis=-1, keepdims=True)
    return xc * lax.rsqrt(var + LN_EPS) * g + b


def _sigmoid(x):
    return 1.0 / (1.0 + jnp.exp(-x))


def _params(*sem, flags=None):
    return pltpu.CompilerParams(dimension_semantics=sem, vmem_limit_bytes=VMEM_LIMIT, flags=flags)


def _rotate_half_chunk(xc, half):
    if 2 * half == LANES:
        return pltpu.roll(xc, half, axis=1)
    lane = lax.broadcasted_iota(i32, xc.shape, 1)
    first = (lane & (2 * half - 1)) < half
    return jnp.where(first, pltpu.roll(xc, LANES - half, axis=1), pltpu.roll(xc, half, axis=1))


def _proj_kernel(x_ref, w_ref, cos_a_ref, sin_a_ref, cos_r_ref, sin_r_ref, o_ref, xb_ref):
    j = pl.program_id(1)

    @pl.when(j == 0)
    def _():
        xb_ref[...] = x_ref[...].astype(bf16)

    acc = _dot(xb_ref[...], w_ref[...])

    def rotary(cos_ref, sin_ref, half, scale):
        cos = cos_ref[...]
        sin = sin_ref[...]
        for c in range(PROJ_TN // LANES):
            xc = acc[:, c * LANES:(c + 1) * LANES]
            y = xc * cos + _rotate_half_chunk(xc, half) * sin
            if scale != 1.0:
                y = y * scale
            o_ref[:, c * LANES:(c + 1) * LANES] = y.astype(o_ref.dtype)

    j_qa = OFF_QA // PROJ_TN
    j_ka = OFF_KA // PROJ_TN
    j_qr = OFF_QR // PROJ_TN
    j_kr = OFF_KR // PROJ_TN

    @pl.when(j == j_qa)
    def _():
        rotary(cos_a_ref, sin_a_ref, MOBA_HEAD_DIM // 2, MOBA_HEAD_DIM ** -0.5)

    @pl.when(j == j_ka)
    def _():
        rotary(cos_a_ref, sin_a_ref, MOBA_HEAD_DIM // 2, 1.0)

    @pl.when(j == j_qr)
    def _():
        rotary(cos_r_ref, sin_r_ref, RET_QK_DIM // 2, 1.0)

    @pl.when(j == j_kr)
    def _():
        rotary(cos_r_ref, sin_r_ref, RET_QK_DIM // 2, RET_QK_DIM ** -0.5)

    plain = (j != j_qa) & (j != j_ka) & (j != j_qr) & (j != j_kr)

    @pl.when(plain)
    def _():
        o_ref[...] = acc.astype(o_ref.dtype)


def _rotary_tables(seq, inv_freq, head_dim):
    ang = jnp.arange(seq, dtype=f32)[:, None] * inv_freq[None, :]
    cos = jnp.cos(ang)
    sin = jnp.sin(ang)
    cos_h = jnp.concatenate([cos, cos], axis=-1)
    sin_h = jnp.concatenate([-sin, sin], axis=-1)
    reps = LANES // head_dim
    return jnp.tile(cos_h, (1, reps)), jnp.tile(sin_h, (1, reps))


def _project(x2d, w_in_b, seq):
    tokens = x2d.shape[0]
    tm = min(PROJ_TM, seq)
    inv_a = 1.0 / (ROPE_THETA ** (jnp.arange(0, MOBA_HEAD_DIM, 2, dtype=f32) / MOBA_HEAD_DIM))
    inv_r = 1.0 / (RET_ANGLE_BASE ** jnp.linspace(0.0, 1.0, RET_QK_DIM // 2, dtype=f32))
    cos_a, sin_a = _rotary_tables(seq, inv_a, MOBA_HEAD_DIM)
    cos_r, sin_r = _rotary_tables(seq, inv_r, RET_QK_DIM)
    seq_tiles = seq // tm
    tab = pl.BlockSpec((tm, LANES), lambda i, j: (i % seq_tiles, 0))
    return pl.pallas_call(
        _proj_kernel,
        grid=(tokens // tm, N_IN // PROJ_TN),
        in_specs=[
            pl.BlockSpec((tm, D_MODEL), lambda i, j: (i, 0)),
            pl.BlockSpec((D_MODEL, PROJ_TN), lambda i, j: (0, j)),
            tab, tab, tab, tab,
        ],
        out_specs=pl.BlockSpec((tm, PROJ_TN), lambda i, j: (i, j)),
        out_shape=jax.ShapeDtypeStruct((tokens, N_IN), bf16),
        scratch_shapes=[pltpu.VMEM((tm, D_MODEL), bf16)],
        compiler_params=_params("parallel", "arbitrary"),
        name="in_proj_rotary",
    )(x2d, w_in_b, cos_a, sin_a, cos_r, sin_r)


def _moba_kernel(q_ref, k_ref, v_ref, o_ref, kmean_ref, vt_ref, selb_ref, gate_ref, *, n_blocks):
    i = pl.program_id(2)
    blk = MOBA_BLOCK
    hd = MOBA_HEAD_DIM
    width = q_ref.shape[1]
    chunks = width // LANES
    heads = width // hd
    per_chunk = LANES // hd

    def chunk_of(h):
        return slice(h // per_chunk * LANES, (h // per_chunk + 1) * LANES)

    @pl.when(i == 0)
    def _():
        seq = n_blocks * blk
        member = (lax.broadcasted_iota(i32, (n_blocks, seq), 1) // blk
                  == lax.broadcasted_iota(i32, (n_blocks, seq), 0))
        kmean_ref[...] = _dot(member.astype(bf16), k_ref[...]) * (1.0 / blk)
        for j in range(n_blocks):
            vt_ref[j] = v_ref[j * blk:(j + 1) * blk, :].astype(f32).T.astype(bf16)

    km_hi, km_lo = _split_bf16(kmean_ref[...])
    key_i = lax.broadcasted_iota(i32, (blk, blk), 0)
    qry_i = lax.broadcasted_iota(i32, (blk, blk), 1)
    causal = key_i <= qry_i
    bid = lax.broadcasted_iota(i32, (n_blocks, blk), 0)
    d_row = lax.broadcasted_iota(i32, (LANES, blk), 0)
    qt = q_ref[...].astype(f32).T

    qts = []
    for h in range(heads):
        r0 = h % per_chunk * hd
        in_head = (d_row >= r0) & (d_row < r0 + hd)
        qth = jnp.where(in_head, qt[chunk_of(h), :], 0.0).astype(bf16)
        qts.append(qth)
        gate = _dot(km_hi[:, chunk_of(h)], qth) + _dot(km_lo[:, chunk_of(h)], qth)
        g = jnp.where(bid < i, gate, -jnp.inf)
        gate_ref[...] = g
        beaten = jnp.zeros((n_blocks, blk), f32)
        for j in range(n_blocks):
            gj = jnp.broadcast_to(gate_ref[j:j + 1, :], (n_blocks, blk))
            beats = (gj > g) | ((gj == g) & (bid > j))
            beaten = beaten + jnp.where(beats, 1.0, 0.0)
        sel = (beaten < MOBA_TOPK) & (bid < i)
        selb_ref[h] = jnp.where(sel, 0.0, NEG)

    kj = k_ref[pl.ds(pl.multiple_of(i * blk, blk), blk), :]
    vtj = vt_ref[i]
    state = []
    for h in range(heads):
        s = jnp.where(causal, _dot(kj[:, chunk_of(h)], qts[h]), NEG)
        m0 = jnp.max(s, axis=0, keepdims=True)
        p = jnp.exp(s - m0)
        l0 = jnp.sum(p, axis=0, keepdims=True)
        acc0 = _dot(vtj[h * hd:(h + 1) * hd, :], p.astype(bf16))
        state += [m0, l0, acc0]

    qt_alls = [jnp.concatenate(qts[c * per_chunk:(c + 1) * per_chunk], axis=1) for c in range(chunks)]

    def update(carry, js):
        nj = len(js)
        k_all = k_ref[pl.ds(pl.multiple_of(js[0] * blk, blk), nj * blk), :]
        s_alls = [_dot(k_all[:, c * LANES:(c + 1) * LANES], qt_alls[c]) for c in range(chunks)]
        vtjs = [vt_ref[j] for j in js]
        new = []
        for h in range(heads):
            m_old, l_old, acc = carry[3 * h:3 * h + 3]
            col = h % per_chunk * blk
            ss = [s_alls[h // per_chunk][n * blk:(n + 1) * blk, col:col + blk] + selb_ref[h, pl.ds(j, 1), :]
                  for n, j in enumerate(js)]
            m_new = m_old
            for s in ss:
                m_new = jnp.maximum(m_new, jnp.max(s, axis=0, keepdims=True))
            a = jnp.exp(m_old - m_new)
            l_new = a * l_old
            acc = a * acc
            for s, vtj in zip(ss, vtjs):
                p = jnp.exp(s - m_new)
                l_new = l_new + jnp.sum(p, axis=0, keepdims=True)
                acc = acc + _dot(vtj[h * hd:(h + 1) * hd, :], p.astype(bf16))
            new += [m_new, l_new, acc]
        return tuple(new)

    quads = i // 4
    fin = lax.fori_loop(0, quads, lambda n, c: update(c, tuple(4 * n + t for t in range(4))), tuple(state))
    done = 4 * quads
    has_pair = (i - done) // 2
    fin = lax.fori_loop(0, has_pair, lambda n, c: update(c, (done, done + 1)), fin)
    done = done + 2 * has_pair
    fin = lax.fori_loop(done, i, lambda j, c: update(c, (j,)), fin)
    out_t = jnp.concatenate([fin[3 * h + 2] / fin[3 * h + 1] for h in range(heads)], axis=0)
    o_ref[...] = out_t.T.astype(o_ref.dtype)


def _moba(proj, batch, seq):
    tokens = batch * seq
    n_blocks = seq // MOBA_BLOCK
    width = MOBA_STEP_WIDTH
    return pl.pallas_call(
        functools.partial(_moba_kernel, n_blocks=n_blocks),
        grid=(batch, MOBA_WIDTH // width, n_blocks),
        in_specs=[
            pl.BlockSpec((MOBA_BLOCK, width), lambda b, c, i: (b * n_blocks + i, OFF_QA // width + c)),
            pl.BlockSpec((seq, width), lambda b, c, i: (b, OFF_KA // width + c)),
            pl.BlockSpec((seq, width), lambda b, c, i: (b, OFF_VA // width + c)),
        ],
        out_specs=pl.BlockSpec((MOBA_BLOCK, width), lambda b, c, i: (b * n_blocks + i, c)),
        out_shape=jax.ShapeDtypeStruct((tokens, MOBA_WIDTH), bf16),
        scratch_shapes=[
            pltpu.VMEM((n_blocks, width), f32),
            pltpu.VMEM((n_blocks, width, MOBA_BLOCK), bf16),
            pltpu.VMEM((width // MOBA_HEAD_DIM, n_blocks, MOBA_BLOCK), f32),
            pltpu.VMEM((n_blocks, MOBA_BLOCK), f32),
        ],
        compiler_params=_params("parallel", "parallel", "arbitrary"),
        name="moba_attention",
    )(proj, proj, proj)


def _ret_kernel(q_ref, k_ref, v0_ref, v1_ref, g0_ref, g1_ref, decay_ref, zeta_ref, xi_ref, cd_ref,
                o_ref, state_ref):
    c = pl.program_id(1)
    heads_per_half = RET_HEADS // 2
    v_refs = (v0_ref, v1_ref)
    g_refs = (g0_ref, g1_ref)

    @pl.when(c == 0)
    def _():
        state_ref[...] = jnp.zeros_like(state_ref)

    for h in range(RET_HEADS):
        qk_cols = slice(h * RET_QK_DIM, (h + 1) * RET_QK_DIM)
        v_cols = slice(h * RET_V_DIM, (h + 1) * RET_V_DIM)
        half_cols = slice((h % heads_per_half) * RET_V_DIM, (h % heads_per_half + 1) * RET_V_DIM)
        q = q_ref[:, qk_cols]
        k = k_ref[:, qk_cols]
        v = v_refs[h // heads_per_half][:, half_cols]
        state = state_ref[h]
        scores = _dot_nt(q, k) * decay_ref[h]
        inner = _dot(scores.astype(bf16), v)
        cross = _dot(q, state.astype(bf16)) * xi_ref[h]
        o = inner + cross
        kz = (k.astype(f32) * zeta_ref[h]).astype(bf16)
        state_ref[h] = cd_ref[h] * state + _dot_tn(kz, v)

        mu = jnp.mean(o, axis=-1, keepdims=True)
        oc = o - mu
        var = jnp.mean(oc * oc, axis=-1, keepdims=True)
        on = oc * lax.rsqrt(var + GN_EPS)
        g = g_refs[h // heads_per_half][:, half_cols].astype(f32)
        o_ref[:, v_cols] = (on * (g * _sigmoid(g))).astype(o_ref.dtype)


def _retention(proj, batch, seq):
    tokens = batch * seq
    C = RET_CHUNK
    n_chunks = seq // C
    half_v = RET_V_WIDTH // 2
    gammas =1.0 - 2.0 ** (-5.0 - jnp.arange(RET_HEADS, dtype=f32))
    log_g = jnp.log(gammas)
    idx = jnp.arange(C, dtype=f32)
    diff = idx[:, None] - idx[None, :]
    decay = jnp.where(diff >= 0, jnp.exp(jnp.maximum(diff, 0.0)[None] * log_g[:, None, None]), 0.0)
    zeta = jnp.exp((C - 1 - idx)[None, :] * log_g[:, None])
    xi = jnp.exp((idx + 1.0)[None, :] * log_g[:, None])
    zeta_t = jnp.broadcast_to(zeta[:, :, None], (RET_HEADS, C, RET_QK_DIM))
    xi_t = jnp.broadcast_to(xi[:, :, None], (RET_HEADS, C, RET_V_DIM))
    cd_t = jnp.broadcast_to(jnp.exp(C * log_g)[:, None, None], (RET_HEADS, 1, RET_V_DIM))
    return pl.pallas_call(
        _ret_kernel,
        grid=(batch, n_chunks),
        in_specs=[
            pl.BlockSpec((C, RET_QK_WIDTH), lambda b, c: (b * n_chunks + c, OFF_QR // RET_QK_WIDTH)),
            pl.BlockSpec((C, RET_QK_WIDTH), lambda b, c: (b * n_chunks + c, OFF_KR // RET_QK_WIDTH)),
            pl.BlockSpec((C, half_v), lambda b, c: (b * n_chunks + c, OFF_VR // half_v)),
            pl.BlockSpec((C, half_v), lambda b, c: (b * n_chunks + c, OFF_VR // half_v + 1)),
            pl.BlockSpec((C, half_v), lambda b, c: (b * n_chunks + c, OFF_GR // half_v)),
            pl.BlockSpec((C, half_v), lambda b, c: (b * n_chunks + c, OFF_GR // half_v + 1)),
            pl.BlockSpec((RET_HEADS, C, C), lambda b, c: (0, 0, 0)),
            pl.BlockSpec((RET_HEADS, C, RET_QK_DIM), lambda b, c: (0, 0, 0)),
            pl.BlockSpec((RET_HEADS, C, RET_V_DIM), lambda b, c: (0, 0, 0)),
            pl.BlockSpec((RET_HEADS, 1, RET_V_DIM), lambda b, c: (0, 0, 0)),
        ],
        out_specs=pl.BlockSpec((C, RET_V_WIDTH), lambda b, c: (b * n_chunks + c, 0)),
        out_shape=jax.ShapeDtypeStruct((tokens, RET_V_WIDTH), bf16),
        scratch_shapes=[pltpu.VMEM((RET_HEADS, RET_QK_DIM, RET_V_DIM), f32)],
        compiler_params=_params("parallel", "arbitrary"),
        name="retention",
    )(proj, proj, proj, proj, proj, proj, decay, zeta_t, xi_t, cd_t)


def _merge_router_kernel(oa_ref, orr_ref, ga0_ref, ga1_ref, gb0_ref, gb1_ref, x_ref,
                         wm_ref, wr_ref, wo_ref, g1_ref, b1_ref, wrt_hi_ref, wrt_lo_ref, rb_ref,
                         h_ref, pos_ref, w_ref, tab_ref, cnt_ref, carry_ref):
    step = pl.program_id(0)
    tm = TOK_TM

    @pl.when(step == 0)
    def _():
        carry_ref[...] = jnp.zeros_like(carry_ref)

    for t in range(x_ref.shape[0] // tm):
        rows = slice(t * tm, (t + 1) * tm)
        _merge_router_tile(t, rows, tm, oa_ref, orr_ref, ga0_ref, ga1_ref, gb0_ref, gb1_ref, x_ref,
                           wm_ref, wr_ref, wo_ref, g1_ref, b1_ref, wrt_hi_ref, wrt_lo_ref, rb_ref,
                           h_ref, pos_ref, w_ref, tab_ref, cnt_ref, carry_ref)


def _merge_router_tile(t, rows, tm, oa_ref, orr_ref, ga0_ref, ga1_ref, gb0_ref, gb1_ref, x_ref,
                       wm_ref, wr_ref, wo_ref, g1_ref, b1_ref, wrt_hi_ref, wrt_lo_ref, rb_ref,
                       h_ref, pos_ref, w_ref, tab_ref, cnt_ref, carry_ref):
    a = _dot(oa_ref[rows, :], wm_ref[...])
    r = _dot(orr_ref[rows, :], wr_ref[...])
    ga = jnp.concatenate([ga0_ref[rows, :], ga1_ref[rows, :]], axis=1).astype(f32)
    gb = jnp.concatenate([gb0_ref[rows, :], gb1_ref[rows, :]], axis=1).astype(f32)
    merged = _sigmoid(ga) * a + _sigmoid(gb) * r
    mix = _dot(merged.astype(bf16), wo_ref[...])
    h = _layer_norm(ALPHA * x_ref[rows, :] + mix, g1_ref[...], b1_ref[...])
    h_ref[rows, :] = h

    h_hi, h_lo = _split_bf16(h)
    w_hi = wrt_hi_ref[...]
    logits = _dot_nt(w_hi, h_hi) + _dot_nt(w_hi, h_lo) + _dot_nt(wrt_lo_ref[...], h_hi)
    scores = _sigmoid(logits)
    biased = scores + rb_ref[...]

    v = biased.reshape(N_GROUPS, GROUP_SIZE, tm)
    sub = lax.broadcasted_iota(i32, v.shape, 1)
    m1 = jnp.max(v, axis=1, keepdims=True)
    i1 = jnp.min(jnp.where(v == m1, sub, GROUP_SIZE), axis=1, keepdims=True)
    m2 = jnp.max(jnp.where(sub == i1, -jnp.inf, v), axis=1, keepdims=True)
    gscore = jnp.broadcast_to(m1 + m2, v.shape).reshape(N_EXPERTS, tm)

    eid = lax.broadcasted_iota(i32, (N_EXPERTS, tm), 0)
    egrp = eid // GROUP_SIZE
    e_mask = jnp.zeros((N_EXPERTS, tm), jnp.bool_)
    for _ in range(TOPK_GROUPS):
        m = jnp.max(gscore, axis=0, keepdims=True)
        idx = jnp.min(jnp.where(gscore == m, egrp, N_GROUPS), axis=0, keepdims=True)
        hit = egrp == idx
        e_mask = e_mask | hit
        gscore = jnp.where(hit, -jnp.inf, gscore)
    cand = jnp.where(e_mask, biased, -jnp.inf)

    chosen = jnp.zeros((N_EXPERTS, tm), jnp.bool_)
    e_rows = []
    w_rows = []
    for _ in range(TOP_K):
        m = jnp.max(cand, axis=0, keepdims=True)
        idx = jnp.min(jnp.where(cand == m, eid, N_EXPERTS), axis=0, keepdims=True)
        hit = eid == idx
        chosen = chosen | hit
        e_rows.append(idx)
        w_rows.append(jnp.sum(jnp.where(hit, scores, 0.0), axis=0, keepdims=True))
        cand = jnp.where(hit, -jnp.inf, cand)
    w_sum = w_rows[0]
    for wk in w_rows[1:]:
        w_sum = w_sum + wk

    t_src = lax.broadcasted_iota(i32, (tm, tm), 0)
    t_dst = lax.broadcasted_iota(i32, (tm, tm), 1)
    before = (t_src < t_dst).astype(bf16)
    chosen_f = chosen.astype(f32)
    prior = _dot(chosen_f.astype(bf16), before)
    cnt = jnp.sum(chosen_f, axis=1, keepdims=True)
    cnt_pad = jnp.ceil(cnt * (1.0 / UNIT)) * UNIT
    e_src = lax.broadcasted_iota(i32, (N_EXPERTS, N_EXPERTS), 1)
    e_dst = lax.broadcasted_iota(i32, (N_EXPERTS, N_EXPERTS), 0)
    earlier = (e_src < e_dst).astype(bf16)
    cnt_pad_l = jnp.broadcast_to(cnt_pad, (N_EXPERTS, LANES))
    loc_start_l = _dot(earlier, cnt_pad_l.astype(bf16))
    loc_start = loc_start_l[:, 0:1]
    where_e = prior + loc_start
    pos_ref[:, rows] = jnp.zeros((pos_ref.shape[0], tm), pos_ref.dtype)
    w_ref[:, rows] = jnp.zeros((w_ref.shape[0], tm), w_ref.dtype)
    for k in range(TOP_K):
        pos_ref[k:k + 1, rows] = jnp.sum(jnp.where(eid == e_rows[k], where_e, 0.0), axis=0, keepdims=True).astype(i32)
        w_ref[k:k + 1, rows] = w_rows[k] / w_sum * ROUTED_SCALE

    carry = carry_ref[...]
    tl = lax.broadcasted_iota(i32, (N_EXPERTS, LANES), 1)
    table = jnp.where(tl == 0, cnt_pad_l, jnp.where(tl == 1, loc_start_l, jnp.broadcast_to(carry, (N_EXPERTS, LANES))))
    tab_ref[t] = table.astype(i32)
    carry = carry + cnt_pad
    carry_ref[...] = carry
    cnt_ref[...] = jnp.broadcast_to(carry, cnt_ref.shape).astype(i32)


def _merge_router(oa, orr, proj, x2d, wm, wr, wo, g1, b1, wrt_hi, wrt_lo, rb):
    tokens = x2d.shape[0]
    tm = MERGE_TILES * TOK_TM
    half = D_MODEL // 2

    def gate_spec(off):
        return pl.BlockSpec((tm, half), lambda i: (i, off // half))

    def full(shape):
        return pl.BlockSpec(shape, lambda i: (0,) * len(shape))

    row8 = pl.BlockSpec((8, tm), lambda i: (0, i))
    return pl.pallas_call(
        _merge_router_kernel,
        grid=(tokens // tm,),
        in_specs=[
            pl.BlockSpec((tm, MOBA_WIDTH), lambda i: (i, 0)),
            pl.BlockSpec((tm, RET_V_WIDTH), lambda i: (i, 0)),
            gate_spec(OFF_GA), gate_spec(OFF_GA + half), gate_spec(OFF_GB), gate_spec(OFF_GB + half),
            pl.BlockSpec((tm, D_MODEL), lambda i: (i, 0)),
            full((MOBA_WIDTH, D_MODEL)), full((RET_V_WIDTH, D_MODEL)), full((D_MODEL, D_MODEL)),
            full((1, D_MODEL)), full((1, D_MODEL)),
            full((N_EXPERTS, D_MODEL)), full((N_EXPERTS, D_MODEL)), full((N_EXPERTS, 1)),
        ],
        out_specs=[
            pl.BlockSpec((tm, D_MODEL), lambda i: (i, 0)),
            row8, row8,
            pl.BlockSpec((MERGE_TILES, N_EXPERTS, LANES), lambda i: (i, 0, 0)),
            full((N_EXPERTS, LANES)),
        ],
        out_shape=[
            jax.ShapeDtypeStruct((tokens, D_MODEL), f32),
            jax.ShapeDtypeStruct((8, tokens), i32),
            jax.ShapeDtypeStruct((8, tokens), f32),
            jax.ShapeDtypeStruct((tokens // TOK_TM, N_EXPERTS, LANES), i32),
            jax.ShapeDtypeStruct((N_EXPERTS, LANES), i32),
        ],
        scratch_shapes=[pltpu.VMEM((N_EXPERTS, 1), f32)],
        compiler_params=_params("arbitrary"),
        name="merge_ln1_router",
    )(oa, orr, proj, proj, proj, proj, x2d, wm, wr, wo, g1, b1, wrt_hi, wrt_lo, rb)


def _pack_rows(x):
    bits = lax.bitcast_convert_type(x, jnp.uint32)
    return (bits[:, :HALF] & jnp.uint32(0xFFFF0000)) | (bits[:, HALF:] >> 16)


def _unpack_rows(words):
    left = lax.bitcast_convert_type(words & jnp.uint32(0xFFFF0000), f32)
    right = lax.bitcast_convert_type(words << 16, f32)
    return left.astype(bf16), right.astype(bf16)


def _unit_slices(local_unit, global_unit):
    loc = pl.ds(pl.multiple_of(local_unit * UNIT, UNIT), UNIT)
    glob = pl.ds(pl.multiple_of(global_unit * UNIT, UNIT), UNIT)
    return loc, glob


def _start_alternating(lo, hi, make_copy):
    n = hi - lo

    def pair(m, _):
        make_copy(lo + 2 * m).start(priority=0)
        make_copy(lo + 2 * m + 1).start(priority=1)
        return 0

    lax.fori_loop(0, n // 2, pair, 0)

    @pl.when(n % 2 == 1)
    def _():
        make_copy(hi - 1).start(priority=0)


def _dispatch_kernel(gu_ref, nu_ref, tail_ref, used_ref, h_ref, pos_ref, rows_ref,
                     xs_ref, zero_ref, sem_ref, zsem_ref):
    step = pl.program_id(0)
    tm = h_ref.shape[0]

    @pl.when(step == 0)
    def _():
        zero_ref[...] = jnp.zeros_like(zero_ref)

        def clear_copy(block):
            start = pl.multiple_of(block * ROW_BLOCK, ROW_BLOCK)
            return pltpu.make_async_copy(zero_ref, rows_ref.at[pl.ds(start, ROW_BLOCK), :], zsem_ref)

        def clear_all(act):
            def tail(e, _):
                tail_block = tail_ref[e]

                @pl.when(tail_block >= 0)
                def _():
                    act(clear_copy(tail_block))
                return 0

            lax.fori_loop(0, N_EXPERTS, tail, 0)

            def unused(b, _):
                act(clear_copy(b))
                return 0

            lax.fori_loop(used_ref[0], rows_ref.shape[0] // ROW_BLOCK, unused, 0)

        clear_all(lambda cp: cp.start())
        clear_all(lambda cp: cp.wait())

    pos = pos_ref[...]
    r_iota = lax.broadcasted_iota(i32, (LOC_ROWS, tm), 0)
    onehot = r_iota == pos[0:1, :]
    for k in range(1, TOP_K):
        onehot = onehot | (r_iota == pos[k:k + 1, :])
    xs = _dot(onehot.astype(bf16), h_ref[...].astype(bf16))
    xs_ref[...] = _pack_rows(xs)

    def unit_copy(s):
        loc, glob = _unit_slices(s, gu_ref[step * LOC_UNITS + s])
        return pltpu.make_async_copy(xs_ref.at[loc, :], rows_ref.at[glob, :], sem_ref)

    _start_alternating(0, nu_ref[step], unit_copy)

    def wait_one(s, _):
        unit_copy(s).wait()
        return 0

    lax.fori_loop(0, nu_ref[step], wait_one, 0)


def _dispatch(unit_map, n_units, tail_block, n_used, h, pos8, n_rows):
    tokens = h.shape[0]
    tm = TOK_TM
    grid_spec = pltpu.PrefetchScalarGridSpec(
        num_scalar_prefetch=4,
        grid=(tokens // tm,),
        in_specs=[
            pl.BlockSpec((tm, D_MODEL), lambda i, *_: (i, 0)),
            pl.BlockSpec((8, tm), lambda i, *_: (0, i)),
        ],
        out_specs=pl.BlockSpec(memory_space=pl.ANY),
        scratch_shapes=[
            pltpu.VMEM((LOC_ROWS, HALF), jnp.uint32),
            pltpu.VMEM((ROW_BLOCK, HALF), jnp.uint32),
            pltpu.SemaphoreType.DMA,
            pltpu.SemaphoreType.DMA,
        ],
    )
    return pl.pallas_call(
        _dispatch_kernel,
        grid_spec=grid_spec,
        out_shape=jax.ShapeDtypeStruct((n_rows, HALF), jnp.uint32),
        compiler_params=_params("arbitrary"),
        name="moe_dispatch",
    )(unit_map, n_units, tail_block, n_used, h, pos8)


def _expert_kernel(eid_ref, used_ref, x_ref, wg_ref, wu_ref, wd_ref, y_ref, wgb_ref, wub_ref, wdb_ref):
    b = pl.program_id(0)
    changed = (b == 0) | (eid_ref[b] != eid_ref[jnp.maximum(b - 1, 0)])

    @pl.when(changed)
    def _():
        wgb_ref[...] = wg_ref[0].astype(bf16)
        wub_ref[...] = wu_ref[0].astype(bf16)
        wdb_ref[...] = wd_ref[0].astype(bf16)

    @pl.when(b < used_ref[0])
    def _():
        for c in range(ROW_BLOCK // EXPERT_CHUNK):
            rows = slice(c * EXPERT_CHUNK, (c + 1) * EXPERT_CHUNK)
            xl, xr = _unpack_rows(x_ref[rows, :])
            g = _dot(xl, wgb_ref[:HALF, :]) + _dot(xr, wgb_ref[HALF:, :])
            u = _dot(xl, wub_ref[:HALF, :]) + _dot(xr, wub_ref[HALF:, :])
            mid = (g * _sigmoid(g)) * u
            y = _dot(mid.astype(bf16), wdb_ref[...])
            y_ref[rows, :] = _pack_rows(y.astype(bf16).astype(f32))

    @pl.when(b >= used_ref[0])
    def _():
        y_ref[...] = jnp.zeros_like(y_ref)


def _experts(block_eid, n_used, rows, wg, wu, wd):
    n_rows = rows.shape[0]
    n_blocks = n_rows // ROW_BLOCK

    def row_map(b, eid, used):
        return (b, 0)

    def w_map(b, eid, used):
        return (eid[b], 0, 0)

    grid_spec = pltpu.PrefetchScalarGridSpec(
        num_scalar_prefetch=2,
        grid=(n_blocks,),
        in_specs=[
            pl.BlockSpec((ROW_BLOCK, HALF), row_map),
            pl.BlockSpec((1, D_MODEL, D_EXPERT), w_map),
            pl.BlockSpec((1, D_MODEL, D_EXPERT), w_map),
            pl.BlockSpec((1, D_EXPERT, D_MODEL), w_map),
        ],
        out_specs=pl.BlockSpec((ROW_BLOCK, HALF), row_map),
        scratch_shapes=[
            pltpu.VMEM((D_MODEL, D_EXPERT), bf16),
            pltpu.VMEM((D_MODEL, D_EXPERT), bf16),
            pltpu.VMEM((D_EXPERT, D_MODEL), bf16),
        ],
    )
    return pl.pallas_call(
        _expert_kernel,
        grid_spec=grid_spec,
        out_shape=jax.ShapeDtypeStruct((n_rows, HALF), jnp.uint32),
        compiler_params=_params("arbitrary"),
        name="moe_experts",
    )(block_eid, n_used, rows, wg, wu, wd)


def _combine_kernel(gu_ref, nu_ref, y_ref, h_ref, pos_ref, wt_ref, p_ref, wsg_ref, wsu_ref, wsd_ref, wpp_ref,
                    wpg_ref, g2_ref, b2_ref, g3_ref, b3_ref, o_ref, ybuf_ref, sem_ref):
    step = pl.program_id(0)
    tm = h_ref.shape[0]

    @pl.when(step == 0)
    def _():
        ybuf_ref[...] = jnp.zeros_like(ybuf_ref)

    def unit_copy(s):
        loc, glob = _unit_slices(s, gu_ref[step * LOC_UNITS + s])
        return pltpu.make_async_copy(y_ref.at[glob, :], ybuf_ref.at[loc, :], sem_ref)

    _start_alternating(0, nu_ref[step], unit_copy)

    def wait_one(s, _):
        unit_copy(s).wait()
        return 0

    lax.fori_loop(0, nu_ref[step], wait_one, 0)

    h = h_ref[...]
    hb = h.astype(bf16)
    sg = _dot(hb, wsg_ref[...])
    su = _dot(hb, wsu_ref[...])
    shared = _dot(((sg * _sigmoid(sg)) * su).astype(bf16), wsd_ref[...])
    ple_in = _dot(p_ref[...].astype(bf16), wpp_ref[...])

    pos = pos_ref[...]
    wt = wt_ref[...]
    c_iota = lax.broadcasted_iota(i32, (tm, LOC_ROWS), 1)
    spread = jnp.where(c_iota == pos[:, 0:1], wt[:, 0:1], 0.0)
    for k in range(1, TOP_K):
        spread = jnp.where(c_iota == pos[:, k:k + 1], wt[:, k:k + 1], spread)
    spread = spread.astype(bf16)

    yl, yr = _unpack_rows(ybuf_ref[...])
    routed = jnp.concatenate([_dot(spread, yl), _dot(spread, yr)], axis=1)

    h2 = _layer_norm(ALPHA * h + (routed + shared), g2_ref[...], b2_ref[...])
    ple = ple_in * _sigmoid(_dot(h2.astype(bf16), wpg_ref[...]))
    o_ref[...] = _layer_norm(ALPHA * h2 + ple, g3_ref[...], b3_ref[...])


def _combine(unit_map, n_units, y_rows, h, pos_tok, w_tok, p2d, wsg, wsu, wsd, wpp, wpg, g2, b2, g3, b3):
    tokens = h.shape[0]
    tm = TOK_TM

    def full(shape):
        return pl.BlockSpec(shape, lambda i, *_: (0,) * len(shape))

    def tile(width):
        return pl.BlockSpec((tm, width), lambda i, *_: (i, 0))

    grid_spec = pltpu.PrefetchScalarGridSpec(
        num_scalar_prefetch=2,
        grid=(tokens // tm,),
        in_specs=[
            pl.BlockSpec(memory_space=pl.ANY),
            tile(D_MODEL), tile(8), tile(8), tile(PLE_DIM),
            full((D_MODEL, D_SHARED)), full((D_MODEL, D_SHARED)), full((D_SHARED, D_MODEL)),
            full((PLE_DIM, D_MODEL)), full((D_MODEL, D_MODEL)),
            full((1, D_MODEL)), full((1, D_MODEL)), full((1, D_MODEL)), full((1, D_MODEL)),
        ],
        out_specs=tile(D_MODEL),
        scratch_shapes=[
            pltpu.VMEM((LOC_ROWS, HALF), jnp.uint32),
            pltpu.SemaphoreType.DMA,
        ],
    )
    return pl.pallas_call(
        _combine_kernel,
        grid_spec=grid_spec,
        out_shape=jax.ShapeDtypeStruct((tokens, D_MODEL), f32),
        compiler_params=_params("arbitrary"),
        name="moe_combine_ln2_ple_ln3",
    )(unit_map, n_units, y_rows, h, pos_tok, w_tok, p2d, wsg, wsu, wsd, wpp, wpg, g2, b2, g3, b3)


def _row_layout(table, totals, n_blocks):
    cnt_u = table[:, :, 0] // UNIT
    loc_u = table[:, :, 1] // UNIT
    padded = (totals + ROW_BLOCK - 1) // ROW_BLOCK * ROW_BLOCK
    pends = jnp.cumsum(padded)
    pstarts = pends - padded
    glob_u = (pstarts[None, :] + table[:, :, 2]) // UNIT
    n_units = jnp.sum(cnt_u, axis=1).astype(i32)
    s = jnp.arange(LOC_UNITS, dtype=i32)
    loc_end = loc_u + cnt_u
    e_of = jnp.sum((loc_end[:, None, :] <= s[None, :, None]).astype(i32), axis=-1)
    is_e = e_of[:, :, None] == jnp.arange(N_EXPERTS, dtype=i32)
    shift = jnp.sum(jnp.where(is_e, (glob_u - loc_u)[:, None, :], 0), axis=-1)
    unit_map = (shift + s[None, :]).astype(i32).reshape(-1)
    tail_block = jnp.where(totals > 0, pends // ROW_BLOCK - 1, -1).astype(i32)
    block_start = jnp.arange(n_blocks, dtype=i32) * ROW_BLOCK
    ends_before = jnp.sum((pends[None, :] <= block_start[:, None]).astype(i32), axis=1)
    block_eid = jnp.minimum(ends_before, N_EXPERTS - 1).astype(i32)
    n_used = (pends[-1:] // ROW_BLOCK).astype(i32)
    return unit_map, n_units, tail_block, block_eid, n_used


def _layer(x2d, p2d, batch, seq, w_in, w_moba_up, w_ret_up, w_out, ln1_g, ln1_b, w_router, router_bias,
           w_eg, w_eu, w_ed, w_sg, w_su, w_sd, ln2_g, ln2_b, w_ple_proj, w_ple_gate, ln3_g, ln3_b):
    tokens = batch * seq
    row = lambda a: a.reshape(1, -1).astype(f32)

    proj = _project(x2d, w_in.astype(bf16), seq)
    oa = _moba(proj, batch, seq)
    orr = _retention(proj, batch, seq)

    wrt = w_router.astype(f32).T
    wrt_hi = wrt.astype(bf16)
    wrt_lo = (wrt - wrt_hi.astype(f32)).astype(bf16)
    h1, pos8, w8, table, totals = _merge_router(
        oa, orr, proj, x2d, w_moba_up.astype(bf16), w_ret_up.astype(bf16), w_out.astype(bf16),
        row(ln1_g), row(ln1_b), wrt_hi, wrt_lo, router_bias.astype(f32).reshape(N_EXPERTS, 1))

    n_tiles = tokens // TOK_TM
    max_rows = tokens * TOP_K + n_tiles * N_EXPERTS * (UNIT - 1) + N_EXPERTS * (ROW_BLOCK - 1)
    n_blocks = -(-max_rows // ROW_BLOCK)
    unit_map, n_units, tail_block, block_eid, n_used = _row_layout(table[:, :, :3], totals[:, 0], n_blocks)

    rows = _dispatch(unit_map, n_units, tail_block, n_used, h1, pos8, n_blocks * ROW_BLOCK)
    y_rows = _experts(block_eid, n_used, rows, w_eg, w_eu, w_ed)
    return _combine(unit_map, n_units, y_rows, h1, pos8.T, w8.T, p2d,
                    w_sg.astype(bf16), w_su.astype(bf16), w_sd.astype(bf16),
                    w_ple_proj.astype(bf16), w_ple_gate.astype(bf16),
                    row(ln2_g), row(ln2_b), row(ln3_g), row(ln3_b))


def kernel(x, p, w_in, w_moba_up, w_ret_up, w_out, ln1_g, ln1_b, w_router, router_bias, w_exp_gate, w_exp_up,
           w_exp_down, w_sh_gate, w_sh_up, w_sh_down, ln2_g, ln2_b, w_ple_proj, w_ple_gate, ln3_g, ln3_b):
    batch, seq, d = x.shape
    assert d == D_MODEL and seq % max(MOBA_BLOCK, RET_CHUNK) == 0 and (batch * seq) % (MERGE_TILES * TOK_TM) == 0
    assert w_in.shape[0] == DEPTH
    h = x.reshape(batch * seq, d)
    for i in range(DEPTH):
        h = _layer(h, p[i].reshape(batch * seq, PLE_DIM), batch, seq,
                   w_in[i], w_moba_up[i], w_ret_up[i], w_out[i], ln1_g[i], ln1_b[i], w_router[i], router_bias[i],
                   w_exp_gate[i], w_exp_up[i], w_exp_down[i], w_sh_gate[i], w_sh_up[i], w_sh_down[i],
                   ln2_g[i], ln2_b[i], w_ple_proj[i], w_ple_gate[i], ln3_g[i], ln3_b[i])
    return h.reshape(batch, seq, d)
```

```python
import functools

import jax
import jax.numpy as jnp
from jax import lax
from jax.experimental import pallas as pl
from jax.experimental.pallas import tpu as pltpu

f32 = jnp.float32
bf16 = jnp.bfloat16
i32 = jnp.int32

D_MODEL = 1024
DEPTH = 1
MOBA_HEADS = 8
MOBA_HEAD_DIM = 64
MOBA_WIDTH = MOBA_HEADS * MOBA_HEAD_DIM
MOBA_BLOCK = 256
MOBA_TOPK = 3
ROPE_THETA = 10000.0
RET_HEADS = 4
RET_QK_DIM = 128
RET_V_DIM = 256
RET_QK_WIDTH = RET_HEADS * RET_QK_DIM
RET_V_WIDTH = RET_HEADS * RET_V_DIM
RET_CHUNK = 256
RET_ANGLE_BASE = 10000.0
N_IN = 3 * MOBA_WIDTH + 2 * RET_QK_WIDTH + 2 * RET_V_WIDTH + 2 * D_MODEL
N_EXPERTS = 64
TOP_K = 6
N_GROUPS = 8
TOPK_GROUPS = 4
GROUP_SIZE = N_EXPERTS // N_GROUPS
D_EXPERT = 256
D_SHARED = 256
ROUTED_SCALE = 2.5
PLE_DIM = 256
LN_EPS = 1e-5
GN_EPS = 1e-6
ALPHA = (2.0 * DEPTH) ** 0.25

OFF_QA = 0
OFF_KA = MOBA_WIDTH
OFF_VA = 2 * MOBA_WIDTH
OFF_QR = 3 * MOBA_WIDTH
OFF_KR = OFF_QR + RET_QK_WIDTH
OFF_VR = OFF_KR + RET_QK_WIDTH
OFF_GR = OFF_VR + RET_V_WIDTH
OFF_GA = OFF_GR + RET_V_WIDTH
OFF_GB = OFF_GA + D_MODEL

LANES = 128
NEG = -1e30
VMEM_LIMIT = 56 * 1024 * 1024

PROJ_TM = 2048
PROJ_TN = 512
MOBA_STEP_WIDTH = 256
MERGE_TILES = 2
TOK_TM = 256
ROW_BLOCK = 512
EXPERT_CHUNK = 256
UNIT = 8
LOC_ROWS = -(-(TOP_K * TOK_TM + N_EXPERTS * (UNIT - 1)) // LANES) * LANES
RUN_CLASSES = (TOK_TM // UNIT).bit_length()
HALF = D_MODEL // 2


def _dot(a, b):
    return jnp.dot(a, b, preferred_element_type=f32)


def _dot_nt(a, b):
    return lax.dot_general(a, b, (((1,), (1,)), ((), ())), preferred_element_type=f32)


def _dot_tn(a, b):
    return lax.dot_general(a, b, (((0,), (0,)), ((), ())), preferred_element_type=f32)


def _split_bf16(a):
    hi = a.astype(bf16)
    lo = (a - hi.astype(f32)).astype(bf16)
    return hi, lo


def _layer_norm(x, g, b):
    mu = jnp.mean(x, axis=-1, keepdims=True)
    xc = x - mu
    var = jnp.mean(xc * xc, axis=-1, keepdims=True)
    return xc * lax.rsqrt(var + LN_EPS) * g + b


def _sigmoid(x):
    return 1.0 / (1.0 + jnp.exp(-x))


def _params(*sem, flags=None):
    return pltpu.CompilerParams(dimension_semantics=sem, vmem_limit_bytes=VMEM_LIMIT, flags=flags)


def _rotate_half_chunk(xc, half):
    if 2 * half == LANES:
        return pltpu.roll(xc, half, axis=1)
    lane = lax.broadcasted_iota(i32, xc.shape, 1)
    first = (lane & (2 * half - 1)) < half
    return jnp.where(first, pltpu.roll(xc, LANES - half, axis=1), pltpu.roll(xc, half, axis=1))


def _proj_kernel(x_ref, w_ref, cos_a_ref, sin_a_ref, cos_r_ref, sin_r_ref, o_ref, xb_ref):
    j = pl.program_id(1)

    @pl.when(j == 0)
    def _():
        xb_ref[...] = x_ref[...].astype(bf16)

    acc = _dot(xb_ref[...], w_ref[...])

    def rotary(cos_ref, sin_ref, half, scale):
        cos = cos_ref[...]
        sin = sin_ref[...]
        for c in range(PROJ_TN // LANES):
            xc = acc[:, c * LANES:(c + 1) * LANES]
            y = xc * cos + _rotate_half_chunk(xc, half) * sin
            if scale != 1.0:
                y = y * scale
            o_ref[:, c * LANES:(c + 1) * LANES] = y.astype(o_ref.dtype)

    j_qa = OFF_QA // PROJ_TN
    j_ka = OFF_KA // PROJ_TN
    j_qr = OFF_QR // PROJ_TN
    j_kr = OFF_KR // PROJ_TN

    @pl.when(j == j_qa)
    def _():
        rotary(cos_a_ref, sin_a_ref, MOBA_HEAD_DIM // 2, MOBA_HEAD_DIM ** -0.5)

    @pl.when(j == j_ka)
    def _():
        rotary(cos_a_ref, sin_a_ref, MOBA_HEAD_DIM // 2, 1.0)

    @pl.when(j == j_qr)
    def _():
        rotary(cos_r_ref, sin_r_ref, RET_QK_DIM // 2, 1.0)

    @pl.when(j == j_kr)
    def _():
        rotary(cos_r_ref, sin_r_ref, RET_QK_DIM // 2, RET_QK_DIM ** -0.5)

    plain = (j != j_qa) & (j != j_ka) & (j != j_qr) & (j != j_kr)

    @pl.when(plain)
    def _():
        o_ref[...] = acc.astype(o_ref.dtype)


def _rotary_tables(seq, inv_freq, head_dim):
    ang = jnp.arange(seq, dtype=f32)[:, None] * inv_freq[None, :]
    cos = jnp.cos(ang)
    sin = jnp.sin(ang)
    cos_h = jnp.concatenate([cos, cos], axis=-1)
    sin_h = jnp.concatenate([-sin, sin], axis=-1)
    reps = LANES // head_dim
    return jnp.tile(cos_h, (1, reps)), jnp.tile(sin_h, (1, reps))


def _project(x2d, w_in_b, seq):
    tokens = x2d.shape[0]
    tm = min(PROJ_TM, seq)
    inv_a = 1.0 / (ROPE_THETA ** (jnp.arange(0, MOBA_HEAD_DIM, 2, dtype=f32) / MOBA_HEAD_DIM))
    inv_r = 1.0 / (RET_ANGLE_BASE ** jnp.linspace(0.0, 1.0, RET_QK_DIM // 2, dtype=f32))
    cos_a, sin_a = _rotary_tables(seq, inv_a, MOBA_HEAD_DIM)
    cos_r, sin_r = _rotary_tables(seq, inv_r, RET_QK_DIM)
    seq_tiles = seq // tm
    tab = pl.BlockSpec((tm, LANES), lambda i, j: (i % seq_tiles, 0))
    return pl.pallas_call(
        _proj_kernel,
        grid=(tokens // tm, N_IN // PROJ_TN),
        in_specs=[
            pl.BlockSpec((tm, D_MODEL), lambda i, j: (i, 0)),
            pl.BlockSpec((D_MODEL, PROJ_TN), lambda i, j: (0, j)),
            tab, tab, tab, tab,
        ],
        out_specs=pl.BlockSpec((tm, PROJ_TN), lambda i, j: (i, j)),
        out_shape=jax.ShapeDtypeStruct((tokens, N_IN), bf16),
        scratch_shapes=[pltpu.VMEM((tm, D_MODEL), bf16)],
        compiler_params=_params("parallel", "arbitrary"),
        name="in_proj_rotary",
    )(x2d, w_in_b, cos_a, sin_a, cos_r, sin_r)


def _moba_kernel(q_ref, k_ref, v_ref, o_ref, kmean_ref, vt_ref, selb_ref, gate_ref, *, n_blocks):
    i = pl.program_id(2)
    blk = MOBA_BLOCK
    hd = MOBA_HEAD_DIM
    width = q_ref.shape[1]
    chunks = width // LANES
    heads = width // hd
    per_chunk = LANES // hd

    def chunk_of(h):
        return slice(h // per_chunk * LANES, (h // per_chunk + 1) * LANES)

    @pl.when(i == 0)
    def _():
        seq = n_blocks * blk
        member = (lax.broadcasted_iota(i32, (n_blocks, seq), 1) // blk
                  == lax.broadcasted_iota(i32, (n_blocks, seq), 0))
        kmean_ref[...] = _dot(member.astype(bf16), k_ref[...]) * (1.0 / blk)
        for j in range(n_blocks):
            vt_ref[j] = v_ref[j * blk:(j + 1) * blk, :].astype(f32).T.astype(bf16)

    km_hi, km_lo = _split_bf16(kmean_ref[...])
    key_i = lax.broadcasted_iota(i32, (blk, blk), 0)
    qry_i = lax.broadcasted_iota(i32, (blk, blk), 1)
    causal = key_i <= qry_i
    bid = lax.broadcasted_iota(i32, (n_blocks, blk), 0)
    d_row = lax.broadcasted_iota(i32, (LANES, blk), 0)
    qt = q_ref[...].astype(f32).T

    qts = []
    for h in range(heads):
        r0 = h % per_chunk * hd
        in_head = (d_row >= r0) & (d_row < r0 + hd)
        qth = jnp.where(in_head, qt[chunk_of(h), :], 0.0).astype(bf16)
        qts.append(qth)
        gate = _dot(km_hi[:, chunk_of(h)], qth) + _dot(km_lo[:, chunk_of(h)], qth)
        g = jnp.where(bid < i, gate, -jnp.inf)
        gate_ref[...] = g
        beaten = jnp.zeros((n_blocks, blk), f32)
        for j in range(n_blocks):
            gj = jnp.broadcast_to(gate_ref[j:j + 1, :], (n_blocks, blk))
            beats = (gj > g) | ((gj == g) & (bid > j))
            beaten = beaten + jnp.where(beats, 1.0, 0.0)
        sel = (beaten < MOBA_TOPK) & (bid < i)
        selb_ref[h] = jnp.where(sel, 0.0, NEG)

    kj = k_ref[pl.ds(pl.multiple_of(i * blk, blk), blk), :]
    vtj = vt_ref[i]
    state = []
    for h in range(heads):
        s = jnp.where(causal, _dot(kj[:, chunk_of(h)], qts[h]), NEG)
        m0 = jnp.max(s, axis=0, keepdims=True)
        p = jnp.exp(s - m0)
        l0 = jnp.sum(p, axis=0, keepdims=True)
        acc0 = _dot(vtj[h * hd:(h + 1) * hd, :], p.astype(bf16))
        state += [m0, l0, acc0]

    qt_alls = [jnp.concatenate(qts[c * per_chunk:(c + 1) * per_chunk], axis=1) for c in range(chunks)]

    def update(carry, js):
        nj = len(js)
        k_all = k_ref[pl.ds(pl.multiple_of(js[0] * blk, blk), nj * blk), :]
        s_alls = [_dot(k_all[:, c * LANES:(c + 1) * LANES], qt_alls[c]) for c in range(chunks)]
        vtjs = [vt_ref[j] for j in js]
        new = []
        for h in range(heads):
            m_old, l_old, acc = carry[3 * h:3 * h + 3]
            col = h % per_chunk * blk
            ss = [s_alls[h // per_chunk][n * blk:(n + 1) * blk, col:col + blk] + selb_ref[h, pl.ds(j, 1), :]
                  for n, j in enumerate(js)]
            m_new = m_old
            for s in ss:
                m_new = jnp.maximum(m_new, jnp.max(s, axis=0, keepdims=True))
            a = jnp.exp(m_old - m_new)
            l_new = a * l_old
            acc = a * acc
            for s, vtj in zip(ss, vtjs):
                p = jnp.exp(s - m_new)
                l_new = l_new + jnp.sum(p, axis=0, keepdims=True)
                acc = acc + _dot(vtj[h * hd:(h + 1) * hd, :], p.astype(bf16))
            new += [m_new, l_new, acc]
        return tuple(new)

    quads = i // 4
    fin = lax.fori_loop(0, quads, lambda n, c: update(c, tuple(4 * n + t for t in range(4))), tuple(state))
    done = 4 * quads
    has_pair = (i - done) // 2
    fin = lax.fori_loop(0, has_pair, lambda n, c: update(c, (done, done + 1)), fin)
    done = done + 2 * has_pair
    fin = lax.fori_loop(done, i, lambda j, c: update(c, (j,)), fin)
    out_t = jnp.concatenate([fin[3 * h + 2] / fin[3 * h + 1] for h in range(heads)], axis=0)
    o_ref[...] = out_t.T.astype(o_ref.dtype)


def _moba(proj, batch, seq):
    tokens = batch * seq
    n_blocks = seq // MOBA_BLOCK
    width = MOBA_STEP_WIDTH
    return pl.pallas_call(
        functools.partial(_moba_kernel, n_blocks=n_blocks),
        grid=(batch, MOBA_WIDTH // width, n_blocks),
        in_specs=[
            pl.BlockSpec((MOBA_BLOCK, width), lambda b, c, i: (b * n_blocks + i, OFF_QA // width + c)),
            pl.BlockSpec((seq, width), lambda b, c, i: (b, OFF_KA // width + c)),
            pl.BlockSpec((seq, width), lambda b, c, i: (b, OFF_VA // width + c)),
        ],
        out_specs=pl.BlockSpec((MOBA_BLOCK, width), lambda b, c, i: (b * n_blocks + i, c)),
        out_shape=jax.ShapeDtypeStruct((tokens, MOBA_WIDTH), bf16),
        scratch_shapes=[
            pltpu.VMEM((n_blocks, width), f32),
            pltpu.VMEM((n_blocks, width, MOBA_BLOCK), bf16),
            pltpu.VMEM((width // MOBA_HEAD_DIM, n_blocks, MOBA_BLOCK), f32),
            pltpu.VMEM((n_blocks, MOBA_BLOCK), f32),
        ],
        compiler_params=_params("parallel", "parallel", "arbitrary"),
        name="moba_attention",
    )(proj, proj, proj)


def _ret_kernel(q_ref, k_ref, v0_ref, v1_ref, g0_ref, g1_ref, decay_ref, zeta_ref, xi_ref, cd_ref,
                o_ref, state_ref):
    c = pl.program_id(1)
    heads_per_half = RET_HEADS // 2
    v_refs = (v0_ref, v1_ref)
    g_refs = (g0_ref, g1_ref)

    @pl.when(c == 0)
    def _():
        state_ref[...] = jnp.zeros_like(state_ref)

    for h in range(RET_HEADS):
        qk_cols = slice(h * RET_QK_DIM, (h + 1) * RET_QK_DIM)
        v_cols = slice(h * RET_V_DIM, (h + 1) * RET_V_DIM)
        half_cols = slice((h % heads_per_half) * RET_V_DIM, (h % heads_per_half + 1) * RET_V_DIM)
        q = q_ref[:, qk_cols]
        k = k_ref[:, qk_cols]
        v = v_refs[h // heads_per_half][:, half_cols]
        state = state_ref[h]
        scores = _dot_nt(q, k) * decay_ref[h]
        inner = _dot(scores.astype(bf16), v)
        cross = _dot(q, state.astype(bf16)) * xi_ref[h]
        o = inner + cross
        kz = (k.astype(f32) * zeta_ref[h]).astype(bf16)
        state_ref[h] = cd_ref[h] * state + _dot_tn(kz, v)

        mu = jnp.mean(o, axis=-1, keepdims=True)
        oc = o - mu
        var = jnp.mean(oc * oc, axis=-1, keepdims=True)
        on = oc * lax.rsqrt(var + GN_EPS)
        g = g_refs[h // heads_per_half][:, half_cols].astype(f32)
        o_ref[:, v_cols] = (on * (g * _sigmoid(g))).astype(o_ref.dtype)


def _retention(proj, batch, seq):
    tokens = batch * seq
    C = RET_CHUNK
    n_chunks = seq // C
    half_v = RET_V_WIDTH // 2
    gammas =1.0 - 2.0 ** (-5.0 - jnp.arange(RET_HEADS, dtype=f32))
    log_g = jnp.log(gammas)
    idx = jnp.arange(C, dtype=f32)
    diff = idx[:, None] - idx[None, :]
    decay = jnp.where(diff >= 0, jnp.exp(jnp.maximum(diff, 0.0)[None] * log_g[:, None, None]), 0.0)
    zeta = jnp.exp((C - 1 - idx)[None, :] * log_g[:, None])
    xi = jnp.exp((idx + 1.0)[None, :] * log_g[:, None])
    zeta_t = jnp.broadcast_to(zeta[:, :, None], (RET_HEADS, C, RET_QK_DIM))
    xi_t = jnp.broadcast_to(xi[:, :, None], (RET_HEADS, C, RET_V_DIM))
    cd_t = jnp.broadcast_to(jnp.exp(C * log_g)[:, None, None], (RET_HEADS, 1, RET_V_DIM))
    return pl.pallas_call(
        _ret_kernel,
        grid=(batch, n_chunks),
        in_specs=[
            pl.BlockSpec((C, RET_QK_WIDTH), lambda b, c: (b * n_chunks + c, OFF_QR // RET_QK_WIDTH)),
            pl.BlockSpec((C, RET_QK_WIDTH), lambda b, c: (b * n_chunks + c, OFF_KR // RET_QK_WIDTH)),
            pl.BlockSpec((C, half_v), lambda b, c: (b * n_chunks + c, OFF_VR // half_v)),
            pl.BlockSpec((C, half_v), lambda b, c: (b * n_chunks + c, OFF_VR // half_v + 1)),
            pl.BlockSpec((C, half_v), lambda b, c: (b * n_chunks + c, OFF_GR // half_v)),
            pl.BlockSpec((C, half_v), lambda b, c: (b * n_chunks + c, OFF_GR // half_v + 1)),
            pl.BlockSpec((RET_HEADS, C, C), lambda b, c: (0, 0, 0)),
            pl.BlockSpec((RET_HEADS, C, RET_QK_DIM), lambda b, c: (0, 0, 0)),
            pl.BlockSpec((RET_HEADS, C, RET_V_DIM), lambda b, c: (0, 0, 0)),
            pl.BlockSpec((RET_HEADS, 1, RET_V_DIM), lambda b, c: (0, 0, 0)),
        ],
        out_specs=pl.BlockSpec((C, RET_V_WIDTH), lambda b, c: (b * n_chunks + c, 0)),
        out_shape=jax.ShapeDtypeStruct((tokens, RET_V_WIDTH), bf16),
        scratch_shapes=[pltpu.VMEM((RET_HEADS, RET_QK_DIM, RET_V_DIM), f32)],
        compiler_params=_params("parallel", "arbitrary"),
        name="retention",
    )(proj, proj, proj, proj, proj, proj, decay, zeta_t, xi_t, cd_t)


def _merge_router_kernel(oa_ref, orr_ref, ga0_ref, ga1_ref, gb0_ref, gb1_ref, x_ref,
                         wm_ref, wr_ref, wo_ref, g1_ref, b1_ref, wrt_hi_ref, wrt_lo_ref, rb_ref,
                         h_ref, pos_ref, w_ref, tab_ref, cnt_ref, carry_ref):
    step = pl.program_id(0)
    tm = TOK_TM

    @pl.when(step == 0)
    def _():
        carry_ref[...] = jnp.zeros_like(carry_ref)

    for t in range(x_ref.shape[0] // tm):
        rows = slice(t * tm, (t + 1) * tm)
        _merge_router_tile(t, rows, tm, oa_ref, orr_ref, ga0_ref, ga1_ref, gb0_ref, gb1_ref, x_ref,
                           wm_ref, wr_ref, wo_ref, g1_ref, b1_ref, wrt_hi_ref, wrt_lo_ref, rb_ref,
                           h_ref, pos_ref, w_ref, tab_ref, cnt_ref, carry_ref)


def _merge_router_tile(t, rows, tm, oa_ref, orr_ref, ga0_ref, ga1_ref, gb0_ref, gb1_ref, x_ref,
                       wm_ref, wr_ref, wo_ref, g1_ref, b1_ref, wrt_hi_ref, wrt_lo_ref, rb_ref,
                       h_ref, pos_ref, w_ref, tab_ref, cnt_ref, carry_ref):
    a = _dot(oa_ref[rows, :], wm_ref[...])
    r = _dot(orr_ref[rows, :], wr_ref[...])
    ga = jnp.concatenate([ga0_ref[rows, :], ga1_ref[rows, :]], axis=1).astype(f32)
    gb = jnp.concatenate([gb0_ref[rows, :], gb1_ref[rows, :]], axis=1).astype(f32)
    merged = _sigmoid(ga) * a + _sigmoid(gb) * r
    mix = _dot(merged.astype(bf16), wo_ref[...])
    h = _layer_norm(ALPHA * x_ref[rows, :] + mix, g1_ref[...], b1_ref[...])
    h_ref[rows, :] = h

    h_hi, h_lo = _split_bf16(h)
    w_hi = wrt_hi_ref[...]
    logits = _dot_nt(w_hi, h_hi) + _dot_nt(w_hi, h_lo) + _dot_nt(wrt_lo_ref[...], h_hi)
    scores = _sigmoid(logits)
    biased = scores + rb_ref[...]

    v = biased.reshape(N_GROUPS, GROUP_SIZE, tm)
    sub = lax.broadcasted_iota(i32, v.shape, 1)
    m1 = jnp.max(v, axis=1, keepdims=True)
    i1 = jnp.min(jnp.where(v == m1, sub, GROUP_SIZE), axis=1, keepdims=True)
    m2 = jnp.max(jnp.where(sub == i1, -jnp.inf, v), axis=1, keepdims=True)
    gscore = jnp.broadcast_to(m1 + m2, v.shape).reshape(N_EXPERTS, tm)

    eid = lax.broadcasted_iota(i32, (N_EXPERTS, tm), 0)
    egrp = eid // GROUP_SIZE
    e_mask = jnp.zeros((N_EXPERTS, tm), jnp.bool_)
    for _ in range(TOPK_GROUPS):
        m = jnp.max(gscore, axis=0, keepdims=True)
        idx = jnp.min(jnp.where(gscore == m, egrp, N_GROUPS), axis=0, keepdims=True)
        hit = egrp == idx
        e_mask = e_mask | hit
        gscore = jnp.where(hit, -jnp.inf, gscore)
    cand = jnp.where(e_mask, biased, -jnp.inf)

    chosen = jnp.zeros((N_EXPERTS, tm), jnp.bool_)
    e_rows = []
    w_rows = []
    for _ in range(TOP_K):
        m = jnp.max(cand, axis=0, keepdims=True)
        idx = jnp.min(jnp.where(cand == m, eid, N_EXPERTS), axis=0, keepdims=True)
        hit = eid == idx
        chosen = chosen | hit
        e_rows.append(idx)
        w_rows.append(jnp.sum(jnp.where(hit, scores, 0.0), axis=0, keepdims=True))
        cand = jnp.where(hit, -jnp.inf, cand)
    w_sum = w_rows[0]
    for wk in w_rows[1:]:
        w_sum = w_sum + wk

    t_src = lax.broadcasted_iota(i32, (tm, tm), 0)
    t_dst = lax.broadcasted_iota(i32, (tm, tm), 1)
    before = (t_src < t_dst).astype(bf16)
    chosen_f = chosen.astype(f32)
    prior = _dot(chosen_f.astype(bf16), before)
    cnt = jnp.sum(chosen_f, axis=1, keepdims=True)
    cnt_pad = jnp.ceil(cnt * (1.0 / UNIT)) * UNIT
    e_src = lax.broadcasted_iota(i32, (N_EXPERTS, N_EXPERTS), 1)
    e_dst = lax.broadcasted_iota(i32, (N_EXPERTS, N_EXPERTS), 0)
    earlier = (e_src < e_dst).astype(bf16)
    cnt_pad_l = jnp.broadcast_to(cnt_pad, (N_EXPERTS, LANES))
    loc_start_l = _dot(earlier, cnt_pad_l.astype(bf16))
    loc_start = loc_start_l[:, 0:1]
    where_e = prior + loc_start
    pos_ref[:, rows] = jnp.zeros((pos_ref.shape[0], tm), pos_ref.dtype)
    w_ref[:, rows] = jnp.zeros((w_ref.shape[0], tm), w_ref.dtype)
    for k in range(TOP_K):
        pos_ref[k:k + 1, rows] = jnp.sum(jnp.where(eid == e_rows[k], where_e, 0.0), axis=0, keepdims=True).astype(i32)
        w_ref[k:k + 1, rows] = w_rows[k] / w_sum * ROUTED_SCALE

    carry = carry_ref[...]
    tl = lax.broadcasted_iota(i32, (N_EXPERTS, LANES), 1)
    table = jnp.where(tl == 0, cnt_pad_l, jnp.where(tl == 1, loc_start_l, jnp.broadcast_to(carry, (N_EXPERTS, LANES))))
    tab_ref[t] = table.astype(i32)
    carry = carry + cnt_pad
    carry_ref[...] = carry
    cnt_ref[...] = jnp.broadcast_to(carry, cnt_ref.shape).astype(i32)


def _merge_router(oa, orr, proj, x2d, wm, wr, wo, g1, b1, wrt_hi, wrt_lo, rb):
    tokens = x2d.shape[0]
    tm = MERGE_TILES * TOK_TM
    half = D_MODEL // 2

    def gate_spec(off):
        return pl.BlockSpec((tm, half), lambda i: (i, off // half))

    def full(shape):
        return pl.BlockSpec(shape, lambda i: (0,) * len(shape))

    row8 = pl.BlockSpec((8, tm), lambda i: (0, i))
    return pl.pallas_call(
        _merge_router_kernel,
        grid=(tokens // tm,),
        in_specs=[
            pl.BlockSpec((tm, MOBA_WIDTH), lambda i: (i, 0)),
            pl.BlockSpec((tm, RET_V_WIDTH), lambda i: (i, 0)),
            gate_spec(OFF_GA), gate_spec(OFF_GA + half), gate_spec(OFF_GB), gate_spec(OFF_GB + half),
            pl.BlockSpec((tm, D_MODEL), lambda i: (i, 0)),
            full((MOBA_WIDTH, D_MODEL)), full((RET_V_WIDTH, D_MODEL)), full((D_MODEL, D_MODEL)),
            full((1, D_MODEL)), full((1, D_MODEL)),
            full((N_EXPERTS, D_MODEL)), full((N_EXPERTS, D_MODEL)), full((N_EXPERTS, 1)),
        ],
        out_specs=[
            pl.BlockSpec((tm, D_MODEL), lambda i: (i, 0)),
            row8, row8,
            pl.BlockSpec((MERGE_TILES, N_EXPERTS, LANES), lambda i: (i, 0, 0)),
            full((N_EXPERTS, LANES)),
        ],
        out_shape=[
            jax.ShapeDtypeStruct((tokens, D_MODEL), f32),
            jax.ShapeDtypeStruct((8, tokens), i32),
            jax.ShapeDtypeStruct((8, tokens), f32),
            jax.ShapeDtypeStruct((tokens // TOK_TM, N_EXPERTS, LANES), i32),
            jax.ShapeDtypeStruct((N_EXPERTS, LANES), i32),
        ],
        scratch_shapes=[pltpu.VMEM((N_EXPERTS, 1), f32)],
        compiler_params=_params("arbitrary"),
        name="merge_ln1_router",
    )(oa, orr, proj, proj, proj, proj, x2d, wm, wr, wo, g1, b1, wrt_hi, wrt_lo, rb)


def _pack_rows(x):
    bits = lax.bitcast_convert_type(x, jnp.uint32)
    return (bits[:, :HALF] & jnp.uint32(0xFFFF0000)) | (bits[:, HALF:] >> 16)


def _unpack_rows(words):
    left = lax.bitcast_convert_type(words & jnp.uint32(0xFFFF0000), f32)
    right = lax.bitcast_convert_type(words << 16, f32)
    return left.astype(bf16), right.astype(bf16)


def _start_alternating(n, make_copy):
    def pair(m, _):
        make_copy(2 * m).start(priority=0)
        make_copy(2 * m + 1).start(priority=1)
        return 0

    lax.fori_loop(0, n // 2, pair, 0)

    @pl.when(n % 2 == 1)
    def _():
        make_copy(n - 1).start(priority=0)


def _move_runs(tile, n_ref, src_ref, dst_ref, make_copy):
    def piece(c, p):
        rows = UNIT << c
        base = (tile * RUN_CLASSES + c) * N_EXPERTS
        loc = pl.ds(pl.multiple_of(src_ref[base + p] * UNIT, UNIT), rows)
        glob = pl.ds(pl.multiple_of(dst_ref[base + p] * UNIT, UNIT), rows)
        return make_copy(loc, glob)

    for c in range(RUN_CLASSES):
        _start_alternating(n_ref[tile * RUN_CLASSES + c], functools.partial(piece, c))

    for c in range(RUN_CLASSES):
        def wait_one(p, _, c=c):
            piece(c, p).wait()
            return 0

        lax.fori_loop(0, n_ref[tile * RUN_CLASSES + c], wait_one, 0)


def _dispatch_kernel(n_ref, src_ref, dst_ref, tail_ref, used_ref, h_ref, pos_ref, rows_ref,
                     xs_ref, zero_ref, sem_ref, zsem_ref):
    step = pl.program_id(0)
    tm = h_ref.shape[0]

    @pl.when(step == 0)
    def _():
        zero_ref[...] = jnp.zeros_like(zero_ref)

        def clear_copy(block):
            start = pl.multiple_of(block * ROW_BLOCK, ROW_BLOCK)
            return pltpu.make_async_copy(zero_ref, rows_ref.at[pl.ds(start, ROW_BLOCK), :], zsem_ref)

        def clear_all(act):
            def tail(e, _):
                tail_block = tail_ref[e]

                @pl.when(tail_block >= 0)
                def _():
                    act(clear_copy(tail_block))
                return 0

            lax.fori_loop(0, N_EXPERTS, tail, 0)

            def unused(b, _):
                act(clear_copy(b))
                return 0

            lax.fori_loop(used_ref[0], rows_ref.shape[0] // ROW_BLOCK, unused, 0)

        clear_all(lambda cp: cp.start())
        clear_all(lambda cp: cp.wait())

    pos = pos_ref[...]
    r_iota = lax.broadcasted_iota(i32, (LOC_ROWS, tm), 0)
    onehot = r_iota == pos[0:1, :]
    for k in range(1, TOP_K):
        onehot = onehot | (r_iota == pos[k:k + 1, :])
    xs = _dot(onehot.astype(bf16), h_ref[...].astype(bf16))
    xs_ref[...] = _pack_rows(xs)

    _move_runs(step, n_ref, src_ref, dst_ref,
               lambda loc, glob: pltpu.make_async_copy(xs_ref.at[loc, :], rows_ref.at[glob, :], sem_ref))


def _dispatch(runs, tail_block, n_used, h, pos8, n_rows):
    tokens = h.shape[0]
    tm = TOK_TM
    grid_spec = pltpu.PrefetchScalarGridSpec(
        num_scalar_prefetch=5,
        grid=(tokens // tm,),
        in_specs=[
            pl.BlockSpec((tm, D_MODEL), lambda i, *_: (i, 0)),
            pl.BlockSpec((8, tm), lambda i, *_: (0, i)),
        ],
        out_specs=pl.BlockSpec(memory_space=pl.ANY),
        scratch_shapes=[
            pltpu.VMEM((LOC_ROWS, HALF), jnp.uint32),
            pltpu.VMEM((ROW_BLOCK, HALF), jnp.uint32),
            pltpu.SemaphoreType.DMA,
            pltpu.SemaphoreType.DMA,
        ],
    )
    return pl.pallas_call(
        _dispatch_kernel,
        grid_spec=grid_spec,
        out_shape=jax.ShapeDtypeStruct((n_rows, HALF), jnp.uint32),
        compiler_params=_params("arbitrary"),
        name="moe_dispatch",
    )(*runs, tail_block, n_used, h, pos8)


def _expert_kernel(eid_ref, used_ref, x_ref, wg_ref, wu_ref, wd_ref, y_ref, wgb_ref, wub_ref, wdb_ref):
    b = pl.program_id(0)
    changed = (b == 0) | (eid_ref[b] != eid_ref[jnp.maximum(b - 1, 0)])

    @pl.when(changed)
    def _():
        wgb_ref[...] = wg_ref[0].astype(bf16)
        wub_ref[...] = wu_ref[0].astype(bf16)
        wdb_ref[...] = wd_ref[0].astype(bf16)

    @pl.when(b < used_ref[0])
    def _():
        for c in range(ROW_BLOCK // EXPERT_CHUNK):
            rows = slice(c * EXPERT_CHUNK, (c + 1) * EXPERT_CHUNK)
            xl, xr = _unpack_rows(x_ref[rows, :])
            g = _dot(xl, wgb_ref[:HALF, :]) + _dot(xr, wgb_ref[HALF:, :])
            u = _dot(xl, wub_ref[:HALF, :]) + _dot(xr, wub_ref[HALF:, :])
            mid = (g * _sigmoid(g)) * u
            y = _dot(mid.astype(bf16), wdb_ref[...])
            y_ref[rows, :] = _pack_rows(y.astype(bf16).astype(f32))

    @pl.when(b >= used_ref[0])
    def _():
        y_ref[...] = jnp.zeros_like(y_ref)


def _experts(block_eid, n_used, rows, wg, wu, wd):
    n_rows = rows.shape[0]
    n_blocks = n_rows // ROW_BLOCK

    def row_map(b, eid, used):
        return (b, 0)

    def w_map(b, eid, used):
        return (eid[b], 0, 0)

    grid_spec = pltpu.PrefetchScalarGridSpec(
        num_scalar_prefetch=2,
        grid=(n_blocks,),
        in_specs=[
            pl.BlockSpec((ROW_BLOCK, HALF), row_map),
            pl.BlockSpec((1, D_MODEL, D_EXPERT), w_map),
            pl.BlockSpec((1, D_MODEL, D_EXPERT), w_map),
            pl.BlockSpec((1, D_EXPERT, D_MODEL), w_map),
        ],
        out_specs=pl.BlockSpec((ROW_BLOCK, HALF), row_map),
        scratch_shapes=[
            pltpu.VMEM((D_MODEL, D_EXPERT), bf16),
            pltpu.VMEM((D_MODEL, D_EXPERT), bf16),
            pltpu.VMEM((D_EXPERT, D_MODEL), bf16),
        ],
    )
    return pl.pallas_call(
        _expert_kernel,
        grid_spec=grid_spec,
        out_shape=jax.ShapeDtypeStruct((n_rows, HALF), jnp.uint32),
        compiler_params=_params("arbitrary"),
        name="moe_experts",
    )(block_eid, n_used, rows, wg, wu, wd)


def _combine_kernel(n_ref, src_ref, dst_ref, y_ref, h_ref, pos_ref, wt_ref, p_ref, wsg_ref, wsu_ref, wsd_ref,
                    wpp_ref, wpg_ref, g2_ref, b2_ref, g3_ref, b3_ref, o_ref, ybuf_ref, sem_ref):
    step = pl.program_id(0)
    tm = h_ref.shape[0]

    @pl.when(step == 0)
    def _():
        ybuf_ref[...] = jnp.zeros_like(ybuf_ref)

    _move_runs(step, n_ref, src_ref, dst_ref,
               lambda loc, glob: pltpu.make_async_copy(y_ref.at[glob, :], ybuf_ref.at[loc, :], sem_ref))

    h = h_ref[...]
    hb = h.astype(bf16)
    sg = _dot(hb, wsg_ref[...])
    su = _dot(hb, wsu_ref[...])
    shared = _dot(((sg * _sigmoid(sg)) * su).astype(bf16), wsd_ref[...])
    ple_in = _dot(p_ref[...].astype(bf16), wpp_ref[...])

    pos = pos_ref[...]
    wt = wt_ref[...]
    c_iota = lax.broadcasted_iota(i32, (tm, LOC_ROWS), 1)
    spread = jnp.where(c_iota == pos[:, 0:1], wt[:, 0:1], 0.0)
    for k in range(1, TOP_K):
        spread = jnp.where(c_iota == pos[:, k:k + 1], wt[:, k:k + 1], spread)
    spread = spread.astype(bf16)

    yl, yr = _unpack_rows(ybuf_ref[...])
    routed = jnp.concatenate([_dot(spread, yl), _dot(spread, yr)], axis=1)

    h2 = _layer_norm(ALPHA * h + (routed + shared), g2_ref[...], b2_ref[...])
    ple = ple_in * _sigmoid(_dot(h2.astype(bf16), wpg_ref[...]))
    o_ref[...] = _layer_norm(ALPHA * h2 + ple, g3_ref[...], b3_ref[...])


def _combine(runs, y_rows, h, pos_tok, w_tok, p2d, wsg, wsu, wsd, wpp, wpg, g2, b2, g3, b3):
    tokens = h.shape[0]
    tm = TOK_TM

    def full(shape):
        return pl.BlockSpec(shape, lambda i, *_: (0,) * len(shape))

    def tile(width):
        return pl.BlockSpec((tm, width), lambda i, *_: (i, 0))

    grid_spec = pltpu.PrefetchScalarGridSpec(
        num_scalar_prefetch=3,
        grid=(tokens // tm,),
        in_specs=[
            pl.BlockSpec(memory_space=pl.ANY),
            tile(D_MODEL), tile(8), tile(8), tile(PLE_DIM),
            full((D_MODEL, D_SHARED)), full((D_MODEL, D_SHARED)), full((D_SHARED, D_MODEL)),
            full((PLE_DIM, D_MODEL)), full((D_MODEL, D_MODEL)),
            full((1, D_MODEL)), full((1, D_MODEL)), full((1, D_MODEL)), full((1, D_MODEL)),
        ],
        out_specs=tile(D_MODEL),
        scratch_shapes=[
            pltpu.VMEM((LOC_ROWS, HALF), jnp.uint32),
            pltpu.SemaphoreType.DMA,
        ],
    )
    return pl.pallas_call(
        _combine_kernel,
        grid_spec=grid_spec,
        out_shape=jax.ShapeDtypeStruct((tokens, D_MODEL), f32),
        compiler_params=_params("arbitrary"),
        name="moe_combine_ln2_ple_ln3",
    )(*runs, y_rows, h, pos_tok, w_tok, p2d, wsg, wsu, wsd, wpp, wpg, g2, b2, g3, b3)


def _row_layout(table, totals, n_blocks):
    cnt_u = table[:, :, 0] // UNIT
    loc_u = table[:, :, 1] // UNIT
    padded = (totals + ROW_BLOCK - 1) // ROW_BLOCK * ROW_BLOCK
    pends = jnp.cumsum(padded)
    pstarts = pends - padded
    glob_u = (pstarts[None, :] + table[:, :, 2]) // UNIT
    cls = jnp.arange(RUN_CLASSES, dtype=i32)
    has = (cnt_u[:, None, :] >> cls[None, :, None]) & 1
    off = (cnt_u[:, None, :] >> (cls[None, :, None] + 1)) << (cls[None, :, None] + 1)
    upto = jnp.cumsum(has, axis=2)
    n_pieces = upto[:, :, -1].astype(i32).reshape(-1)
    slot = jnp.arange(N_EXPERTS, dtype=i32)
    e_of = jnp.sum((upto[:, :, None, :] <= slot[None, None, :, None]).astype(i32), axis=-1)
    is_e = e_of[:, :, :, None] == jnp.arange(N_EXPERTS, dtype=i32)
    src = jnp.sum(jnp.where(is_e, (loc_u[:, None, :] + off)[:, :, None, :], 0), axis=-1).astype(i32).reshape(-1)
    dst = jnp.sum(jnp.where(is_e, (glob_u[:, None, :] + off)[:, :, None, :], 0), axis=-1).astype(i32).reshape(-1)
    runs = (n_pieces, src, dst)
    tail_block = jnp.where(totals > 0, pends // ROW_BLOCK - 1, -1).astype(i32)
    block_start = jnp.arange(n_blocks, dtype=i32) * ROW_BLOCK
    ends_before = jnp.sum((pends[None, :] <= block_start[:, None]).astype(i32), axis=1)
    block_eid = jnp.minimum(ends_before, N_EXPERTS - 1).astype(i32)
    n_used = (pends[-1:] // ROW_BLOCK).astype(i32)
    return runs, tail_block, block_eid, n_used


def _layer(x2d, p2d, batch, seq, w_in, w_moba_up, w_ret_up, w_out, ln1_g, ln1_b, w_router, router_bias,
           w_eg, w_eu, w_ed, w_sg, w_su, w_sd, ln2_g, ln2_b, w_ple_proj, w_ple_gate, ln3_g, ln3_b):
    tokens = batch * seq
    row = lambda a: a.reshape(1, -1).astype(f32)

    proj = _project(x2d, w_in.astype(bf16), seq)
    oa = _moba(proj, batch, seq)
    orr = _retention(proj, batch, seq)

    wrt = w_router.astype(f32).T
    wrt_hi = wrt.astype(bf16)
    wrt_lo = (wrt - wrt_hi.astype(f32)).astype(bf16)
    h1, pos8, w8, table, totals = _merge_router(
        oa, orr, proj, x2d, w_moba_up.astype(bf16), w_ret_up.astype(bf16), w_out.astype(bf16),
        row(ln1_g), row(ln1_b), wrt_hi, wrt_lo, router_bias.astype(f32).reshape(N_EXPERTS, 1))

    n_tiles = tokens // TOK_TM
    max_rows = tokens * TOP_K + n_tiles * N_EXPERTS * (UNIT - 1) + N_EXPERTS * (ROW_BLOCK - 1)
    n_blocks = -(-max_rows // ROW_BLOCK)
    runs, tail_block, block_eid, n_used = _row_layout(table[:, :, :3], totals[:, 0], n_blocks)

    rows = _dispatch(runs, tail_block, n_used, h1, pos8, n_blocks * ROW_BLOCK)
    y_rows = _experts(block_eid, n_used, rows, w_eg, w_eu, w_ed)
    return _combine(runs, y_rows, h1, pos8.T, w8.T, p2d,
                    w_sg.astype(bf16), w_su.astype(bf16), w_sd.astype(bf16),
                    w_ple_proj.astype(bf16), w_ple_gate.astype(bf16),
                    row(ln2_g), row(ln2_b), row(ln3_g), row(ln3_b))


def kernel(x, p, w_in, w_moba_up, w_ret_up, w_out, ln1_g, ln1_b, w_router, router_bias, w_exp_gate, w_exp_up,
           w_exp_down, w_sh_gate, w_sh_up, w_sh_down, ln2_g, ln2_b, w_ple_proj, w_ple_gate, ln3_g, ln3_b):
    batch, seq, d = x.shape
    assert d == D_MODEL and seq % max(MOBA_BLOCK, RET_CHUNK) == 0 and (batch * seq) % (MERGE_TILES * TOK_TM) == 0
    assert w_in.shape[0] == DEPTH
    h = x.reshape(batch * seq, d)
    for i in range(DEPTH):
        h = _layer(h, p[i].reshape(batch * seq, PLE_DIM), batch, seq,
                   w_in[i], w_moba_up[i], w_ret_up[i], w_out[i], ln1_g[i], ln1_b[i], w_router[i], router_bias[i],
                   w_exp_gate[i], w_exp_up[i], w_exp_down[i], w_sh_gate[i], w_sh_up[i], w_sh_down[i],
                   ln2_g[i], ln2_b[i], w_ple_proj[i], w_ple_gate[i], ln3_g[i], ln3_b[i])
    return h.reshape(batch, seq, d)
```

```python
import functools

import jax
import jax.numpy as jnp
from jax import lax
from jax.experimental import pallas as pl
from jax.experimental.pallas import tpu as pltpu

f32 = jnp.float32
bf16 = jnp.bfloat16
i32 = jnp.int32

D_MODEL = 1024
DEPTH = 1
MOBA_HEADS = 8
MOBA_HEAD_DIM = 64
MOBA_WIDTH = MOBA_HEADS * MOBA_HEAD_DIM
MOBA_BLOCK = 256
MOBA_TOPK = 3
ROPE_THETA = 10000.0
RET_HEADS = 4
RET_QK_DIM = 128
RET_V_DIM = 256
RET_QK_WIDTH = RET_HEADS * RET_QK_DIM
RET_V_WIDTH = RET_HEADS * RET_V_DIM
RET_CHUNK = 256
RET_ANGLE_BASE = 10000.0
N_IN = 3 * MOBA_WIDTH + 2 * RET_QK_WIDTH + 2 * RET_V_WIDTH + 2 * D_MODEL
N_EXPERTS = 64
TOP_K = 6
N_GROUPS = 8
TOPK_GROUPS = 4
GROUP_SIZE = N_EXPERTS // N_GROUPS
D_EXPERT = 256
D_SHARED = 256
ROUTED_SCALE = 2.5
PLE_DIM = 256
LN_EPS = 1e-5
GN_EPS = 1e-6
ALPHA = (2.0 * DEPTH) ** 0.25

OFF_QA = 0
OFF_KA = MOBA_WIDTH
OFF_VA = 2 * MOBA_WIDTH
OFF_QR = 3 * MOBA_WIDTH
OFF_KR = OFF_QR + RET_QK_WIDTH
OFF_VR = OFF_KR + RET_QK_WIDTH
OFF_GR = OFF_VR + RET_V_WIDTH
OFF_GA = OFF_GR + RET_V_WIDTH
OFF_GB = OFF_GA + D_MODEL

MOBA_Q_SCALE = MOBA_HEAD_DIM ** -0.5 * 1.4426950408889634

LANES = 128
NEG = -1e30
VMEM_LIMIT = 56 * 1024 * 1024

PROJ_TM = 2048
PROJ_TN = 512
MOBA_STEP_WIDTH = 512
MERGE_TILES = 2
TOK_TM = 256
ROW_BLOCK = 512
EXPERT_CHUNK = 256
UNIT = 8
LOC_ROWS = -(-(TOP_K * TOK_TM + N_EXPERTS * (UNIT - 1)) // LANES) * LANES
RUN_CLASSES = (TOK_TM // UNIT).bit_length()
HALF = D_MODEL // 2


def _dot(a, b):
    return jnp.dot(a, b, preferred_element_type=f32)


def _dot_nt(a, b):
    return lax.dot_general(a, b, (((1,), (1,)), ((), ())), preferred_element_type=f32)


def _dot_tn(a, b):
    return lax.dot_general(a, b, (((0,), (0,)), ((), ())), preferred_element_type=f32)


def _split_bf16(a):
    hi = a.astype(bf16)
    lo = (a - hi.astype(f32)).astype(bf16)
    return hi, lo


def _layer_norm(x, g, b):
    mu = jnp.mean(x, axis=-1, keepdims=True)
    xc = x - mu
    var = jnp.mean(xc * xc, axis=-1, keepdims=True)
    return xc * lax.rsqrt(var + LN_EPS) * g + b


def _sigmoid(x):
    return 1.0 / (1.0 + jnp.exp(-x))


def _params(*sem, flags=None):
    return pltpu.CompilerParams(dimension_semantics=sem, vmem_limit_bytes=VMEM_LIMIT, flags=flags)


def _rotate_half_chunk(xc, half):
    if 2 * half == LANES:
        return pltpu.roll(xc, half, axis=1)
    lane = lax.broadcasted_iota(i32, xc.shape, 1)
    first = (lane & (2 * half - 1)) < half
    return jnp.where(first, pltpu.roll(xc, LANES - half, axis=1), pltpu.roll(xc, half, axis=1))


def _proj_kernel(x_ref, w_ref, cos_a_ref, sin_a_ref, cos_r_ref, sin_r_ref, o_ref, xb_ref):
    j = pl.program_id(1)

    @pl.when(j == 0)
    def _():
        xb_ref[...] = x_ref[...].astype(bf16)

    acc = _dot(xb_ref[...], w_ref[...])

    def rotary(cos_ref, sin_ref, half, scale):
        cos = cos_ref[...]
        sin = sin_ref[...]
        for c in range(PROJ_TN // LANES):
            xc = acc[:, c * LANES:(c + 1) * LANES]
            y = xc * cos + _rotate_half_chunk(xc, half) * sin
            if scale != 1.0:
                y = y * scale
            o_ref[:, c * LANES:(c + 1) * LANES] = y.astype(o_ref.dtype)

    j_qa = OFF_QA // PROJ_TN
    j_ka = OFF_KA // PROJ_TN
    j_qr = OFF_QR // PROJ_TN
    j_kr = OFF_KR // PROJ_TN

    @pl.when(j == j_qa)
    def _():
        rotary(cos_a_ref, sin_a_ref, MOBA_HEAD_DIM // 2, MOBA_Q_SCALE)

    @pl.when(j == j_ka)
    def _():
        rotary(cos_a_ref, sin_a_ref, MOBA_HEAD_DIM // 2, 1.0)

    @pl.when(j == j_qr)
    def _():
        rotary(cos_r_ref, sin_r_ref, RET_QK_DIM // 2, 1.0)

    @pl.when(j == j_kr)
    def _():
        rotary(cos_r_ref, sin_r_ref, RET_QK_DIM // 2, RET_QK_DIM ** -0.5)

    plain = (j != j_qa) & (j != j_ka) & (j != j_qr) & (j != j_kr)

    @pl.when(plain)
    def _():
        o_ref[...] = acc.astype(o_ref.dtype)


def _rotary_tables(seq, inv_freq, head_dim):
    ang = jnp.arange(seq, dtype=f32)[:, None] * inv_freq[None, :]
    cos = jnp.cos(ang)
    sin = jnp.sin(ang)
    cos_h = jnp.concatenate([cos, cos], axis=-1)
    sin_h = jnp.concatenate([-sin, sin], axis=-1)
    reps = LANES // head_dim
    return jnp.tile(cos_h, (1, reps)), jnp.tile(sin_h, (1, reps))


def _project(x2d, w_in_b, seq):
    tokens = x2d.shape[0]
    tm = min(PROJ_TM, seq)
    inv_a = 1.0 / (ROPE_THETA ** (jnp.arange(0, MOBA_HEAD_DIM, 2, dtype=f32) / MOBA_HEAD_DIM))
    inv_r = 1.0 / (RET_ANGLE_BASE ** jnp.linspace(0.0, 1.0, RET_QK_DIM // 2, dtype=f32))
    cos_a, sin_a = _rotary_tables(seq, inv_a, MOBA_HEAD_DIM)
    cos_r, sin_r = _rotary_tables(seq, inv_r, RET_QK_DIM)
    seq_tiles = seq // tm
    tab = pl.BlockSpec((tm, LANES), lambda i, j: (i % seq_tiles, 0))
    return pl.pallas_call(
        _proj_kernel,
        grid=(tokens // tm, N_IN // PROJ_TN),
        in_specs=[
            pl.BlockSpec((tm, D_MODEL), lambda i, j: (i, 0)),
            pl.BlockSpec((D_MODEL, PROJ_TN), lambda i, j: (0, j)),
            tab, tab, tab, tab,
        ],
        out_specs=pl.BlockSpec((tm, PROJ_TN), lambda i, j: (i, j)),
        out_shape=jax.ShapeDtypeStruct((tokens, N_IN), bf16),
        scratch_shapes=[pltpu.VMEM((tm, D_MODEL), bf16)],
        compiler_params=_params("parallel", "arbitrary"),
        name="in_proj_rotary",
    )(x2d, w_in_b, cos_a, sin_a, cos_r, sin_r)


def _moba_kernel(q_ref, k_ref, v_ref, o_ref, kmean_ref, vt_ref, selb_ref, gate_ref, *, n_blocks):
    i = pl.program_id(2)
    blk = MOBA_BLOCK
    hd = MOBA_HEAD_DIM
    width = q_ref.shape[1]
    chunks = width // LANES
    heads = width // hd
    per_chunk = LANES // hd

    def chunk_of(h):
        return slice(h // per_chunk * LANES, (h // per_chunk + 1) * LANES)

    @pl.when(i == 0)
    def _():
        seq = n_blocks * blk
        member = (lax.broadcasted_iota(i32, (n_blocks, seq), 1) // blk
                  == lax.broadcasted_iota(i32, (n_blocks, seq), 0))
        kmean_ref[...] = _dot(member.astype(bf16), k_ref[...]) * (1.0 / blk)
        for j in range(n_blocks):
            vt_ref[j] = v_ref[j * blk:(j + 1) * blk, :].astype(f32).T.astype(bf16)

    km_hi, km_lo = _split_bf16(kmean_ref[...])
    key_i = lax.broadcasted_iota(i32, (blk, blk), 0)
    qry_i = lax.broadcasted_iota(i32, (blk, blk), 1)
    causal = key_i <= qry_i
    bid = lax.broadcasted_iota(i32, (n_blocks, blk), 0)
    d_row = lax.broadcasted_iota(i32, (LANES, blk), 0)
    qt = q_ref[...].astype(f32).T

    qts = []
    for h in range(heads):
        r0 = h % per_chunk * hd
        in_head = (d_row >= r0) & (d_row < r0 + hd)
        qth = jnp.where(in_head, qt[chunk_of(h), :], 0.0).astype(bf16)
        qts.append(qth)
        gate = _dot(km_hi[:, chunk_of(h)], qth) + _dot(km_lo[:, chunk_of(h)], qth)
        g = jnp.where(bid < i, gate, -jnp.inf)
        gate_ref[...] = g
        beaten = jnp.zeros((n_blocks, blk), f32)
        for j in range(n_blocks):
            gj = jnp.broadcast_to(gate_ref[j:j + 1, :], (n_blocks, blk))
            beats = (gj > g) | ((gj == g) & (bid > j))
            beaten = beaten + jnp.where(beats, 1.0, 0.0)
        sel = (beaten < MOBA_TOPK) & (bid < i)
        selb_ref[h] = jnp.where(sel, 0.0, NEG)

    kj = k_ref[pl.ds(pl.multiple_of(i * blk, blk), blk), :]
    vtj = vt_ref[i]
    state = []
    for h in range(heads):
        s = jnp.where(causal, _dot(kj[:, chunk_of(h)], qts[h]), NEG)
        m0 = jnp.max(s, axis=0, keepdims=True)
        p = jnp.exp2(s - m0)
        l0 = jnp.sum(p, axis=0, keepdims=True)
        acc0 = _dot(vtj[h * hd:(h + 1) * hd, :], p.astype(bf16))
        state += [m0, l0, acc0]

    qt_alls = [jnp.concatenate(qts[c * per_chunk:(c + 1) * per_chunk], axis=1) for c in range(chunks)]

    def update(carry, js):
        nj = len(js)
        k_all = k_ref[pl.ds(pl.multiple_of(js[0] * blk, blk), nj * blk), :]
        s_alls = [_dot(k_all[:, c * LANES:(c + 1) * LANES], qt_alls[c]) for c in range(chunks)]
        vtjs = [vt_ref[j] for j in js]
        new = []
        for h in range(heads):
            m_old, l_old, acc = carry[3 * h:3 * h + 3]
            col = h % per_chunk * blk
            ss = [s_alls[h // per_chunk][n * blk:(n + 1) * blk, col:col + blk] + selb_ref[h, pl.ds(j, 1), :]
                  for n, j in enumerate(js)]
            m_new = m_old
            for s in ss:
                m_new = jnp.maximum(m_new, jnp.max(s, axis=0, keepdims=True))
            a = jnp.exp2(m_old - m_new)
            l_new = a * l_old
            acc = a * acc
            for s, vtj in zip(ss, vtjs):
                p = jnp.exp2(s - m_new)
                l_new = l_new + jnp.sum(p, axis=0, keepdims=True)
                acc = acc + _dot(vtj[h * hd:(h + 1) * hd, :], p.astype(bf16))
            new += [m_new, l_new, acc]
        return tuple(new)

    quads = i // 4
    fin = lax.fori_loop(0, quads, lambda n, c: update(c, tuple(4 * n + t for t in range(4))), tuple(state))
    done = 4 * quads
    has_pair = (i - done) // 2
    fin = lax.fori_loop(0, has_pair, lambda n, c: update(c, (done, done + 1)), fin)
    done = done + 2 * has_pair
    fin = lax.fori_loop(done, i, lambda j, c: update(c, (j,)), fin)
    out_t = jnp.concatenate([fin[3 * h + 2] / fin[3 * h + 1] for h in range(heads)], axis=0)
    o_ref[...] = out_t.T.astype(o_ref.dtype)


def _moba(proj, batch, seq):
    tokens = batch * seq
    n_blocks = seq // MOBA_BLOCK
    width = MOBA_STEP_WIDTH
    return pl.pallas_call(
        functools.partial(_moba_kernel, n_blocks=n_blocks),
        grid=(batch, MOBA_WIDTH // width, n_blocks),
        in_specs=[
            pl.BlockSpec((MOBA_BLOCK, width), lambda b, c, i: (b * n_blocks + i, OFF_QA // width + c)),
            pl.BlockSpec((seq, width), lambda b, c, i: (b, OFF_KA // width + c)),
            pl.BlockSpec((seq, width), lambda b, c, i: (b, OFF_VA // width + c)),
        ],
        out_specs=pl.BlockSpec((MOBA_BLOCK, width), lambda b, c, i: (b * n_blocks + i, c)),
        out_shape=jax.ShapeDtypeStruct((tokens, MOBA_WIDTH), bf16),
        scratch_shapes=[
            pltpu.VMEM((n_blocks, width), f32),
            pltpu.VMEM((n_blocks, width, MOBA_BLOCK), bf16),
            pltpu.VMEM((width // MOBA_HEAD_DIM, n_blocks, MOBA_BLOCK), f32),
            pltpu.VMEM((n_blocks, MOBA_BLOCK), f32),
        ],
        compiler_params=_params("parallel", "parallel", "arbitrary"),
        name="moba_attention",
    )(proj, proj, proj)


def _ret_kernel(q_ref, k_ref, v0_ref, v1_ref, g0_ref, g1_ref, decay_ref, zeta_ref, xi_ref, cd_ref,
                o_ref, state_ref):
    c = pl.program_id(1)
    heads_per_half = RET_HEADS // 2
    v_refs = (v0_ref, v1_ref)
    g_refs = (g0_ref, g1_ref)

    @pl.when(c == 0)
    def _():
        state_ref[...] = jnp.zeros_like(state_ref)

    for h in range(RET_HEADS):
        qk_cols = slice(h * RET_QK_DIM, (h + 1) * RET_QK_DIM)
        v_cols = slice(h * RET_V_DIM, (h + 1) * RET_V_DIM)
        half_cols = slice((h % heads_per_half) * RET_V_DIM, (h % heads_per_half + 1) * RET_V_DIM)
        q = q_ref[:, qk_cols]
        k = k_ref[:, qk_cols]
        v = v_refs[h // heads_per_half][:, half_cols]
        state = state_ref[h]
        scores = _dot_nt(q, k) * decay_ref[h]
        inner = _dot(scores.astype(bf16), v)
        cross = _dot(q, state.astype(bf16)) * xi_ref[h]
        o = inner + cross
        kz = (k.astype(f32) * zeta_ref[h]).astype(bf16)
        state_ref[h] = cd_ref[h] * state + _dot_tn(kz, v)

        mu = jnp.mean(o, axis=-1, keepdims=True)
        oc = o - mu
        var = jnp.mean(oc * oc, axis=-1, keepdims=True)
        on = oc * lax.rsqrt(var + GN_EPS)
        g = g_refs[h // heads_per_half][:, half_cols].astype(f32)
        o_ref[:, v_cols] = (on * (g * _sigmoid(g))).astype(o_ref.dtype)


def _retention(proj, batch, seq):
    tokens = batch * seq
    C = RET_CHUNK
    n_chunks = seq // C
    half_v = RET_V_WIDTH // 2
    gammas =1.0 - 2.0 ** (-5.0 - jnp.arange(RET_HEADS, dtype=f32))
    log_g = jnp.log(gammas)
    idx = jnp.arange(C, dtype=f32)
    diff = idx[:, None] - idx[None, :]
    decay = jnp.where(diff >= 0, jnp.exp(jnp.maximum(diff, 0.0)[None] * log_g[:, None, None]), 0.0)
    zeta = jnp.exp((C - 1 - idx)[None, :] * log_g[:, None])
    xi = jnp.exp((idx + 1.0)[None, :] * log_g[:, None])
    zeta_t = jnp.broadcast_to(zeta[:, :, None], (RET_HEADS, C, RET_QK_DIM))
    xi_t = jnp.broadcast_to(xi[:, :, None], (RET_HEADS, C, RET_V_DIM))
    cd_t = jnp.broadcast_to(jnp.exp(C * log_g)[:, None, None], (RET_HEADS, 1, RET_V_DIM))
    return pl.pallas_call(
        _ret_kernel,
        grid=(batch, n_chunks),
        in_specs=[
            pl.BlockSpec((C, RET_QK_WIDTH), lambda b, c: (b * n_chunks + c, OFF_QR // RET_QK_WIDTH)),
            pl.BlockSpec((C, RET_QK_WIDTH), lambda b, c: (b * n_chunks + c, OFF_KR // RET_QK_WIDTH)),
            pl.BlockSpec((C, half_v), lambda b, c: (b * n_chunks + c, OFF_VR // half_v)),
            pl.BlockSpec((C, half_v), lambda b, c: (b * n_chunks + c, OFF_VR // half_v + 1)),
            pl.BlockSpec((C, half_v), lambda b, c: (b * n_chunks + c, OFF_GR // half_v)),
            pl.BlockSpec((C, half_v), lambda b, c: (b * n_chunks + c, OFF_GR // half_v + 1)),
            pl.BlockSpec((RET_HEADS, C, C), lambda b, c: (0, 0, 0)),
            pl.BlockSpec((RET_HEADS, C, RET_QK_DIM), lambda b, c: (0, 0, 0)),
            pl.BlockSpec((RET_HEADS, C, RET_V_DIM), lambda b, c: (0, 0, 0)),
            pl.BlockSpec((RET_HEADS, 1, RET_V_DIM), lambda b, c: (0, 0, 0)),
        ],
        out_specs=pl.BlockSpec((C, RET_V_WIDTH), lambda b, c: (b * n_chunks + c, 0)),
        out_shape=jax.ShapeDtypeStruct((tokens, RET_V_WIDTH), bf16),
        scratch_shapes=[pltpu.VMEM((RET_HEADS, RET_QK_DIM, RET_V_DIM), f32)],
        compiler_params=_params("parallel", "arbitrary"),
        name="retention",
    )(proj, proj, proj, proj, proj, proj, decay, zeta_t, xi_t, cd_t)


def _merge_router_kernel(oa_ref, orr_ref, ga0_ref, ga1_ref, gb0_ref, gb1_ref, x_ref,
                         wm_ref, wr_ref, wo_ref, g1_ref, b1_ref, wrt_hi_ref, wrt_lo_ref, rb_ref,
                         h_ref, pos_ref, w_ref, tab_ref, cnt_ref, carry_ref):
    step = pl.program_id(0)
    tm = TOK_TM

    @pl.when(step == 0)
    def _():
        carry_ref[...] = jnp.zeros_like(carry_ref)

    for t in range(x_ref.shape[0] // tm):
        rows = slice(t * tm, (t + 1) * tm)
        _merge_router_tile(t, rows, tm, oa_ref, orr_ref, ga0_ref, ga1_ref, gb0_ref, gb1_ref, x_ref,
                           wm_ref, wr_ref, wo_ref, g1_ref, b1_ref, wrt_hi_ref, wrt_lo_ref, rb_ref,
                           h_ref, pos_ref, w_ref, tab_ref, cnt_ref, carry_ref)


def _merge_router_tile(t, rows, tm, oa_ref, orr_ref, ga0_ref, ga1_ref, gb0_ref, gb1_ref, x_ref,
                       wm_ref, wr_ref, wo_ref, g1_ref, b1_ref, wrt_hi_ref, wrt_lo_ref, rb_ref,
                       h_ref, pos_ref, w_ref, tab_ref, cnt_ref, carry_ref):
    a = _dot(oa_ref[rows, :], wm_ref[...])
    r = _dot(orr_ref[rows, :], wr_ref[...])
    ga = jnp.concatenate([ga0_ref[rows, :], ga1_ref[rows, :]], axis=1).astype(f32)
    gb = jnp.concatenate([gb0_ref[rows, :], gb1_ref[rows, :]], axis=1).astype(f32)
    merged = _sigmoid(ga) * a + _sigmoid(gb) * r
    mix = _dot(merged.astype(bf16), wo_ref[...])
    h = _layer_norm(ALPHA * x_ref[rows, :] + mix, g1_ref[...], b1_ref[...])
    h_ref[rows, :] = h

    h_hi, h_lo = _split_bf16(h)
    w_hi = wrt_hi_ref[...]
    logits = _dot_nt(w_hi, h_hi) + _dot_nt(w_hi, h_lo) + _dot_nt(wrt_lo_ref[...], h_hi)
    scores = _sigmoid(logits)
    biased = scores + rb_ref[...]

    v = biased.reshape(N_GROUPS, GROUP_SIZE, tm)
    sub = lax.broadcasted_iota(i32, v.shape, 1)
    m1 = jnp.max(v, axis=1, keepdims=True)
    i1 = jnp.min(jnp.where(v == m1, sub, GROUP_SIZE), axis=1, keepdims=True)
    m2 = jnp.max(jnp.where(sub == i1, -jnp.inf, v), axis=1, keepdims=True)
    gscore = jnp.broadcast_to(m1 + m2, v.shape).reshape(N_EXPERTS, tm)

    eid = lax.broadcasted_iota(i32, (N_EXPERTS, tm), 0)
    egrp = eid // GROUP_SIZE
    e_mask = jnp.zeros((N_EXPERTS, tm), jnp.bool_)
    for _ in range(TOPK_GROUPS):
        m = jnp.max(gscore, axis=0, keepdims=True)
        idx = jnp.min(jnp.where(gscore == m, egrp, N_GROUPS), axis=0, keepdims=True)
        hit = egrp == idx
        e_mask = e_mask | hit
        gscore = jnp.where(hit, -jnp.inf, gscore)
    cand = jnp.where(e_mask, biased, -jnp.inf)

    chosen = jnp.zeros((N_EXPERTS, tm), jnp.bool_)
    e_rows = []
    w_rows = []
    for _ in range(TOP_K):
        m = jnp.max(cand, axis=0, keepdims=True)
        idx = jnp.min(jnp.where(cand == m, eid, N_EXPERTS), axis=0, keepdims=True)
        hit = eid == idx
        chosen = chosen | hit
        e_rows.append(idx)
        w_rows.append(jnp.sum(jnp.where(hit, scores, 0.0), axis=0, keepdims=True))
        cand = jnp.where(hit, -jnp.inf, cand)
    w_sum = w_rows[0]
    for wk in w_rows[1:]:
        w_sum = w_sum + wk

    t_src = lax.broadcasted_iota(i32, (tm, tm), 0)
    t_dst = lax.broadcasted_iota(i32, (tm, tm), 1)
    before = (t_src < t_dst).astype(bf16)
    chosen_f = chosen.astype(f32)
    prior = _dot(chosen_f.astype(bf16), before)
    cnt = jnp.sum(chosen_f, axis=1, keepdims=True)
    cnt_pad = jnp.ceil(cnt * (1.0 / UNIT)) * UNIT
    e_src = lax.broadcasted_iota(i32, (N_EXPERTS, N_EXPERTS), 1)
    e_dst = lax.broadcasted_iota(i32, (N_EXPERTS, N_EXPERTS), 0)
    earlier = (e_src < e_dst).astype(bf16)
    cnt_pad_l = jnp.broadcast_to(cnt_pad, (N_EXPERTS, LANES))
    loc_start_l = _dot(earlier, cnt_pad_l.astype(bf16))
    loc_start = loc_start_l[:, 0:1]
    where_e = prior + loc_start
    pos_ref[:, rows] = jnp.zeros((pos_ref.shape[0], tm), pos_ref.dtype)
    w_ref[:, rows] = jnp.zeros((w_ref.shape[0], tm), w_ref.dtype)
    for k in range(TOP_K):
        pos_ref[k:k + 1, rows] = jnp.sum(jnp.where(eid == e_rows[k], where_e, 0.0), axis=0, keepdims=True).astype(i32)
        w_ref[k:k + 1, rows] = w_rows[k] / w_sum * ROUTED_SCALE

    carry = carry_ref[...]
    tl = lax.broadcasted_iota(i32, (N_EXPERTS, LANES), 1)
    table = jnp.where(tl == 0, cnt_pad_l, jnp.where(tl == 1, loc_start_l, jnp.broadcast_to(carry, (N_EXPERTS, LANES))))
    tab_ref[t] = table.astype(i32)
    carry = carry + cnt_pad
    carry_ref[...] = carry
    cnt_ref[...] = jnp.broadcast_to(carry, cnt_ref.shape).astype(i32)


def _merge_router(oa, orr, proj, x2d, wm, wr, wo, g1, b1, wrt_hi, wrt_lo, rb):
    tokens = x2d.shape[0]
    tm = MERGE_TILES * TOK_TM
    half = D_MODEL // 2

    def gate_spec(off):
        return pl.BlockSpec((tm, half), lambda i: (i, off // half))

    def full(shape):
        return pl.BlockSpec(shape, lambda i: (0,) * len(shape))

    row8 = pl.BlockSpec((8, tm), lambda i: (0, i))
    return pl.pallas_call(
        _merge_router_kernel,
        grid=(tokens // tm,),
        in_specs=[
            pl.BlockSpec((tm, MOBA_WIDTH), lambda i: (i, 0)),
            pl.BlockSpec((tm, RET_V_WIDTH), lambda i: (i, 0)),
            gate_spec(OFF_GA), gate_spec(OFF_GA + half), gate_spec(OFF_GB), gate_spec(OFF_GB + half),
            pl.BlockSpec((tm, D_MODEL), lambda i: (i, 0)),
            full((MOBA_WIDTH, D_MODEL)), full((RET_V_WIDTH, D_MODEL)), full((D_MODEL, D_MODEL)),
            full((1, D_MODEL)), full((1, D_MODEL)),
            full((N_EXPERTS, D_MODEL)), full((N_EXPERTS, D_MODEL)), full((N_EXPERTS, 1)),
        ],
        out_specs=[
            pl.BlockSpec((tm, D_MODEL), lambda i: (i, 0)),
            row8, row8,
            pl.BlockSpec((MERGE_TILES, N_EXPERTS, LANES), lambda i: (i, 0, 0)),
            full((N_EXPERTS, LANES)),
        ],
        out_shape=[
            jax.ShapeDtypeStruct((tokens, D_MODEL), f32),
            jax.ShapeDtypeStruct((8, tokens), i32),
            jax.ShapeDtypeStruct((8, tokens), f32),
            jax.ShapeDtypeStruct((tokens // TOK_TM, N_EXPERTS, LANES), i32),
            jax.ShapeDtypeStruct((N_EXPERTS, LANES), i32),
        ],
        scratch_shapes=[pltpu.VMEM((N_EXPERTS, 1), f32)],
        compiler_params=_params("arbitrary"),
        name="merge_ln1_router",
    )(oa, orr, proj, proj, proj, proj, x2d, wm, wr, wo, g1, b1, wrt_hi, wrt_lo, rb)


def _pack_rows(x):
    bits = lax.bitcast_convert_type(x, jnp.uint32)
    return (bits[:, :HALF] & jnp.uint32(0xFFFF0000)) | (bits[:, HALF:] >> 16)


def _unpack_rows(words):
    left = lax.bitcast_convert_type(words & jnp.uint32(0xFFFF0000), f32)
    right = lax.bitcast_convert_type(words << 16, f32)
    return left.astype(bf16), right.astype(bf16)


def _start_alternating(n, make_copy):
    def pair(m, _):
        make_copy(2 * m).start(priority=0)
        make_copy(2 * m + 1).start(priority=1)
        return 0

    lax.fori_loop(0, n // 2, pair, 0)

    @pl.when(n % 2 == 1)
    def _():
        make_copy(n - 1).start(priority=0)


def _move_runs(tile, n_ref, src_ref, dst_ref, make_copy):
    def piece(c, p):
        rows = UNIT << c
        base = (tile * RUN_CLASSES + c) * N_EXPERTS
        loc = pl.ds(pl.multiple_of(src_ref[base + p] * UNIT, UNIT), rows)
        glob = pl.ds(pl.multiple_of(dst_ref[base + p] * UNIT, UNIT), rows)
        return make_copy(loc, glob)

    for c in range(RUN_CLASSES):
        _start_alternating(n_ref[tile * RUN_CLASSES + c], functools.partial(piece, c))

    for c in range(RUN_CLASSES):
        def wait_one(p, _, c=c):
            piece(c, p).wait()
            return 0

        lax.fori_loop(0, n_ref[tile * RUN_CLASSES + c], wait_one, 0)


def _dispatch_kernel(n_ref, src_ref, dst_ref, tail_ref, used_ref, h_ref, pos_ref, rows_ref,
                     xs_ref, zero_ref, sem_ref, zsem_ref):
    step = pl.program_id(0)
    tm = h_ref.shape[0]

    @pl.when(step == 0)
    def _():
        zero_ref[...] = jnp.zeros_like(zero_ref)

        def clear_copy(block):
            start = pl.multiple_of(block * ROW_BLOCK, ROW_BLOCK)
            return pltpu.make_async_copy(zero_ref, rows_ref.at[pl.ds(start, ROW_BLOCK), :], zsem_ref)

        def clear_all(act):
            def tail(e, _):
                tail_block = tail_ref[e]

                @pl.when(tail_block >= 0)
                def _():
                    act(clear_copy(tail_block))
                return 0

            lax.fori_loop(0, N_EXPERTS, tail, 0)

            def unused(b, _):
                act(clear_copy(b))
                return 0

            lax.fori_loop(used_ref[0], rows_ref.shape[0] // ROW_BLOCK, unused, 0)

        clear_all(lambda cp: cp.start())
        clear_all(lambda cp: cp.wait())

    pos = pos_ref[...]
    r_iota = lax.broadcasted_iota(i32, (LOC_ROWS, tm), 0)
    onehot = r_iota == pos[0:1, :]
    for k in range(1, TOP_K):
        onehot = onehot | (r_iota == pos[k:k + 1, :])
    xs = _dot(onehot.astype(bf16), h_ref[...].astype(bf16))
    xs_ref[...] = _pack_rows(xs)

    _move_runs(step, n_ref, src_ref, dst_ref,
               lambda loc, glob: pltpu.make_async_copy(xs_ref.at[loc, :], rows_ref.at[glob, :], sem_ref))


def _dispatch(runs, tail_block, n_used, h, pos8, n_rows):
    tokens = h.shape[0]
    tm = TOK_TM
    grid_spec = pltpu.PrefetchScalarGridSpec(
        num_scalar_prefetch=5,
        grid=(tokens // tm,),
        in_specs=[
            pl.BlockSpec((tm, D_MODEL), lambda i, *_: (i, 0)),
            pl.BlockSpec((8, tm), lambda i, *_: (0, i)),
        ],
        out_specs=pl.BlockSpec(memory_space=pl.ANY),
        scratch_shapes=[
            pltpu.VMEM((LOC_ROWS, HALF), jnp.uint32),
            pltpu.VMEM((ROW_BLOCK, HALF), jnp.uint32),
            pltpu.SemaphoreType.DMA,
            pltpu.SemaphoreType.DMA,
        ],
    )
    return pl.pallas_call(
        _dispatch_kernel,
        grid_spec=grid_spec,
        out_shape=jax.ShapeDtypeStruct((n_rows, HALF), jnp.uint32),
        compiler_params=_params("arbitrary"),
        name="moe_dispatch",
    )(*runs, tail_block, n_used, h, pos8)


def _expert_kernel(eid_ref, used_ref, x_ref, wg_ref, wu_ref, wd_ref, y_ref, wgb_ref, wub_ref, wdb_ref):
    b = pl.program_id(0)
    changed = (b == 0) | (eid_ref[b] != eid_ref[jnp.maximum(b - 1, 0)])

    @pl.when(changed)
    def _():
        wgb_ref[...] = wg_ref[0].astype(bf16)
        wub_ref[...] = wu_ref[0].astype(bf16)
        wdb_ref[...] = wd_ref[0].astype(bf16)

    @pl.when(b < used_ref[0])
    def _():
        for c in range(ROW_BLOCK // EXPERT_CHUNK):
            rows = slice(c * EXPERT_CHUNK, (c + 1) * EXPERT_CHUNK)
            xl, xr = _unpack_rows(x_ref[rows, :])
            g = _dot(xl, wgb_ref[:HALF, :]) + _dot(xr, wgb_ref[HALF:, :])
            u = _dot(xl, wub_ref[:HALF, :]) + _dot(xr, wub_ref[HALF:, :])
            mid = (g * _sigmoid(g)) * u
            y = _dot(mid.astype(bf16), wdb_ref[...])
            y_ref[rows, :] = _pack_rows(y.astype(bf16).astype(f32))

    @pl.when(b >= used_ref[0])
    def _():
        y_ref[...] = jnp.zeros_like(y_ref)


def _experts(block_eid, n_used, rows, wg, wu, wd):
    n_rows = rows.shape[0]
    n_blocks = n_rows // ROW_BLOCK

    def row_map(b, eid, used):
        return (b, 0)

    def w_map(b, eid, used):
        return (eid[b], 0, 0)

    grid_spec = pltpu.PrefetchScalarGridSpec(
        num_scalar_prefetch=2,
        grid=(n_blocks,),
        in_specs=[
            pl.BlockSpec((ROW_BLOCK, HALF), row_map),
            pl.BlockSpec((1, D_MODEL, D_EXPERT), w_map),
            pl.BlockSpec((1, D_MODEL, D_EXPERT), w_map),
            pl.BlockSpec((1, D_EXPERT, D_MODEL), w_map),
        ],
        out_specs=pl.BlockSpec((ROW_BLOCK, HALF), row_map),
        scratch_shapes=[
            pltpu.VMEM((D_MODEL, D_EXPERT), bf16),
            pltpu.VMEM((D_MODEL, D_EXPERT), bf16),
            pltpu.VMEM((D_EXPERT, D_MODEL), bf16),
        ],
    )
    return pl.pallas_call(
        _expert_kernel,
        grid_spec=grid_spec,
        out_shape=jax.ShapeDtypeStruct((n_rows, HALF), jnp.uint32),
        compiler_params=_params("arbitrary"),
        name="moe_experts",
    )(block_eid, n_used, rows, wg, wu, wd)


def _combine_kernel(n_ref, src_ref, dst_ref, y_ref, h_ref, pos_ref, wt_ref, p_ref, wsg_ref, wsu_ref, wsd_ref,
                    wpp_ref, wpg_ref, g2_ref, b2_ref, g3_ref, b3_ref, o_ref, ybuf_ref, sem_ref):
    step = pl.program_id(0)
    tm = h_ref.shape[0]

    @pl.when(step == 0)
    def _():
        ybuf_ref[...] = jnp.zeros_like(ybuf_ref)

    _move_runs(step, n_ref, src_ref, dst_ref,
               lambda loc, glob: pltpu.make_async_copy(y_ref.at[glob, :], ybuf_ref.at[loc, :], sem_ref))

    h = h_ref[...]
    hb = h.astype(bf16)
    sg = _dot(hb, wsg_ref[...])
    su = _dot(hb, wsu_ref[...])
    shared = _dot(((sg * _sigmoid(sg)) * su).astype(bf16), wsd_ref[...])
    ple_in = _dot(p_ref[...].astype(bf16), wpp_ref[...])

    pos = pos_ref[...]
    wt = wt_ref[...]
    c_iota = lax.broadcasted_iota(i32, (tm, LOC_ROWS), 1)
    spread = jnp.where(c_iota == pos[:, 0:1], wt[:, 0:1], 0.0)
    for k in range(1, TOP_K):
        spread = jnp.where(c_iota == pos[:, k:k + 1], wt[:, k:k + 1], spread)
    spread = spread.astype(bf16)

    yl, yr = _unpack_rows(ybuf_ref[...])
    routed = jnp.concatenate([_dot(spread, yl), _dot(spread, yr)], axis=1)

    h2 = _layer_norm(ALPHA * h + (routed + shared), g2_ref[...], b2_ref[...])
    ple = ple_in * _sigmoid(_dot(h2.astype(bf16), wpg_ref[...]))
    o_ref[...] = _layer_norm(ALPHA * h2 + ple, g3_ref[...], b3_ref[...])


def _combine(runs, y_rows, h, pos_tok, w_tok, p2d, wsg, wsu, wsd, wpp, wpg, g2, b2, g3, b3):
    tokens = h.shape[0]
    tm = TOK_TM

    def full(shape):
        return pl.BlockSpec(shape, lambda i, *_: (0,) * len(shape))

    def tile(width):
        return pl.BlockSpec((tm, width), lambda i, *_: (i, 0))

    grid_spec = pltpu.PrefetchScalarGridSpec(
        num_scalar_prefetch=3,
        grid=(tokens // tm,),
        in_specs=[
            pl.BlockSpec(memory_space=pl.ANY),
            tile(D_MODEL), tile(8), tile(8), tile(PLE_DIM),
            full((D_MODEL, D_SHARED)), full((D_MODEL, D_SHARED)), full((D_SHARED, D_MODEL)),
            full((PLE_DIM, D_MODEL)), full((D_MODEL, D_MODEL)),
            full((1, D_MODEL)), full((1, D_MODEL)), full((1, D_MODEL)), full((1, D_MODEL)),
        ],
        out_specs=tile(D_MODEL),
        scratch_shapes=[
            pltpu.VMEM((LOC_ROWS, HALF), jnp.uint32),
            pltpu.SemaphoreType.DMA,
        ],
    )
    return pl.pallas_call(
        _combine_kernel,
        grid_spec=grid_spec,
        out_shape=jax.ShapeDtypeStruct((tokens, D_MODEL), f32),
        compiler_params=_params("arbitrary"),
        name="moe_combine_ln2_ple_ln3",
    )(*runs, y_rows, h, pos_tok, w_tok, p2d, wsg, wsu, wsd, wpp, wpg, g2, b2, g3, b3)


def _row_layout(table, totals, n_blocks):
    cnt_u = table[:, :, 0] // UNIT
    loc_u = table[:, :, 1] // UNIT
    padded = (totals + ROW_BLOCK - 1) // ROW_BLOCK * ROW_BLOCK
    pends = jnp.cumsum(padded)
    pstarts = pends - padded
    glob_u = (pstarts[None, :] + table[:, :, 2]) // UNIT
    cls = jnp.arange(RUN_CLASSES, dtype=i32)
    has = (cnt_u[:, None, :] >> cls[None, :, None]) & 1
    off = (cnt_u[:, None, :] >> (cls[None, :, None] + 1)) << (cls[None, :, None] + 1)
    upto = jnp.cumsum(has, axis=2)
    n_pieces = upto[:, :, -1].astype(i32).reshape(-1)
    slot = jnp.arange(N_EXPERTS, dtype=i32)
    e_of = jnp.sum((upto[:, :, None, :] <= slot[None, None, :, None]).astype(i32), axis=-1)
    is_e = e_of[:, :, :, None] == jnp.arange(N_EXPERTS, dtype=i32)
    src = jnp.sum(jnp.where(is_e, (loc_u[:, None, :] + off)[:, :, None, :], 0), axis=-1).astype(i32).reshape(-1)
    dst = jnp.sum(jnp.where(is_e, (glob_u[:, None, :] + off)[:, :, None, :], 0), axis=-1).astype(i32).reshape(-1)
    runs = (n_pieces, src, dst)
    tail_block = jnp.where(totals > 0, pends // ROW_BLOCK - 1, -1).astype(i32)
    block_start = jnp.arange(n_blocks, dtype=i32) * ROW_BLOCK
    ends_before = jnp.sum((pends[None, :] <= block_start[:, None]).astype(i32), axis=1)
    block_eid = jnp.minimum(ends_before, N_EXPERTS - 1).astype(i32)
    n_used = (pends[-1:] // ROW_BLOCK).astype(i32)
    return runs, tail_block, block_eid, n_used


def _layer(x2d, p2d, batch, seq, w_in, w_moba_up, w_ret_up, w_out, ln1_g, ln1_b, w_router, router_bias,
           w_eg, w_eu, w_ed, w_sg, w_su, w_sd, ln2_g, ln2_b, w_ple_proj, w_ple_gate, ln3_g, ln3_b):
    tokens = batch * seq
    row = lambda a: a.reshape(1, -1).astype(f32)

    proj = _project(x2d, w_in.astype(bf16), seq)
    oa = _moba(proj, batch, seq)
    orr = _retention(proj, batch, seq)

    wrt = w_router.astype(f32).T
    wrt_hi = wrt.astype(bf16)
    wrt_lo = (wrt - wrt_hi.astype(f32)).astype(bf16)
    h1, pos8, w8, table, totals = _merge_router(
        oa, orr, proj, x2d, w_moba_up.astype(bf16), w_ret_up.astype(bf16), w_out.astype(bf16),
        row(ln1_g), row(ln1_b), wrt_hi, wrt_lo, router_bias.astype(f32).reshape(N_EXPERTS, 1))

    n_tiles = tokens // TOK_TM
    max_rows = tokens * TOP_K + n_tiles * N_EXPERTS * (UNIT - 1) + N_EXPERTS * (ROW_BLOCK - 1)
    n_blocks = -(-max_rows // ROW_BLOCK)
    runs, tail_block, block_eid, n_used = _row_layout(table[:, :, :3], totals[:, 0], n_blocks)

    rows = _dispatch(runs, tail_block, n_used, h1, pos8, n_blocks * ROW_BLOCK)
    y_rows = _experts(block_eid, n_used, rows, w_eg, w_eu, w_ed)
    return _combine(runs, y_rows, h1, pos8.T, w8.T, p2d,
                    w_sg.astype(bf16), w_su.astype(bf16), w_sd.astype(bf16),
                    w_ple_proj.astype(bf16), w_ple_gate.astype(bf16),
                    row(ln2_g), row(ln2_b), row(ln3_g), row(ln3_b))


def kernel(x, p, w_in, w_moba_up, w_ret_up, w_out, ln1_g, ln1_b, w_router, router_bias, w_exp_gate, w_exp_up,
           w_exp_down, w_sh_gate, w_sh_up, w_sh_down, ln2_g, ln2_b, w_ple_proj, w_ple_gate, ln3_g, ln3_b):
    batch, seq, d = x.shape
    assert d == D_MODEL and seq % max(MOBA_BLOCK, RET_CHUNK) == 0 and (batch * seq) % (MERGE_TILES * TOK_TM) == 0
    assert w_in.shape[0] == DEPTH
    h = x.reshape(batch * seq, d)
    for i in range(DEPTH):
        h = _layer(h, p[i].reshape(batch * seq, PLE_DIM), batch, seq,
                   w_in[i], w_moba_up[i], w_ret_up[i], w_out[i], ln1_g[i], ln1_b[i], w_router[i], router_bias[i],
                   w_exp_gate[i], w_exp_up[i], w_exp_down[i], w_sh_gate[i], w_sh_up[i], w_sh_down[i],
                   ln2_g[i], ln2_b[i], w_ple_proj[i], w_ple_gate[i], ln3_g[i], ln3_b[i])
    return h.reshape(batch, seq, d)
```

```python
import functools

import jax
import jax.numpy as jnp
from jax import lax
from jax.experimental import pallas as pl
from jax.experimental.pallas import tpu as pltpu

f32 = jnp.float32
bf16 = jnp.bfloat16
i32 = jnp.int32

D_MODEL = 1024
DEPTH = 1
MOBA_HEADS = 8
MOBA_HEAD_DIM = 64
MOBA_WIDTH = MOBA_HEADS * MOBA_HEAD_DIM
MOBA_BLOCK = 256
MOBA_TOPK = 3
ROPE_THETA = 10000.0
RET_HEADS = 4
RET_QK_DIM = 128
RET_V_DIM = 256
RET_QK_WIDTH = RET_HEADS * RET_QK_DIM
RET_V_WIDTH = RET_HEADS * RET_V_DIM
RET_CHUNK = 256
RET_ANGLE_BASE = 10000.0
N_IN = 3 * MOBA_WIDTH + 2 * RET_QK_WIDTH + 2 * RET_V_WIDTH + 2 * D_MODEL
N_EXPERTS = 64
TOP_K = 6
N_GROUPS = 8
TOPK_GROUPS = 4
GROUP_SIZE = N_EXPERTS // N_GROUPS
D_EXPERT = 256
D_SHARED = 256
ROUTED_SCALE = 2.5
PLE_DIM = 256
LN_EPS = 1e-5
GN_EPS = 1e-6
ALPHA = (2.0 * DEPTH) ** 0.25

OFF_QA = 0
OFF_KA = MOBA_WIDTH
OFF_VA = 2 * MOBA_WIDTH
OFF_QR = 3 * MOBA_WIDTH
OFF_KR = OFF_QR + RET_QK_WIDTH
OFF_VR = OFF_KR + RET_QK_WIDTH
OFF_GR = OFF_VR + RET_V_WIDTH
OFF_GA = OFF_GR + RET_V_WIDTH
OFF_GB = OFF_GA + D_MODEL

MOBA_Q_SCALE = MOBA_HEAD_DIM ** -0.5 * 1.4426950408889634

LANES = 128
NEG = -1e30
VMEM_LIMIT = 56 * 1024 * 1024

PROJ_TM = 2048
PROJ_TN = 512
MOBA_STEP_WIDTH = 512
MOE_TILES = 2
MERGE_TILES = 4
TOK_TM = 256
ROW_BLOCK = 512
EXPERT_CHUNK = 256
UNIT = 8
LOC_ROWS = -(-(TOP_K * TOK_TM + N_EXPERTS * (UNIT - 1)) // LANES) * LANES
RUN_CLASSES = (TOK_TM // UNIT).bit_length()
HALF = D_MODEL // 2


def _dot(a, b):
    return jnp.dot(a, b, preferred_element_type=f32)


def _dot_nt(a, b):
    return lax.dot_general(a, b, (((1,), (1,)), ((), ())), preferred_element_type=f32)


def _dot_tn(a, b):
    return lax.dot_general(a, b, (((0,), (0,)), ((), ())), preferred_element_type=f32)


def _split_bf16(a):
    hi = a.astype(bf16)
    lo = (a - hi.astype(f32)).astype(bf16)
    return hi, lo


def _layer_norm(x, g, b):
    mu = jnp.mean(x, axis=-1, keepdims=True)
    xc = x - mu
    var = jnp.mean(xc * xc, axis=-1, keepdims=True)
    return xc * lax.rsqrt(var + LN_EPS) * g + b


def _sigmoid(x):
    return 1.0 / (1.0 + jnp.exp(-x))


def _interleave(chains):
    chains = list(chains)
    while chains:
        chains = [chain for chain in chains if next(chain, "done") != "done"]


def _params(*sem, flags=None):
    return pltpu.CompilerParams(dimension_semantics=sem, vmem_limit_bytes=VMEM_LIMIT, flags=flags)


def _rotate_half_chunk(xc, half):
    if 2 * half == LANES:
        return pltpu.roll(xc, half, axis=1)
    lane = lax.broadcasted_iota(i32, xc.shape, 1)
    first = (lane & (2 * half - 1)) < half
    return jnp.where(first, pltpu.roll(xc, LANES - half, axis=1), pltpu.roll(xc, half, axis=1))


def _proj_kernel(x_ref, w_ref, cos_a_ref, sin_a_ref, cos_r_ref, sin_r_ref, o_ref, xb_ref):
    j = pl.program_id(1)

    @pl.when(j == 0)
    def _():
        xb_ref[...] = x_ref[...].astype(bf16)

    acc = _dot(xb_ref[...], w_ref[...])

    def rotary(cos_ref, sin_ref, half, scale):
        cos = cos_ref[...]
        sin = sin_ref[...]
        for c in range(PROJ_TN // LANES):
            xc = acc[:, c * LANES:(c + 1) * LANES]
            y = xc * cos + _rotate_half_chunk(xc, half) * sin
            if scale != 1.0:
                y = y * scale
            o_ref[:, c * LANES:(c + 1) * LANES] = y.astype(o_ref.dtype)

    j_qa = OFF_QA // PROJ_TN
    j_ka = OFF_KA // PROJ_TN
    j_qr = OFF_QR // PROJ_TN
    j_kr = OFF_KR // PROJ_TN

    @pl.when(j == j_qa)
    def _():
        rotary(cos_a_ref, sin_a_ref, MOBA_HEAD_DIM // 2, MOBA_Q_SCALE)

    @pl.when(j == j_ka)
    def _():
        rotary(cos_a_ref, sin_a_ref, MOBA_HEAD_DIM // 2, 1.0)

    @pl.when(j == j_qr)
    def _():
        rotary(cos_r_ref, sin_r_ref, RET_QK_DIM // 2, 1.0)

    @pl.when(j == j_kr)
    def _():
        rotary(cos_r_ref, sin_r_ref, RET_QK_DIM // 2, RET_QK_DIM ** -0.5)

    plain = (j != j_qa) & (j != j_ka) & (j != j_qr) & (j != j_kr)

    @pl.when(plain)
    def _():
        o_ref[...] = acc.astype(o_ref.dtype)


def _rotary_tables(seq, inv_freq, head_dim):
    ang = jnp.arange(seq, dtype=f32)[:, None] * inv_freq[None, :]
    cos = jnp.cos(ang)
    sin = jnp.sin(ang)
    cos_h = jnp.concatenate([cos, cos], axis=-1)
    sin_h = jnp.concatenate([-sin, sin], axis=-1)
    reps = LANES // head_dim
    return jnp.tile(cos_h, (1, reps)), jnp.tile(sin_h, (1, reps))


def _project(x2d, w_in_b, seq):
    tokens = x2d.shape[0]
    tm = min(PROJ_TM, seq)
    inv_a = 1.0 / (ROPE_THETA ** (jnp.arange(0, MOBA_HEAD_DIM, 2, dtype=f32) / MOBA_HEAD_DIM))
    inv_r = 1.0 / (RET_ANGLE_BASE ** jnp.linspace(0.0, 1.0, RET_QK_DIM // 2, dtype=f32))
    cos_a, sin_a = _rotary_tables(seq, inv_a, MOBA_HEAD_DIM)
    cos_r, sin_r = _rotary_tables(seq, inv_r, RET_QK_DIM)
    seq_tiles = seq // tm
    tab = pl.BlockSpec((tm, LANES), lambda i, j: (i % seq_tiles, 0))
    return pl.pallas_call(
        _proj_kernel,
        grid=(tokens // tm, N_IN // PROJ_TN),
        in_specs=[
            pl.BlockSpec((tm, D_MODEL), lambda i, j: (i, 0)),
            pl.BlockSpec((D_MODEL, PROJ_TN), lambda i, j: (0, j)),
            tab, tab, tab, tab,
        ],
        out_specs=pl.BlockSpec((tm, PROJ_TN), lambda i, j: (i, j)),
        out_shape=jax.ShapeDtypeStruct((tokens, N_IN), bf16),
        scratch_shapes=[pltpu.VMEM((tm, D_MODEL), bf16)],
        compiler_params=_params("parallel", "arbitrary"),
        name="in_proj_rotary",
    )(x2d, w_in_b, cos_a, sin_a, cos_r, sin_r)


def _moba_kernel(q_ref, k_ref, v_ref, o_ref, kmean_ref, vt_ref, selb_ref, gate_ref, *, n_blocks):
    i = pl.program_id(2)
    blk = MOBA_BLOCK
    hd = MOBA_HEAD_DIM
    width = q_ref.shape[1]
    chunks = width // LANES
    heads = width // hd
    per_chunk = LANES // hd

    def chunk_of(h):
        return slice(h // per_chunk * LANES, (h // per_chunk + 1) * LANES)

    @pl.when(i == 0)
    def _():
        seq = n_blocks * blk
        member = (lax.broadcasted_iota(i32, (n_blocks, seq), 1) // blk
                  == lax.broadcasted_iota(i32, (n_blocks, seq), 0))
        kmean_ref[...] = _dot(member.astype(bf16), k_ref[...]) * (1.0 / blk)
        for j in range(n_blocks):
            vt_ref[j] = v_ref[j * blk:(j + 1) * blk, :].astype(f32).T.astype(bf16)

    km_hi, km_lo = _split_bf16(kmean_ref[...])
    key_i = lax.broadcasted_iota(i32, (blk, blk), 0)
    qry_i = lax.broadcasted_iota(i32, (blk, blk), 1)
    causal = key_i <= qry_i
    bid = lax.broadcasted_iota(i32, (n_blocks, blk), 0)
    d_row = lax.broadcasted_iota(i32, (LANES, blk), 0)
    qt = q_ref[...].astype(f32).T

    qts = []
    for h in range(heads):
        r0 = h % per_chunk * hd
        in_head = (d_row >= r0) & (d_row < r0 + hd)
        qth = jnp.where(in_head, qt[chunk_of(h), :], 0.0).astype(bf16)
        qts.append(qth)
        gate = _dot(km_hi[:, chunk_of(h)], qth) + _dot(km_lo[:, chunk_of(h)], qth)
        g = jnp.where(bid < i, gate, -jnp.inf)
        gate_ref[...] = g
        beaten = jnp.zeros((n_blocks, blk), f32)
        for j in range(n_blocks):
            gj = jnp.broadcast_to(gate_ref[j:j + 1, :], (n_blocks, blk))
            beats = (gj > g) | ((gj == g) & (bid > j))
            beaten = beaten + jnp.where(beats, 1.0, 0.0)
        sel = (beaten < MOBA_TOPK) & (bid < i)
        selb_ref[h] = jnp.where(sel, 0.0, NEG)

    kj = k_ref[pl.ds(pl.multiple_of(i * blk, blk), blk), :]
    vtj = vt_ref[i]
    state = []
    for h in range(heads):
        s = jnp.where(causal, _dot(kj[:, chunk_of(h)], qts[h]), NEG)
        m0 = jnp.max(s, axis=0, keepdims=True)
        p = jnp.exp2(s - m0)
        l0 = jnp.sum(p, axis=0, keepdims=True)
        acc0 = _dot(vtj[h * hd:(h + 1) * hd, :], p.astype(bf16))
        state += [m0, l0, acc0]

    qt_alls = [jnp.concatenate(qts[c * per_chunk:(c + 1) * per_chunk], axis=1) for c in range(chunks)]

    def update(carry, js):
        nj = len(js)
        k_all = k_ref[pl.ds(pl.multiple_of(js[0] * blk, blk), nj * blk), :]
        s_alls = [_dot(k_all[:, c * LANES:(c + 1) * LANES], qt_alls[c]) for c in range(chunks)]
        vtjs = [vt_ref[j] for j in js]
        new = []
        for h in range(heads):
            m_old, l_old, acc = carry[3 * h:3 * h + 3]
            col = h % per_chunk * blk
            ss = [s_alls[h // per_chunk][n * blk:(n + 1) * blk, col:col + blk] + selb_ref[h, pl.ds(j, 1), :]
                  for n, j in enumerate(js)]
            m_new = m_old
            for s in ss:
                m_new = jnp.maximum(m_new, jnp.max(s, axis=0, keepdims=True))
            a = jnp.exp2(m_old - m_new)
            l_new = a * l_old
            acc = a * acc
            for s, vtj in zip(ss, vtjs):
                p = jnp.exp2(s - m_new)
                l_new = l_new + jnp.sum(p, axis=0, keepdims=True)
                acc = acc + _dot(vtj[h * hd:(h + 1) * hd, :], p.astype(bf16))
            new += [m_new, l_new, acc]
        return tuple(new)

    quads = i // 4
    fin = lax.fori_loop(0, quads, lambda n, c: update(c, tuple(4 * n + t for t in range(4))), tuple(state))
    done = 4 * quads
    has_pair = (i - done) // 2
    fin = lax.fori_loop(0, has_pair, lambda n, c: update(c, (done, done + 1)), fin)
    done = done + 2 * has_pair
    fin = lax.fori_loop(done, i, lambda j, c: update(c, (j,)), fin)
    out_t = jnp.concatenate([fin[3 * h + 2] / fin[3 * h + 1] for h in range(heads)], axis=0)
    o_ref[...] = out_t.T.astype(o_ref.dtype)


def _moba(proj, batch, seq):
    tokens = batch * seq
    n_blocks = seq // MOBA_BLOCK
    width = MOBA_STEP_WIDTH
    return pl.pallas_call(
        functools.partial(_moba_kernel, n_blocks=n_blocks),
        grid=(batch, MOBA_WIDTH // width, n_blocks),
        in_specs=[
            pl.BlockSpec((MOBA_BLOCK, width), lambda b, c, i: (b * n_blocks + i, OFF_QA // width + c)),
            pl.BlockSpec((seq, width), lambda b, c, i: (b, OFF_KA // width + c)),
            pl.BlockSpec((seq, width), lambda b, c, i: (b, OFF_VA // width + c)),
        ],
        out_specs=pl.BlockSpec((MOBA_BLOCK, width), lambda b, c, i: (b * n_blocks + i, c)),
        out_shape=jax.ShapeDtypeStruct((tokens, MOBA_WIDTH), bf16),
        scratch_shapes=[
            pltpu.VMEM((n_blocks, width), f32),
            pltpu.VMEM((n_blocks, width, MOBA_BLOCK), bf16),
            pltpu.VMEM((width // MOBA_HEAD_DIM, n_blocks, MOBA_BLOCK), f32),
            pltpu.VMEM((n_blocks, MOBA_BLOCK), f32),
        ],
        compiler_params=_params("parallel", "parallel", "arbitrary"),
        name="moba_attention",
    )(proj, proj, proj)


def _ret_kernel(q_ref, k_ref, v0_ref, v1_ref, g0_ref, g1_ref, decay_ref, zeta_ref, xi_ref, cd_ref,
                o_ref, state_ref):
    c = pl.program_id(1)
    heads_per_half = RET_HEADS // 2
    v_refs = (v0_ref, v1_ref)
    g_refs = (g0_ref, g1_ref)

    @pl.when(c == 0)
    def _():
        state_ref[...] = jnp.zeros_like(state_ref)

    for h in range(RET_HEADS):
        qk_cols = slice(h * RET_QK_DIM, (h + 1) * RET_QK_DIM)
        v_cols = slice(h * RET_V_DIM, (h + 1) * RET_V_DIM)
        half_cols = slice((h % heads_per_half) * RET_V_DIM, (h % heads_per_half + 1) * RET_V_DIM)
        q = q_ref[:, qk_cols]
        k = k_ref[:, qk_cols]
        v = v_refs[h // heads_per_half][:, half_cols]
        state = state_ref[h]
        scores = _dot_nt(q, k) * decay_ref[h]
        inner = _dot(scores.astype(bf16), v)
        cross = _dot(q, state.astype(bf16)) * xi_ref[h]
        o = inner + cross
        kz = (k.astype(f32) * zeta_ref[h]).astype(bf16)
        state_ref[h] = cd_ref[h] * state + _dot_tn(kz, v)

        mu = jnp.mean(o, axis=-1, keepdims=True)
        oc = o - mu
        var = jnp.mean(oc * oc, axis=-1, keepdims=True)
        on = oc * lax.rsqrt(var + GN_EPS)
        g = g_refs[h // heads_per_half][:, half_cols].astype(f32)
        o_ref[:, v_cols] = (on * (g * _sigmoid(g))).astype(o_ref.dtype)


def _retention(proj, batch, seq):
    tokens = batch * seq
    C = RET_CHUNK
    n_chunks = seq // C
    half_v = RET_V_WIDTH // 2
    gammas =1.0 - 2.0 ** (-5.0 - jnp.arange(RET_HEADS, dtype=f32))
    log_g = jnp.log(gammas)
    idx = jnp.arange(C, dtype=f32)
    diff = idx[:, None] - idx[None, :]
    decay = jnp.where(diff >= 0, jnp.exp(jnp.maximum(diff, 0.0)[None] * log_g[:, None, None]), 0.0)
    zeta = jnp.exp((C - 1 - idx)[None, :] * log_g[:, None])
    xi = jnp.exp((idx + 1.0)[None, :] * log_g[:, None])
    zeta_t = jnp.broadcast_to(zeta[:, :, None], (RET_HEADS, C, RET_QK_DIM))
    xi_t = jnp.broadcast_to(xi[:, :, None], (RET_HEADS, C, RET_V_DIM))
    cd_t = jnp.broadcast_to(jnp.exp(C * log_g)[:, None, None], (RET_HEADS, 1, RET_V_DIM))
    return pl.pallas_call(
        _ret_kernel,
        grid=(batch, n_chunks),
        in_specs=[
            pl.BlockSpec((C, RET_QK_WIDTH), lambda b, c: (b * n_chunks + c, OFF_QR // RET_QK_WIDTH)),
            pl.BlockSpec((C, RET_QK_WIDTH), lambda b, c: (b * n_chunks + c, OFF_KR // RET_QK_WIDTH)),
            pl.BlockSpec((C, half_v), lambda b, c: (b * n_chunks + c, OFF_VR // half_v)),
            pl.BlockSpec((C, half_v), lambda b, c: (b * n_chunks + c, OFF_VR // half_v + 1)),
            pl.BlockSpec((C, half_v), lambda b, c: (b * n_chunks + c, OFF_GR // half_v)),
            pl.BlockSpec((C, half_v), lambda b, c: (b * n_chunks + c, OFF_GR // half_v + 1)),
            pl.BlockSpec((RET_HEADS, C, C), lambda b, c: (0, 0, 0)),
            pl.BlockSpec((RET_HEADS, C, RET_QK_DIM), lambda b, c: (0, 0, 0)),
            pl.BlockSpec((RET_HEADS, C, RET_V_DIM), lambda b, c: (0, 0, 0)),
            pl.BlockSpec((RET_HEADS, 1, RET_V_DIM), lambda b, c: (0, 0, 0)),
        ],
        out_specs=pl.BlockSpec((C, RET_V_WIDTH), lambda b, c: (b * n_chunks + c, 0)),
        out_shape=jax.ShapeDtypeStruct((tokens, RET_V_WIDTH), bf16),
        scratch_shapes=[pltpu.VMEM((RET_HEADS, RET_QK_DIM, RET_V_DIM), f32)],
        compiler_params=_params("parallel", "arbitrary"),
        name="retention",
    )(proj, proj, proj, proj, proj, proj, decay, zeta_t, xi_t, cd_t)


def _merge_router_kernel(oa_ref, orr_ref, ga0_ref, ga1_ref, gb0_ref, gb1_ref, x_ref,
                         wm_ref, wr_ref, wo_ref, g1_ref, b1_ref, wrt_hi_ref, wrt_lo_ref, rb_ref,
                         h_ref, pos_ref, w_ref, tab_ref, cnt_ref, carry_ref):
    step = pl.program_id(0)
    tm = TOK_TM

    @pl.when(step == 0)
    def _():
        carry_ref[...] = jnp.zeros_like(carry_ref)

    _interleave(
        _merge_router_tile(t, slice(t * tm, (t + 1) * tm), tm, oa_ref, orr_ref, ga0_ref, ga1_ref, gb0_ref, gb1_ref,
                           x_ref, wm_ref, wr_ref, wo_ref, g1_ref, b1_ref, wrt_hi_ref, wrt_lo_ref, rb_ref,
                           h_ref, pos_ref, w_ref, tab_ref, cnt_ref, carry_ref)
        for t in range(x_ref.shape[0] // tm))


def _merge_router_tile(t, rows, tm, oa_ref, orr_ref, ga0_ref, ga1_ref, gb0_ref, gb1_ref, x_ref,
                       wm_ref, wr_ref, wo_ref, g1_ref, b1_ref, wrt_hi_ref, wrt_lo_ref, rb_ref,
                       h_ref, pos_ref, w_ref, tab_ref, cnt_ref, carry_ref):
    a = _dot(oa_ref[rows, :], wm_ref[...])
    r = _dot(orr_ref[rows, :], wr_ref[...])
    yield
    ga = jnp.concatenate([ga0_ref[rows, :], ga1_ref[rows, :]], axis=1).astype(f32)
    gb = jnp.concatenate([gb0_ref[rows, :], gb1_ref[rows, :]], axis=1).astype(f32)
    merged = _sigmoid(ga) * a + _sigmoid(gb) * r
    yield
    mix = _dot(merged.astype(bf16), wo_ref[...])
    yield
    h = _layer_norm(ALPHA * x_ref[rows, :] + mix, g1_ref[...], b1_ref[...])
    h_ref[rows, :] = h
    yield

    h_hi, h_lo = _split_bf16(h)
    w_hi = wrt_hi_ref[...]
    logits = _dot_nt(w_hi, h_hi) + _dot_nt(w_hi, h_lo) + _dot_nt(wrt_lo_ref[...], h_hi)
    yield
    scores = _sigmoid(logits)
    biased = scores + rb_ref[...]

    v = biased.reshape(N_GROUPS, GROUP_SIZE, tm)
    sub = lax.broadcasted_iota(i32, v.shape, 1)
    m1 = jnp.max(v, axis=1, keepdims=True)
    i1 = jnp.min(jnp.where(v == m1, sub, GROUP_SIZE), axis=1, keepdims=True)
    m2 = jnp.max(jnp.where(sub == i1, -jnp.inf, v), axis=1, keepdims=True)
    gscore = jnp.broadcast_to(m1 + m2, v.shape).reshape(N_EXPERTS, tm)

    eid = lax.broadcasted_iota(i32, (N_EXPERTS, tm), 0)
    egrp = eid // GROUP_SIZE
    e_mask = jnp.zeros((N_EXPERTS, tm), jnp.bool_)
    for _ in range(TOPK_GROUPS):
        m = jnp.max(gscore, axis=0, keepdims=True)
        idx = jnp.min(jnp.where(gscore == m, egrp, N_GROUPS), axis=0, keepdims=True)
        hit = egrp == idx
        e_mask = e_mask | hit
        gscore = jnp.where(hit, -jnp.inf, gscore)
    cand = jnp.where(e_mask, biased, -jnp.inf)

    chosen = jnp.zeros((N_EXPERTS, tm), jnp.bool_)
    e_rows = []
    w_rows = []
    for _ in range(TOP_K):
        m = jnp.max(cand, axis=0, keepdims=True)
        idx = jnp.min(jnp.where(cand == m, eid, N_EXPERTS), axis=0, keepdims=True)
        hit = eid == idx
        chosen = chosen | hit
        e_rows.append(idx)
        w_rows.append(jnp.sum(jnp.where(hit, scores, 0.0), axis=0, keepdims=True))
        cand = jnp.where(hit, -jnp.inf, cand)
    w_sum = w_rows[0]
    for wk in w_rows[1:]:
        w_sum = w_sum + wk

    t_src = lax.broadcasted_iota(i32, (tm, tm), 0)
    t_dst = lax.broadcasted_iota(i32, (tm, tm), 1)
    before = (t_src < t_dst).astype(bf16)
    chosen_f = chosen.astype(f32)
    prior = _dot(chosen_f.astype(bf16), before)
    cnt = jnp.sum(chosen_f, axis=1, keepdims=True)
    cnt_pad = jnp.ceil(cnt * (1.0 / UNIT)) * UNIT
    e_src = lax.broadcasted_iota(i32, (N_EXPERTS, N_EXPERTS), 1)
    e_dst = lax.broadcasted_iota(i32, (N_EXPERTS, N_EXPERTS), 0)
    earlier = (e_src < e_dst).astype(bf16)
    cnt_pad_l = jnp.broadcast_to(cnt_pad, (N_EXPERTS, LANES))
    loc_start_l = _dot(earlier, cnt_pad_l.astype(bf16))
    loc_start = loc_start_l[:, 0:1]
    where_e = prior + loc_start
    pos_ref[:, rows] = jnp.zeros((pos_ref.shape[0], tm), pos_ref.dtype)
    w_ref[:, rows] = jnp.zeros((w_ref.shape[0], tm), w_ref.dtype)
    for k in range(TOP_K):
        pos_ref[k:k + 1, rows] = jnp.sum(jnp.where(eid == e_rows[k], where_e, 0.0), axis=0, keepdims=True).astype(i32)
        w_ref[k:k + 1, rows] = w_rows[k] / w_sum * ROUTED_SCALE

    carry = carry_ref[...]
    tl = lax.broadcasted_iota(i32, (N_EXPERTS, LANES), 1)
    table = jnp.where(tl == 0, cnt_pad_l, jnp.where(tl == 1, loc_start_l, jnp.broadcast_to(carry, (N_EXPERTS, LANES))))
    tab_ref[t] = table.astype(i32)
    carry = carry + cnt_pad
    carry_ref[...] = carry
    cnt_ref[...] = jnp.broadcast_to(carry, cnt_ref.shape).astype(i32)


def _merge_router(oa, orr, proj, x2d, wm, wr, wo, g1, b1, wrt_hi, wrt_lo, rb):
    tokens = x2d.shape[0]
    tm = MERGE_TILES * TOK_TM
    half = D_MODEL // 2

    def gate_spec(off):
        return pl.BlockSpec((tm, half), lambda i: (i, off // half))

    def full(shape):
        return pl.BlockSpec(shape, lambda i: (0,) * len(shape))

    row8 = pl.BlockSpec((8, tm), lambda i: (0, i))
    return pl.pallas_call(
        _merge_router_kernel,
        grid=(tokens // tm,),
        in_specs=[
            pl.BlockSpec((tm, MOBA_WIDTH), lambda i: (i, 0)),
            pl.BlockSpec((tm, RET_V_WIDTH), lambda i: (i, 0)),
            gate_spec(OFF_GA), gate_spec(OFF_GA + half), gate_spec(OFF_GB), gate_spec(OFF_GB + half),
            pl.BlockSpec((tm, D_MODEL), lambda i: (i, 0)),
            full((MOBA_WIDTH, D_MODEL)), full((RET_V_WIDTH, D_MODEL)), full((D_MODEL, D_MODEL)),
            full((1, D_MODEL)), full((1, D_MODEL)),
            full((N_EXPERTS, D_MODEL)), full((N_EXPERTS, D_MODEL)), full((N_EXPERTS, 1)),
        ],
        out_specs=[
            pl.BlockSpec((tm, D_MODEL), lambda i: (i, 0)),
            row8, row8,
            pl.BlockSpec((MERGE_TILES, N_EXPERTS, LANES), lambda i: (i, 0, 0)),
            full((N_EXPERTS, LANES)),
        ],
        out_shape=[
            jax.ShapeDtypeStruct((tokens, D_MODEL), f32),
            jax.ShapeDtypeStruct((8, tokens), i32),
            jax.ShapeDtypeStruct((8, tokens), f32),
            jax.ShapeDtypeStruct((tokens // TOK_TM, N_EXPERTS, LANES), i32),
            jax.ShapeDtypeStruct((N_EXPERTS, LANES), i32),
        ],
        scratch_shapes=[pltpu.VMEM((N_EXPERTS, 1), f32)],
        compiler_params=_params("arbitrary"),
        name="merge_ln1_router",
    )(oa, orr, proj, proj, proj, proj, x2d, wm, wr, wo, g1, b1, wrt_hi, wrt_lo, rb)


def _pack_rows(x):
    bits = lax.bitcast_convert_type(x, jnp.uint32)
    return (bits[:, :HALF] & jnp.uint32(0xFFFF0000)) | (bits[:, HALF:] >> 16)


def _unpack_rows(words):
    left = lax.bitcast_convert_type(words & jnp.uint32(0xFFFF0000), f32)
    right = lax.bitcast_convert_type(words << 16, f32)
    return left.astype(bf16), right.astype(bf16)


def _start_alternating(n, make_copy):
    def pair(m, _):
        make_copy(2 * m).start(priority=0)
        make_copy(2 * m + 1).start(priority=1)
        return 0

    lax.fori_loop(0, n // 2, pair, 0)

    @pl.when(n % 2 == 1)
    def _():
        make_copy(n - 1).start(priority=0)


def _move_runs(tile, n_ref, src_ref, dst_ref, make_copy):
    def piece(c, p):
        rows = UNIT << c
        base = (tile * RUN_CLASSES + c) * N_EXPERTS
        loc = pl.ds(pl.multiple_of(src_ref[base + p] * UNIT, UNIT), rows)
        glob = pl.ds(pl.multiple_of(dst_ref[base + p] * UNIT, UNIT), rows)
        return make_copy(loc, glob)

    for c in range(RUN_CLASSES):
        _start_alternating(n_ref[tile * RUN_CLASSES + c], functools.partial(piece, c))

    for c in range(RUN_CLASSES):
        def wait_one(p, _, c=c):
            piece(c, p).wait()
            return 0

        lax.fori_loop(0, n_ref[tile * RUN_CLASSES + c], wait_one, 0)


def _dispatch_kernel(n_ref, src_ref, dst_ref, tail_ref, used_ref, h_ref, pos_ref, rows_ref,
                     xs_ref, zero_ref, sem_ref, zsem_ref):
    step = pl.program_id(0)
    tm = TOK_TM
    tiles = h_ref.shape[0] // tm

    @pl.when(step == 0)
    def _():
        zero_ref[...] = jnp.zeros_like(zero_ref)

        def clear_copy(block):
            start = pl.multiple_of(block * ROW_BLOCK, ROW_BLOCK)
            return pltpu.make_async_copy(zero_ref, rows_ref.at[pl.ds(start, ROW_BLOCK), :], zsem_ref)

        def clear_all(act):
            def tail(e, _):
                tail_block = tail_ref[e]

                @pl.when(tail_block >= 0)
                def _():
                    act(clear_copy(tail_block))
                return 0

            lax.fori_loop(0, N_EXPERTS, tail, 0)

            def unused(b, _):
                act(clear_copy(b))
                return 0

            lax.fori_loop(used_ref[0], rows_ref.shape[0] // ROW_BLOCK, unused, 0)

        clear_all(lambda cp: cp.start())
        clear_all(lambda cp: cp.wait())

    def sort_tile(u):
        cols = slice(u * tm, (u + 1) * tm)
        pos = pos_ref[:, cols]
        r_iota = lax.broadcasted_iota(i32, (LOC_ROWS, tm), 0)
        onehot = r_iota == pos[0:1, :]
        for k in range(1, TOP_K):
            onehot = onehot | (r_iota == pos[k:k + 1, :])
        onehot = onehot.astype(bf16)
        yield
        xs = _dot(onehot, h_ref[cols, :].astype(bf16))
        yield
        xs_ref[u] = _pack_rows(xs)

    _interleave(sort_tile(u) for u in range(tiles))

    for u in range(tiles):
        _move_runs(step * tiles + u, n_ref, src_ref, dst_ref,
                   lambda loc, glob, u=u: pltpu.make_async_copy(xs_ref.at[u, loc, :], rows_ref.at[glob, :], sem_ref))


def _dispatch(runs, tail_block, n_used, h, pos8, n_rows):
    tokens = h.shape[0]
    tm = MOE_TILES * TOK_TM
    grid_spec = pltpu.PrefetchScalarGridSpec(
        num_scalar_prefetch=5,
        grid=(tokens // tm,),
        in_specs=[
            pl.BlockSpec((tm, D_MODEL), lambda i, *_: (i, 0)),
            pl.BlockSpec((8, tm), lambda i, *_: (0, i)),
        ],
        out_specs=pl.BlockSpec(memory_space=pl.ANY),
        scratch_shapes=[
            pltpu.VMEM((MOE_TILES, LOC_ROWS, HALF), jnp.uint32),
            pltpu.VMEM((ROW_BLOCK, HALF), jnp.uint32),
            pltpu.SemaphoreType.DMA,
            pltpu.SemaphoreType.DMA,
        ],
    )
    return pl.pallas_call(
        _dispatch_kernel,
        grid_spec=grid_spec,
        out_shape=jax.ShapeDtypeStruct((n_rows, HALF), jnp.uint32),
        compiler_params=_params("arbitrary"),
        name="moe_dispatch",
    )(*runs, tail_block, n_used, h, pos8)


def _expert_kernel(eid_ref, used_ref, x_ref, wg_ref, wu_ref, wd_ref, y_ref, wgb_ref, wub_ref, wdb_ref):
    b = pl.program_id(0)
    changed = (b == 0) | (eid_ref[b] != eid_ref[jnp.maximum(b - 1, 0)])

    @pl.when(changed)
    def _():
        wgb_ref[...] = wg_ref[0].astype(bf16)
        wub_ref[...] = wu_ref[0].astype(bf16)
        wdb_ref[...] = wd_ref[0].astype(bf16)

    @pl.when(b < used_ref[0])
    def _():
        def chunk_phases(c):
            rows = slice(c * EXPERT_CHUNK, (c + 1) * EXPERT_CHUNK)
            xl, xr = _unpack_rows(x_ref[rows, :])
            yield
            g = _dot(xl, wgb_ref[:HALF, :]) + _dot(xr, wgb_ref[HALF:, :])
            u = _dot(xl, wub_ref[:HALF, :]) + _dot(xr, wub_ref[HALF:, :])
            yield
            mid = ((g * _sigmoid(g)) * u).astype(bf16)
            yield
            y = _dot(mid, wdb_ref[...])
            yield
            y_ref[rows, :] = _pack_rows(y.astype(bf16).astype(f32))

        _interleave(chunk_phases(c) for c in range(ROW_BLOCK // EXPERT_CHUNK))

    @pl.when(b >= used_ref[0])
    def _():
        y_ref[...] = jnp.zeros_like(y_ref)


def _experts(block_eid, n_used, rows, wg, wu, wd):
    n_rows = rows.shape[0]
    n_blocks = n_rows // ROW_BLOCK

    def row_map(b, eid, used):
        return (b, 0)

    def w_map(b, eid, used):
        return (eid[b], 0, 0)

    grid_spec = pltpu.PrefetchScalarGridSpec(
        num_scalar_prefetch=2,
        grid=(n_blocks,),
        in_specs=[
            pl.BlockSpec((ROW_BLOCK, HALF), row_map),
            pl.BlockSpec((1, D_MODEL, D_EXPERT), w_map),
            pl.BlockSpec((1, D_MODEL, D_EXPERT), w_map),
            pl.BlockSpec((1, D_EXPERT, D_MODEL), w_map),
        ],
        out_specs=pl.BlockSpec((ROW_BLOCK, HALF), row_map),
        scratch_shapes=[
            pltpu.VMEM((D_MODEL, D_EXPERT), bf16),
            pltpu.VMEM((D_MODEL, D_EXPERT), bf16),
            pltpu.VMEM((D_EXPERT, D_MODEL), bf16),
        ],
    )
    return pl.pallas_call(
        _expert_kernel,
        grid_spec=grid_spec,
        out_shape=jax.ShapeDtypeStruct((n_rows, HALF), jnp.uint32),
        compiler_params=_params("arbitrary"),
        name="moe_experts",
    )(block_eid, n_used, rows, wg, wu, wd)


def _combine_kernel(n_ref, src_ref, dst_ref, y_ref, h_ref, pos_ref, wt_ref, p_ref, wsg_ref, wsu_ref, wsd_ref,
                    wpp_ref, wpg_ref, g2_ref, b2_ref, g3_ref, b3_ref, o_ref, ybuf_ref, sem_ref):
    step = pl.program_id(0)
    tm = TOK_TM
    tiles = h_ref.shape[0] // tm

    @pl.when(step == 0)
    def _():
        ybuf_ref[...] = jnp.zeros_like(ybuf_ref)

    for u in range(tiles):
        _move_runs(step * tiles + u, n_ref, src_ref, dst_ref,
                   lambda loc, glob, u=u: pltpu.make_async_copy(y_ref.at[glob, :], ybuf_ref.at[u, loc, :], sem_ref))

    def tile_phases(u):
        rows = slice(u * tm, (u + 1) * tm)
        h = h_ref[rows, :]
        hb = h.astype(bf16)
        sg = _dot(hb, wsg_ref[...])
        su = _dot(hb, wsu_ref[...])
        yield
        mid = ((sg * _sigmoid(sg)) * su).astype(bf16)
        yield
        shared = _dot(mid, wsd_ref[...])
        ple_in = _dot(p_ref[rows, :].astype(bf16), wpp_ref[...])
        yield
        pos = pos_ref[rows, :]
        wt = wt_ref[rows, :]
        c_iota = lax.broadcasted_iota(i32, (tm, LOC_ROWS), 1)
        spread = jnp.where(c_iota == pos[:, 0:1], wt[:, 0:1], 0.0)
        for k in range(1, TOP_K):
            spread = jnp.where(c_iota == pos[:, k:k + 1], wt[:, k:k + 1], spread)
        spread = spread.astype(bf16)
        yield
        yl, yr = _unpack_rows(ybuf_ref[u])
        yield
        routed = jnp.concatenate([_dot(spread, yl), _dot(spread, yr)], axis=1)
        yield
        h2 = _layer_norm(ALPHA * h + (routed + shared), g2_ref[...], b2_ref[...])
        yield
        gate = _dot(h2.astype(bf16), wpg_ref[...])
        yield
        ple = ple_in * _sigmoid(gate)
        o_ref[rows, :] = _layer_norm(ALPHA * h2 + ple, g3_ref[...], b3_ref[...])

    _interleave(tile_phases(u) for u in range(tiles))


def _combine(runs, y_rows, h, pos_tok, w_tok, p2d, wsg, wsu, wsd, wpp, wpg, g2, b2, g3, b3):
    tokens = h.shape[0]
    tm = MOE_TILES * TOK_TM

    def full(shape):
        return pl.BlockSpec(shape, lambda i, *_: (0,) * len(shape))

    def tile(width):
        return pl.BlockSpec((tm, width), lambda i, *_: (i, 0))

    grid_spec = pltpu.PrefetchScalarGridSpec(
        num_scalar_prefetch=3,
        grid=(tokens // tm,),
        in_specs=[
            pl.BlockSpec(memory_space=pl.ANY),
            tile(D_MODEL), tile(8), tile(8), tile(PLE_DIM),
            full((D_MODEL, D_SHARED)), full((D_MODEL, D_SHARED)), full((D_SHARED, D_MODEL)),
            full((PLE_DIM, D_MODEL)), full((D_MODEL, D_MODEL)),
            full((1, D_MODEL)), full((1, D_MODEL)), full((1, D_MODEL)), full((1, D_MODEL)),
        ],
        out_specs=tile(D_MODEL),
        scratch_shapes=[
            pltpu.VMEM((MOE_TILES, LOC_ROWS, HALF), jnp.uint32),
            pltpu.SemaphoreType.DMA,
        ],
    )
    return pl.pallas_call(
        _combine_kernel,
        grid_spec=grid_spec,
        out_shape=jax.ShapeDtypeStruct((tokens, D_MODEL), f32),
        compiler_params=_params("arbitrary"),
        name="moe_combine_ln2_ple_ln3",
    )(*runs, y_rows, h, pos_tok, w_tok, p2d, wsg, wsu, wsd, wpp, wpg, g2, b2, g3, b3)


def _row_layout(table, totals, n_blocks):
    cnt_u = table[:, :, 0] // UNIT
    loc_u = table[:, :, 1] // UNIT
    padded = (totals + ROW_BLOCK - 1) // ROW_BLOCK * ROW_BLOCK
    pends = jnp.cumsum(padded)
    pstarts = pends - padded
    glob_u = (pstarts[None, :] + table[:, :, 2]) // UNIT
    cls = jnp.arange(RUN_CLASSES, dtype=i32)
    has = (cnt_u[:, None, :] >> cls[None, :, None]) & 1
    off = (cnt_u[:, None, :] >> (cls[None, :, None] + 1)) << (cls[None, :, None] + 1)
    upto = jnp.cumsum(has, axis=2)
    n_pieces = upto[:, :, -1].astype(i32).reshape(-1)
    slot = jnp.arange(N_EXPERTS, dtype=i32)
    e_of = jnp.sum((upto[:, :, None, :] <= slot[None, None, :, None]).astype(i32), axis=-1)
    is_e = e_of[:, :, :, None] == jnp.arange(N_EXPERTS, dtype=i32)
    src = jnp.sum(jnp.where(is_e, (loc_u[:, None, :] + off)[:, :, None, :], 0), axis=-1).astype(i32).reshape(-1)
    dst = jnp.sum(jnp.where(is_e, (glob_u[:, None, :] + off)[:, :, None, :], 0), axis=-1).astype(i32).reshape(-1)
    runs = (n_pieces, src, dst)
    tail_block = jnp.where(totals > 0, pends // ROW_BLOCK - 1, -1).astype(i32)
    block_start = jnp.arange(n_blocks, dtype=i32) * ROW_BLOCK
    ends_before = jnp.sum((pends[None, :] <= block_start[:, None]).astype(i32), axis=1)
    block_eid = jnp.minimum(ends_before, N_EXPERTS - 1).astype(i32)
    n_used = (pends[-1:] // ROW_BLOCK).astype(i32)
    return runs, tail_block, block_eid, n_used


def _layer(x2d, p2d, batch, seq, w_in, w_moba_up, w_ret_up, w_out, ln1_g, ln1_b, w_router, router_bias,
           w_eg, w_eu, w_ed, w_sg, w_su, w_sd, ln2_g, ln2_b, w_ple_proj, w_ple_gate, ln3_g, ln3_b):
    tokens = batch * seq
    row = lambda a: a.reshape(1, -1).astype(f32)

    proj = _project(x2d, w_in.astype(bf16), seq)
    oa = _moba(proj, batch, seq)
    orr = _retention(proj, batch, seq)

    wrt = w_router.astype(f32).T
    wrt_hi = wrt.astype(bf16)
    wrt_lo = (wrt - wrt_hi.astype(f32)).astype(bf16)
    h1, pos8, w8, table, totals = _merge_router(
        oa, orr, proj, x2d, w_moba_up.astype(bf16), w_ret_up.astype(bf16), w_out.astype(bf16),
        row(ln1_g), row(ln1_b), wrt_hi, wrt_lo, router_bias.astype(f32).reshape(N_EXPERTS, 1))

    n_tiles = tokens // TOK_TM
    max_rows = tokens * TOP_K + n_tiles * N_EXPERTS * (UNIT - 1) + N_EXPERTS * (ROW_BLOCK - 1)
    n_blocks = -(-max_rows // ROW_BLOCK)
    runs, tail_block, block_eid, n_used = _row_layout(table[:, :, :3], totals[:, 0], n_blocks)

    rows = _dispatch(runs, tail_block, n_used, h1, pos8, n_blocks * ROW_BLOCK)
    y_rows = _experts(block_eid, n_used, rows, w_eg, w_eu, w_ed)
    return _combine(runs, y_rows, h1, pos8.T, w8.T, p2d,
                    w_sg.astype(bf16), w_su.astype(bf16), w_sd.astype(bf16),
                    w_ple_proj.astype(bf16), w_ple_gate.astype(bf16),
                    row(ln2_g), row(ln2_b), row(ln3_g), row(ln3_b))


def kernel(x, p, w_in, w_moba_up, w_ret_up, w_out, ln1_g, ln1_b, w_router, router_bias, w_exp_gate, w_exp_up,
           w_exp_down, w_sh_gate, w_sh_up, w_sh_down, ln2_g, ln2_b, w_ple_proj, w_ple_gate, ln3_g, ln3_b):
    batch, seq, d = x.shape
    assert d == D_MODEL and seq % max(MOBA_BLOCK, RET_CHUNK) == 0 and (batch * seq) % (MERGE_TILES * TOK_TM) == 0
    assert w_in.shape[0] == DEPTH
    h = x.reshape(batch * seq, d)
    for i in range(DEPTH):
        h = _layer(h, p[i].reshape(batch * seq, PLE_DIM), batch, seq,
                   w_in[i], w_moba_up[i], w_ret_up[i], w_out[i], ln1_g[i], ln1_b[i], w_router[i], router_bias[i],
                   w_exp_gate[i], w_exp_up[i], w_exp_down[i], w_sh_gate[i], w_sh_up[i], w_sh_down[i],
                   ln2_g[i], ln2_b[i], w_ple_proj[i], w_ple_gate[i], ln3_g[i], ln3_b[i])
    return h.reshape(batch, seq, d)
```

```python
import functools

import jax
import jax.numpy as jnp
from jax import lax
from jax.experimental import pallas as pl
from jax.experimental.pallas import tpu as pltpu

f32 = jnp.float32
bf16 = jnp.bfloat16
i32 = jnp.int32

D_MODEL = 1024
DEPTH = 1
MOBA_HEADS = 8
MOBA_HEAD_DIM = 64
MOBA_WIDTH = MOBA_HEADS * MOBA_HEAD_DIM
MOBA_BLOCK = 256
MOBA_TOPK = 3
ROPE_THETA = 10000.0
RET_HEADS = 4
RET_QK_DIM = 128
RET_V_DIM = 256
RET_QK_WIDTH = RET_HEADS * RET_QK_DIM
RET_V_WIDTH = RET_HEADS * RET_V_DIM
RET_CHUNK = 256
RET_ANGLE_BASE = 10000.0
N_IN = 3 * MOBA_WIDTH + 2 * RET_QK_WIDTH + 2 * RET_V_WIDTH + 2 * D_MODEL
N_EXPERTS = 64
TOP_K = 6
N_GROUPS = 8
TOPK_GROUPS = 4
GROUP_SIZE = N_EXPERTS // N_GROUPS
D_EXPERT = 256
D_SHARED = 256
ROUTED_SCALE = 2.5
PLE_DIM = 256
LN_EPS = 1e-5
GN_EPS = 1e-6
ALPHA = (2.0 * DEPTH) ** 0.25

OFF_QA = 0
OFF_KA = MOBA_WIDTH
OFF_VA = 2 * MOBA_WIDTH
OFF_QR = 3 * MOBA_WIDTH
OFF_KR = OFF_QR + RET_QK_WIDTH
OFF_VR = OFF_KR + RET_QK_WIDTH
OFF_GR = OFF_VR + RET_V_WIDTH
OFF_GA = OFF_GR + RET_V_WIDTH
OFF_GB = OFF_GA + D_MODEL

MOBA_Q_SCALE = MOBA_HEAD_DIM ** -0.5 * 1.4426950408889634

LANES = 128
NEG = -1e30
VMEM_LIMIT = 56 * 1024 * 1024

PROJ_TM = 2048
PROJ_PARTS = 8
PROJ_TN = 512
MOBA_STEP_WIDTH = 512
MOE_TILES = 2
MERGE_TILES = 4
TOK_TM = 256
ROW_BLOCK = 512
EXPERT_CHUNK = 256
UNIT = 8
LOC_ROWS = -(-(TOP_K * TOK_TM + N_EXPERTS * (UNIT - 1)) // LANES) * LANES
RUN_CLASSES = (TOK_TM // UNIT).bit_length()
HALF = D_MODEL // 2


def _dot(a, b):
    return jnp.dot(a, b, preferred_element_type=f32)


def _dot_nt(a, b):
    return lax.dot_general(a, b, (((1,), (1,)), ((), ())), preferred_element_type=f32)


def _dot_tn(a, b):
    return lax.dot_general(a, b, (((0,), (0,)), ((), ())), preferred_element_type=f32)


def _split_bf16(a):
    hi = a.astype(bf16)
    lo = (a - hi.astype(f32)).astype(bf16)
    return hi, lo


def _layer_norm(x, g, b):
    mu = jnp.mean(x, axis=-1, keepdims=True)
    xc = x - mu
    var = jnp.mean(xc * xc, axis=-1, keepdims=True)
    return xc * lax.rsqrt(var + LN_EPS) * g + b


def _sigmoid(x):
    return 1.0 / (1.0 + jnp.exp(-x))


def _interleave(chains):
    chains = list(chains)
    while chains:
        chains = [chain for chain in chains if next(chain, "done") != "done"]


def _params(*sem, flags=None):
    return pltpu.CompilerParams(dimension_semantics=sem, vmem_limit_bytes=VMEM_LIMIT, flags=flags)


def _rotate_half_chunk(xc, half):
    if 2 * half == LANES:
        return pltpu.roll(xc, half, axis=1)
    lane = lax.broadcasted_iota(i32, xc.shape, 1)
    first = (lane & (2 * half - 1)) < half
    return jnp.where(first, pltpu.roll(xc, LANES - half, axis=1), pltpu.roll(xc, half, axis=1))


def _proj_kernel(x_ref, w_ref, cos_a_ref, sin_a_ref, cos_r_ref, sin_r_ref, o_ref, xb_ref):
    j = pl.program_id(1)

    @pl.when(j == 0)
    def _():
        xb_ref[...] = x_ref[...].astype(bf16)

    part = x_ref.shape[0] // PROJ_PARTS

    def run(epilogue):
        def chain(r):
            rows = slice(r * part, (r + 1) * part)
            acc = _dot(xb_ref[rows, :], w_ref[...])
            yield
            epilogue(acc, rows)

        _interleave(chain(r) for r in range(PROJ_PARTS))

    def rotary(cos_ref, sin_ref, half, scale):
        def epilogue(acc, rows):
            cos = cos_ref[rows, :]
            sin = sin_ref[rows, :]
            for c in range(PROJ_TN // LANES):
                xc = acc[:, c * LANES:(c + 1) * LANES]
                y = xc * cos + _rotate_half_chunk(xc, half) * sin
                if scale != 1.0:
                    y = y * scale
                o_ref[rows, c * LANES:(c + 1) * LANES] = y.astype(o_ref.dtype)

        run(epilogue)

    def plain_epilogue(acc, rows):
        o_ref[rows, :] = acc.astype(o_ref.dtype)

    j_qa = OFF_QA // PROJ_TN
    j_ka = OFF_KA // PROJ_TN
    j_qr = OFF_QR // PROJ_TN
    j_kr = OFF_KR // PROJ_TN

    @pl.when(j == j_qa)
    def _():
        rotary(cos_a_ref, sin_a_ref, MOBA_HEAD_DIM // 2, MOBA_Q_SCALE)

    @pl.when(j == j_ka)
    def _():
        rotary(cos_a_ref, sin_a_ref, MOBA_HEAD_DIM // 2, 1.0)

    @pl.when(j == j_qr)
    def _():
        rotary(cos_r_ref, sin_r_ref, RET_QK_DIM // 2, 1.0)

    @pl.when(j == j_kr)
    def _():
        rotary(cos_r_ref, sin_r_ref, RET_QK_DIM // 2, RET_QK_DIM ** -0.5)

    plain = (j != j_qa) & (j != j_ka) & (j != j_qr) & (j != j_kr)

    @pl.when(plain)
    def _():
        run(plain_epilogue)


def _rotary_tables(seq, inv_freq, head_dim):
    ang = jnp.arange(seq, dtype=f32)[:, None] * inv_freq[None, :]
    cos = jnp.cos(ang)
    sin = jnp.sin(ang)
    cos_h = jnp.concatenate([cos, cos], axis=-1)
    sin_h = jnp.concatenate([-sin, sin], axis=-1)
    reps = LANES // head_dim
    return jnp.tile(cos_h, (1, reps)), jnp.tile(sin_h, (1, reps))


def _project(x2d, w_in_b, seq):
    tokens = x2d.shape[0]
    tm = min(PROJ_TM, seq)
    inv_a = 1.0 / (ROPE_THETA ** (jnp.arange(0, MOBA_HEAD_DIM, 2, dtype=f32) / MOBA_HEAD_DIM))
    inv_r = 1.0 / (RET_ANGLE_BASE ** jnp.linspace(0.0, 1.0, RET_QK_DIM // 2, dtype=f32))
    cos_a, sin_a = _rotary_tables(seq, inv_a, MOBA_HEAD_DIM)
    cos_r, sin_r = _rotary_tables(seq, inv_r, RET_QK_DIM)
    seq_tiles = seq // tm
    tab = pl.BlockSpec((tm, LANES), lambda i, j: (i % seq_tiles, 0))
    return pl.pallas_call(
        _proj_kernel,
        grid=(tokens // tm, N_IN // PROJ_TN),
        in_specs=[
            pl.BlockSpec((tm, D_MODEL), lambda i, j: (i, 0)),
            pl.BlockSpec((D_MODEL, PROJ_TN), lambda i, j: (0, j)),
            tab, tab, tab, tab,
        ],
        out_specs=pl.BlockSpec((tm, PROJ_TN), lambda i, j: (i, j)),
        out_shape=jax.ShapeDtypeStruct((tokens, N_IN), bf16),
        scratch_shapes=[pltpu.VMEM((tm, D_MODEL), bf16)],
        compiler_params=_params("parallel", "arbitrary"),
        name="in_proj_rotary",
    )(x2d, w_in_b, cos_a, sin_a, cos_r, sin_r)


def _moba_kernel(q_ref, k_ref, v_ref, o_ref, kmean_ref, vt_ref, selb_ref, gate_ref, *, n_blocks):
    i = pl.program_id(2)
    blk = MOBA_BLOCK
    hd = MOBA_HEAD_DIM
    width = q_ref.shape[1]
    chunks = width // LANES
    heads = width // hd
    per_chunk = LANES // hd

    def chunk_of(h):
        return slice(h // per_chunk * LANES, (h // per_chunk + 1) * LANES)

    @pl.when(i == 0)
    def _():
        seq = n_blocks * blk
        member = (lax.broadcasted_iota(i32, (n_blocks, seq), 1) // blk
                  == lax.broadcasted_iota(i32, (n_blocks, seq), 0))
        kmean_ref[...] = _dot(member.astype(bf16), k_ref[...]) * (1.0 / blk)
        for j in range(n_blocks):
            vt_ref[j] = v_ref[j * blk:(j + 1) * blk, :].astype(f32).T.astype(bf16)

    km_hi, km_lo = _split_bf16(kmean_ref[...])
    key_i = lax.broadcasted_iota(i32, (blk, blk), 0)
    qry_i = lax.broadcasted_iota(i32, (blk, blk), 1)
    causal = key_i <= qry_i
    bid = lax.broadcasted_iota(i32, (n_blocks, blk), 0)
    d_row = lax.broadcasted_iota(i32, (LANES, blk), 0)
    qt = q_ref[...].astype(f32).T

    qts = []
    for h in range(heads):
        r0 = h % per_chunk * hd
        in_head = (d_row >= r0) & (d_row < r0 + hd)
        qth = jnp.where(in_head, qt[chunk_of(h), :], 0.0).astype(bf16)
        qts.append(qth)
        gate = _dot(km_hi[:, chunk_of(h)], qth) + _dot(km_lo[:, chunk_of(h)], qth)
        g = jnp.where(bid < i, gate, -jnp.inf)
        gate_ref[...] = g
        beaten = jnp.zeros((n_blocks, blk), f32)
        for j in range(n_blocks):
            gj = jnp.broadcast_to(gate_ref[j:j + 1, :], (n_blocks, blk))
            beats = (gj > g) | ((gj == g) & (bid > j))
            beaten = beaten + jnp.where(beats, 1.0, 0.0)
        sel = (beaten < MOBA_TOPK) & (bid < i)
        selb_ref[h] = jnp.where(sel, 0.0, NEG)

    kj = k_ref[pl.ds(pl.multiple_of(i * blk, blk), blk), :]
    vtj = vt_ref[i]
    state = []
    for h in range(heads):
        s = jnp.where(causal, _dot(kj[:, chunk_of(h)], qts[h]), NEG)
        m0 = jnp.max(s, axis=0, keepdims=True)
        p = jnp.exp2(s - m0)
        l0 = jnp.sum(p, axis=0, keepdims=True)
        acc0 = _dot(vtj[h * hd:(h + 1) * hd, :], p.astype(bf16))
        state += [m0, l0, acc0]

    qt_alls = [jnp.concatenate(qts[c * per_chunk:(c + 1) * per_chunk], axis=1) for c in range(chunks)]

    def update(carry, js):
        nj = len(js)
        k_all = k_ref[pl.ds(pl.multiple_of(js[0] * blk, blk), nj * blk), :]
        s_alls = [_dot(k_all[:, c * LANES:(c + 1) * LANES], qt_alls[c]) for c in range(chunks)]
        vtjs = [vt_ref[j] for j in js]
        new = []
        for h in range(heads):
            m_old, l_old, acc = carry[3 * h:3 * h + 3]
            col = h % per_chunk * blk
            ss = [s_alls[h // per_chunk][n * blk:(n + 1) * blk, col:col + blk] + selb_ref[h, pl.ds(j, 1), :]
                  for n, j in enumerate(js)]
            m_new = m_old
            for s in ss:
                m_new = jnp.maximum(m_new, jnp.max(s, axis=0, keepdims=True))
            a = jnp.exp2(m_old - m_new)
            l_new = a * l_old
            acc = a * acc
            for s, vtj in zip(ss, vtjs):
                p = jnp.exp2(s - m_new)
                l_new = l_new + jnp.sum(p, axis=0, keepdims=True)
                acc = acc + _dot(vtj[h * hd:(h + 1) * hd, :], p.astype(bf16))
            new += [m_new, l_new, acc]
        return tuple(new)

    quads = i // 4
    fin = lax.fori_loop(0, quads, lambda n, c: update(c, tuple(4 * n + t for t in range(4))), tuple(state))
    done = 4 * quads
    has_pair = (i - done) // 2
    fin = lax.fori_loop(0, has_pair, lambda n, c: update(c, (done, done + 1)), fin)
    done = done + 2 * has_pair
    fin = lax.fori_loop(done, i, lambda j, c: update(c, (j,)), fin)
    out_t = jnp.concatenate([fin[3 * h + 2] / fin[3 * h + 1] for h in range(heads)], axis=0)
    o_ref[...] = out_t.T.astype(o_ref.dtype)


def _moba(proj, batch, seq):
    tokens = batch * seq
    n_blocks = seq // MOBA_BLOCK
    width = MOBA_STEP_WIDTH
    return pl.pallas_call(
        functools.partial(_moba_kernel, n_blocks=n_blocks),
        grid=(batch, MOBA_WIDTH // width, n_blocks),
        in_specs=[
            pl.BlockSpec((MOBA_BLOCK, width), lambda b, c, i: (b * n_blocks + i, OFF_QA // width + c)),
            pl.BlockSpec((seq, width), lambda b, c, i: (b, OFF_KA // width + c)),
            pl.BlockSpec((seq, width), lambda b, c, i: (b, OFF_VA // width + c)),
        ],
        out_specs=pl.BlockSpec((MOBA_BLOCK, width), lambda b, c, i: (b * n_blocks + i, c)),
        out_shape=jax.ShapeDtypeStruct((tokens, MOBA_WIDTH), bf16),
        scratch_shapes=[
            pltpu.VMEM((n_blocks, width), f32),
            pltpu.VMEM((n_blocks, width, MOBA_BLOCK), bf16),
            pltpu.VMEM((width // MOBA_HEAD_DIM, n_blocks, MOBA_BLOCK), f32),
            pltpu.VMEM((n_blocks, MOBA_BLOCK), f32),
        ],
        compiler_params=_params("parallel", "parallel", "arbitrary"),
        name="moba_attention",
    )(proj, proj, proj)


def _ret_kernel(q_ref, k_ref, v0_ref, v1_ref, g0_ref, g1_ref, decay_ref, zeta_ref, xi_ref, cd_ref,
                o_ref, state_ref):
    c = pl.program_id(1)
    heads_per_half = RET_HEADS // 2
    v_refs = (v0_ref, v1_ref)
    g_refs = (g0_ref, g1_ref)

    @pl.when(c == 0)
    def _():
        state_ref[...] = jnp.zeros_like(state_ref)

    for h in range(RET_HEADS):
        qk_cols = slice(h * RET_QK_DIM, (h + 1) * RET_QK_DIM)
        v_cols = slice(h * RET_V_DIM, (h + 1) * RET_V_DIM)
        half_cols = slice((h % heads_per_half) * RET_V_DIM, (h % heads_per_half + 1) * RET_V_DIM)
        q = q_ref[:, qk_cols]
        k = k_ref[:, qk_cols]
        v = v_refs[h // heads_per_half][:, half_cols]
        state = state_ref[h]
        scores = _dot_nt(q, k) * decay_ref[h]
        inner = _dot(scores.astype(bf16), v)
        cross = _dot(q, state.astype(bf16)) * xi_ref[h]
        o = inner + cross
        kz = (k.astype(f32) * zeta_ref[h]).astype(bf16)
        state_ref[h] = cd_ref[h] * state + _dot_tn(kz, v)

        mu = jnp.mean(o, axis=-1, keepdims=True)
        oc = o - mu
        var = jnp.mean(oc * oc, axis=-1, keepdims=True)
        on = oc * lax.rsqrt(var + GN_EPS)
        g = g_refs[h // heads_per_half][:, half_cols].astype(f32)
        o_ref[:, v_cols] = (on * (g * _sigmoid(g))).astype(o_ref.dtype)


def _retention(proj, batch, seq):
    tokens = batch * seq
    C = RET_CHUNK
    n_chunks = seq // C
    half_v = RET_V_WIDTH // 2
    gammas =1.0 - 2.0 ** (-5.0 - jnp.arange(RET_HEADS, dtype=f32))
    log_g = jnp.log(gammas)
    idx = jnp.arange(C, dtype=f32)
    diff = idx[:, None] - idx[None, :]
    decay = jnp.where(diff >= 0, jnp.exp(jnp.maximum(diff, 0.0)[None] * log_g[:, None, None]), 0.0)
    zeta = jnp.exp((C - 1 - idx)[None, :] * log_g[:, None])
    xi = jnp.exp((idx + 1.0)[None, :] * log_g[:, None])
    zeta_t = jnp.broadcast_to(zeta[:, :, None], (RET_HEADS, C, RET_QK_DIM))
    xi_t = jnp.broadcast_to(xi[:, :, None], (RET_HEADS, C, RET_V_DIM))
    cd_t = jnp.broadcast_to(jnp.exp(C * log_g)[:, None, None], (RET_HEADS, 1, RET_V_DIM))
    return pl.pallas_call(
        _ret_kernel,
        grid=(batch, n_chunks),
        in_specs=[
            pl.BlockSpec((C, RET_QK_WIDTH), lambda b, c: (b * n_chunks + c, OFF_QR // RET_QK_WIDTH)),
            pl.BlockSpec((C, RET_QK_WIDTH), lambda b, c: (b * n_chunks + c, OFF_KR // RET_QK_WIDTH)),
            pl.BlockSpec((C, half_v), lambda b, c: (b * n_chunks + c, OFF_VR // half_v)),
            pl.BlockSpec((C, half_v), lambda b, c: (b * n_chunks + c, OFF_VR // half_v + 1)),
            pl.BlockSpec((C, half_v), lambda b, c: (b * n_chunks + c, OFF_GR // half_v)),
            pl.BlockSpec((C, half_v), lambda b, c: (b * n_chunks + c, OFF_GR // half_v + 1)),
            pl.BlockSpec((RET_HEADS, C, C), lambda b, c: (0, 0, 0)),
            pl.BlockSpec((RET_HEADS, C, RET_QK_DIM), lambda b, c: (0, 0, 0)),
            pl.BlockSpec((RET_HEADS, C, RET_V_DIM), lambda b, c: (0, 0, 0)),
            pl.BlockSpec((RET_HEADS, 1, RET_V_DIM), lambda b, c: (0, 0, 0)),
        ],
        out_specs=pl.BlockSpec((C, RET_V_WIDTH), lambda b, c: (b * n_chunks + c, 0)),
        out_shape=jax.ShapeDtypeStruct((tokens, RET_V_WIDTH), bf16),
        scratch_shapes=[pltpu.VMEM((RET_HEADS, RET_QK_DIM, RET_V_DIM), f32)],
        compiler_params=_params("parallel", "arbitrary"),
        name="retention",
    )(proj, proj, proj, proj, proj, proj, decay, zeta_t, xi_t, cd_t)


def _merge_router_kernel(oa_ref, orr_ref, ga0_ref, ga1_ref, gb0_ref, gb1_ref, x_ref,
                         wm_ref, wr_ref, wo_ref, g1_ref, b1_ref, wrt_hi_ref, wrt_lo_ref, rb_ref,
                         h_ref, pos_ref, w_ref, tab_ref, cnt_ref, carry_ref):
    step = pl.program_id(0)
    tm = TOK_TM

    @pl.when(step == 0)
    def _():
        carry_ref[...] = jnp.zeros_like(carry_ref)

    _interleave(
        _merge_router_tile(t, slice(t * tm, (t + 1) * tm), tm, oa_ref, orr_ref, ga0_ref, ga1_ref, gb0_ref, gb1_ref,
                           x_ref, wm_ref, wr_ref, wo_ref, g1_ref, b1_ref, wrt_hi_ref, wrt_lo_ref, rb_ref,
                           h_ref, pos_ref, w_ref, tab_ref, cnt_ref, carry_ref)
        for t in range(x_ref.shape[0] // tm))


def _merge_router_tile(t, rows, tm, oa_ref, orr_ref, ga0_ref, ga1_ref, gb0_ref, gb1_ref, x_ref,
                       wm_ref, wr_ref, wo_ref, g1_ref, b1_ref, wrt_hi_ref, wrt_lo_ref, rb_ref,
                       h_ref, pos_ref, w_ref, tab_ref, cnt_ref, carry_ref):
    a = _dot(oa_ref[rows, :], wm_ref[...])
    r = _dot(orr_ref[rows, :], wr_ref[...])
    yield
    ga = jnp.concatenate([ga0_ref[rows, :], ga1_ref[rows, :]], axis=1).astype(f32)
    gb = jnp.concatenate([gb0_ref[rows, :], gb1_ref[rows, :]], axis=1).astype(f32)
    merged = _sigmoid(ga) * a + _sigmoid(gb) * r
    yield
    mix = _dot(merged.astype(bf16), wo_ref[...])
    yield
    h = _layer_norm(ALPHA * x_ref[rows, :] + mix, g1_ref[...], b1_ref[...])
    h_ref[rows, :] = h
    yield

    h_hi, h_lo = _split_bf16(h)
    w_hi = wrt_hi_ref[...]
    logits = _dot_nt(w_hi, h_hi) + _dot_nt(w_hi, h_lo) + _dot_nt(wrt_lo_ref[...], h_hi)
    yield
    scores = _sigmoid(logits)
    biased = scores + rb_ref[...]

    v = biased.reshape(N_GROUPS, GROUP_SIZE, tm)
    sub = lax.broadcasted_iota(i32, v.shape, 1)
    m1 = jnp.max(v, axis=1, keepdims=True)
    i1 = jnp.min(jnp.where(v == m1, sub, GROUP_SIZE), axis=1, keepdims=True)
    m2 = jnp.max(jnp.where(sub == i1, -jnp.inf, v), axis=1, keepdims=True)
    gscore = jnp.broadcast_to(m1 + m2, v.shape).reshape(N_EXPERTS, tm)

    eid = lax.broadcasted_iota(i32, (N_EXPERTS, tm), 0)
    egrp = eid // GROUP_SIZE
    e_mask = jnp.zeros((N_EXPERTS, tm), jnp.bool_)
    for _ in range(TOPK_GROUPS):
        m = jnp.max(gscore, axis=0, keepdims=True)
        idx = jnp.min(jnp.where(gscore == m, egrp, N_GROUPS), axis=0, keepdims=True)
        hit = egrp == idx
        e_mask = e_mask | hit
        gscore = jnp.where(hit, -jnp.inf, gscore)
    cand = jnp.where(e_mask, biased, -jnp.inf)

    chosen = jnp.zeros((N_EXPERTS, tm), jnp.bool_)
    e_rows = []
    w_rows = []
    for _ in range(TOP_K):
        m = jnp.max(cand, axis=0, keepdims=True)
        idx = jnp.min(jnp.where(cand == m, eid, N_EXPERTS), axis=0, keepdims=True)
        hit = eid == idx
        chosen = chosen | hit
        e_rows.append(idx)
        w_rows.append(jnp.sum(jnp.where(hit, scores, 0.0), axis=0, keepdims=True))
        cand = jnp.where(hit, -jnp.inf, cand)
    w_sum = w_rows[0]
    for wk in w_rows[1:]:
        w_sum = w_sum + wk

    t_src = lax.broadcasted_iota(i32, (tm, tm), 0)
    t_dst = lax.broadcasted_iota(i32, (tm, tm), 1)
    before = (t_src < t_dst).astype(bf16)
    chosen_f = chosen.astype(f32)
    prior = _dot(chosen_f.astype(bf16), before)
    cnt = jnp.sum(chosen_f, axis=1, keepdims=True)
    cnt_pad = jnp.ceil(cnt * (1.0 / UNIT)) * UNIT
    e_src = lax.broadcasted_iota(i32, (N_EXPERTS, N_EXPERTS), 1)
    e_dst = lax.broadcasted_iota(i32, (N_EXPERTS, N_EXPERTS), 0)
    earlier = (e_src < e_dst).astype(bf16)
    cnt_pad_l = jnp.broadcast_to(cnt_pad, (N_EXPERTS, LANES))
    loc_start_l = _dot(earlier, cnt_pad_l.astype(bf16))
    loc_start = loc_start_l[:, 0:1]
    where_e = prior + loc_start
    pos_ref[:, rows] = jnp.zeros((pos_ref.shape[0], tm), pos_ref.dtype)
    w_ref[:, rows] = jnp.zeros((w_ref.shape[0], tm), w_ref.dtype)
    for k in range(TOP_K):
        pos_ref[k:k + 1, rows] = jnp.sum(jnp.where(eid == e_rows[k], where_e, 0.0), axis=0, keepdims=True).astype(i32)
        w_ref[k:k + 1, rows] = w_rows[k] / w_sum * ROUTED_SCALE

    carry = carry_ref[...]
    tl = lax.broadcasted_iota(i32, (N_EXPERTS, LANES), 1)
    table = jnp.where(tl == 0, cnt_pad_l, jnp.where(tl == 1, loc_start_l, jnp.broadcast_to(carry, (N_EXPERTS, LANES))))
    tab_ref[t] = table.astype(i32)
    carry = carry + cnt_pad
    carry_ref[...] = carry
    cnt_ref[...] = jnp.broadcast_to(carry, cnt_ref.shape).astype(i32)


def _merge_router(oa, orr, proj, x2d, wm, wr, wo, g1, b1, wrt_hi, wrt_lo, rb):
    tokens = x2d.shape[0]
    tm = MERGE_TILES * TOK_TM
    half = D_MODEL // 2

    def gate_spec(off):
        return pl.BlockSpec((tm, half), lambda i: (i, off // half))

    def full(shape):
        return pl.BlockSpec(shape, lambda i: (0,) * len(shape))

    row8 = pl.BlockSpec((8, tm), lambda i: (0, i))
    return pl.pallas_call(
        _merge_router_kernel,
        grid=(tokens // tm,),
        in_specs=[
            pl.BlockSpec((tm, MOBA_WIDTH), lambda i: (i, 0)),
            pl.BlockSpec((tm, RET_V_WIDTH), lambda i: (i, 0)),
            gate_spec(OFF_GA), gate_spec(OFF_GA + half), gate_spec(OFF_GB), gate_spec(OFF_GB + half),
            pl.BlockSpec((tm, D_MODEL), lambda i: (i, 0)),
            full((MOBA_WIDTH, D_MODEL)), full((RET_V_WIDTH, D_MODEL)), full((D_MODEL, D_MODEL)),
            full((1, D_MODEL)), full((1, D_MODEL)),
            full((N_EXPERTS, D_MODEL)), full((N_EXPERTS, D_MODEL)), full((N_EXPERTS, 1)),
        ],
        out_specs=[
            pl.BlockSpec((tm, D_MODEL), lambda i: (i, 0)),
            row8, row8,
            pl.BlockSpec((MERGE_TILES, N_EXPERTS, LANES), lambda i: (i, 0, 0)),
            full((N_EXPERTS, LANES)),
        ],
        out_shape=[
            jax.ShapeDtypeStruct((tokens, D_MODEL), f32),
            jax.ShapeDtypeStruct((8, tokens), i32),
            jax.ShapeDtypeStruct((8, tokens), f32),
            jax.ShapeDtypeStruct((tokens // TOK_TM, N_EXPERTS, LANES), i32),
            jax.ShapeDtypeStruct((N_EXPERTS, LANES), i32),
        ],
        scratch_shapes=[pltpu.VMEM((N_EXPERTS, 1), f32)],
        compiler_params=_params("arbitrary"),
        name="merge_ln1_router",
    )(oa, orr, proj, proj, proj, proj, x2d, wm, wr, wo, g1, b1, wrt_hi, wrt_lo, rb)


def _pack_rows(x):
    bits = lax.bitcast_convert_type(x, jnp.uint32)
    return (bits[:, :HALF] & jnp.uint32(0xFFFF0000)) | (bits[:, HALF:] >> 16)


def _unpack_rows(words):
    left = lax.bitcast_convert_type(words & jnp.uint32(0xFFFF0000), f32)
    right = lax.bitcast_convert_type(words << 16, f32)
    return left.astype(bf16), right.astype(bf16)


def _start_alternating(n, make_copy):
    def pair(m, _):
        make_copy(2 * m).start(priority=0)
        make_copy(2 * m + 1).start(priority=1)
        return 0

    lax.fori_loop(0, n // 2, pair, 0)

    @pl.when(n % 2 == 1)
    def _():
        make_copy(n - 1).start(priority=0)


def _move_runs(tile, n_ref, src_ref, dst_ref, make_copy):
    def piece(c, p):
        rows = UNIT << c
        base = (tile * RUN_CLASSES + c) * N_EXPERTS
        loc = pl.ds(pl.multiple_of(src_ref[base + p] * UNIT, UNIT), rows)
        glob = pl.ds(pl.multiple_of(dst_ref[base + p] * UNIT, UNIT), rows)
        return make_copy(loc, glob)

    for c in range(RUN_CLASSES):
        _start_alternating(n_ref[tile * RUN_CLASSES + c], functools.partial(piece, c))

    for c in range(RUN_CLASSES):
        def wait_one(p, _, c=c):
            piece(c, p).wait()
            return 0

        lax.fori_loop(0, n_ref[tile * RUN_CLASSES + c], wait_one, 0)


def _dispatch_kernel(n_ref, src_ref, dst_ref, tail_ref, used_ref, h_ref, pos_ref, rows_ref,
                     xs_ref, zero_ref, sem_ref, zsem_ref):
    step = pl.program_id(0)
    tm = TOK_TM
    tiles = h_ref.shape[0] // tm

    @pl.when(step == 0)
    def _():
        zero_ref[...] = jnp.zeros_like(zero_ref)

        def clear_copy(block):
            start = pl.multiple_of(block * ROW_BLOCK, ROW_BLOCK)
            return pltpu.make_async_copy(zero_ref, rows_ref.at[pl.ds(start, ROW_BLOCK), :], zsem_ref)

        def clear_all(act):
            def tail(e, _):
                tail_block = tail_ref[e]

                @pl.when(tail_block >= 0)
                def _():
                    act(clear_copy(tail_block))
                return 0

            lax.fori_loop(0, N_EXPERTS, tail, 0)

            def unused(b, _):
                act(clear_copy(b))
                return 0

            lax.fori_loop(used_ref[0], rows_ref.shape[0] // ROW_BLOCK, unused, 0)

        clear_all(lambda cp: cp.start())
        clear_all(lambda cp: cp.wait())

    def sort_tile(u):
        cols = slice(u * tm, (u + 1) * tm)
        pos = pos_ref[:, cols]
        r_iota = lax.broadcasted_iota(i32, (LOC_ROWS, tm), 0)
        onehot = r_iota == pos[0:1, :]
        for k in range(1, TOP_K):
            onehot = onehot | (r_iota == pos[k:k + 1, :])
        onehot = onehot.astype(bf16)
        yield
        xs = _dot(onehot, h_ref[cols, :].astype(bf16))
        yield
        xs_ref[u] = _pack_rows(xs)

    _interleave(sort_tile(u) for u in range(tiles))

    for u in range(tiles):
        _move_runs(step * tiles + u, n_ref, src_ref, dst_ref,
                   lambda loc, glob, u=u: pltpu.make_async_copy(xs_ref.at[u, loc, :], rows_ref.at[glob, :], sem_ref))


def _dispatch(runs, tail_block, n_used, h, pos8, n_rows):
    tokens = h.shape[0]
    tm = MOE_TILES * TOK_TM
    grid_spec = pltpu.PrefetchScalarGridSpec(
        num_scalar_prefetch=5,
        grid=(tokens // tm,),
        in_specs=[
            pl.BlockSpec((tm, D_MODEL), lambda i, *_: (i, 0)),
            pl.BlockSpec((8, tm), lambda i, *_: (0, i)),
        ],
        out_specs=pl.BlockSpec(memory_space=pl.ANY),
        scratch_shapes=[
            pltpu.VMEM((MOE_TILES, LOC_ROWS, HALF), jnp.uint32),
            pltpu.VMEM((ROW_BLOCK, HALF), jnp.uint32),
            pltpu.SemaphoreType.DMA,
            pltpu.SemaphoreType.DMA,
        ],
    )
    return pl.pallas_call(
        _dispatch_kernel,
        grid_spec=grid_spec,
        out_shape=jax.ShapeDtypeStruct((n_rows, HALF), jnp.uint32),
        compiler_params=_params("arbitrary"),
        name="moe_dispatch",
    )(*runs, tail_block, n_used, h, pos8)


def _expert_kernel(eid_ref, used_ref, x_ref, wg_ref, wu_ref, wd_ref, y_ref, wgb_ref, wub_ref, wdb_ref):
    b = pl.program_id(0)
    changed = (b == 0) | (eid_ref[b] != eid_ref[jnp.maximum(b - 1, 0)])

    @pl.when(changed)
    def _():
        wgb_ref[...] = wg_ref[0].astype(bf16)
        wub_ref[...] = wu_ref[0].astype(bf16)
        wdb_ref[...] = wd_ref[0].astype(bf16)

    @pl.when(b < used_ref[0])
    def _():
        def chunk_phases(c):
            rows = slice(c * EXPERT_CHUNK, (c + 1) * EXPERT_CHUNK)
            xl, xr = _unpack_rows(x_ref[rows, :])
            yield
            g = _dot(xl, wgb_ref[:HALF, :]) + _dot(xr, wgb_ref[HALF:, :])
            u = _dot(xl, wub_ref[:HALF, :]) + _dot(xr, wub_ref[HALF:, :])
            yield
            mid = ((g * _sigmoid(g)) * u).astype(bf16)
            yield
            y = _dot(mid, wdb_ref[...])
            yield
            y_ref[rows, :] = _pack_rows(y.astype(bf16).astype(f32))

        _interleave(chunk_phases(c) for c in range(ROW_BLOCK // EXPERT_CHUNK))

    @pl.when(b >= used_ref[0])
    def _():
        y_ref[...] = jnp.zeros_like(y_ref)


def _experts(block_eid, n_used, rows, wg, wu, wd):
    n_rows = rows.shape[0]
    n_blocks = n_rows // ROW_BLOCK

    def row_map(b, eid, used):
        return (b, 0)

    def w_map(b, eid, used):
        return (eid[b], 0, 0)

    grid_spec = pltpu.PrefetchScalarGridSpec(
        num_scalar_prefetch=2,
        grid=(n_blocks,),
        in_specs=[
            pl.BlockSpec((ROW_BLOCK, HALF), row_map),
            pl.BlockSpec((1, D_MODEL, D_EXPERT), w_map),
            pl.BlockSpec((1, D_MODEL, D_EXPERT), w_map),
            pl.BlockSpec((1, D_EXPERT, D_MODEL), w_map),
        ],
        out_specs=pl.BlockSpec((ROW_BLOCK, HALF), row_map),
        scratch_shapes=[
            pltpu.VMEM((D_MODEL, D_EXPERT), bf16),
            pltpu.VMEM((D_MODEL, D_EXPERT), bf16),
            pltpu.VMEM((D_EXPERT, D_MODEL), bf16),
        ],
    )
    return pl.pallas_call(
        _expert_kernel,
        grid_spec=grid_spec,
        out_shape=jax.ShapeDtypeStruct((n_rows, HALF), jnp.uint32),
        compiler_params=_params("arbitrary"),
        name="moe_experts",
    )(block_eid, n_used, rows, wg, wu, wd)


def _combine_kernel(n_ref, src_ref, dst_ref, y_ref, h_ref, pos_ref, wt_ref, p_ref, wsg_ref, wsu_ref, wsd_ref,
                    wpp_ref, wpg_ref, g2_ref, b2_ref, g3_ref, b3_ref, o_ref, ybuf_ref, sem_ref):
    step = pl.program_id(0)
    tm = TOK_TM
    tiles = h_ref.shape[0] // tm

    @pl.when(step == 0)
    def _():
        ybuf_ref[...] = jnp.zeros_like(ybuf_ref)

    for u in range(tiles):
        _move_runs(step * tiles + u, n_ref, src_ref, dst_ref,
                   lambda loc, glob, u=u: pltpu.make_async_copy(y_ref.at[glob, :], ybuf_ref.at[u, loc, :], sem_ref))

    def tile_phases(u):
        rows = slice(u * tm, (u + 1) * tm)
        h = h_ref[rows, :]
        hb = h.astype(bf16)
        sg = _dot(hb, wsg_ref[...])
        su = _dot(hb, wsu_ref[...])
        yield
        mid = ((sg * _sigmoid(sg)) * su).astype(bf16)
        yield
        shared = _dot(mid, wsd_ref[...])
        ple_in = _dot(p_ref[rows, :].astype(bf16), wpp_ref[...])
        yield
        pos = pos_ref[rows, :]
        wt = wt_ref[rows, :]
        c_iota = lax.broadcasted_iota(i32, (tm, LOC_ROWS), 1)
        spread = jnp.where(c_iota == pos[:, 0:1], wt[:, 0:1], 0.0)
        for k in range(1, TOP_K):
            spread = jnp.where(c_iota == pos[:, k:k + 1], wt[:, k:k + 1], spread)
        spread = spread.astype(bf16)
        yield
        yl, yr = _unpack_rows(ybuf_ref[u])
        yield
        routed = jnp.concatenate([_dot(spread, yl), _dot(spread, yr)], axis=1)
        yield
        h2 = _layer_norm(ALPHA * h + (routed + shared), g2_ref[...], b2_ref[...])
        yield
        gate = _dot(h2.astype(bf16), wpg_ref[...])
        yield
        ple = ple_in * _sigmoid(gate)
        o_ref[rows, :] = _layer_norm(ALPHA * h2 + ple, g3_ref[...], b3_ref[...])

    _interleave(tile_phases(u) for u in range(tiles))


def _combine(runs, y_rows, h, pos_tok, w_tok, p2d, wsg, wsu, wsd, wpp, wpg, g2, b2, g3, b3):
    tokens = h.shape[0]
    tm = MOE_TILES * TOK_TM

    def full(shape):
        return pl.BlockSpec(shape, lambda i, *_: (0,) * len(shape))

    def tile(width):
        return pl.BlockSpec((tm, width), lambda i, *_: (i, 0))

    grid_spec = pltpu.PrefetchScalarGridSpec(
        num_scalar_prefetch=3,
        grid=(tokens // tm,),
        in_specs=[
            pl.BlockSpec(memory_space=pl.ANY),
            tile(D_MODEL), tile(8), tile(8), tile(PLE_DIM),
            full((D_MODEL, D_SHARED)), full((D_MODEL, D_SHARED)), full((D_SHARED, D_MODEL)),
            full((PLE_DIM, D_MODEL)), full((D_MODEL, D_MODEL)),
            full((1, D_MODEL)), full((1, D_MODEL)), full((1, D_MODEL)), full((1, D_MODEL)),
        ],
        out_specs=tile(D_MODEL),
        scratch_shapes=[
            pltpu.VMEM((MOE_TILES, LOC_ROWS, HALF), jnp.uint32),
            pltpu.SemaphoreType.DMA,
        ],
    )
    return pl.pallas_call(
        _combine_kernel,
        grid_spec=grid_spec,
        out_shape=jax.ShapeDtypeStruct((tokens, D_MODEL), f32),
        compiler_params=_params("arbitrary"),
        name="moe_combine_ln2_ple_ln3",
    )(*runs, y_rows, h, pos_tok, w_tok, p2d, wsg, wsu, wsd, wpp, wpg, g2, b2, g3, b3)


def _row_layout(table, totals, n_blocks):
    cnt_u = table[:, :, 0] // UNIT
    loc_u = table[:, :, 1] // UNIT
    padded = (totals + ROW_BLOCK - 1) // ROW_BLOCK * ROW_BLOCK
    pends = jnp.cumsum(padded)
    pstarts = pends - padded
    glob_u = (pstarts[None, :] + table[:, :, 2]) // UNIT
    cls = jnp.arange(RUN_CLASSES, dtype=i32)
    has = (cnt_u[:, None, :] >> cls[None, :, None]) & 1
    off = (cnt_u[:, None, :] >> (cls[None, :, None] + 1)) << (cls[None, :, None] + 1)
    upto = jnp.cumsum(has, axis=2)
    n_pieces = upto[:, :, -1].astype(i32).reshape(-1)
    slot = jnp.arange(N_EXPERTS, dtype=i32)
    e_of = jnp.sum((upto[:, :, None, :] <= slot[None, None, :, None]).astype(i32), axis=-1)
    is_e = e_of[:, :, :, None] == jnp.arange(N_EXPERTS, dtype=i32)
    src = jnp.sum(jnp.where(is_e, (loc_u[:, None, :] + off)[:, :, None, :], 0), axis=-1).astype(i32).reshape(-1)
    dst = jnp.sum(jnp.where(is_e, (glob_u[:, None, :] + off)[:, :, None, :], 0), axis=-1).astype(i32).reshape(-1)
    runs = (n_pieces, src, dst)
    tail_block = jnp.where(totals > 0, pends // ROW_BLOCK - 1, -1).astype(i32)
    block_start = jnp.arange(n_blocks, dtype=i32) * ROW_BLOCK
    ends_before = jnp.sum((pends[None, :] <= block_start[:, None]).astype(i32), axis=1)
    block_eid = jnp.minimum(ends_before, N_EXPERTS - 1).astype(i32)
    n_used = (pends[-1:] // ROW_BLOCK).astype(i32)
    return runs, tail_block, block_eid, n_used


def _layer(x2d, p2d, batch, seq, w_in, w_moba_up, w_ret_up, w_out, ln1_g, ln1_b, w_router, router_bias,
           w_eg, w_eu, w_ed, w_sg, w_su, w_sd, ln2_g, ln2_b, w_ple_proj, w_ple_gate, ln3_g, ln3_b):
    tokens = batch * seq
    row = lambda a: a.reshape(1, -1).astype(f32)

    proj = _project(x2d, w_in.astype(bf16), seq)
    oa = _moba(proj, batch, seq)
    orr = _retention(proj, batch, seq)

    wrt = w_router.astype(f32).T
    wrt_hi = wrt.astype(bf16)
    wrt_lo = (wrt - wrt_hi.astype(f32)).astype(bf16)
    h1, pos8, w8, table, totals = _merge_router(
        oa, orr, proj, x2d, w_moba_up.astype(bf16), w_ret_up.astype(bf16), w_out.astype(bf16),
        row(ln1_g), row(ln1_b), wrt_hi, wrt_lo, router_bias.astype(f32).reshape(N_EXPERTS, 1))

    n_tiles = tokens // TOK_TM
    max_rows = tokens * TOP_K + n_tiles * N_EXPERTS * (UNIT - 1) + N_EXPERTS * (ROW_BLOCK - 1)
    n_blocks = -(-max_rows // ROW_BLOCK)
    runs, tail_block, block_eid, n_used = _row_layout(table[:, :, :3], totals[:, 0], n_blocks)

    rows = _dispatch(runs, tail_block, n_used, h1, pos8, n_blocks * ROW_BLOCK)
    y_rows = _experts(block_eid, n_used, rows, w_eg, w_eu, w_ed)
    return _combine(runs, y_rows, h1, pos8.T, w8.T, p2d,
                    w_sg.astype(bf16), w_su.astype(bf16), w_sd.astype(bf16),
                    w_ple_proj.astype(bf16), w_ple_gate.astype(bf16),
                    row(ln2_g), row(ln2_b), row(ln3_g), row(ln3_b))


def kernel(x, p, w_in, w_moba_up, w_ret_up, w_out, ln1_g, ln1_b, w_router, router_bias, w_exp_gate, w_exp_up,
           w_exp_down, w_sh_gate, w_sh_up, w_sh_down, ln2_g, ln2_b, w_ple_proj, w_ple_gate, ln3_g, ln3_b):
    batch, seq, d = x.shape
    assert d == D_MODEL and seq % max(MOBA_BLOCK, RET_CHUNK) == 0 and (batch * seq) % (MERGE_TILES * TOK_TM) == 0
    assert w_in.shape[0] == DEPTH
    h = x.reshape(batch * seq, d)
    for i in range(DEPTH):
        h = _layer(h, p[i].reshape(batch * seq, PLE_DIM), batch, seq,
                   w_in[i], w_moba_up[i], w_ret_up[i], w_out[i], ln1_g[i], ln1_b[i], w_router[i], router_bias[i],
                   w_exp_gate[i], w_exp_up[i], w_exp_down[i], w_sh_gate[i], w_sh_up[i], w_sh_down[i],
                   ln2_g[i], ln2_b[i], w_ple_proj[i], w_ple_gate[i], ln3_g[i], ln3_b[i])
    return h.reshape(batch, seq, d)
```

```python
import functools

import jax
import jax.numpy as jnp
from jax import lax
from jax.experimental import pallas as pl
from jax.experimental.pallas import tpu as pltpu

f32 = jnp.float32
bf16 = jnp.bfloat16
i32 = jnp.int32

D_MODEL = 1024
DEPTH = 1
MOBA_HEADS = 8
MOBA_HEAD_DIM = 64
MOBA_WIDTH = MOBA_HEADS * MOBA_HEAD_DIM
MOBA_BLOCK = 256
MOBA_TOPK = 3
ROPE_THETA = 10000.0
RET_HEADS = 4
RET_QK_DIM = 128
RET_V_DIM = 256
RET_QK_WIDTH = RET_HEADS * RET_QK_DIM
RET_V_WIDTH = RET_HEADS * RET_V_DIM
RET_CHUNK = 256
RET_ANGLE_BASE = 10000.0
N_IN = 3 * MOBA_WIDTH + 2 * RET_QK_WIDTH + 2 * RET_V_WIDTH + 2 * D_MODEL
N_EXPERTS = 64
TOP_K = 6
N_GROUPS = 8
TOPK_GROUPS = 4
GROUP_SIZE = N_EXPERTS // N_GROUPS
D_EXPERT = 256
D_SHARED = 256
ROUTED_SCALE = 2.5
PLE_DIM = 256
LN_EPS = 1e-5
GN_EPS = 1e-6
ALPHA = (2.0 * DEPTH) ** 0.25

OFF_QA = 0
OFF_KA = MOBA_WIDTH
OFF_VA = 2 * MOBA_WIDTH
OFF_QR = 3 * MOBA_WIDTH
OFF_KR = OFF_QR + RET_QK_WIDTH
OFF_VR = OFF_KR + RET_QK_WIDTH
OFF_GR = OFF_VR + RET_V_WIDTH
OFF_GA = OFF_GR + RET_V_WIDTH
OFF_GB = OFF_GA + D_MODEL

MOBA_Q_SCALE = MOBA_HEAD_DIM ** -0.5 * 1.4426950408889634

LANES = 128
NEG = -1e30
VMEM_LIMIT = 56 * 1024 * 1024

PROJ_TM = 2048
PROJ_PARTS = 8
PROJ_TN = 512
MOBA_STEP_WIDTH = 512
MOE_TILES = 2
MERGE_TILES = 4
TOK_TM = 256
ROW_BLOCK = 512
EXPERT_CHUNK = 256
UNIT = 8
LOC_ROWS = -(-(TOP_K * TOK_TM + N_EXPERTS * (UNIT - 1)) // LANES) * LANES
RUN_CLASSES = (TOK_TM // UNIT).bit_length()
HALF = D_MODEL // 2


def _dot(a, b):
    return jnp.dot(a, b, preferred_element_type=f32)


def _dot_nt(a, b):
    return lax.dot_general(a, b, (((1,), (1,)), ((), ())), preferred_element_type=f32)


def _dot_tn(a, b):
    return lax.dot_general(a, b, (((0,), (0,)), ((), ())), preferred_element_type=f32)


def _split_bf16(a):
    hi = a.astype(bf16)
    lo = (a - hi.astype(f32)).astype(bf16)
    return hi, lo


def _layer_norm(x, g, b):
    mu = jnp.mean(x, axis=-1, keepdims=True)
    xc = x - mu
    var = jnp.mean(xc * xc, axis=-1, keepdims=True)
    return xc * lax.rsqrt(var + LN_EPS) * g + b


def _sigmoid(x):
    return 1.0 / (1.0 + jnp.exp(-x))


def _interleave(chains):
    chains = list(chains)
    while chains:
        chains = [chain for chain in chains if next(chain, "done") != "done"]


def _params(*sem):
    return pltpu.CompilerParams(dimension_semantics=sem, vmem_limit_bytes=VMEM_LIMIT)


def _rotate_half_chunk(xc, half):
    if 2 * half == LANES:
        return pltpu.roll(xc, half, axis=1)
    lane = lax.broadcasted_iota(i32, xc.shape, 1)
    first = (lane & (2 * half - 1)) < half
    return jnp.where(first, pltpu.roll(xc, LANES - half, axis=1), pltpu.roll(xc, half, axis=1))


def _proj_kernel(x_ref, w_ref, cos_a_ref, sin_a_ref, cos_r_ref, sin_r_ref, o_ref, xb_ref):
    j = pl.program_id(1)

    @pl.when(j == 0)
    def _():
        xb_ref[...] = x_ref[...].astype(bf16)

    part = x_ref.shape[0] // PROJ_PARTS

    def run(epilogue):
        def chain(r):
            rows = slice(r * part, (r + 1) * part)
            acc = _dot(xb_ref[rows, :], w_ref[...])
            yield
            epilogue(acc, rows)

        _interleave(chain(r) for r in range(PROJ_PARTS))

    def rotary(cos_ref, sin_ref, half, scale):
        def epilogue(acc, rows):
            cos = cos_ref[rows, :]
            sin = sin_ref[rows, :]
            for c in range(PROJ_TN // LANES):
                xc = acc[:, c * LANES:(c + 1) * LANES]
                y = xc * cos + _rotate_half_chunk(xc, half) * sin
                if scale != 1.0:
                    y = y * scale
                o_ref[rows, c * LANES:(c + 1) * LANES] = y.astype(o_ref.dtype)

        run(epilogue)

    def plain_epilogue(acc, rows):
        o_ref[rows, :] = acc.astype(o_ref.dtype)

    j_qa = OFF_QA // PROJ_TN
    j_ka = OFF_KA // PROJ_TN
    j_qr = OFF_QR // PROJ_TN
    j_kr = OFF_KR // PROJ_TN

    @pl.when(j == j_qa)
    def _():
        rotary(cos_a_ref, sin_a_ref, MOBA_HEAD_DIM // 2, MOBA_Q_SCALE)

    @pl.when(j == j_ka)
    def _():
        rotary(cos_a_ref, sin_a_ref, MOBA_HEAD_DIM // 2, 1.0)

    @pl.when(j == j_qr)
    def _():
        rotary(cos_r_ref, sin_r_ref, RET_QK_DIM // 2, 1.0)

    @pl.when(j == j_kr)
    def _():
        rotary(cos_r_ref, sin_r_ref, RET_QK_DIM // 2, RET_QK_DIM ** -0.5)

    plain = (j != j_qa) & (j != j_ka) & (j != j_qr) & (j != j_kr)

    @pl.when(plain)
    def _():
        run(plain_epilogue)


def _rotary_tables(seq, inv_freq, head_dim):
    ang = jnp.arange(seq, dtype=f32)[:, None] * inv_freq[None, :]
    cos = jnp.cos(ang)
    sin = jnp.sin(ang)
    cos_h = jnp.concatenate([cos, cos], axis=-1)
    sin_h = jnp.concatenate([-sin, sin], axis=-1)
    reps = LANES // head_dim
    return jnp.tile(cos_h, (1, reps)), jnp.tile(sin_h, (1, reps))


def _project(x2d, w_in_b, seq):
    tokens = x2d.shape[0]
    tm = min(PROJ_TM, seq)
    inv_a = 1.0 / (ROPE_THETA ** (jnp.arange(0, MOBA_HEAD_DIM, 2, dtype=f32) / MOBA_HEAD_DIM))
    inv_r = 1.0 / (RET_ANGLE_BASE ** jnp.linspace(0.0, 1.0, RET_QK_DIM // 2, dtype=f32))
    cos_a, sin_a = _rotary_tables(seq, inv_a, MOBA_HEAD_DIM)
    cos_r, sin_r = _rotary_tables(seq, inv_r, RET_QK_DIM)
    seq_tiles = seq // tm
    tab = pl.BlockSpec((tm, LANES), lambda i, j: (i % seq_tiles, 0))
    return pl.pallas_call(
        _proj_kernel,
        grid=(tokens // tm, N_IN // PROJ_TN),
        in_specs=[
            pl.BlockSpec((tm, D_MODEL), lambda i, j: (i, 0)),
            pl.BlockSpec((D_MODEL, PROJ_TN), lambda i, j: (0, j)),
            tab, tab, tab, tab,
        ],
        out_specs=pl.BlockSpec((tm, PROJ_TN), lambda i, j: (i, j)),
        out_shape=jax.ShapeDtypeStruct((tokens, N_IN), bf16),
        scratch_shapes=[pltpu.VMEM((tm, D_MODEL), bf16)],
        compiler_params=_params("parallel", "arbitrary"),
        name="in_proj_rotary",
    )(x2d, w_in_b, cos_a, sin_a, cos_r, sin_r)


def _moba_kernel(q_ref, k_ref, v_ref, o_ref, kmean_ref, vt_ref, selb_ref, gate_ref, *, n_blocks):
    i = pl.program_id(2)
    blk = MOBA_BLOCK
    hd = MOBA_HEAD_DIM
    width = q_ref.shape[1]
    chunks = width // LANES
    heads = width // hd
    per_chunk = LANES // hd

    def chunk_of(h):
        return slice(h // per_chunk * LANES, (h // per_chunk + 1) * LANES)

    @pl.when(i == 0)
    def _():
        seq = n_blocks * blk
        member = (lax.broadcasted_iota(i32, (n_blocks, seq), 1) // blk
                  == lax.broadcasted_iota(i32, (n_blocks, seq), 0))
        kmean_ref[...] = _dot(member.astype(bf16), k_ref[...]) * (1.0 / blk)
        for j in range(n_blocks):
            vt_ref[j] = v_ref[j * blk:(j + 1) * blk, :].astype(f32).T.astype(bf16)

    km_hi, km_lo = _split_bf16(kmean_ref[...])
    key_i = lax.broadcasted_iota(i32, (blk, blk), 0)
    qry_i = lax.broadcasted_iota(i32, (blk, blk), 1)
    causal = key_i <= qry_i
    bid = lax.broadcasted_iota(i32, (n_blocks, blk), 0)
    d_row = lax.broadcasted_iota(i32, (LANES, blk), 0)
    qt = q_ref[...].astype(f32).T

    qts = []
    for h in range(heads):
        r0 = h % per_chunk * hd
        in_head = (d_row >= r0) & (d_row < r0 + hd)
        qth = jnp.where(in_head, qt[chunk_of(h), :], 0.0).astype(bf16)
        qts.append(qth)
        gate = _dot(km_hi[:, chunk_of(h)], qth) + _dot(km_lo[:, chunk_of(h)], qth)
        g = jnp.where(bid < i, gate, -jnp.inf)
        gate_ref[...] = g
        beaten = jnp.zeros((n_blocks, blk), f32)
        for j in range(n_blocks):
            gj = jnp.broadcast_to(gate_ref[j:j + 1, :], (n_blocks, blk))
            beats = (gj > g) | ((gj == g) & (bid > j))
            beaten = beaten + jnp.where(beats, 1.0, 0.0)
        sel = (beaten < MOBA_TOPK) & (bid < i)
        selb_ref[h] = jnp.where(sel, 0.0, NEG)

    kj = k_ref[pl.ds(pl.multiple_of(i * blk, blk), blk), :]
    vtj = vt_ref[i]
    state = []
    for h in range(heads):
        s = jnp.where(causal, _dot(kj[:, chunk_of(h)], qts[h]), NEG)
        m0 = jnp.max(s, axis=0, keepdims=True)
        p = jnp.exp2(s - m0)
        l0 = jnp.sum(p, axis=0, keepdims=True)
        acc0 = _dot(vtj[h * hd:(h + 1) * hd, :], p.astype(bf16))
        state += [m0, l0, acc0]

    qt_alls = [jnp.concatenate(qts[c * per_chunk:(c + 1) * per_chunk], axis=1) for c in range(chunks)]

    def update(carry, js):
        nj = len(js)
        k_all = k_ref[pl.ds(pl.multiple_of(js[0] * blk, blk), nj * blk), :]
        s_alls = [_dot(k_all[:, c * LANES:(c + 1) * LANES], qt_alls[c]) for c in range(chunks)]
        vtjs = [vt_ref[j] for j in js]
        new = []
        for h in range(heads):
            m_old, l_old, acc = carry[3 * h:3 * h + 3]
            col = h % per_chunk * blk
            ss = [s_alls[h // per_chunk][n * blk:(n + 1) * blk, col:col + blk] + selb_ref[h, pl.ds(j, 1), :]
                  for n, j in enumerate(js)]
            m_new = m_old
            for s in ss:
                m_new = jnp.maximum(m_new, jnp.max(s, axis=0, keepdims=True))
            a = jnp.exp2(m_old - m_new)
            l_new = a * l_old
            acc = a * acc
            for s, vtj in zip(ss, vtjs):
                p = jnp.exp2(s - m_new)
                l_new = l_new + jnp.sum(p, axis=0, keepdims=True)
                acc = acc + _dot(vtj[h * hd:(h + 1) * hd, :], p.astype(bf16))
            new += [m_new, l_new, acc]
        return tuple(new)

    quads = i // 4
    fin = lax.fori_loop(0, quads, lambda n, c: update(c, tuple(4 * n + t for t in range(4))), tuple(state))
    done = 4 * quads
    has_pair = (i - done) // 2
    fin = lax.fori_loop(0, has_pair, lambda n, c: update(c, (done, done + 1)), fin)
    done = done + 2 * has_pair
    fin = lax.fori_loop(done, i, lambda j, c: update(c, (j,)), fin)
    out_t = jnp.concatenate([fin[3 * h + 2] / fin[3 * h + 1] for h in range(heads)], axis=0)
    o_ref[...] = out_t.T.astype(o_ref.dtype)


def _moba(proj, batch, seq):
    tokens = batch * seq
    n_blocks = seq // MOBA_BLOCK
    width = MOBA_STEP_WIDTH
    return pl.pallas_call(
        functools.partial(_moba_kernel, n_blocks=n_blocks),
        grid=(batch, MOBA_WIDTH // width, n_blocks),
        in_specs=[
            pl.BlockSpec((MOBA_BLOCK, width), lambda b, c, i: (b * n_blocks + i, OFF_QA // width + c)),
            pl.BlockSpec((seq, width), lambda b, c, i: (b, OFF_KA // width + c)),
            pl.BlockSpec((seq, width), lambda b, c, i: (b, OFF_VA // width + c)),
        ],
        out_specs=pl.BlockSpec((MOBA_BLOCK, width), lambda b, c, i: (b * n_blocks + i, c)),
        out_shape=jax.ShapeDtypeStruct((tokens, MOBA_WIDTH), bf16),
        scratch_shapes=[
            pltpu.VMEM((n_blocks, width), f32),
            pltpu.VMEM((n_blocks, width, MOBA_BLOCK), bf16),
            pltpu.VMEM((width // MOBA_HEAD_DIM, n_blocks, MOBA_BLOCK), f32),
            pltpu.VMEM((n_blocks, MOBA_BLOCK), f32),
        ],
        compiler_params=_params("parallel", "parallel", "arbitrary"),
        name="moba_attention",
    )(proj, proj, proj)


def _ret_kernel(q_ref, k_ref, v0_ref, v1_ref, g0_ref, g1_ref, decay_ref, zeta_ref, xi_ref, cd_ref,
                o_ref, state_ref):
    c = pl.program_id(1)
    heads_per_half = RET_HEADS // 2
    v_refs = (v0_ref, v1_ref)
    g_refs = (g0_ref, g1_ref)

    @pl.when(c == 0)
    def _():
        state_ref[...] = jnp.zeros_like(state_ref)

    for h in range(RET_HEADS):
        qk_cols = slice(h * RET_QK_DIM, (h + 1) * RET_QK_DIM)
        v_cols = slice(h * RET_V_DIM, (h + 1) * RET_V_DIM)
        half_cols = slice((h % heads_per_half) * RET_V_DIM, (h % heads_per_half + 1) * RET_V_DIM)
        q = q_ref[:, qk_cols]
        k = k_ref[:, qk_cols]
        v = v_refs[h // heads_per_half][:, half_cols]
        state = state_ref[h]
        scores = _dot_nt(q, k) * decay_ref[h]
        inner = _dot(scores.astype(bf16), v)
        cross = _dot(q, state.astype(bf16)) * xi_ref[h]
        o = inner + cross
        kz = (k.astype(f32) * zeta_ref[h]).astype(bf16)
        state_ref[h] = cd_ref[h] * state + _dot_tn(kz, v)

        mu = jnp.mean(o, axis=-1, keepdims=True)
        oc = o - mu
        var = jnp.mean(oc * oc, axis=-1, keepdims=True)
        on = oc * lax.rsqrt(var + GN_EPS)
        g = g_refs[h // heads_per_half][:, half_cols].astype(f32)
        o_ref[:, v_cols] = (on * (g * _sigmoid(g))).astype(o_ref.dtype)


def _retention(proj, batch, seq):
    tokens = batch * seq
    C = RET_CHUNK
    n_chunks = seq // C
    half_v = RET_V_WIDTH // 2
    gammas =1.0 - 2.0 ** (-5.0 - jnp.arange(RET_HEADS, dtype=f32))
    log_g = jnp.log(gammas)
    idx = jnp.arange(C, dtype=f32)
    diff = idx[:, None] - idx[None, :]
    decay = jnp.where(diff >= 0, jnp.exp(jnp.maximum(diff, 0.0)[None] * log_g[:, None, None]), 0.0)
    zeta = jnp.exp((C - 1 - idx)[None, :] * log_g[:, None])
    xi = jnp.exp((idx + 1.0)[None, :] * log_g[:, None])
    zeta_t = jnp.broadcast_to(zeta[:, :, None], (RET_HEADS, C, RET_QK_DIM))
    xi_t = jnp.broadcast_to(xi[:, :, None], (RET_HEADS, C, RET_V_DIM))
    cd_t = jnp.broadcast_to(jnp.exp(C * log_g)[:, None, None], (RET_HEADS, 1, RET_V_DIM))
    return pl.pallas_call(
        _ret_kernel,
        grid=(batch, n_chunks),
        in_specs=[
            pl.BlockSpec((C, RET_QK_WIDTH), lambda b, c: (b * n_chunks + c, OFF_QR // RET_QK_WIDTH)),
            pl.BlockSpec((C, RET_QK_WIDTH), lambda b, c: (b * n_chunks + c, OFF_KR // RET_QK_WIDTH)),
            pl.BlockSpec((C, half_v), lambda b, c: (b * n_chunks + c, OFF_VR // half_v)),
            pl.BlockSpec((C, half_v), lambda b, c: (b * n_chunks + c, OFF_VR // half_v + 1)),
            pl.BlockSpec((C, half_v), lambda b, c: (b * n_chunks + c, OFF_GR // half_v)),
            pl.BlockSpec((C, half_v), lambda b, c: (b * n_chunks + c, OFF_GR // half_v + 1)),
            pl.BlockSpec((RET_HEADS, C, C), lambda b, c: (0, 0, 0)),
            pl.BlockSpec((RET_HEADS, C, RET_QK_DIM), lambda b, c: (0, 0, 0)),
            pl.BlockSpec((RET_HEADS, C, RET_V_DIM), lambda b, c: (0, 0, 0)),
            pl.BlockSpec((RET_HEADS, 1, RET_V_DIM), lambda b, c: (0, 0, 0)),
        ],
        out_specs=pl.BlockSpec((C, RET_V_WIDTH), lambda b, c: (b * n_chunks + c, 0)),
        out_shape=jax.ShapeDtypeStruct((tokens, RET_V_WIDTH), bf16),
        scratch_shapes=[pltpu.VMEM((RET_HEADS, RET_QK_DIM, RET_V_DIM), f32)],
        compiler_params=_params("parallel", "arbitrary"),
        name="retention",
    )(proj, proj, proj, proj, proj, proj, decay, zeta_t, xi_t, cd_t)


def _merge_router_kernel(oa_ref, orr_ref, ga0_ref, ga1_ref, gb0_ref, gb1_ref, x_ref,
                         wm_ref, wr_ref, wo_ref, g1_ref, b1_ref, wrt_hi_ref, wrt_lo_ref, rb_ref,
                         h_ref, pos_ref, w_ref, tab_ref, cnt_ref, carry_ref):
    step = pl.program_id(0)
    tm = TOK_TM

    @pl.when(step == 0)
    def _():
        carry_ref[...] = jnp.zeros_like(carry_ref)

    _interleave(
        _merge_router_tile(t, slice(t * tm, (t + 1) * tm), tm, oa_ref, orr_ref, ga0_ref, ga1_ref, gb0_ref, gb1_ref,
                           x_ref, wm_ref, wr_ref, wo_ref, g1_ref, b1_ref, wrt_hi_ref, wrt_lo_ref, rb_ref,
                           h_ref, pos_ref, w_ref, tab_ref, cnt_ref, carry_ref)
        for t in range(x_ref.shape[0] // tm))


def _merge_router_tile(t, rows, tm, oa_ref, orr_ref, ga0_ref, ga1_ref, gb0_ref, gb1_ref, x_ref,
                       wm_ref, wr_ref, wo_ref, g1_ref, b1_ref, wrt_hi_ref, wrt_lo_ref, rb_ref,
                       h_ref, pos_ref, w_ref, tab_ref, cnt_ref, carry_ref):
    a = _dot(oa_ref[rows, :], wm_ref[...])
    r = _dot(orr_ref[rows, :], wr_ref[...])
    yield
    ga = jnp.concatenate([ga0_ref[rows, :], ga1_ref[rows, :]], axis=1).astype(f32)
    gb = jnp.concatenate([gb0_ref[rows, :], gb1_ref[rows, :]], axis=1).astype(f32)
    merged = _sigmoid(ga) * a + _sigmoid(gb) * r
    yield
    mix = _dot(merged.astype(bf16), wo_ref[...])
    yield
    h = _layer_norm(ALPHA * x_ref[rows, :] + mix, g1_ref[...], b1_ref[...])
    h_ref[rows, :] = h
    yield

    h_hi, h_lo = _split_bf16(h)
    w_hi = wrt_hi_ref[...]
    logits = _dot_nt(w_hi, h_hi) + _dot_nt(w_hi, h_lo) + _dot_nt(wrt_lo_ref[...], h_hi)
    yield
    scores = _sigmoid(logits)
    biased = scores + rb_ref[...]

    v = biased.reshape(N_GROUPS, GROUP_SIZE, tm)
    sub = lax.broadcasted_iota(i32, v.shape, 1)
    m1 = jnp.max(v, axis=1, keepdims=True)
    i1 = jnp.min(jnp.where(v == m1, sub, GROUP_SIZE), axis=1, keepdims=True)
    m2 = jnp.max(jnp.where(sub == i1, -jnp.inf, v), axis=1, keepdims=True)
    gscore = jnp.broadcast_to(m1 + m2, v.shape).reshape(N_EXPERTS, tm)

    eid = lax.broadcasted_iota(i32, (N_EXPERTS, tm), 0)
    egrp = eid // GROUP_SIZE
    e_mask = jnp.zeros((N_EXPERTS, tm), jnp.bool_)
    for _ in range(TOPK_GROUPS):
        m = jnp.max(gscore, axis=0, keepdims=True)
        idx = jnp.min(jnp.where(gscore == m, egrp, N_GROUPS), axis=0, keepdims=True)
        hit = egrp == idx
        e_mask = e_mask | hit
        gscore = jnp.where(hit, -jnp.inf, gscore)
    cand = jnp.where(e_mask, biased, -jnp.inf)

    chosen = jnp.zeros((N_EXPERTS, tm), jnp.bool_)
    e_rows = []
    w_rows = []
    for _ in range(TOP_K):
        m = jnp.max(cand, axis=0, keepdims=True)
        idx = jnp.min(jnp.where(cand == m, eid, N_EXPERTS), axis=0, keepdims=True)
        hit = eid == idx
        chosen = chosen | hit
        e_rows.append(idx)
        w_rows.append(jnp.sum(jnp.where(hit, scores, 0.0), axis=0, keepdims=True))
        cand = jnp.where(hit, -jnp.inf, cand)
    w_sum = w_rows[0]
    for wk in w_rows[1:]:
        w_sum = w_sum + wk

    t_src = lax.broadcasted_iota(i32, (tm, tm), 0)
    t_dst = lax.broadcasted_iota(i32, (tm, tm), 1)
    before = (t_src < t_dst).astype(bf16)
    chosen_f = chosen.astype(f32)
    prior = _dot(chosen_f.astype(bf16), before)
    cnt = jnp.sum(chosen_f, axis=1, keepdims=True)
    cnt_pad = jnp.ceil(cnt * (1.0 / UNIT)) * UNIT
    e_src = lax.broadcasted_iota(i32, (N_EXPERTS, N_EXPERTS), 1)
    e_dst = lax.broadcasted_iota(i32, (N_EXPERTS, N_EXPERTS), 0)
    earlier = (e_src < e_dst).astype(bf16)
    cnt_pad_l = jnp.broadcast_to(cnt_pad, (N_EXPERTS, LANES))
    loc_start_l = _dot(earlier, cnt_pad_l.astype(bf16))
    loc_start = loc_start_l[:, 0:1]
    where_e = prior + loc_start
    pos_ref[:, rows] = jnp.zeros((pos_ref.shape[0], tm), pos_ref.dtype)
    w_ref[:, rows] = jnp.zeros((w_ref.shape[0], tm), w_ref.dtype)
    for k in range(TOP_K):
        pos_ref[k:k + 1, rows] = jnp.sum(jnp.where(eid == e_rows[k], where_e, 0.0), axis=0, keepdims=True).astype(i32)
        w_ref[k:k + 1, rows] = w_rows[k] / w_sum * ROUTED_SCALE

    carry = carry_ref[...]
    tl = lax.broadcasted_iota(i32, (N_EXPERTS, LANES), 1)
    table = jnp.where(tl == 0, cnt_pad_l, jnp.where(tl == 1, loc_start_l, jnp.broadcast_to(carry, (N_EXPERTS, LANES))))
    tab_ref[t] = table.astype(i32)
    carry = carry + cnt_pad
    carry_ref[...] = carry
    cnt_ref[...] = jnp.broadcast_to(carry, cnt_ref.shape).astype(i32)


def _merge_router(oa, orr, proj, x2d, wm, wr, wo, g1, b1, wrt_hi, wrt_lo, rb):
    tokens = x2d.shape[0]
    tm = MERGE_TILES * TOK_TM
    half = D_MODEL // 2

    def gate_spec(off):
        return pl.BlockSpec((tm, half), lambda i: (i, off // half))

    def full(shape):
        return pl.BlockSpec(shape, lambda i: (0,) * len(shape))

    row8 = pl.BlockSpec((8, tm), lambda i: (0, i))
    return pl.pallas_call(
        _merge_router_kernel,
        grid=(tokens // tm,),
        in_specs=[
            pl.BlockSpec((tm, MOBA_WIDTH), lambda i: (i, 0)),
            pl.BlockSpec((tm, RET_V_WIDTH), lambda i: (i, 0)),
            gate_spec(OFF_GA), gate_spec(OFF_GA + half), gate_spec(OFF_GB), gate_spec(OFF_GB + half),
            pl.BlockSpec((tm, D_MODEL), lambda i: (i, 0)),
            full((MOBA_WIDTH, D_MODEL)), full((RET_V_WIDTH, D_MODEL)), full((D_MODEL, D_MODEL)),
            full((1, D_MODEL)), full((1, D_MODEL)),
            full((N_EXPERTS, D_MODEL)), full((N_EXPERTS, D_MODEL)), full((N_EXPERTS, 1)),
        ],
        out_specs=[
            pl.BlockSpec((tm, D_MODEL), lambda i: (i, 0)),
            row8, row8,
            pl.BlockSpec((MERGE_TILES, N_EXPERTS, LANES), lambda i: (i, 0, 0)),
            full((N_EXPERTS, LANES)),
        ],
        out_shape=[
            jax.ShapeDtypeStruct((tokens, D_MODEL), f32),
            jax.ShapeDtypeStruct((8, tokens), i32),
            jax.ShapeDtypeStruct((8, tokens), f32),
            jax.ShapeDtypeStruct((tokens // TOK_TM, N_EXPERTS, LANES), i32),
            jax.ShapeDtypeStruct((N_EXPERTS, LANES), i32),
        ],
        scratch_shapes=[pltpu.VMEM((N_EXPERTS, 1), f32)],
        compiler_params=_params("arbitrary"),
        name="merge_ln1_router",
    )(oa, orr, proj, proj, proj, proj, x2d, wm, wr, wo, g1, b1, wrt_hi, wrt_lo, rb)


def _pack_rows(x):
    bits = lax.bitcast_convert_type(x, jnp.uint32)
    return (bits[:, :HALF] & jnp.uint32(0xFFFF0000)) | (bits[:, HALF:] >> 16)


def _unpack_rows(words):
    left = lax.bitcast_convert_type(words & jnp.uint32(0xFFFF0000), f32)
    right = lax.bitcast_convert_type(words << 16, f32)
    return left.astype(bf16), right.astype(bf16)


def _start_alternating(n, make_copy):
    def pair(m, _):
        make_copy(2 * m).start(priority=0)
        make_copy(2 * m + 1).start(priority=1)
        return 0

    lax.fori_loop(0, n // 2, pair, 0)

    @pl.when(n % 2 == 1)
    def _():
        make_copy(n - 1).start(priority=0)


def _move_runs(tile, n_ref, src_ref, dst_ref, make_copy):
    def piece(c, p):
        rows = UNIT << c
        base = (tile * RUN_CLASSES + c) * N_EXPERTS
        loc = pl.ds(pl.multiple_of(src_ref[base + p] * UNIT, UNIT), rows)
        glob = pl.ds(pl.multiple_of(dst_ref[base + p] * UNIT, UNIT), rows)
        return make_copy(loc, glob)

    for c in range(RUN_CLASSES):
        _start_alternating(n_ref[tile * RUN_CLASSES + c], functools.partial(piece, c))

    for c in range(RUN_CLASSES):
        def wait_one(p, _, c=c):
            piece(c, p).wait()
            return 0

        lax.fori_loop(0, n_ref[tile * RUN_CLASSES + c], wait_one, 0)


def _dispatch_kernel(n_ref, src_ref, dst_ref, tail_ref, used_ref, h_ref, pos_ref, rows_ref,
                     xs_ref, zero_ref, sem_ref, zsem_ref):
    step = pl.program_id(0)
    tm = TOK_TM
    tiles = h_ref.shape[0] // tm

    @pl.when(step == 0)
    def _():
        zero_ref[...] = jnp.zeros_like(zero_ref)

        def clear_copy(block):
            start = pl.multiple_of(block * ROW_BLOCK, ROW_BLOCK)
            return pltpu.make_async_copy(zero_ref, rows_ref.at[pl.ds(start, ROW_BLOCK), :], zsem_ref)

        def clear_all(act):
            def tail(e, _):
                tail_block = tail_ref[e]

                @pl.when(tail_block >= 0)
                def _():
                    act(clear_copy(tail_block))
                return 0

            lax.fori_loop(0, N_EXPERTS, tail, 0)

            def unused(b, _):
                act(clear_copy(b))
                return 0

            lax.fori_loop(used_ref[0], rows_ref.shape[0] // ROW_BLOCK, unused, 0)

        clear_all(lambda cp: cp.start())
        clear_all(lambda cp: cp.wait())

    def sort_tile(u):
        cols = slice(u * tm, (u + 1) * tm)
        pos = pos_ref[:, cols]
        r_iota = lax.broadcasted_iota(i32, (LOC_ROWS, tm), 0)
        onehot = r_iota == pos[0:1, :]
        for k in range(1, TOP_K):
            onehot = onehot | (r_iota == pos[k:k + 1, :])
        onehot = onehot.astype(bf16)
        yield
        xs = _dot(onehot, h_ref[cols, :].astype(bf16))
        yield
        xs_ref[u] = _pack_rows(xs)

    _interleave(sort_tile(u) for u in range(tiles))

    for u in range(tiles):
        _move_runs(step * tiles + u, n_ref, src_ref, dst_ref,
                   lambda loc, glob, u=u: pltpu.make_async_copy(xs_ref.at[u, loc, :], rows_ref.at[glob, :], sem_ref))


def _dispatch(runs, tail_block, n_used, h, pos8, n_rows):
    tokens = h.shape[0]
    tm = MOE_TILES * TOK_TM
    grid_spec = pltpu.PrefetchScalarGridSpec(
        num_scalar_prefetch=5,
        grid=(tokens // tm,),
        in_specs=[
            pl.BlockSpec((tm, D_MODEL), lambda i, *_: (i, 0)),
            pl.BlockSpec((8, tm), lambda i, *_: (0, i)),
        ],
        out_specs=pl.BlockSpec(memory_space=pl.ANY),
        scratch_shapes=[
            pltpu.VMEM((MOE_TILES, LOC_ROWS, HALF), jnp.uint32),
            pltpu.VMEM((ROW_BLOCK, HALF), jnp.uint32),
            pltpu.SemaphoreType.DMA,
            pltpu.SemaphoreType.DMA,
        ],
    )
    return pl.pallas_call(
        _dispatch_kernel,
        grid_spec=grid_spec,
        out_shape=jax.ShapeDtypeStruct((n_rows, HALF), jnp.uint32),
        compiler_params=_params("arbitrary"),
        name="moe_dispatch",
    )(*runs, tail_block, n_used, h, pos8)


def _expert_kernel(eid_ref, used_ref, x_ref, wg_ref, wu_ref, wd_ref, y_ref, wgb_ref, wub_ref, wdb_ref):
    b = pl.program_id(0)
    changed = (b == 0) | (eid_ref[b] != eid_ref[jnp.maximum(b - 1, 0)])

    @pl.when(changed)
    def _():
        wgb_ref[...] = wg_ref[0].astype(bf16)
        wub_ref[...] = wu_ref[0].astype(bf16)
        wdb_ref[...] = wd_ref[0].astype(bf16)

    @pl.when(b < used_ref[0])
    def _():
        def chunk_phases(c):
            rows = slice(c * EXPERT_CHUNK, (c + 1) * EXPERT_CHUNK)
            xl, xr = _unpack_rows(x_ref[rows, :])
            yield
            g = _dot(xl, wgb_ref[:HALF, :]) + _dot(xr, wgb_ref[HALF:, :])
            u = _dot(xl, wub_ref[:HALF, :]) + _dot(xr, wub_ref[HALF:, :])
            yield
            mid = ((g * _sigmoid(g)) * u).astype(bf16)
            yield
            y = _dot(mid, wdb_ref[...])
            yield
            y_ref[rows, :] = _pack_rows(y.astype(bf16).astype(f32))

        _interleave(chunk_phases(c) for c in range(ROW_BLOCK // EXPERT_CHUNK))

    @pl.when(b >= used_ref[0])
    def _():
        y_ref[...] = jnp.zeros_like(y_ref)


def _experts(block_eid, n_used, rows, wg, wu, wd):
    n_rows = rows.shape[0]
    n_blocks = n_rows // ROW_BLOCK

    def row_map(b, eid, used):
        return (b, 0)

    def w_map(b, eid, used):
        return (eid[b], 0, 0)

    grid_spec = pltpu.PrefetchScalarGridSpec(
        num_scalar_prefetch=2,
        grid=(n_blocks,),
        in_specs=[
            pl.BlockSpec((ROW_BLOCK, HALF), row_map),
            pl.BlockSpec((1, D_MODEL, D_EXPERT), w_map),
            pl.BlockSpec((1, D_MODEL, D_EXPERT), w_map),
            pl.BlockSpec((1, D_EXPERT, D_MODEL), w_map),
        ],
        out_specs=pl.BlockSpec((ROW_BLOCK, HALF), row_map),
        scratch_shapes=[
            pltpu.VMEM((D_MODEL, D_EXPERT), bf16),
            pltpu.VMEM((D_MODEL, D_EXPERT), bf16),
            pltpu.VMEM((D_EXPERT, D_MODEL), bf16),
        ],
    )
    return pl.pallas_call(
        _expert_kernel,
        grid_spec=grid_spec,
        out_shape=jax.ShapeDtypeStruct((n_rows, HALF), jnp.uint32),
        compiler_params=_params("arbitrary"),
        name="moe_experts",
    )(block_eid, n_used, rows, wg, wu, wd)


def _combine_kernel(n_ref, src_ref, dst_ref, y_ref, h_ref, pos_ref, wt_ref, p_ref, wsg_ref, wsu_ref, wsd_ref,
                    wpp_ref, wpg_ref, g2_ref, b2_ref, g3_ref, b3_ref, o_ref, ybuf_ref, sem_ref):
    step = pl.program_id(0)
    tm = TOK_TM
    tiles = h_ref.shape[0] // tm

    @pl.when(step == 0)
    def _():
        ybuf_ref[...] = jnp.zeros_like(ybuf_ref)

    for u in range(tiles):
        _move_runs(step * tiles + u, n_ref, src_ref, dst_ref,
                   lambda loc, glob, u=u: pltpu.make_async_copy(y_ref.at[glob, :], ybuf_ref.at[u, loc, :], sem_ref))

    def tile_phases(u):
        rows = slice(u * tm, (u + 1) * tm)
        h = h_ref[rows, :]
        hb = h.astype(bf16)
        sg = _dot(hb, wsg_ref[...])
        su = _dot(hb, wsu_ref[...])
        yield
        mid = ((sg * _sigmoid(sg)) * su).astype(bf16)
        yield
        shared = _dot(mid, wsd_ref[...])
        ple_in = _dot(p_ref[rows, :].astype(bf16), wpp_ref[...])
        yield
        pos = pos_ref[rows, :]
        wt = wt_ref[rows, :]
        c_iota = lax.broadcasted_iota(i32, (tm, LOC_ROWS), 1)
        spread = jnp.where(c_iota == pos[:, 0:1], wt[:, 0:1], 0.0)
        for k in range(1, TOP_K):
            spread = jnp.where(c_iota == pos[:, k:k + 1], wt[:, k:k + 1], spread)
        spread = spread.astype(bf16)
        yield
        yl, yr = _unpack_rows(ybuf_ref[u])
        yield
        routed = jnp.concatenate([_dot(spread, yl), _dot(spread, yr)], axis=1)
        yield
        h2 = _layer_norm(ALPHA * h + (routed + shared), g2_ref[...], b2_ref[...])
        yield
        gate = _dot(h2.astype(bf16), wpg_ref[...])
        yield
        ple = ple_in * _sigmoid(gate)
        o_ref[rows, :] = _layer_norm(ALPHA * h2 + ple, g3_ref[...], b3_ref[...])

    _interleave(tile_phases(u) for u in range(tiles))


def _combine(runs, y_rows, h, pos_tok, w_tok, p2d, wsg, wsu, wsd, wpp, wpg, g2, b2, g3, b3):
    tokens = h.shape[0]
    tm = MOE_TILES * TOK_TM

    def full(shape):
        return pl.BlockSpec(shape, lambda i, *_: (0,) * len(shape))

    def tile(width):
        return pl.BlockSpec((tm, width), lambda i, *_: (i, 0))

    grid_spec = pltpu.PrefetchScalarGridSpec(
        num_scalar_prefetch=3,
        grid=(tokens // tm,),
        in_specs=[
            pl.BlockSpec(memory_space=pl.ANY),
            tile(D_MODEL), tile(8), tile(8), tile(PLE_DIM),
            full((D_MODEL, D_SHARED)), full((D_MODEL, D_SHARED)), full((D_SHARED, D_MODEL)),
            full((PLE_DIM, D_MODEL)), full((D_MODEL, D_MODEL)),
            full((1, D_MODEL)), full((1, D_MODEL)), full((1, D_MODEL)), full((1, D_MODEL)),
        ],
        out_specs=tile(D_MODEL),
        scratch_shapes=[
            pltpu.VMEM((MOE_TILES, LOC_ROWS, HALF), jnp.uint32),
            pltpu.SemaphoreType.DMA,
        ],
    )
    return pl.pallas_call(
        _combine_kernel,
        grid_spec=grid_spec,
        out_shape=jax.ShapeDtypeStruct((tokens, D_MODEL), f32),
        compiler_params=_params("arbitrary"),
        name="moe_combine_ln2_ple_ln3",
    )(*runs, y_rows, h, pos_tok, w_tok, p2d, wsg, wsu, wsd, wpp, wpg, g2, b2, g3, b3)


def _row_layout(table, totals, n_blocks):
    cnt_u = table[:, :, 0] // UNIT
    loc_u = table[:, :, 1] // UNIT
    padded = (totals + ROW_BLOCK - 1) // ROW_BLOCK * ROW_BLOCK
    pends = jnp.cumsum(padded)
    pstarts = pends - padded
    glob_u = (pstarts[None, :] + table[:, :, 2]) // UNIT
    cls = jnp.arange(RUN_CLASSES, dtype=i32)
    has = (cnt_u[:, None, :] >> cls[None, :, None]) & 1
    off = (cnt_u[:, None, :] >> (cls[None, :, None] + 1)) << (cls[None, :, None] + 1)
    upto = jnp.cumsum(has, axis=2)
    n_pieces = upto[:, :, -1].astype(i32).reshape(-1)
    slot = jnp.arange(N_EXPERTS, dtype=i32)
    e_of = jnp.sum((upto[:, :, None, :] <= slot[None, None, :, None]).astype(i32), axis=-1)
    is_e = e_of[:, :, :, None] == jnp.arange(N_EXPERTS, dtype=i32)
    src = jnp.sum(jnp.where(is_e, (loc_u[:, None, :] + off)[:, :, None, :], 0), axis=-1).astype(i32).reshape(-1)
    dst = jnp.sum(jnp.where(is_e, (glob_u[:, None, :] + off)[:, :, None, :], 0), axis=-1).astype(i32).reshape(-1)
    runs = (n_pieces, src, dst)
    tail_block = jnp.where(totals > 0, pends // ROW_BLOCK - 1, -1).astype(i32)
    block_start = jnp.arange(n_blocks, dtype=i32) * ROW_BLOCK
    ends_before = jnp.sum((pends[None, :] <= block_start[:, None]).astype(i32), axis=1)
    block_eid = jnp.minimum(ends_before, N_EXPERTS - 1).astype(i32)
    n_used = (pends[-1:] // ROW_BLOCK).astype(i32)
    return runs, tail_block, block_eid, n_used


def _layer(x2d, p2d, batch, seq, w_in, w_moba_up, w_ret_up, w_out, ln1_g, ln1_b, w_router, router_bias,
           w_eg, w_eu, w_ed, w_sg, w_su, w_sd, ln2_g, ln2_b, w_ple_proj, w_ple_gate, ln3_g, ln3_b):
    tokens = batch * seq
    row = lambda a: a.reshape(1, -1).astype(f32)

    proj = _project(x2d, w_in.astype(bf16), seq)
    oa = _moba(proj, batch, seq)
    orr = _retention(proj, batch, seq)

    wrt = w_router.astype(f32).T
    wrt_hi = wrt.astype(bf16)
    wrt_lo = (wrt - wrt_hi.astype(f32)).astype(bf16)
    h1, pos8, w8, table, totals = _merge_router(
        oa, orr, proj, x2d, w_moba_up.astype(bf16), w_ret_up.astype(bf16), w_out.astype(bf16),
        row(ln1_g), row(ln1_b), wrt_hi, wrt_lo, router_bias.astype(f32).reshape(N_EXPERTS, 1))

    n_tiles = tokens // TOK_TM
    max_rows = tokens * TOP_K + n_tiles * N_EXPERTS * (UNIT - 1) + N_EXPERTS * (ROW_BLOCK - 1)
    n_blocks = -(-max_rows // ROW_BLOCK)
    runs, tail_block, block_eid, n_used = _row_layout(table[:, :, :3], totals[:, 0], n_blocks)

    rows = _dispatch(runs, tail_block, n_used, h1, pos8, n_blocks * ROW_BLOCK)
    y_rows = _experts(block_eid, n_used, rows, w_eg, w_eu, w_ed)
    return _combine(runs, y_rows, h1, pos8.T, w8.T, p2d,
                    w_sg.astype(bf16), w_su.astype(bf16), w_sd.astype(bf16),
                    w_ple_proj.astype(bf16), w_ple_gate.astype(bf16),
                    row(ln2_g), row(ln2_b), row(ln3_g), row(ln3_b))


def kernel(x, p, w_in, w_moba_up, w_ret_up, w_out, ln1_g, ln1_b, w_router, router_bias, w_exp_gate, w_exp_up,
           w_exp_down, w_sh_gate, w_sh_up, w_sh_down, ln2_g, ln2_b, w_ple_proj, w_ple_gate, ln3_g, ln3_b):
    batch, seq, d = x.shape
    assert d == D_MODEL and seq % max(MOBA_BLOCK, RET_CHUNK) == 0 and (batch * seq) % (MERGE_TILES * TOK_TM) == 0
    assert w_in.shape[0] == DEPTH
    h = x.reshape(batch * seq, d)
    for i in range(DEPTH):
        h = _layer(h, p[i].reshape(batch * seq, PLE_DIM), batch, seq,
                   w_in[i], w_moba_up[i], w_ret_up[i], w_out[i], ln1_g[i], ln1_b[i], w_router[i], router_bias[i],
                   w_exp_gate[i], w_exp_up[i], w_exp_down[i], w_sh_gate[i], w_sh_up[i], w_sh_down[i],
                   ln2_g[i], ln2_b[i], w_ple_proj[i], w_ple_gate[i], ln3_g[i], ln3_b[i])
    return h.reshape(batch, seq, d)
```

```python
import functools

import jax
import jax.numpy as jnp
from jax import lax
from jax.experimental import pallas as pl
from jax.experimental.pallas import tpu as pltpu

f32 = jnp.float32
bf16 = jnp.bfloat16
i32 = jnp.int32

D_MODEL = 1024
DEPTH = 1
MOBA_HEADS = 8
MOBA_HEAD_DIM = 64
MOBA_WIDTH = MOBA_HEADS * MOBA_HEAD_DIM
MOBA_BLOCK = 256
MOBA_TOPK = 3
ROPE_THETA = 10000.0
RET_HEADS = 4
RET_QK_DIM = 128
RET_V_DIM = 256
RET_QK_WIDTH = RET_HEADS * RET_QK_DIM
RET_V_WIDTH = RET_HEADS * RET_V_DIM
RET_CHUNK = 256
RET_ANGLE_BASE = 10000.0
N_IN = 3 * MOBA_WIDTH + 2 * RET_QK_WIDTH + 2 * RET_V_WIDTH + 2 * D_MODEL
N_EXPERTS = 64
TOP_K = 6
N_GROUPS = 8
TOPK_GROUPS = 4
GROUP_SIZE = N_EXPERTS // N_GROUPS
D_EXPERT = 256
D_SHARED = 256
ROUTED_SCALE = 2.5
PLE_DIM = 256
LN_EPS = 1e-5
GN_EPS = 1e-6
ALPHA = (2.0 * DEPTH) ** 0.25

OFF_QA = 0
OFF_KA = MOBA_WIDTH
OFF_VA = 2 * MOBA_WIDTH
OFF_QR = 3 * MOBA_WIDTH
OFF_KR = OFF_QR + RET_QK_WIDTH
OFF_VR = OFF_KR + RET_QK_WIDTH
OFF_GR = OFF_VR + RET_V_WIDTH
OFF_GA = OFF_GR + RET_V_WIDTH
OFF_GB = OFF_GA + D_MODEL

MOBA_Q_SCALE = MOBA_HEAD_DIM ** -0.5 * 1.4426950408889634

LANES = 128
NEG = -1e30
VMEM_LIMIT = 56 * 1024 * 1024

PROJ_TM = 2048
PROJ_PARTS = 8
PROJ_TN = 512
MOBA_STEP_WIDTH = 512
MOE_TILES = 2
MERGE_TILES = 4
TOK_TM = 256
ROW_BLOCK = 512
EXPERT_CHUNK = 256
UNIT = 8
LOC_ROWS = -(-(TOP_K * TOK_TM + N_EXPERTS * (UNIT - 1)) // LANES) * LANES
RUN_CLASSES = (TOK_TM // UNIT).bit_length()
HALF = D_MODEL // 2


def _dot(a, b):
    return jnp.dot(a, b, preferred_element_type=f32)


def _dot_nt(a, b):
    return lax.dot_general(a, b, (((1,), (1,)), ((), ())), preferred_element_type=f32)


def _dot_tn(a, b):
    return lax.dot_general(a, b, (((0,), (0,)), ((), ())), preferred_element_type=f32)


def _split_bf16(a):
    hi = a.astype(bf16)
    lo = (a - hi.astype(f32)).astype(bf16)
    return hi, lo


def _layer_norm(x, g, b):
    mu = jnp.mean(x, axis=-1, keepdims=True)
    xc = x - mu
    var = jnp.mean(xc * xc, axis=-1, keepdims=True)
    return xc * lax.rsqrt(var + LN_EPS) * g + b


def _sigmoid(x):
    return 1.0 / (1.0 + jnp.exp(-x))


def _interleave(chains):
    chains = list(chains)
    while chains:
        chains = [chain for chain in chains if next(chain, "done") != "done"]


def _params(*sem):
    return pltpu.CompilerParams(dimension_semantics=sem, vmem_limit_bytes=VMEM_LIMIT)


def _rotate_half_chunk(xc, half):
    if 2 * half == LANES:
        return pltpu.roll(xc, half, axis=1)
    lane = lax.broadcasted_iota(i32, xc.shape, 1)
    first = (lane & (2 * half - 1)) < half
    return jnp.where(first, pltpu.roll(xc, LANES - half, axis=1), pltpu.roll(xc, half, axis=1))


def _proj_kernel(x_ref, w_ref, cos_a_ref, sin_a_ref, cos_r_ref, sin_r_ref, o_ref, xb_ref):
    j = pl.program_id(1)

    @pl.when(j == 0)
    def _():
        xb_ref[...] = x_ref[...].astype(bf16)

    part = x_ref.shape[0] // PROJ_PARTS

    def run(epilogue):
        def chain(r):
            rows = slice(r * part, (r + 1) * part)
            acc = _dot(xb_ref[rows, :], w_ref[...])
            yield
            epilogue(acc, rows)

        _interleave(chain(r) for r in range(PROJ_PARTS))

    def rotary(cos_ref, sin_ref, half, scale):
        def epilogue(acc, rows):
            cos = cos_ref[rows, :]
            sin = sin_ref[rows, :]
            for c in range(PROJ_TN // LANES):
                xc = acc[:, c * LANES:(c + 1) * LANES]
                y = xc * cos + _rotate_half_chunk(xc, half) * sin
                if scale != 1.0:
                    y = y * scale
                o_ref[rows, c * LANES:(c + 1) * LANES] = y.astype(o_ref.dtype)

        run(epilogue)

    def plain_epilogue(acc, rows):
        o_ref[rows, :] = acc.astype(o_ref.dtype)

    j_qa = OFF_QA // PROJ_TN
    j_ka = OFF_KA // PROJ_TN
    j_qr = OFF_QR // PROJ_TN
    j_kr = OFF_KR // PROJ_TN

    @pl.when(j == j_qa)
    def _():
        rotary(cos_a_ref, sin_a_ref, MOBA_HEAD_DIM // 2, MOBA_Q_SCALE)

    @pl.when(j == j_ka)
    def _():
        rotary(cos_a_ref, sin_a_ref, MOBA_HEAD_DIM // 2, 1.0)

    @pl.when(j == j_qr)
    def _():
        rotary(cos_r_ref, sin_r_ref, RET_QK_DIM // 2, 1.0)

    @pl.when(j == j_kr)
    def _():
        rotary(cos_r_ref, sin_r_ref, RET_QK_DIM // 2, RET_QK_DIM ** -0.5)

    plain = (j != j_qa) & (j != j_ka) & (j != j_qr) & (j != j_kr)

    @pl.when(plain)
    def _():
        run(plain_epilogue)


def _rotary_tables(seq, inv_freq, head_dim):
    ang = jnp.arange(seq, dtype=f32)[:, None] * inv_freq[None, :]
    cos = jnp.cos(ang)
    sin = jnp.sin(ang)
    cos_h = jnp.concatenate([cos, cos], axis=-1)
    sin_h = jnp.concatenate([-sin, sin], axis=-1)
    reps = LANES // head_dim
    return jnp.tile(cos_h, (1, reps)), jnp.tile(sin_h, (1, reps))


def _project(x2d, w_in_b, seq):
    tokens = x2d.shape[0]
    tm = min(PROJ_TM, seq)
    inv_a = 1.0 / (ROPE_THETA ** (jnp.arange(0, MOBA_HEAD_DIM, 2, dtype=f32) / MOBA_HEAD_DIM))
    inv_r = 1.0 / (RET_ANGLE_BASE ** jnp.linspace(0.0, 1.0, RET_QK_DIM // 2, dtype=f32))
    cos_a, sin_a = _rotary_tables(seq, inv_a, MOBA_HEAD_DIM)
    cos_r, sin_r = _rotary_tables(seq, inv_r, RET_QK_DIM)
    seq_tiles = seq // tm
    tab = pl.BlockSpec((tm, LANES), lambda i, j: (i % seq_tiles, 0))
    return pl.pallas_call(
        _proj_kernel,
        grid=(tokens // tm, N_IN // PROJ_TN),
        in_specs=[
            pl.BlockSpec((tm, D_MODEL), lambda i, j: (i, 0)),
            pl.BlockSpec((D_MODEL, PROJ_TN), lambda i, j: (0, j)),
            tab, tab, tab, tab,
        ],
        out_specs=pl.BlockSpec((tm, PROJ_TN), lambda i, j: (i, j)),
        out_shape=jax.ShapeDtypeStruct((tokens, N_IN), bf16),
        scratch_shapes=[pltpu.VMEM((tm, D_MODEL), bf16)],
        compiler_params=_params("parallel", "arbitrary"),
        name="in_proj_rotary",
    )(x2d, w_in_b, cos_a, sin_a, cos_r, sin_r)


def _moba_kernel(q_ref, k_ref, v_ref, o_ref, kmean_ref, vt_ref, selb_ref, gate_ref, *, n_blocks):
    i = pl.program_id(2)
    blk = MOBA_BLOCK
    hd = MOBA_HEAD_DIM
    width = q_ref.shape[1]
    chunks = width // LANES
    heads = width // hd
    per_chunk = LANES // hd

    def chunk_of(h):
        return slice(h // per_chunk * LANES, (h // per_chunk + 1) * LANES)

    @pl.when(i == 0)
    def _():
        seq = n_blocks * blk
        member = (lax.broadcasted_iota(i32, (n_blocks, seq), 1) // blk
                  == lax.broadcasted_iota(i32, (n_blocks, seq), 0))
        kmean_ref[...] = _dot(member.astype(bf16), k_ref[...]) * (1.0 / blk)
        for j in range(n_blocks):
            vt_ref[j] = v_ref[j * blk:(j + 1) * blk, :].astype(f32).T.astype(bf16)

    km_hi, km_lo = _split_bf16(kmean_ref[...])
    key_i = lax.broadcasted_iota(i32, (blk, blk), 0)
    qry_i = lax.broadcasted_iota(i32, (blk, blk), 1)
    causal = key_i <= qry_i
    bid = lax.broadcasted_iota(i32, (n_blocks, blk), 0)
    d_row = lax.broadcasted_iota(i32, (LANES, blk), 0)
    qt = q_ref[...].astype(f32).T

    kj = k_ref[pl.ds(pl.multiple_of(i * blk, blk), blk), :]
    vtj = vt_ref[i]
    qts = [None] * heads
    state = [None] * (3 * heads)

    def head_setup(h):
        r0 = h % per_chunk * hd
        in_head = (d_row >= r0) & (d_row < r0 + hd)
        qth = jnp.where(in_head, qt[chunk_of(h), :], 0.0).astype(bf16)
        qts[h] = qth
        gate = _dot(km_hi[:, chunk_of(h)], qth) + _dot(km_lo[:, chunk_of(h)], qth)
        s = jnp.where(causal, _dot(kj[:, chunk_of(h)], qth), NEG)
        yield
        g = jnp.where(bid < i, gate, -jnp.inf)
        gate_ref[h] = g
        beaten = jnp.zeros((n_blocks, blk), f32)
        for j in range(n_blocks):
            gj = jnp.broadcast_to(gate_ref[h, j:j + 1, :], (n_blocks, blk))
            beats = (gj > g) | ((gj == g) & (bid > j))
            beaten = beaten + jnp.where(beats, 1.0, 0.0)
        sel = (beaten < MOBA_TOPK) & (bid < i)
        selb_ref[h] = jnp.where(sel, 0.0, NEG)
        yield
        m0 = jnp.max(s, axis=0, keepdims=True)
        p = jnp.exp2(s - m0)
        l0 = jnp.sum(p, axis=0, keepdims=True)
        yield
        acc0 = _dot(vtj[h * hd:(h + 1) * hd, :], p.astype(bf16))
        state[3 * h:3 * h + 3] = [m0, l0, acc0]

    _interleave(head_setup(h) for h in range(heads))

    qt_alls = [jnp.concatenate(qts[c * per_chunk:(c + 1) * per_chunk], axis=1) for c in range(chunks)]

    def update(carry, js):
        nj = len(js)
        k_all = k_ref[pl.ds(pl.multiple_of(js[0] * blk, blk), nj * blk), :]
        s_alls = [_dot(k_all[:, c * LANES:(c + 1) * LANES], qt_alls[c]) for c in range(chunks)]
        vtjs = [vt_ref[j] for j in js]
        new = []
        for h in range(heads):
            m_old, l_old, acc = carry[3 * h:3 * h + 3]
            col = h % per_chunk * blk
            ss = [s_alls[h // per_chunk][n * blk:(n + 1) * blk, col:col + blk] + selb_ref[h, pl.ds(j, 1), :]
                  for n, j in enumerate(js)]
            m_new = m_old
            for s in ss:
                m_new = jnp.maximum(m_new, jnp.max(s, axis=0, keepdims=True))
            a = jnp.exp2(m_old - m_new)
            l_new = a * l_old
            acc = a * acc
            for s, vtj in zip(ss, vtjs):
                p = jnp.exp2(s - m_new)
                l_new = l_new + jnp.sum(p, axis=0, keepdims=True)
                acc = acc + _dot(vtj[h * hd:(h + 1) * hd, :], p.astype(bf16))
            new += [m_new, l_new, acc]
        return tuple(new)

    quads = i // 4
    fin = lax.fori_loop(0, quads, lambda n, c: update(c, tuple(4 * n + t for t in range(4))), tuple(state))
    done = 4 * quads
    has_pair = (i - done) // 2
    fin = lax.fori_loop(0, has_pair, lambda n, c: update(c, (done, done + 1)), fin)
    done = done + 2 * has_pair
    fin = lax.fori_loop(done, i, lambda j, c: update(c, (j,)), fin)
    out_t = jnp.concatenate([fin[3 * h + 2] / fin[3 * h + 1] for h in range(heads)], axis=0)
    o_ref[...] = out_t.T.astype(o_ref.dtype)


def _moba(proj, batch, seq):
    tokens = batch * seq
    n_blocks = seq // MOBA_BLOCK
    width = MOBA_STEP_WIDTH
    return pl.pallas_call(
        functools.partial(_moba_kernel, n_blocks=n_blocks),
        grid=(batch, MOBA_WIDTH // width, n_blocks),
        in_specs=[
            pl.BlockSpec((MOBA_BLOCK, width), lambda b, c, i: (b * n_blocks + i, OFF_QA // width + c)),
            pl.BlockSpec((seq, width), lambda b, c, i: (b, OFF_KA // width + c)),
            pl.BlockSpec((seq, width), lambda b, c, i: (b, OFF_VA // width + c)),
        ],
        out_specs=pl.BlockSpec((MOBA_BLOCK, width), lambda b, c, i: (b * n_blocks + i, c)),
        out_shape=jax.ShapeDtypeStruct((tokens, MOBA_WIDTH), bf16),
        scratch_shapes=[
            pltpu.VMEM((n_blocks, width), f32),
            pltpu.VMEM((n_blocks, width, MOBA_BLOCK), bf16),
            pltpu.VMEM((width // MOBA_HEAD_DIM, n_blocks, MOBA_BLOCK), f32),
            pltpu.VMEM((width // MOBA_HEAD_DIM, n_blocks, MOBA_BLOCK), f32),
        ],
        compiler_params=_params("parallel", "parallel", "arbitrary"),
        name="moba_attention",
    )(proj, proj, proj)


def _ret_kernel(q_ref, k_ref, v0_ref, v1_ref, g0_ref, g1_ref, decay_ref, zeta_ref, xi_ref, cd_ref,
                o_ref, state_ref):
    c = pl.program_id(1)
    heads_per_half = RET_HEADS // 2
    v_refs = (v0_ref, v1_ref)
    g_refs = (g0_ref, g1_ref)

    @pl.when(c == 0)
    def _():
        state_ref[...] = jnp.zeros_like(state_ref)

    for h in range(RET_HEADS):
        qk_cols = slice(h * RET_QK_DIM, (h + 1) * RET_QK_DIM)
        v_cols = slice(h * RET_V_DIM, (h + 1) * RET_V_DIM)
        half_cols = slice((h % heads_per_half) * RET_V_DIM, (h % heads_per_half + 1) * RET_V_DIM)
        q = q_ref[:, qk_cols]
        k = k_ref[:, qk_cols]
        v = v_refs[h // heads_per_half][:, half_cols]
        state = state_ref[h]
        scores = _dot_nt(q, k) * decay_ref[h]
        inner = _dot(scores.astype(bf16), v)
        cross = _dot(q, state.astype(bf16)) * xi_ref[h]
        o = inner + cross
        kz = (k.astype(f32) * zeta_ref[h]).astype(bf16)
        state_ref[h] = cd_ref[h] * state + _dot_tn(kz, v)

        mu = jnp.mean(o, axis=-1, keepdims=True)
        oc = o - mu
        var = jnp.mean(oc * oc, axis=-1, keepdims=True)
        on = oc * lax.rsqrt(var + GN_EPS)
        g = g_refs[h // heads_per_half][:, half_cols].astype(f32)
        o_ref[:, v_cols] = (on * (g * _sigmoid(g))).astype(o_ref.dtype)


def _retention(proj, batch, seq):
    tokens = batch * seq
    C = RET_CHUNK
    n_chunks = seq // C
    half_v = RET_V_WIDTH // 2
    gammas =1.0 - 2.0 ** (-5.0 - jnp.arange(RET_HEADS, dtype=f32))
    log_g = jnp.log(gammas)
    idx = jnp.arange(C, dtype=f32)
    diff = idx[:, None] - idx[None, :]
    decay = jnp.where(diff >= 0, jnp.exp(jnp.maximum(diff, 0.0)[None] * log_g[:, None, None]), 0.0)
    zeta = jnp.exp((C - 1 - idx)[None, :] * log_g[:, None])
    xi = jnp.exp((idx + 1.0)[None, :] * log_g[:, None])
    zeta_t = jnp.broadcast_to(zeta[:, :, None], (RET_HEADS, C, RET_QK_DIM))
    xi_t = jnp.broadcast_to(xi[:, :, None], (RET_HEADS, C, RET_V_DIM))
    cd_t = jnp.broadcast_to(jnp.exp(C * log_g)[:, None, None], (RET_HEADS, 1, RET_V_DIM))
    return pl.pallas_call(
        _ret_kernel,
        grid=(batch, n_chunks),
        in_specs=[
            pl.BlockSpec((C, RET_QK_WIDTH), lambda b, c: (b * n_chunks + c, OFF_QR // RET_QK_WIDTH)),
            pl.BlockSpec((C, RET_QK_WIDTH), lambda b, c: (b * n_chunks + c, OFF_KR // RET_QK_WIDTH)),
            pl.BlockSpec((C, half_v), lambda b, c: (b * n_chunks + c, OFF_VR // half_v)),
            pl.BlockSpec((C, half_v), lambda b, c: (b * n_chunks + c, OFF_VR // half_v + 1)),
            pl.BlockSpec((C, half_v), lambda b, c: (b * n_chunks + c, OFF_GR // half_v)),
            pl.BlockSpec((C, half_v), lambda b, c: (b * n_chunks + c, OFF_GR // half_v + 1)),
            pl.BlockSpec((RET_HEADS, C, C), lambda b, c: (0, 0, 0)),
            pl.BlockSpec((RET_HEADS, C, RET_QK_DIM), lambda b, c: (0, 0, 0)),
            pl.BlockSpec((RET_HEADS, C, RET_V_DIM), lambda b, c: (0, 0, 0)),
            pl.BlockSpec((RET_HEADS, 1, RET_V_DIM), lambda b, c: (0, 0, 0)),
        ],
        out_specs=pl.BlockSpec((C, RET_V_WIDTH), lambda b, c: (b * n_chunks + c, 0)),
        out_shape=jax.ShapeDtypeStruct((tokens, RET_V_WIDTH), bf16),
        scratch_shapes=[pltpu.VMEM((RET_HEADS, RET_QK_DIM, RET_V_DIM), f32)],
        compiler_params=_params("parallel", "arbitrary"),
        name="retention",
    )(proj, proj, proj, proj, proj, proj, decay, zeta_t, xi_t, cd_t)


def _merge_router_kernel(oa_ref, orr_ref, ga0_ref, ga1_ref, gb0_ref, gb1_ref, x_ref,
                         wm_ref, wr_ref, wo_ref, g1_ref, b1_ref, wrt_hi_ref, wrt_lo_ref, rb_ref,
                         h_ref, pos_ref, w_ref, tab_ref, cnt_ref, carry_ref):
    step = pl.program_id(0)
    tm = TOK_TM

    @pl.when(step == 0)
    def _():
        carry_ref[...] = jnp.zeros_like(carry_ref)

    _interleave(
        _merge_router_tile(t, slice(t * tm, (t + 1) * tm), tm, oa_ref, orr_ref, ga0_ref, ga1_ref, gb0_ref, gb1_ref,
                           x_ref, wm_ref, wr_ref, wo_ref, g1_ref, b1_ref, wrt_hi_ref, wrt_lo_ref, rb_ref,
                           h_ref, pos_ref, w_ref, tab_ref, cnt_ref, carry_ref)
        for t in range(x_ref.shape[0] // tm))


def _merge_router_tile(t, rows, tm, oa_ref, orr_ref, ga0_ref, ga1_ref, gb0_ref, gb1_ref, x_ref,
                       wm_ref, wr_ref, wo_ref, g1_ref, b1_ref, wrt_hi_ref, wrt_lo_ref, rb_ref,
                       h_ref, pos_ref, w_ref, tab_ref, cnt_ref, carry_ref):
    a = _dot(oa_ref[rows, :], wm_ref[...])
    r = _dot(orr_ref[rows, :], wr_ref[...])
    yield
    ga = jnp.concatenate([ga0_ref[rows, :], ga1_ref[rows, :]], axis=1).astype(f32)
    gb = jnp.concatenate([gb0_ref[rows, :], gb1_ref[rows, :]], axis=1).astype(f32)
    merged = _sigmoid(ga) * a + _sigmoid(gb) * r
    yield
    mix = _dot(merged.astype(bf16), wo_ref[...])
    yield
    h = _layer_norm(ALPHA * x_ref[rows, :] + mix, g1_ref[...], b1_ref[...])
    h_ref[rows, :] = h
    yield

    h_hi, h_lo = _split_bf16(h)
    w_hi = wrt_hi_ref[...]
    logits = _dot_nt(w_hi, h_hi) + _dot_nt(w_hi, h_lo) + _dot_nt(wrt_lo_ref[...], h_hi)
    yield
    scores = _sigmoid(logits)
    biased = scores + rb_ref[...]

    v = biased.reshape(N_GROUPS, GROUP_SIZE, tm)
    sub = lax.broadcasted_iota(i32, v.shape, 1)
    m1 = jnp.max(v, axis=1, keepdims=True)
    i1 = jnp.min(jnp.where(v == m1, sub, GROUP_SIZE), axis=1, keepdims=True)
    m2 = jnp.max(jnp.where(sub == i1, -jnp.inf, v), axis=1, keepdims=True)
    gscore = jnp.broadcast_to(m1 + m2, v.shape).reshape(N_EXPERTS, tm)

    eid = lax.broadcasted_iota(i32, (N_EXPERTS, tm), 0)
    egrp = eid // GROUP_SIZE
    e_mask = jnp.zeros((N_EXPERTS, tm), jnp.bool_)
    for _ in range(TOPK_GROUPS):
        m = jnp.max(gscore, axis=0, keepdims=True)
        idx = jnp.min(jnp.where(gscore == m, egrp, N_GROUPS), axis=0, keepdims=True)
        hit = egrp == idx
        e_mask = e_mask | hit
        gscore = jnp.where(hit, -jnp.inf, gscore)
    cand = jnp.where(e_mask, biased, -jnp.inf)

    chosen = jnp.zeros((N_EXPERTS, tm), jnp.bool_)
    e_rows = []
    w_rows = []
    for _ in range(TOP_K):
        m = jnp.max(cand, axis=0, keepdims=True)
        idx = jnp.min(jnp.where(cand == m, eid, N_EXPERTS), axis=0, keepdims=True)
        hit = eid == idx
        chosen = chosen | hit
        e_rows.append(idx)
        w_rows.append(jnp.sum(jnp.where(hit, scores, 0.0), axis=0, keepdims=True))
        cand = jnp.where(hit, -jnp.inf, cand)
    w_sum = w_rows[0]
    for wk in w_rows[1:]:
        w_sum = w_sum + wk

    t_src = lax.broadcasted_iota(i32, (tm, tm), 0)
    t_dst = lax.broadcasted_iota(i32, (tm, tm), 1)
    before = (t_src < t_dst).astype(bf16)
    chosen_f = chosen.astype(f32)
    prior = _dot(chosen_f.astype(bf16), before)
    cnt = jnp.sum(chosen_f, axis=1, keepdims=True)
    cnt_pad = jnp.ceil(cnt * (1.0 / UNIT)) * UNIT
    e_src = lax.broadcasted_iota(i32, (N_EXPERTS, N_EXPERTS), 1)
    e_dst = lax.broadcasted_iota(i32, (N_EXPERTS, N_EXPERTS), 0)
    earlier = (e_src < e_dst).astype(bf16)
    cnt_pad_l = jnp.broadcast_to(cnt_pad, (N_EXPERTS, LANES))
    loc_start_l = _dot(earlier, cnt_pad_l.astype(bf16))
    loc_start = loc_start_l[:, 0:1]
    where_e = prior + loc_start
    pos_ref[:, rows] = jnp.zeros((pos_ref.shape[0], tm), pos_ref.dtype)
    w_ref[:, rows] = jnp.zeros((w_ref.shape[0], tm), w_ref.dtype)
    for k in range(TOP_K):
        pos_ref[k:k + 1, rows] = jnp.sum(jnp.where(eid == e_rows[k], where_e, 0.0), axis=0, keepdims=True).astype(i32)
        w_ref[k:k + 1, rows] = w_rows[k] / w_sum * ROUTED_SCALE

    carry = carry_ref[...]
    tl = lax.broadcasted_iota(i32, (N_EXPERTS, LANES), 1)
    table = jnp.where(tl == 0, cnt_pad_l, jnp.where(tl == 1, loc_start_l, jnp.broadcast_to(carry, (N_EXPERTS, LANES))))
    tab_ref[t] = table.astype(i32)
    carry = carry + cnt_pad
    carry_ref[...] = carry
    cnt_ref[...] = jnp.broadcast_to(carry, cnt_ref.shape).astype(i32)


def _merge_router(oa, orr, proj, x2d, wm, wr, wo, g1, b1, wrt_hi, wrt_lo, rb):
    tokens = x2d.shape[0]
    tm = MERGE_TILES * TOK_TM
    half = D_MODEL // 2

    def gate_spec(off):
        return pl.BlockSpec((tm, half), lambda i: (i, off // half))

    def full(shape):
        return pl.BlockSpec(shape, lambda i: (0,) * len(shape))

    row8 = pl.BlockSpec((8, tm), lambda i: (0, i))
    return pl.pallas_call(
        _merge_router_kernel,
        grid=(tokens // tm,),
        in_specs=[
            pl.BlockSpec((tm, MOBA_WIDTH), lambda i: (i, 0)),
            pl.BlockSpec((tm, RET_V_WIDTH), lambda i: (i, 0)),
            gate_spec(OFF_GA), gate_spec(OFF_GA + half), gate_spec(OFF_GB), gate_spec(OFF_GB + half),
            pl.BlockSpec((tm, D_MODEL), lambda i: (i, 0)),
            full((MOBA_WIDTH, D_MODEL)), full((RET_V_WIDTH, D_MODEL)), full((D_MODEL, D_MODEL)),
            full((1, D_MODEL)), full((1, D_MODEL)),
            full((N_EXPERTS, D_MODEL)), full((N_EXPERTS, D_MODEL)), full((N_EXPERTS, 1)),
        ],
        out_specs=[
            pl.BlockSpec((tm, D_MODEL), lambda i: (i, 0)),
            row8, row8,
            pl.BlockSpec((MERGE_TILES, N_EXPERTS, LANES), lambda i: (i, 0, 0)),
            full((N_EXPERTS, LANES)),
        ],
        out_shape=[
            jax.ShapeDtypeStruct((tokens, D_MODEL), f32),
            jax.ShapeDtypeStruct((8, tokens), i32),
            jax.ShapeDtypeStruct((8, tokens), f32),
            jax.ShapeDtypeStruct((tokens // TOK_TM, N_EXPERTS, LANES), i32),
            jax.ShapeDtypeStruct((N_EXPERTS, LANES), i32),
        ],
        scratch_shapes=[pltpu.VMEM((N_EXPERTS, 1), f32)],
        compiler_params=_params("arbitrary"),
        name="merge_ln1_router",
    )(oa, orr, proj, proj, proj, proj, x2d, wm, wr, wo, g1, b1, wrt_hi, wrt_lo, rb)


def _pack_rows(x):
    bits = lax.bitcast_convert_type(x, jnp.uint32)
    return (bits[:, :HALF] & jnp.uint32(0xFFFF0000)) | (bits[:, HALF:] >> 16)


def _unpack_rows(words):
    left = lax.bitcast_convert_type(words & jnp.uint32(0xFFFF0000), f32)
    right = lax.bitcast_convert_type(words << 16, f32)
    return left.astype(bf16), right.astype(bf16)


def _start_alternating(n, make_copy):
    def pair(m, _):
        make_copy(2 * m).start(priority=0)
        make_copy(2 * m + 1).start(priority=1)
        return 0

    lax.fori_loop(0, n // 2, pair, 0)

    @pl.when(n % 2 == 1)
    def _():
        make_copy(n - 1).start(priority=0)


def _move_runs(tile, n_ref, src_ref, dst_ref, make_copy):
    def piece(c, p):
        rows = UNIT << c
        base = (tile * RUN_CLASSES + c) * N_EXPERTS
        loc = pl.ds(pl.multiple_of(src_ref[base + p] * UNIT, UNIT), rows)
        glob = pl.ds(pl.multiple_of(dst_ref[base + p] * UNIT, UNIT), rows)
        return make_copy(loc, glob)

    for c in range(RUN_CLASSES):
        _start_alternating(n_ref[tile * RUN_CLASSES + c], functools.partial(piece, c))

    for c in range(RUN_CLASSES):
        def wait_one(p, _, c=c):
            piece(c, p).wait()
            return 0

        lax.fori_loop(0, n_ref[tile * RUN_CLASSES + c], wait_one, 0)


def _dispatch_kernel(n_ref, src_ref, dst_ref, tail_ref, used_ref, h_ref, pos_ref, rows_ref,
                     xs_ref, zero_ref, sem_ref, zsem_ref):
    step = pl.program_id(0)
    tm = TOK_TM
    tiles = h_ref.shape[0] // tm

    @pl.when(step == 0)
    def _():
        zero_ref[...] = jnp.zeros_like(zero_ref)

        def clear_copy(block):
            start = pl.multiple_of(block * ROW_BLOCK, ROW_BLOCK)
            return pltpu.make_async_copy(zero_ref, rows_ref.at[pl.ds(start, ROW_BLOCK), :], zsem_ref)

        def clear_all(act):
            def tail(e, _):
                tail_block = tail_ref[e]

                @pl.when(tail_block >= 0)
                def _():
                    act(clear_copy(tail_block))
                return 0

            lax.fori_loop(0, N_EXPERTS, tail, 0)

            def unused(b, _):
                act(clear_copy(b))
                return 0

            lax.fori_loop(used_ref[0], rows_ref.shape[0] // ROW_BLOCK, unused, 0)

        clear_all(lambda cp: cp.start())
        clear_all(lambda cp: cp.wait())

    def sort_tile(u):
        cols = slice(u * tm, (u + 1) * tm)
        pos = pos_ref[:, cols]
        r_iota = lax.broadcasted_iota(i32, (LOC_ROWS, tm), 0)
        onehot = r_iota == pos[0:1, :]
        for k in range(1, TOP_K):
            onehot = onehot | (r_iota == pos[k:k + 1, :])
        onehot = onehot.astype(bf16)
        yield
        xs = _dot(onehot, h_ref[cols, :].astype(bf16))
        yield
        xs_ref[u] = _pack_rows(xs)

    _interleave(sort_tile(u) for u in range(tiles))

    for u in range(tiles):
        _move_runs(step * tiles + u, n_ref, src_ref, dst_ref,
                   lambda loc, glob, u=u: pltpu.make_async_copy(xs_ref.at[u, loc, :], rows_ref.at[glob, :], sem_ref))


def _dispatch(runs, tail_block, n_used, h, pos8, n_rows):
    tokens = h.shape[0]
    tm = MOE_TILES * TOK_TM
    grid_spec = pltpu.PrefetchScalarGridSpec(
        num_scalar_prefetch=5,
        grid=(tokens // tm,),
        in_specs=[
            pl.BlockSpec((tm, D_MODEL), lambda i, *_: (i, 0)),
            pl.BlockSpec((8, tm), lambda i, *_: (0, i)),
        ],
        out_specs=pl.BlockSpec(memory_space=pl.ANY),
        scratch_shapes=[
            pltpu.VMEM((MOE_TILES, LOC_ROWS, HALF), jnp.uint32),
            pltpu.VMEM((ROW_BLOCK, HALF), jnp.uint32),
            pltpu.SemaphoreType.DMA,
            pltpu.SemaphoreType.DMA,
        ],
    )
    return pl.pallas_call(
        _dispatch_kernel,
        grid_spec=grid_spec,
        out_shape=jax.ShapeDtypeStruct((n_rows, HALF), jnp.uint32),
        compiler_params=_params("arbitrary"),
        name="moe_dispatch",
    )(*runs, tail_block, n_used, h, pos8)


def _expert_kernel(eid_ref, used_ref, x_ref, wg_ref, wu_ref, wd_ref, y_ref, wgb_ref, wub_ref, wdb_ref):
    b = pl.program_id(0)
    changed = (b == 0) | (eid_ref[b] != eid_ref[jnp.maximum(b - 1, 0)])

    @pl.when(changed)
    def _():
        wgb_ref[...] = wg_ref[0].astype(bf16)
        wub_ref[...] = wu_ref[0].astype(bf16)
        wdb_ref[...] = wd_ref[0].astype(bf16)

    @pl.when(b < used_ref[0])
    def _():
        def chunk_phases(c):
            rows = slice(c * EXPERT_CHUNK, (c + 1) * EXPERT_CHUNK)
            xl, xr = _unpack_rows(x_ref[rows, :])
            yield
            g = _dot(xl, wgb_ref[:HALF, :]) + _dot(xr, wgb_ref[HALF:, :])
            u = _dot(xl, wub_ref[:HALF, :]) + _dot(xr, wub_ref[HALF:, :])
            yield
            mid = ((g * _sigmoid(g)) * u).astype(bf16)
            yield
            y = _dot(mid, wdb_ref[...])
            yield
            y_ref[rows, :] = _pack_rows(y.astype(bf16).astype(f32))

        _interleave(chunk_phases(c) for c in range(ROW_BLOCK // EXPERT_CHUNK))

    @pl.when(b >= used_ref[0])
    def _():
        y_ref[...] = jnp.zeros_like(y_ref)


def _experts(block_eid, n_used, rows, wg, wu, wd):
    n_rows = rows.shape[0]
    n_blocks = n_rows // ROW_BLOCK

    def row_map(b, eid, used):
        return (b, 0)

    def w_map(b, eid, used):
        return (eid[b], 0, 0)

    grid_spec = pltpu.PrefetchScalarGridSpec(
        num_scalar_prefetch=2,
        grid=(n_blocks,),
        in_specs=[
            pl.BlockSpec((ROW_BLOCK, HALF), row_map),
            pl.BlockSpec((1, D_MODEL, D_EXPERT), w_map),
            pl.BlockSpec((1, D_MODEL, D_EXPERT), w_map),
            pl.BlockSpec((1, D_EXPERT, D_MODEL), w_map),
        ],
        out_specs=pl.BlockSpec((ROW_BLOCK, HALF), row_map),
        scratch_shapes=[
            pltpu.VMEM((D_MODEL, D_EXPERT), bf16),
            pltpu.VMEM((D_MODEL, D_EXPERT), bf16),
            pltpu.VMEM((D_EXPERT, D_MODEL), bf16),
        ],
    )
    return pl.pallas_call(
        _expert_kernel,
        grid_spec=grid_spec,
        out_shape=jax.ShapeDtypeStruct((n_rows, HALF), jnp.uint32),
        compiler_params=_params("arbitrary"),
        name="moe_experts",
    )(block_eid, n_used, rows, wg, wu, wd)


def _combine_kernel(n_ref, src_ref, dst_ref, y_ref, h_ref, pos_ref, wt_ref, p_ref, wsg_ref, wsu_ref, wsd_ref,
                    wpp_ref, wpg_ref, g2_ref, b2_ref, g3_ref, b3_ref, o_ref, ybuf_ref, sem_ref):
    step = pl.program_id(0)
    tm = TOK_TM
    tiles = h_ref.shape[0] // tm

    @pl.when(step == 0)
    def _():
        ybuf_ref[...] = jnp.zeros_like(ybuf_ref)

    for u in range(tiles):
        _move_runs(step * tiles + u, n_ref, src_ref, dst_ref,
                   lambda loc, glob, u=u: pltpu.make_async_copy(y_ref.at[glob, :], ybuf_ref.at[u, loc, :], sem_ref))

    def tile_phases(u):
        rows = slice(u * tm, (u + 1) * tm)
        h = h_ref[rows, :]
        hb = h.astype(bf16)
        sg = _dot(hb, wsg_ref[...])
        su = _dot(hb, wsu_ref[...])
        yield
        mid = ((sg * _sigmoid(sg)) * su).astype(bf16)
        yield
        shared = _dot(mid, wsd_ref[...])
        ple_in = _dot(p_ref[rows, :].astype(bf16), wpp_ref[...])
        yield
        pos = pos_ref[rows, :]
        wt = wt_ref[rows, :]
        c_iota = lax.broadcasted_iota(i32, (tm, LOC_ROWS), 1)
        spread = jnp.where(c_iota == pos[:, 0:1], wt[:, 0:1], 0.0)
        for k in range(1, TOP_K):
            spread = jnp.where(c_iota == pos[:, k:k + 1], wt[:, k:k + 1], spread)
        spread = spread.astype(bf16)
        yield
        yl, yr = _unpack_rows(ybuf_ref[u])
        yield
        routed = jnp.concatenate([_dot(spread, yl), _dot(spread, yr)], axis=1)
        yield
        h2 = _layer_norm(ALPHA * h + (routed + shared), g2_ref[...], b2_ref[...])
        yield
        gate = _dot(h2.astype(bf16), wpg_ref[...])
        yield
        ple = ple_in * _sigmoid(gate)
        o_ref[rows, :] = _layer_norm(ALPHA * h2 + ple, g3_ref[...], b3_ref[...])

    _interleave(tile_phases(u) for u in range(tiles))


def _combine(runs, y_rows, h, pos_tok, w_tok, p2d, wsg, wsu, wsd, wpp, wpg, g2, b2, g3, b3):
    tokens = h.shape[0]
    tm = MOE_TILES * TOK_TM

    def full(shape):
        return pl.BlockSpec(shape, lambda i, *_: (0,) * len(shape))

    def tile(width):
        return pl.BlockSpec((tm, width), lambda i, *_: (i, 0))

    grid_spec = pltpu.PrefetchScalarGridSpec(
        num_scalar_prefetch=3,
        grid=(tokens // tm,),
        in_specs=[
            pl.BlockSpec(memory_space=pl.ANY),
            tile(D_MODEL), tile(8), tile(8), tile(PLE_DIM),
            full((D_MODEL, D_SHARED)), full((D_MODEL, D_SHARED)), full((D_SHARED, D_MODEL)),
            full((PLE_DIM, D_MODEL)), full((D_MODEL, D_MODEL)),
            full((1, D_MODEL)), full((1, D_MODEL)), full((1, D_MODEL)), full((1, D_MODEL)),
        ],
        out_specs=tile(D_MODEL),
        scratch_shapes=[
            pltpu.VMEM((MOE_TILES, LOC_ROWS, HALF), jnp.uint32),
            pltpu.SemaphoreType.DMA,
        ],
    )
    return pl.pallas_call(
        _combine_kernel,
        grid_spec=grid_spec,
        out_shape=jax.ShapeDtypeStruct((tokens, D_MODEL), f32),
        compiler_params=_params("arbitrary"),
        name="moe_combine_ln2_ple_ln3",
    )(*runs, y_rows, h, pos_tok, w_tok, p2d, wsg, wsu, wsd, wpp, wpg, g2, b2, g3, b3)


def _row_layout(table, totals, n_blocks):
    cnt_u = table[:, :, 0] // UNIT
    loc_u = table[:, :, 1] // UNIT
    padded = (totals + ROW_BLOCK - 1) // ROW_BLOCK * ROW_BLOCK
    pends = jnp.cumsum(padded)
    pstarts = pends - padded
    glob_u = (pstarts[None, :] + table[:, :, 2]) // UNIT
    cls = jnp.arange(RUN_CLASSES, dtype=i32)
    has = (cnt_u[:, None, :] >> cls[None, :, None]) & 1
    off = (cnt_u[:, None, :] >> (cls[None, :, None] + 1)) << (cls[None, :, None] + 1)
    upto = jnp.cumsum(has, axis=2)
    n_pieces = upto[:, :, -1].astype(i32).reshape(-1)
    slot = jnp.arange(N_EXPERTS, dtype=i32)
    e_of = jnp.sum((upto[:, :, None, :] <= slot[None, None, :, None]).astype(i32), axis=-1)
    is_e = e_of[:, :, :, None] == jnp.arange(N_EXPERTS, dtype=i32)
    src = jnp.sum(jnp.where(is_e, (loc_u[:, None, :] + off)[:, :, None, :], 0), axis=-1).astype(i32).reshape(-1)
    dst = jnp.sum(jnp.where(is_e, (glob_u[:, None, :] + off)[:, :, None, :], 0), axis=-1).astype(i32).reshape(-1)
    runs = (n_pieces, src, dst)
    tail_block = jnp.where(totals > 0, pends // ROW_BLOCK - 1, -1).astype(i32)
    block_start = jnp.arange(n_blocks, dtype=i32) * ROW_BLOCK
    ends_before = jnp.sum((pends[None, :] <= block_start[:, None]).astype(i32), axis=1)
    block_eid = jnp.minimum(ends_before, N_EXPERTS - 1).astype(i32)
    n_used = (pends[-1:] // ROW_BLOCK).astype(i32)
    return runs, tail_block, block_eid, n_used


def _layer(x2d, p2d, batch, seq, w_in, w_moba_up, w_ret_up, w_out, ln1_g, ln1_b, w_router, router_bias,
           w_eg, w_eu, w_ed, w_sg, w_su, w_sd, ln2_g, ln2_b, w_ple_proj, w_ple_gate, ln3_g, ln3_b):
    tokens = batch * seq
    row = lambda a: a.reshape(1, -1).astype(f32)

    proj = _project(x2d, w_in.astype(bf16), seq)
    oa = _moba(proj, batch, seq)
    orr = _retention(proj, batch, seq)

    wrt = w_router.astype(f32).T
    wrt_hi = wrt.astype(bf16)
    wrt_lo = (wrt - wrt_hi.astype(f32)).astype(bf16)
    h1, pos8, w8, table, totals = _merge_router(
        oa, orr, proj, x2d, w_moba_up.astype(bf16), w_ret_up.astype(bf16), w_out.astype(bf16),
        row(ln1_g), row(ln1_b), wrt_hi, wrt_lo, router_bias.astype(f32).reshape(N_EXPERTS, 1))

    n_tiles = tokens // TOK_TM
    max_rows = tokens * TOP_K + n_tiles * N_EXPERTS * (UNIT - 1) + N_EXPERTS * (ROW_BLOCK - 1)
    n_blocks = -(-max_rows // ROW_BLOCK)
    runs, tail_block, block_eid, n_used = _row_layout(table[:, :, :3], totals[:, 0], n_blocks)

    rows = _dispatch(runs, tail_block, n_used, h1, pos8, n_blocks * ROW_BLOCK)
    y_rows = _experts(block_eid, n_used, rows, w_eg, w_eu, w_ed)
    return _combine(runs, y_rows, h1, pos8.T, w8.T, p2d,
                    w_sg.astype(bf16), w_su.astype(bf16), w_sd.astype(bf16),
                    w_ple_proj.astype(bf16), w_ple_gate.astype(bf16),
                    row(ln2_g), row(ln2_b), row(ln3_g), row(ln3_b))


def kernel(x, p, w_in, w_moba_up, w_ret_up, w_out, ln1_g, ln1_b, w_router, router_bias, w_exp_gate, w_exp_up,
           w_exp_down, w_sh_gate, w_sh_up, w_sh_down, ln2_g, ln2_b, w_ple_proj, w_ple_gate, ln3_g, ln3_b):
    batch, seq, d = x.shape
    assert d == D_MODEL and seq % max(MOBA_BLOCK, RET_CHUNK) == 0 and (batch * seq) % (MERGE_TILES * TOK_TM) == 0
    assert w_in.shape[0] == DEPTH
    h = x.reshape(batch * seq, d)
    for i in range(DEPTH):
        h = _layer(h, p[i].reshape(batch * seq, PLE_DIM), batch, seq,
                   w_in[i], w_moba_up[i], w_ret_up[i], w_out[i], ln1_g[i], ln1_b[i], w_router[i], router_bias[i],
                   w_exp_gate[i], w_exp_up[i], w_exp_down[i], w_sh_gate[i], w_sh_up[i], w_sh_down[i],
                   ln2_g[i], ln2_b[i], w_ple_proj[i], w_ple_gate[i], ln3_g[i], ln3_b[i])
    return h.reshape(batch, seq, d)
```

```python
import functools

import jax
import jax.numpy as jnp
from jax import lax
from jax.experimental import pallas as pl
from jax.experimental.pallas import tpu as pltpu

f32 = jnp.float32
bf16 = jnp.bfloat16
i32 = jnp.int32

D_MODEL = 1024
DEPTH = 1
MOBA_HEADS = 8
MOBA_HEAD_DIM = 64
MOBA_WIDTH = MOBA_HEADS * MOBA_HEAD_DIM
MOBA_BLOCK = 256
MOBA_TOPK = 3
ROPE_THETA = 10000.0
RET_HEADS = 4
RET_QK_DIM = 128
RET_V_DIM = 256
RET_QK_WIDTH = RET_HEADS * RET_QK_DIM
RET_V_WIDTH = RET_HEADS * RET_V_DIM
RET_CHUNK = 256
RET_ANGLE_BASE = 10000.0
N_IN = 3 * MOBA_WIDTH + 2 * RET_QK_WIDTH + 2 * RET_V_WIDTH + 2 * D_MODEL
N_EXPERTS = 64
TOP_K = 6
N_GROUPS = 8
TOPK_GROUPS = 4
GROUP_SIZE = N_EXPERTS // N_GROUPS
D_EXPERT = 256
D_SHARED = 256
ROUTED_SCALE = 2.5
PLE_DIM = 256
LN_EPS = 1e-5
GN_EPS = 1e-6
ALPHA = (2.0 * DEPTH) ** 0.25

OFF_QA = 0
OFF_KA = MOBA_WIDTH
OFF_VA = 2 * MOBA_WIDTH
OFF_QR = 3 * MOBA_WIDTH
OFF_KR = OFF_QR + RET_QK_WIDTH
OFF_VR = OFF_KR + RET_QK_WIDTH
OFF_GR = OFF_VR + RET_V_WIDTH
OFF_GA = OFF_GR + RET_V_WIDTH
OFF_GB = OFF_GA + D_MODEL

MOBA_Q_SCALE = MOBA_HEAD_DIM ** -0.5 * 1.4426950408889634

LANES = 128
NEG = -1e30
VMEM_LIMIT = 56 * 1024 * 1024

PROJ_TM = 2048
PROJ_PARTS = 8
PROJ_TN = 512
MOBA_STEP_WIDTH = 512
MOE_TILES = 2
MERGE_TILES = 4
TOK_TM = 256
ROW_BLOCK = 512
EXPERT_CHUNK = 256
UNIT = 8
LOC_ROWS = -(-(TOP_K * TOK_TM + N_EXPERTS * (UNIT - 1)) // LANES) * LANES
RUN_CLASSES = (TOK_TM // UNIT).bit_length()
HALF = D_MODEL // 2


def _dot(a, b):
    return jnp.dot(a, b, preferred_element_type=f32)


def _dot_nt(a, b):
    return lax.dot_general(a, b, (((1,), (1,)), ((), ())), preferred_element_type=f32)


def _dot_tn(a, b):
    return lax.dot_general(a, b, (((0,), (0,)), ((), ())), preferred_element_type=f32)


def _split_bf16(a):
    hi = a.astype(bf16)
    lo = (a - hi.astype(f32)).astype(bf16)
    return hi, lo


def _layer_norm(x, g, b):
    mu = jnp.mean(x, axis=-1, keepdims=True)
    xc = x - mu
    var = jnp.mean(xc * xc, axis=-1, keepdims=True)
    return xc * lax.rsqrt(var + LN_EPS) * g + b


def _sigmoid(x):
    return 1.0 / (1.0 + jnp.exp(-x))


def _interleave(chains):
    chains = list(chains)
    while chains:
        chains = [chain for chain in chains if next(chain, "done") != "done"]


def _params(*sem):
    return pltpu.CompilerParams(dimension_semantics=sem, vmem_limit_bytes=VMEM_LIMIT)


def _rotate_half_chunk(xc, half):
    if 2 * half == LANES:
        return pltpu.roll(xc, half, axis=1)
    lane = lax.broadcasted_iota(i32, xc.shape, 1)
    first = (lane & (2 * half - 1)) < half
    return jnp.where(first, pltpu.roll(xc, LANES - half, axis=1), pltpu.roll(xc, half, axis=1))


def _proj_kernel(x_ref, w_ref, cos_a_ref, sin_a_ref, cos_r_ref, sin_r_ref, o_ref, xb_ref):
    j = pl.program_id(1)

    @pl.when(j == 0)
    def _():
        xb_ref[...] = x_ref[...].astype(bf16)

    part = x_ref.shape[0] // PROJ_PARTS

    def run(epilogue):
        def chain(r):
            rows = slice(r * part, (r + 1) * part)
            acc = _dot(xb_ref[rows, :], w_ref[...])
            yield
            epilogue(acc, rows)

        _interleave(chain(r) for r in range(PROJ_PARTS))

    def rotary(cos_ref, sin_ref, half, scale):
        def epilogue(acc, rows):
            cos = cos_ref[rows, :]
            sin = sin_ref[rows, :]
            for c in range(PROJ_TN // LANES):
                xc = acc[:, c * LANES:(c + 1) * LANES]
                y = xc * cos + _rotate_half_chunk(xc, half) * sin
                if scale != 1.0:
                    y = y * scale
                o_ref[rows, c * LANES:(c + 1) * LANES] = y.astype(o_ref.dtype)

        run(epilogue)

    def plain_epilogue(acc, rows):
        o_ref[rows, :] = acc.astype(o_ref.dtype)

    j_qa = OFF_QA // PROJ_TN
    j_ka = OFF_KA // PROJ_TN
    j_qr = OFF_QR // PROJ_TN
    j_kr = OFF_KR // PROJ_TN

    @pl.when(j == j_qa)
    def _():
        rotary(cos_a_ref, sin_a_ref, MOBA_HEAD_DIM // 2, MOBA_Q_SCALE)

    @pl.when(j == j_ka)
    def _():
        rotary(cos_a_ref, sin_a_ref, MOBA_HEAD_DIM // 2, 1.0)

    @pl.when(j == j_qr)
    def _():
        rotary(cos_r_ref, sin_r_ref, RET_QK_DIM // 2, 1.0)

    @pl.when(j == j_kr)
    def _():
        rotary(cos_r_ref, sin_r_ref, RET_QK_DIM // 2, RET_QK_DIM ** -0.5)

    plain = (j != j_qa) & (j != j_ka) & (j != j_qr) & (j != j_kr)

    @pl.when(plain)
    def _():
        run(plain_epilogue)


def _rotary_tables(seq, inv_freq, head_dim):
    ang = jnp.arange(seq, dtype=f32)[:, None] * inv_freq[None, :]
    cos = jnp.cos(ang)
    sin = jnp.sin(ang)
    cos_h = jnp.concatenate([cos, cos], axis=-1)
    sin_h = jnp.concatenate([-sin, sin], axis=-1)
    reps = LANES // head_dim
    return jnp.tile(cos_h, (1, reps)), jnp.tile(sin_h, (1, reps))


def _project(x2d, w_in_b, seq):
    tokens = x2d.shape[0]
    tm = min(PROJ_TM, seq)
    inv_a = 1.0 / (ROPE_THETA ** (jnp.arange(0, MOBA_HEAD_DIM, 2, dtype=f32) / MOBA_HEAD_DIM))
    inv_r = 1.0 / (RET_ANGLE_BASE ** jnp.linspace(0.0, 1.0, RET_QK_DIM // 2, dtype=f32))
    cos_a, sin_a = _rotary_tables(seq, inv_a, MOBA_HEAD_DIM)
    cos_r, sin_r = _rotary_tables(seq, inv_r, RET_QK_DIM)
    seq_tiles = seq // tm
    tab = pl.BlockSpec((tm, LANES), lambda i, j: (i % seq_tiles, 0))
    return pl.pallas_call(
        _proj_kernel,
        grid=(tokens // tm, N_IN // PROJ_TN),
        in_specs=[
            pl.BlockSpec((tm, D_MODEL), lambda i, j: (i, 0)),
            pl.BlockSpec((D_MODEL, PROJ_TN), lambda i, j: (0, j)),
            tab, tab, tab, tab,
        ],
        out_specs=pl.BlockSpec((tm, PROJ_TN), lambda i, j: (i, j)),
        out_shape=jax.ShapeDtypeStruct((tokens, N_IN), bf16),
        scratch_shapes=[pltpu.VMEM((tm, D_MODEL), bf16)],
        compiler_params=_params("parallel", "arbitrary"),
        name="in_proj_rotary",
    )(x2d, w_in_b, cos_a, sin_a, cos_r, sin_r)


def _moba_kernel(q_ref, k_ref, v_ref, o_ref, kmean_ref, vt_ref, selb_ref, gate_ref, *, n_blocks):
    i = pl.program_id(2)
    blk = MOBA_BLOCK
    hd = MOBA_HEAD_DIM
    width = q_ref.shape[1]
    chunks = width // LANES
    heads = width // hd
    per_chunk = LANES // hd

    def chunk_of(h):
        return slice(h // per_chunk * LANES, (h // per_chunk + 1) * LANES)

    @pl.when(i == 0)
    def _():
        seq = n_blocks * blk
        member = (lax.broadcasted_iota(i32, (n_blocks, seq), 1) // blk
                  == lax.broadcasted_iota(i32, (n_blocks, seq), 0))
        kmean_ref[...] = _dot(member.astype(bf16), k_ref[...]) * (1.0 / blk)
        for j in range(n_blocks):
            vt_ref[j] = v_ref[j * blk:(j + 1) * blk, :].astype(f32).T.astype(bf16)

    km_hi, km_lo = _split_bf16(kmean_ref[...])
    key_i = lax.broadcasted_iota(i32, (blk, blk), 0)
    qry_i = lax.broadcasted_iota(i32, (blk, blk), 1)
    causal = key_i <= qry_i
    bid = lax.broadcasted_iota(i32, (n_blocks, blk), 0)
    d_row = lax.broadcasted_iota(i32, (LANES, blk), 0)
    qt = q_ref[...].astype(f32).T

    kj = k_ref[pl.ds(pl.multiple_of(i * blk, blk), blk), :]
    vtj = vt_ref[i]
    qts = [None] * heads
    state = [None] * (3 * heads)

    def head_setup(h):
        r0 = h % per_chunk * hd
        in_head = (d_row >= r0) & (d_row < r0 + hd)
        qth = jnp.where(in_head, qt[chunk_of(h), :], 0.0).astype(bf16)
        qts[h] = qth
        gate = _dot(km_hi[:, chunk_of(h)], qth) + _dot(km_lo[:, chunk_of(h)], qth)
        s = jnp.where(causal, _dot(kj[:, chunk_of(h)], qth), NEG)
        yield
        g = jnp.where(bid < i, gate, -jnp.inf)
        gate_ref[h] = g
        beaten = jnp.zeros((n_blocks, blk), f32)
        for j in range(n_blocks):
            gj = jnp.broadcast_to(gate_ref[h, j:j + 1, :], (n_blocks, blk))
            beats = (gj > g) | ((gj == g) & (bid > j))
            beaten = beaten + jnp.where(beats, 1.0, 0.0)
        sel = (beaten < MOBA_TOPK) & (bid < i)
        selb_ref[h] = jnp.where(sel, 0.0, NEG)
        yield
        m0 = jnp.max(s, axis=0, keepdims=True)
        p = jnp.exp2(s - m0)
        l0 = jnp.sum(p, axis=0, keepdims=True)
        yield
        acc0 = _dot(vtj[h * hd:(h + 1) * hd, :], p.astype(bf16))
        state[3 * h:3 * h + 3] = [m0, l0, acc0]

    _interleave(head_setup(h) for h in range(heads))

    qt_alls = [jnp.concatenate(qts[c * per_chunk:(c + 1) * per_chunk], axis=1) for c in range(chunks)]

    def update(carry, js):
        nj = len(js)
        k_all = k_ref[pl.ds(pl.multiple_of(js[0] * blk, blk), nj * blk), :]
        s_alls = [_dot(k_all[:, c * LANES:(c + 1) * LANES], qt_alls[c]) for c in range(chunks)]
        vtjs = [vt_ref[j] for j in js]
        new = []
        for h in range(heads):
            m_old, l_old, acc = carry[3 * h:3 * h + 3]
            col = h % per_chunk * blk
            ss = [s_alls[h // per_chunk][n * blk:(n + 1) * blk, col:col + blk] + selb_ref[h, pl.ds(j, 1), :]
                  for n, j in enumerate(js)]
            m_new = m_old
            for s in ss:
                m_new = jnp.maximum(m_new, jnp.max(s, axis=0, keepdims=True))
            a = jnp.exp2(m_old - m_new)
            l_new = a * l_old
            acc = a * acc
            for s, vtj in zip(ss, vtjs):
                p = jnp.exp2(s - m_new)
                l_new = l_new + jnp.sum(p, axis=0, keepdims=True)
                acc = acc + _dot(vtj[h * hd:(h + 1) * hd, :], p.astype(bf16))
            new += [m_new, l_new, acc]
        return tuple(new)

    quads = i // 4
    fin = lax.fori_loop(0, quads, lambda n, c: update(c, tuple(4 * n + t for t in range(4))), tuple(state))
    done = 4 * quads
    has_pair = (i - done) // 2
    fin = lax.fori_loop(0, has_pair, lambda n, c: update(c, (done, done + 1)), fin)
    done = done + 2 * has_pair
    fin = lax.fori_loop(done, i, lambda j, c: update(c, (j,)), fin)
    out_t = jnp.concatenate([fin[3 * h + 2] / fin[3 * h + 1] for h in range(heads)], axis=0)
    o_ref[...] = out_t.T.astype(o_ref.dtype)


def _moba(proj, batch, seq):
    tokens = batch * seq
    n_blocks = seq // MOBA_BLOCK
    width = MOBA_STEP_WIDTH
    return pl.pallas_call(
        functools.partial(_moba_kernel, n_blocks=n_blocks),
        grid=(batch, MOBA_WIDTH // width, n_blocks),
        in_specs=[
            pl.BlockSpec((MOBA_BLOCK, width), lambda b, c, i: (b * n_blocks + i, OFF_QA // width + c)),
            pl.BlockSpec((seq, width), lambda b, c, i: (b, OFF_KA // width + c)),
            pl.BlockSpec((seq, width), lambda b, c, i: (b, OFF_VA // width + c)),
        ],
        out_specs=pl.BlockSpec((MOBA_BLOCK, width), lambda b, c, i: (b * n_blocks + i, c)),
        out_shape=jax.ShapeDtypeStruct((tokens, MOBA_WIDTH), bf16),
        scratch_shapes=[
            pltpu.VMEM((n_blocks, width), f32),
            pltpu.VMEM((n_blocks, width, MOBA_BLOCK), bf16),
            pltpu.VMEM((width // MOBA_HEAD_DIM, n_blocks, MOBA_BLOCK), f32),
            pltpu.VMEM((width // MOBA_HEAD_DIM, n_blocks, MOBA_BLOCK), f32),
        ],
        compiler_params=_params("parallel", "parallel", "arbitrary"),
        name="moba_attention",
    )(proj, proj, proj)


def _ret_kernel(q_ref, k_ref, v0_ref, v1_ref, g0_ref, g1_ref, decay_ref, zeta_ref, xi_ref, cd_ref,
                o_ref, state_ref):
    c = pl.program_id(1)
    heads_per_half = RET_HEADS // 2
    v_refs = (v0_ref, v1_ref)
    g_refs = (g0_ref, g1_ref)

    @pl.when(c == 0)
    def _():
        state_ref[...] = jnp.zeros_like(state_ref)

    for h in range(RET_HEADS):
        qk_cols = slice(h * RET_QK_DIM, (h + 1) * RET_QK_DIM)
        v_cols = slice(h * RET_V_DIM, (h + 1) * RET_V_DIM)
        half_cols = slice((h % heads_per_half) * RET_V_DIM, (h % heads_per_half + 1) * RET_V_DIM)
        q = q_ref[:, qk_cols]
        k = k_ref[:, qk_cols]
        v = v_refs[h // heads_per_half][:, half_cols]
        state = state_ref[h]
        scores = _dot_nt(q, k) * decay_ref[h]
        inner = _dot(scores.astype(bf16), v)
        cross = _dot(q, state.astype(bf16)) * xi_ref[h]
        o = inner + cross
        kz = (k.astype(f32) * zeta_ref[h]).astype(bf16)
        state_ref[h] = cd_ref[h] * state + _dot_tn(kz, v)

        mu = jnp.mean(o, axis=-1, keepdims=True)
        oc = o - mu
        var = jnp.mean(oc * oc, axis=-1, keepdims=True)
        on = oc * lax.rsqrt(var + GN_EPS)
        g = g_refs[h // heads_per_half][:, half_cols].astype(f32)
        o_ref[:, v_cols] = (on * (g * _sigmoid(g))).astype(o_ref.dtype)


def _retention(proj, batch, seq):
    tokens = batch * seq
    C = RET_CHUNK
    n_chunks = seq // C
    half_v = RET_V_WIDTH // 2
    gammas =1.0 - 2.0 ** (-5.0 - jnp.arange(RET_HEADS, dtype=f32))
    log_g = jnp.log(gammas)
    idx = jnp.arange(C, dtype=f32)
    diff = idx[:, None] - idx[None, :]
    decay = jnp.where(diff >= 0, jnp.exp(jnp.maximum(diff, 0.0)[None] * log_g[:, None, None]), 0.0)
    zeta = jnp.exp((C - 1 - idx)[None, :] * log_g[:, None])
    xi = jnp.exp((idx + 1.0)[None, :] * log_g[:, None])
    zeta_t = jnp.broadcast_to(zeta[:, :, None], (RET_HEADS, C, RET_QK_DIM))
    xi_t = jnp.broadcast_to(xi[:, :, None], (RET_HEADS, C, RET_V_DIM))
    cd_t = jnp.broadcast_to(jnp.exp(C * log_g)[:, None, None], (RET_HEADS, 1, RET_V_DIM))
    return pl.pallas_call(
        _ret_kernel,
        grid=(batch, n_chunks),
        in_specs=[
            pl.BlockSpec((C, RET_QK_WIDTH), lambda b, c: (b * n_chunks + c, OFF_QR // RET_QK_WIDTH)),
            pl.BlockSpec((C, RET_QK_WIDTH), lambda b, c: (b * n_chunks + c, OFF_KR // RET_QK_WIDTH)),
            pl.BlockSpec((C, half_v), lambda b, c: (b * n_chunks + c, OFF_VR // half_v)),
            pl.BlockSpec((C, half_v), lambda b, c: (b * n_chunks + c, OFF_VR // half_v + 1)),
            pl.BlockSpec((C, half_v), lambda b, c: (b * n_chunks + c, OFF_GR // half_v)),
            pl.BlockSpec((C, half_v), lambda b, c: (b * n_chunks + c, OFF_GR // half_v + 1)),
            pl.BlockSpec((RET_HEADS, C, C), lambda b, c: (0, 0, 0)),
            pl.BlockSpec((RET_HEADS, C, RET_QK_DIM), lambda b, c: (0, 0, 0)),
            pl.BlockSpec((RET_HEADS, C, RET_V_DIM), lambda b, c: (0, 0, 0)),
            pl.BlockSpec((RET_HEADS, 1, RET_V_DIM), lambda b, c: (0, 0, 0)),
        ],
        out_specs=pl.BlockSpec((C, RET_V_WIDTH), lambda b, c: (b * n_chunks + c, 0)),
        out_shape=jax.ShapeDtypeStruct((tokens, RET_V_WIDTH), bf16),
        scratch_shapes=[pltpu.VMEM((RET_HEADS, RET_QK_DIM, RET_V_DIM), f32)],
        compiler_params=_params("parallel", "arbitrary"),
        name="retention",
    )(proj, proj, proj, proj, proj, proj, decay, zeta_t, xi_t, cd_t)


def _merge_router_kernel(oa_ref, orr_ref, ga0_ref, ga1_ref, gb0_ref, gb1_ref, x_ref,
                         wm_ref, wr_ref, wo_ref, g1_ref, b1_ref, wrt_hi_ref, wrt_lo_ref, rb_ref,
                         h_ref, pos_ref, w_ref, tab_ref, cnt_ref, carry_ref):
    step = pl.program_id(0)
    tm = TOK_TM

    @pl.when(step == 0)
    def _():
        carry_ref[...] = jnp.zeros_like(carry_ref)

    _interleave(
        _merge_router_tile(t, slice(t * tm, (t + 1) * tm), tm, oa_ref, orr_ref, ga0_ref, ga1_ref, gb0_ref, gb1_ref,
                           x_ref, wm_ref, wr_ref, wo_ref, g1_ref, b1_ref, wrt_hi_ref, wrt_lo_ref, rb_ref,
                           h_ref, pos_ref, w_ref, tab_ref, cnt_ref, carry_ref)
        for t in range(x_ref.shape[0] // tm))


def _merge_router_tile(t, rows, tm, oa_ref, orr_ref, ga0_ref, ga1_ref, gb0_ref, gb1_ref, x_ref,
                       wm_ref, wr_ref, wo_ref, g1_ref, b1_ref, wrt_hi_ref, wrt_lo_ref, rb_ref,
                       h_ref, pos_ref, w_ref, tab_ref, cnt_ref, carry_ref):
    a = _dot(oa_ref[rows, :], wm_ref[...])
    r = _dot(orr_ref[rows, :], wr_ref[...])
    yield
    ga = jnp.concatenate([ga0_ref[rows, :], ga1_ref[rows, :]], axis=1).astype(f32)
    gb = jnp.concatenate([gb0_ref[rows, :], gb1_ref[rows, :]], axis=1).astype(f32)
    merged = _sigmoid(ga) * a + _sigmoid(gb) * r
    yield
    mix = _dot(merged.astype(bf16), wo_ref[...])
    yield
    h = _layer_norm(ALPHA * x_ref[rows, :] + mix, g1_ref[...], b1_ref[...])
    h_ref[rows, :] = h
    yield

    h_hi, h_lo = _split_bf16(h)
    w_hi = wrt_hi_ref[...]
    logits = _dot_nt(w_hi, h_hi) + _dot_nt(w_hi, h_lo) + _dot_nt(wrt_lo_ref[...], h_hi)
    yield
    scores = _sigmoid(logits)
    biased = scores + rb_ref[...]

    v = biased.reshape(N_GROUPS, GROUP_SIZE, tm)
    sub = lax.broadcasted_iota(i32, v.shape, 1)
    m1 = jnp.max(v, axis=1, keepdims=True)
    i1 = jnp.min(jnp.where(v == m1, sub, GROUP_SIZE), axis=1, keepdims=True)
    m2 = jnp.max(jnp.where(sub == i1, -jnp.inf, v), axis=1, keepdims=True)
    gscore = jnp.broadcast_to(m1 + m2, v.shape).reshape(N_EXPERTS, tm)

    eid = lax.broadcasted_iota(i32, (N_EXPERTS, tm), 0)
    egrp = eid // GROUP_SIZE
    e_mask = jnp.zeros((N_EXPERTS, tm), jnp.bool_)
    for _ in range(TOPK_GROUPS):
        m = jnp.max(gscore, axis=0, keepdims=True)
        idx = jnp.min(jnp.where(gscore == m, egrp, N_GROUPS), axis=0, keepdims=True)
        hit = egrp == idx
        e_mask = e_mask | hit
        gscore = jnp.where(hit, -jnp.inf, gscore)
    cand = jnp.where(e_mask, biased, -jnp.inf)

    chosen = jnp.zeros((N_EXPERTS, tm), jnp.bool_)
    e_rows = []
    w_rows = []
    for _ in range(TOP_K):
        m = jnp.max(cand, axis=0, keepdims=True)
        idx = jnp.min(jnp.where(cand == m, eid, N_EXPERTS), axis=0, keepdims=True)
        hit = eid == idx
        chosen = chosen | hit
        e_rows.append(idx)
        w_rows.append(jnp.sum(jnp.where(hit, scores, 0.0), axis=0, keepdims=True))
        cand = jnp.where(hit, -jnp.inf, cand)
    w_sum = w_rows[0]
    for wk in w_rows[1:]:
        w_sum = w_sum + wk

    t_src = lax.broadcasted_iota(i32, (tm, tm), 0)
    t_dst = lax.broadcasted_iota(i32, (tm, tm), 1)
    before = (t_src < t_dst).astype(bf16)
    chosen_f = chosen.astype(f32)
    prior = _dot(chosen_f.astype(bf16), before)
    cnt = jnp.sum(chosen_f, axis=1, keepdims=True)
    cnt_pad = jnp.ceil(cnt * (1.0 / UNIT)) * UNIT
    e_src = lax.broadcasted_iota(i32, (N_EXPERTS, N_EXPERTS), 1)
    e_dst = lax.broadcasted_iota(i32, (N_EXPERTS, N_EXPERTS), 0)
    earlier = (e_src < e_dst).astype(bf16)
    cnt_pad_l = jnp.broadcast_to(cnt_pad, (N_EXPERTS, LANES))
    loc_start_l = _dot(earlier, cnt_pad_l.astype(bf16))
    loc_start = loc_start_l[:, 0:1]
    where_e = prior + loc_start
    pos_ref[:, rows] = jnp.zeros((pos_ref.shape[0], tm), pos_ref.dtype)
    w_ref[:, rows] = jnp.zeros((w_ref.shape[0], tm), w_ref.dtype)
    for k in range(TOP_K):
        pos_ref[k:k + 1, rows] = jnp.sum(jnp.where(eid == e_rows[k], where_e, 0.0), axis=0, keepdims=True).astype(i32)
        w_ref[k:k + 1, rows] = w_rows[k] / w_sum * ROUTED_SCALE

    carry = carry_ref[...]
    tl = lax.broadcasted_iota(i32, (N_EXPERTS, LANES), 1)
    table = jnp.where(tl == 0, cnt_pad_l, jnp.where(tl == 1, loc_start_l, jnp.broadcast_to(carry, (N_EXPERTS, LANES))))
    tab_ref[t] = table.astype(i32)
    carry = carry + cnt_pad
    carry_ref[...] = carry
    cnt_ref[...] = jnp.broadcast_to(carry, cnt_ref.shape).astype(i32)


def _merge_router(oa, orr, proj, x2d, wm, wr, wo, g1, b1, wrt_hi, wrt_lo, rb):
    tokens = x2d.shape[0]
    tm = MERGE_TILES * TOK_TM
    half = D_MODEL // 2

    def gate_spec(off):
        return pl.BlockSpec((tm, half), lambda i: (i, off // half))

    def full(shape):
        return pl.BlockSpec(shape, lambda i: (0,) * len(shape))

    row8 = pl.BlockSpec((8, tm), lambda i: (0, i))
    return pl.pallas_call(
        _merge_router_kernel,
        grid=(tokens // tm,),
        in_specs=[
            pl.BlockSpec((tm, MOBA_WIDTH), lambda i: (i, 0)),
            pl.BlockSpec((tm, RET_V_WIDTH), lambda i: (i, 0)),
            gate_spec(OFF_GA), gate_spec(OFF_GA + half), gate_spec(OFF_GB), gate_spec(OFF_GB + half),
            pl.BlockSpec((tm, D_MODEL), lambda i: (i, 0)),
            full((MOBA_WIDTH, D_MODEL)), full((RET_V_WIDTH, D_MODEL)), full((D_MODEL, D_MODEL)),
            full((1, D_MODEL)), full((1, D_MODEL)),
            full((N_EXPERTS, D_MODEL)), full((N_EXPERTS, D_MODEL)), full((N_EXPERTS, 1)),
        ],
        out_specs=[
            pl.BlockSpec((tm, D_MODEL), lambda i: (i, 0)),
            row8, row8,
            pl.BlockSpec((MERGE_TILES, N_EXPERTS, LANES), lambda i: (i, 0, 0)),
            full((N_EXPERTS, LANES)),
        ],
        out_shape=[
            jax.ShapeDtypeStruct((tokens, D_MODEL), f32),
            jax.ShapeDtypeStruct((8, tokens), i32),
            jax.ShapeDtypeStruct((8, tokens), f32),
            jax.ShapeDtypeStruct((tokens // TOK_TM, N_EXPERTS, LANES), i32),
            jax.ShapeDtypeStruct((N_EXPERTS, LANES), i32),
        ],
        scratch_shapes=[pltpu.VMEM((N_EXPERTS, 1), f32)],
        compiler_params=_params("arbitrary"),
        name="merge_ln1_router",
    )(oa, orr, proj, proj, proj, proj, x2d, wm, wr, wo, g1, b1, wrt_hi, wrt_lo, rb)


def _pack_rows(x):
    bits = lax.bitcast_convert_type(x, jnp.uint32)
    return (bits[:, :HALF] & jnp.uint32(0xFFFF0000)) | (bits[:, HALF:] >> 16)


def _unpack_rows(words):
    left = lax.bitcast_convert_type(words & jnp.uint32(0xFFFF0000), f32)
    right = lax.bitcast_convert_type(words << 16, f32)
    return left.astype(bf16), right.astype(bf16)


def _start_alternating(n, make_copy):
    def pair(m, _):
        make_copy(2 * m).start(priority=0)
        make_copy(2 * m + 1).start(priority=1)
        return 0

    lax.fori_loop(0, n // 2, pair, 0)

    @pl.when(n % 2 == 1)
    def _():
        make_copy(n - 1).start(priority=0)


def _move_runs(tiles, n_ref, src_ref, dst_ref, make_copy):
    def piece(u, tile, c, p):
        rows = UNIT << c
        base = (tile * RUN_CLASSES + c) * N_EXPERTS
        loc = pl.ds(pl.multiple_of(src_ref[base + p] * UNIT, UNIT), rows)
        glob = pl.ds(pl.multiple_of(dst_ref[base + p] * UNIT, UNIT), rows)
        return make_copy(u, loc, glob)

    for u, tile in enumerate(tiles):
        for c in range(RUN_CLASSES):
            _start_alternating(n_ref[tile * RUN_CLASSES + c], functools.partial(piece, u, tile, c))

    for u, tile in enumerate(tiles):
        for c in range(RUN_CLASSES):
            def wait_one(p, _, u=u, tile=tile, c=c):
                piece(u, tile, c, p).wait()
                return 0

            lax.fori_loop(0, n_ref[tile * RUN_CLASSES + c], wait_one, 0)


def _dispatch_kernel(n_ref, src_ref, dst_ref, tail_ref, used_ref, h_ref, pos_ref, rows_ref,
                     xs_ref, zero_ref, sem_ref, zsem_ref):
    step = pl.program_id(0)
    tm = TOK_TM
    tiles = h_ref.shape[0] // tm

    @pl.when(step == 0)
    def _():
        zero_ref[...] = jnp.zeros_like(zero_ref)

        def clear_copy(block):
            start = pl.multiple_of(block * ROW_BLOCK, ROW_BLOCK)
            return pltpu.make_async_copy(zero_ref, rows_ref.at[pl.ds(start, ROW_BLOCK), :], zsem_ref)

        def clear_all(act):
            def tail(e, _):
                tail_block = tail_ref[e]

                @pl.when(tail_block >= 0)
                def _():
                    act(clear_copy(tail_block))
                return 0

            lax.fori_loop(0, N_EXPERTS, tail, 0)

            def unused(b, _):
                act(clear_copy(b))
                return 0

            lax.fori_loop(used_ref[0], rows_ref.shape[0] // ROW_BLOCK, unused, 0)

        clear_all(lambda cp: cp.start())
        clear_all(lambda cp: cp.wait())

    def sort_tile(u):
        cols = slice(u * tm, (u + 1) * tm)
        pos = pos_ref[:, cols]
        r_iota = lax.broadcasted_iota(i32, (LOC_ROWS, tm), 0)
        onehot = r_iota == pos[0:1, :]
        for k in range(1, TOP_K):
            onehot = onehot | (r_iota == pos[k:k + 1, :])
        onehot = onehot.astype(bf16)
        yield
        xs = _dot(onehot, h_ref[cols, :].astype(bf16))
        yield
        xs_ref[u] = _pack_rows(xs)

    _interleave(sort_tile(u) for u in range(tiles))

    _move_runs([step * tiles + u for u in range(tiles)], n_ref, src_ref, dst_ref,
               lambda u, loc, glob: pltpu.make_async_copy(xs_ref.at[u, loc, :], rows_ref.at[glob, :], sem_ref))


def _dispatch(runs, tail_block, n_used, h, pos8, n_rows):
    tokens = h.shape[0]
    tm = MOE_TILES * TOK_TM
    grid_spec = pltpu.PrefetchScalarGridSpec(
        num_scalar_prefetch=5,
        grid=(tokens // tm,),
        in_specs=[
            pl.BlockSpec((tm, D_MODEL), lambda i, *_: (i, 0)),
            pl.BlockSpec((8, tm), lambda i, *_: (0, i)),
        ],
        out_specs=pl.BlockSpec(memory_space=pl.ANY),
        scratch_shapes=[
            pltpu.VMEM((MOE_TILES, LOC_ROWS, HALF), jnp.uint32),
            pltpu.VMEM((ROW_BLOCK, HALF), jnp.uint32),
            pltpu.SemaphoreType.DMA,
            pltpu.SemaphoreType.DMA,
        ],
    )
    return pl.pallas_call(
        _dispatch_kernel,
        grid_spec=grid_spec,
        out_shape=jax.ShapeDtypeStruct((n_rows, HALF), jnp.uint32),
        compiler_params=_params("arbitrary"),
        name="moe_dispatch",
    )(*runs, tail_block, n_used, h, pos8)


def _expert_kernel(eid_ref, used_ref, x_ref, wg_ref, wu_ref, wd_ref, y_ref, wgb_ref, wub_ref, wdb_ref):
    b = pl.program_id(0)
    changed = (b == 0) | (eid_ref[b] != eid_ref[jnp.maximum(b - 1, 0)])

    @pl.when(changed)
    def _():
        wgb_ref[...] = wg_ref[0].astype(bf16)
        wub_ref[...] = wu_ref[0].astype(bf16)
        wdb_ref[...] = wd_ref[0].astype(bf16)

    @pl.when(b < used_ref[0])
    def _():
        def chunk_phases(c):
            rows = slice(c * EXPERT_CHUNK, (c + 1) * EXPERT_CHUNK)
            xl, xr = _unpack_rows(x_ref[rows, :])
            yield
            g = _dot(xl, wgb_ref[:HALF, :]) + _dot(xr, wgb_ref[HALF:, :])
            u = _dot(xl, wub_ref[:HALF, :]) + _dot(xr, wub_ref[HALF:, :])
            yield
            mid = ((g * _sigmoid(g)) * u).astype(bf16)
            yield
            y = _dot(mid, wdb_ref[...])
            yield
            y_ref[rows, :] = _pack_rows(y.astype(bf16).astype(f32))

        _interleave(chunk_phases(c) for c in range(ROW_BLOCK // EXPERT_CHUNK))

    @pl.when(b >= used_ref[0])
    def _():
        y_ref[...] = jnp.zeros_like(y_ref)


def _experts(block_eid, n_used, rows, wg, wu, wd):
    n_rows = rows.shape[0]
    n_blocks = n_rows // ROW_BLOCK

    def row_map(b, eid, used):
        return (b, 0)

    def w_map(b, eid, used):
        return (eid[b], 0, 0)

    grid_spec = pltpu.PrefetchScalarGridSpec(
        num_scalar_prefetch=2,
        grid=(n_blocks,),
        in_specs=[
            pl.BlockSpec((ROW_BLOCK, HALF), row_map),
            pl.BlockSpec((1, D_MODEL, D_EXPERT), w_map),
            pl.BlockSpec((1, D_MODEL, D_EXPERT), w_map),
            pl.BlockSpec((1, D_EXPERT, D_MODEL), w_map),
        ],
        out_specs=pl.BlockSpec((ROW_BLOCK, HALF), row_map),
        scratch_shapes=[
            pltpu.VMEM((D_MODEL, D_EXPERT), bf16),
            pltpu.VMEM((D_MODEL, D_EXPERT), bf16),
            pltpu.VMEM((D_EXPERT, D_MODEL), bf16),
        ],
    )
    return pl.pallas_call(
        _expert_kernel,
        grid_spec=grid_spec,
        out_shape=jax.ShapeDtypeStruct((n_rows, HALF), jnp.uint32),
        compiler_params=_params("arbitrary"),
        name="moe_experts",
    )(block_eid, n_used, rows, wg, wu, wd)


def _combine_kernel(n_ref, src_ref, dst_ref, y_ref, h_ref, pos_ref, wt_ref, p_ref, wsg_ref, wsu_ref, wsd_ref,
                    wpp_ref, wpg_ref, g2_ref, b2_ref, g3_ref, b3_ref, o_ref, ybuf_ref, sem_ref):
    step = pl.program_id(0)
    tm = TOK_TM
    tiles = h_ref.shape[0] // tm

    @pl.when(step == 0)
    def _():
        ybuf_ref[...] = jnp.zeros_like(ybuf_ref)

    _move_runs([step * tiles + u for u in range(tiles)], n_ref, src_ref, dst_ref,
               lambda u, loc, glob: pltpu.make_async_copy(y_ref.at[glob, :], ybuf_ref.at[u, loc, :], sem_ref))

    def tile_phases(u):
        rows = slice(u * tm, (u + 1) * tm)
        h = h_ref[rows, :]
        hb = h.astype(bf16)
        sg = _dot(hb, wsg_ref[...])
        su = _dot(hb, wsu_ref[...])
        yield
        mid = ((sg * _sigmoid(sg)) * su).astype(bf16)
        yield
        shared = _dot(mid, wsd_ref[...])
        ple_in = _dot(p_ref[rows, :].astype(bf16), wpp_ref[...])
        yield
        pos = pos_ref[rows, :]
        wt = wt_ref[rows, :]
        c_iota = lax.broadcasted_iota(i32, (tm, LOC_ROWS), 1)
        spread = jnp.where(c_iota == pos[:, 0:1], wt[:, 0:1], 0.0)
        for k in range(1, TOP_K):
            spread = jnp.where(c_iota == pos[:, k:k + 1], wt[:, k:k + 1], spread)
        spread = spread.astype(bf16)
        yield
        yl, yr = _unpack_rows(ybuf_ref[u])
        yield
        routed = jnp.concatenate([_dot(spread, yl), _dot(spread, yr)], axis=1)
        yield
        h2 = _layer_norm(ALPHA * h + (routed + shared), g2_ref[...], b2_ref[...])
        yield
        gate = _dot(h2.astype(bf16), wpg_ref[...])
        yield
        ple = ple_in * _sigmoid(gate)
        o_ref[rows, :] = _layer_norm(ALPHA * h2 + ple, g3_ref[...], b3_ref[...])

    _interleave(tile_phases(u) for u in range(tiles))


def _combine(runs, y_rows, h, pos_tok, w_tok, p2d, wsg, wsu, wsd, wpp, wpg, g2, b2, g3, b3):
    tokens = h.shape[0]
    tm = MOE_TILES * TOK_TM

    def full(shape):
        return pl.BlockSpec(shape, lambda i, *_: (0,) * len(shape))

    def tile(width):
        return pl.BlockSpec((tm, width), lambda i, *_: (i, 0))

    grid_spec = pltpu.PrefetchScalarGridSpec(
        num_scalar_prefetch=3,
        grid=(tokens // tm,),
        in_specs=[
            pl.BlockSpec(memory_space=pl.ANY),
            tile(D_MODEL), tile(8), tile(8), tile(PLE_DIM),
            full((D_MODEL, D_SHARED)), full((D_MODEL, D_SHARED)), full((D_SHARED, D_MODEL)),
            full((PLE_DIM, D_MODEL)), full((D_MODEL, D_MODEL)),
            full((1, D_MODEL)), full((1, D_MODEL)), full((1, D_MODEL)), full((1, D_MODEL)),
        ],
        out_specs=tile(D_MODEL),
        scratch_shapes=[
            pltpu.VMEM((MOE_TILES, LOC_ROWS, HALF), jnp.uint32),
            pltpu.SemaphoreType.DMA,
        ],
    )
    return pl.pallas_call(
        _combine_kernel,
        grid_spec=grid_spec,
        out_shape=jax.ShapeDtypeStruct((tokens, D_MODEL), f32),
        compiler_params=_params("arbitrary"),
        name="moe_combine_ln2_ple_ln3",
    )(*runs, y_rows, h, pos_tok, w_tok, p2d, wsg, wsu, wsd, wpp, wpg, g2, b2, g3, b3)


def _row_layout(table, totals, n_blocks):
    cnt_u = table[:, :, 0] // UNIT
    loc_u = table[:, :, 1] // UNIT
    padded = (totals + ROW_BLOCK - 1) // ROW_BLOCK * ROW_BLOCK
    pends = jnp.cumsum(padded)
    pstarts = pends - padded
    glob_u = (pstarts[None, :] + table[:, :, 2]) // UNIT
    cls = jnp.arange(RUN_CLASSES, dtype=i32)
    has = (cnt_u[:, None, :] >> cls[None, :, None]) & 1
    off = (cnt_u[:, None, :] >> (cls[None, :, None] + 1)) << (cls[None, :, None] + 1)
    upto = jnp.cumsum(has, axis=2)
    n_pieces = upto[:, :, -1].astype(i32).reshape(-1)
    slot = jnp.arange(N_EXPERTS, dtype=i32)
    e_of = jnp.sum((upto[:, :, None, :] <= slot[None, None, :, None]).astype(i32), axis=-1)
    is_e = e_of[:, :, :, None] == jnp.arange(N_EXPERTS, dtype=i32)
    src = jnp.sum(jnp.where(is_e, (loc_u[:, None, :] + off)[:, :, None, :], 0), axis=-1).astype(i32).reshape(-1)
    dst = jnp.sum(jnp.where(is_e, (glob_u[:, None, :] + off)[:, :, None, :], 0), axis=-1).astype(i32).reshape(-1)
    runs = (n_pieces, src, dst)
    tail_block = jnp.where(totals > 0, pends // ROW_BLOCK - 1, -1).astype(i32)
    block_start = jnp.arange(n_blocks, dtype=i32) * ROW_BLOCK
    ends_before = jnp.sum((pends[None, :] <= block_start[:, None]).astype(i32), axis=1)
    block_eid = jnp.minimum(ends_before, N_EXPERTS - 1).astype(i32)
    n_used = (pends[-1:] // ROW_BLOCK).astype(i32)
    return runs, tail_block, block_eid, n_used


def _layer(x2d, p2d, batch, seq, w_in, w_moba_up, w_ret_up, w_out, ln1_g, ln1_b, w_router, router_bias,
           w_eg, w_eu, w_ed, w_sg, w_su, w_sd, ln2_g, ln2_b, w_ple_proj, w_ple_gate, ln3_g, ln3_b):
    tokens = batch * seq
    row = lambda a: a.reshape(1, -1).astype(f32)

    proj = _project(x2d, w_in.astype(bf16), seq)
    oa = _moba(proj, batch, seq)
    orr = _retention(proj, batch, seq)

    wrt = w_router.astype(f32).T
    wrt_hi = wrt.astype(bf16)
    wrt_lo = (wrt - wrt_hi.astype(f32)).astype(bf16)
    h1, pos8, w8, table, totals = _merge_router(
        oa, orr, proj, x2d, w_moba_up.astype(bf16), w_ret_up.astype(bf16), w_out.astype(bf16),
        row(ln1_g), row(ln1_b), wrt_hi, wrt_lo, router_bias.astype(f32).reshape(N_EXPERTS, 1))

    n_tiles = tokens // TOK_TM
    max_rows = tokens * TOP_K + n_tiles * N_EXPERTS * (UNIT - 1) + N_EXPERTS * (ROW_BLOCK - 1)
    n_blocks = -(-max_rows // ROW_BLOCK)
    runs, tail_block, block_eid, n_used = _row_layout(table[:, :, :3], totals[:, 0], n_blocks)

    rows = _dispatch(runs, tail_block, n_used, h1, pos8, n_blocks * ROW_BLOCK)
    y_rows = _experts(block_eid, n_used, rows, w_eg, w_eu, w_ed)
    return _combine(runs, y_rows, h1, pos8.T, w8.T, p2d,
                    w_sg.astype(bf16), w_su.astype(bf16), w_sd.astype(bf16),
                    w_ple_proj.astype(bf16), w_ple_gate.astype(bf16),
                    row(ln2_g), row(ln2_b), row(ln3_g), row(ln3_b))


def kernel(x, p, w_in, w_moba_up, w_ret_up, w_out, ln1_g, ln1_b, w_router, router_bias, w_exp_gate, w_exp_up,
           w_exp_down, w_sh_gate, w_sh_up, w_sh_down, ln2_g, ln2_b, w_ple_proj, w_ple_gate, ln3_g, ln3_b):
    batch, seq, d = x.shape
    assert d == D_MODEL and seq % max(MOBA_BLOCK, RET_CHUNK) == 0 and (batch * seq) % (MERGE_TILES * TOK_TM) == 0
    assert w_in.shape[0] == DEPTH
    h = x.reshape(batch * seq, d)
    for i in range(DEPTH):
        h = _layer(h, p[i].reshape(batch * seq, PLE_DIM), batch, seq,
                   w_in[i], w_moba_up[i], w_ret_up[i], w_out[i], ln1_g[i], ln1_b[i], w_router[i], router_bias[i],
                   w_exp_gate[i], w_exp_up[i], w_exp_down[i], w_sh_gate[i], w_sh_up[i], w_sh_down[i],
                   ln2_g[i], ln2_b[i], w_ple_proj[i], w_ple_gate[i], ln3_g[i], ln3_b[i])
    return h.reshape(batch, seq, d)
```
